```python
import jax, jax.numpy as jnp
from jax import lax
import numpy as np

D_MODEL = 2048
BATCH = 8
SEQ = 2048
DEPTH = 4

D_FF = 5632
SSM_WIDTH = 1024
SSM_GROUP = 16
SSM_GROUPS = SSM_WIDTH // SSM_GROUP
SSM_STATE = 64
DT_MIN = 1e-3
DT_MAX = 1e-1
GDN_HEADS = 8
GDN_HEAD_DIM = 128
GDN_WIDTH = GDN_HEADS * GDN_HEAD_DIM
CONV_K = 4
CHUNK = 64
IN_SIZES = (SSM_WIDTH, GDN_WIDTH, GDN_WIDTH, GDN_WIDTH, GDN_WIDTH, GDN_HEADS, GDN_HEADS, D_MODEL, D_MODEL)
IN_COLS = sum(IN_SIZES)
LN_EPS = 1e-5
RMS_EPS = 1e-6
L2_EPS = 1e-6

kernel_name = 'hybrid_s5_gdn_macaron_deepnorm'


def layer_norm(x, g, b):
    xf = x.astype(jnp.float32)
    mu = jnp.mean(xf, axis=-1, keepdims=True)
    var = jnp.mean(jnp.square(xf - mu), axis=-1, keepdims=True)
    y = (xf - mu) * lax.rsqrt(var + LN_EPS) * g.astype(jnp.float32) + b.astype(jnp.float32)
    return y.astype(x.dtype)


def swiglu_ffn(x, w_gu, w_down):
    gate, up = jnp.split(x @ w_gu, 2, axis=-1)
    return (jax.nn.silu(gate) * up) @ w_down


def cmul(ar, ai, br, bi):
    return ar * br - ai * bi, ar * bi + ai * br


def s5_branch(u, a_re, a_im, log_dt, b_re, b_im, c_re, c_im, d_skip, glu_w, glu_b):
    f32 = jnp.float32
    bsz, seq, _ = u.shape
    ug = u.astype(f32).reshape(bsz, seq, SSM_GROUPS, SSM_GROUP)
    dt = jnp.exp(log_dt.astype(f32))[:, None]
    lr, li = a_re.astype(f32), a_im.astype(f32)
    mag = jnp.exp(lr * dt)
    lbar_r, lbar_i = mag * jnp.cos(li * dt), mag * jnp.sin(li * dt)
    den = lr * lr + li * li
    zr, zi = cmul(lbar_r - 1.0, lbar_i, lr / den, -li / den)
    bbar_r, bbar_i = cmul(zr[:, :, None], zi[:, :, None], b_re.astype(f32), b_im.astype(f32))
    bu_r = jnp.einsum('blgh,gph->blgp', ug, bbar_r)
    bu_i = jnp.einsum('blgh,gph->blgp', ug, bbar_i)
    a_r = jnp.broadcast_to(lbar_r, (1, seq) + lbar_r.shape)
    a_i = jnp.broadcast_to(lbar_i, (1, seq) + lbar_i.shape)

    def combine(e_early, e_late):
        a1r, a1i, b1r, b1i = e_early
        a2r, a2i, b2r, b2i = e_late
        ar, ai = cmul(a2r, a2i, a1r, a1i)
        br, bi = cmul(a2r, a2i, b1r, b1i)
        return (ar, ai, br + b2r, bi + b2i)

    _, _, s_r, s_i = lax.associative_scan(combine, (a_r, a_i, bu_r, bu_i), axis=1)
    y = (jnp.einsum('ghp,blgp->blgh', c_re.astype(f32), s_r)
         - jnp.einsum('ghp,blgp->blgh', c_im.astype(f32), s_i)
         + d_skip.astype(f32) * ug)
    y = y.reshape(bsz, seq, SSM_WIDTH).astype(u.dtype)
    y = jax.nn.gelu(y)
    return y * jax.nn.sigmoid(y @ glu_w + glu_b)


def causal_dwconv(x, w):
    return lax.conv_general_dilated(
        x, w[:, None, :], window_strides=(1,), padding=[(CONV_K - 1, 0)],
        dimension_numbers=('NWC', 'WIO', 'NWC'), feature_group_count=x.shape[-1])


def l2norm(t):
    return t * lax.rsqrt(jnp.sum(t * t, axis=-1, keepdims=True) + L2_EPS)


def gated_deltanet_branch(q, k, v, z, beta_logit, a_in, conv_w, a_log, dt_bias, norm_w):
    f32 = jnp.float32
    bsz, seq, _ = q.shape
    n_chunks = seq // CHUNK
    qkv = jax.nn.silu(causal_dwconv(jnp.concatenate([q, k, v], axis=-1), conv_w)).astype(f32)
    q, k, v = jnp.split(qkv, 3, axis=-1)

    def heads(t):
        return t.reshape(bsz, n_chunks, CHUNK, GDN_HEADS, GDN_HEAD_DIM).transpose(0, 3, 1, 2, 4)

    def head_scalars(t):
        return t.reshape(bsz, n_chunks, CHUNK, GDN_HEADS).transpose(0, 3, 1, 2)

    q = l2norm(heads(q)) * (GDN_HEAD_DIM ** -0.5)
    k = l2norm(heads(k))
    v = heads(v)
    beta = head_scalars(jax.nn.sigmoid(beta_logit.astype(f32)))
    g = -jnp.exp(a_log.astype(f32)) * jax.nn.softplus(a_in.astype(f32) + dt_bias.astype(f32))
    gcum = jnp.cumsum(head_scalars(g), axis=-1)
    idx = jnp.arange(CHUNK)
    causal = idx[:, None] >= idx[None, :]
    strict = idx[:, None] > idx[None, :]
    decay = jnp.exp(jnp.where(causal, gcum[..., :, None] - gcum[..., None, :], -jnp.inf))
    k_beta = k * beta[..., None]
    lower = jnp.where(strict, jnp.einsum('bhncd,bhnsd->bhncs', k_beta, k) * decay, 0.0)
    rhs = jnp.concatenate([v * beta[..., None], k_beta * jnp.exp(gcum)[..., None]], axis=-1)
    sol = lax.linalg.triangular_solve(lower + jnp.eye(CHUNK, dtype=f32), rhs,
                                      left_side=True, lower=True, unit_diagonal=True)
    u_val, w_key = jnp.split(sol, 2, axis=-1)
    attn_intra = jnp.einsum('bhncd,bhnsd->bhncs', q, k) * decay
    q_dec = q * jnp.exp(gcum)[..., None]
    k_dec = k * jnp.exp(gcum[..., -1:] - gcum)[..., None]
    g_last = jnp.exp(gcum[..., -1])

    def chunk_step(state, xs):
        u_c, w_c, a_c, qd_c, kd_c, gl_c = xs
        v_new = u_c - jnp.einsum('bhcd,bhde->bhce', w_c, state)
        out = (jnp.einsum('bhcd,bhde->bhce', qd_c, state)
               + jnp.einsum('bhcs,bhse->bhce', a_c, v_new))
        state = state * gl_c[..., None, None] + jnp.einsum('bhcd,bhce->bhde', kd_c, v_new)
        return state, out

    xs = tuple(jnp.moveaxis(t, 2, 0) for t in (u_val, w_key, attn_intra, q_dec, k_dec, g_last))
    state0 = jnp.zeros((bsz, GDN_HEADS, GDN_HEAD_DIM, GDN_HEAD_DIM), f32)
    _, o = lax.scan(chunk_step, state0, xs)
    o = o.transpose(1, 0, 3, 2, 4).reshape(bsz, seq, GDN_HEADS, GDN_HEAD_DIM)
    o = o * lax.rsqrt(jnp.mean(o * o, axis=-1, keepdims=True) + RMS_EPS) * norm_w.astype(f32)
    o = o * jax.nn.silu(z.astype(f32).reshape(bsz, seq, GDN_HEADS, GDN_HEAD_DIM))
    return o.reshape(bsz, seq, GDN_WIDTH).astype(q.dtype if q.dtype != f32 else z.dtype)


def hybrid_mixer(h, w_in, conv_w, ssm_a_re, ssm_a_im, ssm_log_dt, ssm_b_re, ssm_b_im, ssm_c_re,
                 ssm_c_im, ssm_d, glu_w, glu_b, gdn_a_log, gdn_dt_bias, gdn_norm_w,
                 w_br_ssm, w_br_gdn, w_out):
    offsets = np.cumsum(IN_SIZES)[:-1].tolist()
    u, q, k, v, z, beta_logit, a_in, gate_ssm, gate_gdn = jnp.split(h @ w_in, offsets, axis=-1)
    y_ssm = s5_branch(u, ssm_a_re, ssm_a_im, ssm_log_dt, ssm_b_re, ssm_b_im,
                      ssm_c_re, ssm_c_im, ssm_d, glu_w, glu_b)
    y_gdn = gated_deltanet_branch(q, k, v, z, beta_logit, a_in, conv_w,
                                  gdn_a_log, gdn_dt_bias, gdn_norm_w)
    merged = (jax.nn.sigmoid(gate_ssm) * (y_ssm @ w_br_ssm)
              + jax.nn.sigmoid(gate_gdn) * (y_gdn @ w_br_gdn))
    return merged @ w_out


def _fwd_setup_inputs(seed: int = 0) -> dict:
    key = jax.random.key(seed)
    ks = jax.random.split(key, 32)
    f32 = jnp.float32
    L = DEPTH
    dn_beta = (8.0 * DEPTH) ** -0.25

    def nrm(k, shape, scale):
        return scale * jax.random.normal(k, shape, f32)

    lo, hi = float(np.log(DT_MIN)), float(np.log(DT_MAX))
    gdn_dt = jnp.exp(jax.random.uniform(ks[20], (L, GDN_HEADS), f32, lo, hi))
    return {
        'x': nrm(ks[0], (BATCH, SEQ, D_MODEL), 1.0),
        'ffn1_w_gu': nrm(ks[1], (L, D_MODEL, 2 * D_FF), D_MODEL ** -0.5),
        'ffn1_w_down': nrm(ks[2], (L, D_FF, D_MODEL), dn_beta * D_FF ** -0.5),
        'ln1_g': 1.0 + nrm(ks[3], (L, D_MODEL), 0.02),
        'ln1_b': nrm(ks[4], (L, D_MODEL), 0.02),
        'w_in': nrm(ks[5], (L, D_MODEL, IN_COLS), D_MODEL ** -0.5),
        'conv_w': nrm(ks[6], (L, CONV_K, 3 * GDN_WIDTH), CONV_K ** -0.5),
        'ssm_a_re': -0.5 + nrm(ks[7], (L, SSM_GROUPS, SSM_STATE), 0.02),
        'ssm_a_im': jnp.pi * jnp.arange(SSM_STATE, dtype=f32) + nrm(ks[8], (L, SSM_GROUPS, SSM_STATE), 0.02),
        'ssm_log_dt': jax.random.uniform(ks[9], (L, SSM_GROUPS), f32, lo, hi),
        'ssm_b_re': nrm(ks[10], (L, SSM_GROUPS, SSM_STATE, SSM_GROUP), (2 * SSM_GROUP) ** -0.5),
        'ssm_b_im': nrm(ks[11], (L, SSM_GROUPS, SSM_STATE, SSM_GROUP), (2 * SSM_GROUP) ** -0.5),
        'ssm_c_re': nrm(ks[12], (L, SSM_GROUPS, SSM_GROUP, SSM_STATE), (2 * SSM_STATE) ** -0.5),
        'ssm_c_im': nrm(ks[13], (L, SSM_GROUPS, SSM_GROUP, SSM_STATE), (2 * SSM_STATE) ** -0.5),
        'ssm_d': nrm(ks[14], (L, SSM_GROUPS, SSM_GROUP), 1.0),
        'glu_w': nrm(ks[15], (L, SSM_WIDTH, SSM_WIDTH), SSM_WIDTH ** -0.5),
        'glu_b': nrm(ks[16], (L, SSM_WIDTH), 0.02),
        'gdn_a_log': jnp.log(jax.random.uniform(ks[17], (L, GDN_HEADS), f32, 1.0, 16.0)),
        'gdn_dt_bias': gdn_dt + jnp.log(-jnp.expm1(-gdn_dt)),
        'gdn_norm_w': 1.0 + nrm(ks[18], (L, GDN_HEAD_DIM), 0.02),
        'w_br_ssm': nrm(ks[19], (L, SSM_WIDTH, D_MODEL), SSM_WIDTH ** -0.5),
        'w_br_gdn': nrm(ks[21], (L, GDN_WIDTH, D_MODEL), GDN_WIDTH ** -0.5),
        'w_out': nrm(ks[22], (L, D_MODEL, D_MODEL), dn_beta * D_MODEL ** -0.5),
        'ln2_g': 1.0 + nrm(ks[23], (L, D_MODEL), 0.02),
        'ln2_b': nrm(ks[24], (L, D_MODEL), 0.02),
        'ffn2_w_gu': nrm(ks[25], (L, D_MODEL, 2 * D_FF), D_MODEL ** -0.5),
        'ffn2_w_down': nrm(ks[26], (L, D_FF, D_MODEL), dn_beta * D_FF ** -0.5),
        'ln3_g': 1.0 + nrm(ks[27], (L, D_MODEL), 0.02),
        'ln3_b': nrm(ks[28], (L, D_MODEL), 0.02),
    }


def _fwd_reference(x, ffn1_w_gu, ffn1_w_down, ln1_g, ln1_b, w_in, conv_w, ssm_a_re, ssm_a_im,
              ssm_log_dt, ssm_b_re, ssm_b_im, ssm_c_re, ssm_c_im, ssm_d, glu_w, glu_b,
              gdn_a_log, gdn_dt_bias, gdn_norm_w, w_br_ssm, w_br_gdn, w_out, ln2_g, ln2_b,
              ffn2_w_gu, ffn2_w_down, ln3_g, ln3_b):
    alpha = (2.0 * DEPTH) ** 0.25
    for l in range(DEPTH):
        x = layer_norm(alpha * x + 0.5 * swiglu_ffn(x, ffn1_w_gu[l], ffn1_w_down[l]), ln1_g[l], ln1_b[l])
        mix = hybrid_mixer(x, w_in[l], conv_w[l], ssm_a_re[l], ssm_a_im[l], ssm_log_dt[l],
                           ssm_b_re[l], ssm_b_im[l], ssm_c_re[l], ssm_c_im[l], ssm_d[l],
                           glu_w[l], glu_b[l], gdn_a_log[l], gdn_dt_bias[l], gdn_norm_w[l],
                           w_br_ssm[l], w_br_gdn[l], w_out[l])
        x = layer_norm(alpha * x + mix, ln2_g[l], ln2_b[l])
        x = layer_norm(alpha * x + 0.5 * swiglu_ffn(x, ffn2_w_gu[l], ffn2_w_down[l]), ln3_g[l], ln3_b[l])
    return x


import jax as _jax
import jax.numpy as _jnp

TWIN_FORMAT = 'train_step'
FWD_PARAMS = ['x', 'ffn1_w_gu', 'ffn1_w_down', 'ln1_g', 'ln1_b', 'w_in', 'conv_w', 'ssm_a_re', 'ssm_a_im', 'ssm_log_dt', 'ssm_b_re', 'ssm_b_im', 'ssm_c_re', 'ssm_c_im', 'ssm_d', 'glu_w', 'glu_b', 'gdn_a_log', 'gdn_dt_bias', 'gdn_norm_w', 'w_br_ssm', 'w_br_gdn', 'w_out', 'ln2_g', 'ln2_b', 'ffn2_w_gu', 'ffn2_w_down', 'ln3_g', 'ln3_b']
TWIN_WEIGHTS = ['ffn1_w_gu', 'ffn1_w_down', 'ln1_g', 'ln1_b', 'w_in', 'conv_w', 'ssm_a_re', 'ssm_a_im', 'ssm_log_dt', 'ssm_b_re', 'ssm_b_im', 'ssm_c_re', 'ssm_c_im', 'ssm_d', 'glu_w', 'glu_b', 'gdn_a_log', 'gdn_dt_bias', 'gdn_norm_w', 'w_br_ssm', 'w_br_gdn', 'w_out', 'ln2_g', 'ln2_b', 'ffn2_w_gu', 'ffn2_w_down', 'ln3_g', 'ln3_b']
TWIN_DIFF_INPUT = 'x'
TWIN_INPUTS = ['x', 'ffn1_w_gu', 'ffn1_w_down', 'ln1_g', 'ln1_b', 'w_in', 'conv_w', 'ssm_a_re', 'ssm_a_im', 'ssm_log_dt', 'ssm_b_re', 'ssm_b_im', 'ssm_c_re', 'ssm_c_im', 'ssm_d', 'glu_w', 'glu_b', 'gdn_a_log', 'gdn_dt_bias', 'gdn_norm_w', 'w_br_ssm', 'w_br_gdn', 'w_out', 'ln2_g', 'ln2_b', 'ffn2_w_gu', 'ffn2_w_down', 'ln3_g', 'ln3_b', 'loss_target', 'm_ffn1_w_gu', 'm_ffn1_w_down', 'm_ln1_g', 'm_ln1_b', 'm_w_in', 'm_conv_w', 'm_ssm_a_re', 'm_ssm_a_im', 'm_ssm_log_dt', 'm_ssm_b_re', 'm_ssm_b_im', 'm_ssm_c_re', 'm_ssm_c_im', 'm_ssm_d', 'm_glu_w', 'm_glu_b', 'm_gdn_a_log', 'm_gdn_dt_bias', 'm_gdn_norm_w', 'm_w_br_ssm', 'm_w_br_gdn', 'm_w_out', 'm_ln2_g', 'm_ln2_b', 'm_ffn2_w_gu', 'm_ffn2_w_down', 'm_ln3_g', 'm_ln3_b', 'v_ffn1_w_gu', 'v_ffn1_w_down', 'v_ln1_g', 'v_ln1_b', 'v_w_in', 'v_conv_w', 'v_ssm_a_re', 'v_ssm_a_im', 'v_ssm_log_dt', 'v_ssm_b_re', 'v_ssm_b_im', 'v_ssm_c_re', 'v_ssm_c_im', 'v_ssm_d', 'v_glu_w', 'v_glu_b', 'v_gdn_a_log', 'v_gdn_dt_bias', 'v_gdn_norm_w', 'v_w_br_ssm', 'v_w_br_gdn', 'v_w_out', 'v_ln2_g', 'v_ln2_b', 'v_ffn2_w_gu', 'v_ffn2_w_down', 'v_ln3_g', 'v_ln3_b']
TWIN_OUTPUTS = ['loss', 'grad_x', 'grad_ffn1_w_gu', 'grad_ffn1_w_down', 'grad_ln1_g', 'grad_ln1_b', 'grad_w_in', 'grad_conv_w', 'grad_ssm_a_re', 'grad_ssm_a_im', 'grad_ssm_log_dt', 'grad_ssm_b_re', 'grad_ssm_b_im', 'grad_ssm_c_re', 'grad_ssm_c_im', 'grad_ssm_d', 'grad_glu_w', 'grad_glu_b', 'grad_gdn_a_log', 'grad_gdn_dt_bias', 'grad_gdn_norm_w', 'grad_w_br_ssm', 'grad_w_br_gdn', 'grad_w_out', 'grad_ln2_g', 'grad_ln2_b', 'grad_ffn2_w_gu', 'grad_ffn2_w_down', 'grad_ln3_g', 'grad_ln3_b', 'delta_ffn1_w_gu', 'delta_ffn1_w_down', 'delta_ln1_g', 'delta_ln1_b', 'delta_w_in', 'delta_conv_w', 'delta_ssm_a_re', 'delta_ssm_a_im', 'delta_ssm_log_dt', 'delta_ssm_b_re', 'delta_ssm_b_im', 'delta_ssm_c_re', 'delta_ssm_c_im', 'delta_ssm_d', 'delta_glu_w', 'delta_glu_b', 'delta_gdn_a_log', 'delta_gdn_dt_bias', 'delta_gdn_norm_w', 'delta_w_br_ssm', 'delta_w_br_gdn', 'delta_w_out', 'delta_ln2_g', 'delta_ln2_b', 'delta_ffn2_w_gu', 'delta_ffn2_w_down', 'delta_ln3_g', 'delta_ln3_b', 'new_m_ffn1_w_gu', 'new_m_ffn1_w_down', 'new_m_ln1_g', 'new_m_ln1_b', 'new_m_w_in', 'new_m_conv_w', 'new_m_ssm_a_re', 'new_m_ssm_a_im', 'new_m_ssm_log_dt', 'new_m_ssm_b_re', 'new_m_ssm_b_im', 'new_m_ssm_c_re', 'new_m_ssm_c_im', 'new_m_ssm_d', 'new_m_glu_w', 'new_m_glu_b', 'new_m_gdn_a_log', 'new_m_gdn_dt_bias', 'new_m_gdn_norm_w', 'new_m_w_br_ssm', 'new_m_w_br_gdn', 'new_m_w_out', 'new_m_ln2_g', 'new_m_ln2_b', 'new_m_ffn2_w_gu', 'new_m_ffn2_w_down', 'new_m_ln3_g', 'new_m_ln3_b', 'new_v_ffn1_w_gu', 'new_v_ffn1_w_down', 'new_v_ln1_g', 'new_v_ln1_b', 'new_v_w_in', 'new_v_conv_w', 'new_v_ssm_a_re', 'new_v_ssm_a_im', 'new_v_ssm_log_dt', 'new_v_ssm_b_re', 'new_v_ssm_b_im', 'new_v_ssm_c_re', 'new_v_ssm_c_im', 'new_v_ssm_d', 'new_v_glu_w', 'new_v_glu_b', 'new_v_gdn_a_log', 'new_v_gdn_dt_bias', 'new_v_gdn_norm_w', 'new_v_w_br_ssm', 'new_v_w_br_gdn', 'new_v_w_out', 'new_v_ln2_g', 'new_v_ln2_b', 'new_v_ffn2_w_gu', 'new_v_ffn2_w_down', 'new_v_ln3_g', 'new_v_ln3_b']
TWIN_LEAF_KINDS = {'loss': 'loss', 'grad_x': 'grad_x', 'grad_ffn1_w_gu': 'grad_w', 'grad_ffn1_w_down': 'grad_w', 'grad_ln1_g': 'grad_w', 'grad_ln1_b': 'grad_w', 'grad_w_in': 'grad_w', 'grad_conv_w': 'grad_w', 'grad_ssm_a_re': 'grad_w', 'grad_ssm_a_im': 'grad_w', 'grad_ssm_log_dt': 'grad_w', 'grad_ssm_b_re': 'grad_w', 'grad_ssm_b_im': 'grad_w', 'grad_ssm_c_re': 'grad_w', 'grad_ssm_c_im': 'grad_w', 'grad_ssm_d': 'grad_w', 'grad_glu_w': 'grad_w', 'grad_glu_b': 'grad_w', 'grad_gdn_a_log': 'grad_w', 'grad_gdn_dt_bias': 'grad_w', 'grad_gdn_norm_w': 'grad_w', 'grad_w_br_ssm': 'grad_w', 'grad_w_br_gdn': 'grad_w', 'grad_w_out': 'grad_w', 'grad_ln2_g': 'grad_w', 'grad_ln2_b': 'grad_w', 'grad_ffn2_w_gu': 'grad_w', 'grad_ffn2_w_down': 'grad_w', 'grad_ln3_g': 'grad_w', 'grad_ln3_b': 'grad_w', 'delta_ffn1_w_gu': 'delta_w', 'delta_ffn1_w_down': 'delta_w', 'delta_ln1_g': 'delta_w', 'delta_ln1_b': 'delta_w', 'delta_w_in': 'delta_w', 'delta_conv_w': 'delta_w', 'delta_ssm_a_re': 'delta_w', 'delta_ssm_a_im': 'delta_w', 'delta_ssm_log_dt': 'delta_w', 'delta_ssm_b_re': 'delta_w', 'delta_ssm_b_im': 'delta_w', 'delta_ssm_c_re': 'delta_w', 'delta_ssm_c_im': 'delta_w', 'delta_ssm_d': 'delta_w', 'delta_glu_w': 'delta_w', 'delta_glu_b': 'delta_w', 'delta_gdn_a_log': 'delta_w', 'delta_gdn_dt_bias': 'delta_w', 'delta_gdn_norm_w': 'delta_w', 'delta_w_br_ssm': 'delta_w', 'delta_w_br_gdn': 'delta_w', 'delta_w_out': 'delta_w', 'delta_ln2_g': 'delta_w', 'delta_ln2_b': 'delta_w', 'delta_ffn2_w_gu': 'delta_w', 'delta_ffn2_w_down': 'delta_w', 'delta_ln3_g': 'delta_w', 'delta_ln3_b': 'delta_w', 'new_m_ffn1_w_gu': 'new_m', 'new_m_ffn1_w_down': 'new_m', 'new_m_ln1_g': 'new_m', 'new_m_ln1_b': 'new_m', 'new_m_w_in': 'new_m', 'new_m_conv_w': 'new_m', 'new_m_ssm_a_re': 'new_m', 'new_m_ssm_a_im': 'new_m', 'new_m_ssm_log_dt': 'new_m', 'new_m_ssm_b_re': 'new_m', 'new_m_ssm_b_im': 'new_m', 'new_m_ssm_c_re': 'new_m', 'new_m_ssm_c_im': 'new_m', 'new_m_ssm_d': 'new_m', 'new_m_glu_w': 'new_m', 'new_m_glu_b': 'new_m', 'new_m_gdn_a_log': 'new_m', 'new_m_gdn_dt_bias': 'new_m', 'new_m_gdn_norm_w': 'new_m', 'new_m_w_br_ssm': 'new_m', 'new_m_w_br_gdn': 'new_m', 'new_m_w_out': 'new_m', 'new_m_ln2_g': 'new_m', 'new_m_ln2_b': 'new_m', 'new_m_ffn2_w_gu': 'new_m', 'new_m_ffn2_w_down': 'new_m', 'new_m_ln3_g': 'new_m', 'new_m_ln3_b': 'new_m', 'new_v_ffn1_w_gu': 'new_v', 'new_v_ffn1_w_down': 'new_v', 'new_v_ln1_g': 'new_v', 'new_v_ln1_b': 'new_v', 'new_v_w_in': 'new_v', 'new_v_conv_w': 'new_v', 'new_v_ssm_a_re': 'new_v', 'new_v_ssm_a_im': 'new_v', 'new_v_ssm_log_dt': 'new_v', 'new_v_ssm_b_re': 'new_v', 'new_v_ssm_b_im': 'new_v', 'new_v_ssm_c_re': 'new_v', 'new_v_ssm_c_im': 'new_v', 'new_v_ssm_d': 'new_v', 'new_v_glu_w': 'new_v', 'new_v_glu_b': 'new_v', 'new_v_gdn_a_log': 'new_v', 'new_v_gdn_dt_bias': 'new_v', 'new_v_gdn_norm_w': 'new_v', 'new_v_w_br_ssm': 'new_v', 'new_v_w_br_gdn': 'new_v', 'new_v_w_out': 'new_v', 'new_v_ln2_g': 'new_v', 'new_v_ln2_b': 'new_v', 'new_v_ffn2_w_gu': 'new_v', 'new_v_ffn2_w_down': 'new_v', 'new_v_ln3_g': 'new_v', 'new_v_ln3_b': 'new_v'}


def _forward(args):
    return _fwd_reference(*[args[k] for k in FWD_PARAMS])


def _output_shape():
    out = _jax.eval_shape(lambda: _forward(_fwd_setup_inputs(0)))
    return out.shape, out.dtype

N_MICROBATCH = 1
ADAM_LR = 0.001
ADAM_B1 = 0.9
ADAM_B2 = 0.999
ADAM_EPS = 1e-08
ADAM_WD = 0.01
ADAM_STEP = 10
PER_EXAMPLE_BATCH_AXIS = {'x': 0, 'loss_target': 0}
SHARED_INPUTS = []
_WEIGHT_DTYPES = {'ffn1_w_gu': _jnp.float32, 'ffn1_w_down': _jnp.float32, 'ln1_g': _jnp.float32, 'ln1_b': _jnp.float32, 'w_in': _jnp.float32, 'conv_w': _jnp.float32, 'ssm_a_re': _jnp.float32, 'ssm_a_im': _jnp.float32, 'ssm_log_dt': _jnp.float32, 'ssm_b_re': _jnp.float32, 'ssm_b_im': _jnp.float32, 'ssm_c_re': _jnp.float32, 'ssm_c_im': _jnp.float32, 'ssm_d': _jnp.float32, 'glu_w': _jnp.float32, 'glu_b': _jnp.float32, 'gdn_a_log': _jnp.float32, 'gdn_dt_bias': _jnp.float32, 'gdn_norm_w': _jnp.float32, 'w_br_ssm': _jnp.float32, 'w_br_gdn': _jnp.float32, 'w_out': _jnp.float32, 'ln2_g': _jnp.float32, 'ln2_b': _jnp.float32, 'ffn2_w_gu': _jnp.float32, 'ffn2_w_down': _jnp.float32, 'ln3_g': _jnp.float32, 'ln3_b': _jnp.float32}
MOMENT_SCALE = {'ffn1_w_gu': 2.932074e-03, 'ffn1_w_down': 1.137600e-02, 'ln1_g': 2.824031e-01, 'ln1_b': 1.437236e-01, 'w_in': 4.445894e-03, 'conv_w': 5.567317e-03, 'ssm_a_re': 2.265713e-04, 'ssm_a_im': 2.279933e-04, 'ssm_log_dt': 1.890262e-01, 'ssm_b_re': 1.477342e-04, 'ssm_b_im': 1.468912e-04, 'ssm_c_re': 2.971540e-04, 'ssm_c_im': 2.962034e-04, 'ssm_d': 5.839035e-03, 'glu_w': 1.398035e-03, 'glu_b': 2.540311e-03, 'gdn_a_log': 3.384079e-02, 'gdn_dt_bias': 3.283104e-02, 'gdn_norm_w': 2.011660e-02, 'w_br_ssm': 3.845554e-03, 'w_br_gdn': 5.281968e-03, 'w_out': 1.551879e-02, 'ln2_g': 2.854105e-01, 'ln2_b': 1.440014e-01, 'ffn2_w_gu': 2.903420e-03, 'ffn2_w_down': 1.126983e-02, 'ln3_g': 4.039306e+00, 'ln3_b': 3.412032e-01}


def _to_microbatches(a, axis):
    t = _jnp.moveaxis(a, axis, 0)
    t = t.reshape((N_MICROBATCH, t.shape[0] // N_MICROBATCH) + t.shape[1:])
    return _jnp.moveaxis(t, 1, axis + 1)


def setup_inputs(seed: int = 0) -> dict:
    inp = _fwd_setup_inputs(seed)
    key = _jax.random.fold_in(_jax.random.key(seed), 7919)
    shape, _ = _output_shape()
    out = dict(inp)
    out["loss_target"] = _jax.random.normal(_jax.random.fold_in(key, 0), shape, _jnp.float32)
    for i, name in enumerate(TWIN_WEIGHTS):
        w = inp[name].astype(_jnp.float32)
        if MOMENT_SCALE is None:
            s = _jnp.sqrt(_jnp.mean(_jnp.square(w)) + 1e-30)
        else:
            s = MOMENT_SCALE[name]
        km, kv = _jax.random.split(_jax.random.fold_in(key, i + 1))
        out[name] = w
        out["m_" + name] = s * _jax.random.normal(km, w.shape, _jnp.float32)
        out["v_" + name] = (s * s) * _jax.random.uniform(kv, w.shape, _jnp.float32, 0.5, 1.5)
    if N_MICROBATCH > 1:
        for name, axis in PER_EXAMPLE_BATCH_AXIS.items():
            out[name] = _to_microbatches(out[name], axis)
    return {'x': out['x'], 'ffn1_w_gu': out['ffn1_w_gu'], 'ffn1_w_down': out['ffn1_w_down'], 'ln1_g': out['ln1_g'], 'ln1_b': out['ln1_b'], 'w_in': out['w_in'], 'conv_w': out['conv_w'], 'ssm_a_re': out['ssm_a_re'], 'ssm_a_im': out['ssm_a_im'], 'ssm_log_dt': out['ssm_log_dt'], 'ssm_b_re': out['ssm_b_re'], 'ssm_b_im': out['ssm_b_im'], 'ssm_c_re': out['ssm_c_re'], 'ssm_c_im': out['ssm_c_im'], 'ssm_d': out['ssm_d'], 'glu_w': out['glu_w'], 'glu_b': out['glu_b'], 'gdn_a_log': out['gdn_a_log'], 'gdn_dt_bias': out['gdn_dt_bias'], 'gdn_norm_w': out['gdn_norm_w'], 'w_br_ssm': out['w_br_ssm'], 'w_br_gdn': out['w_br_gdn'], 'w_out': out['w_out'], 'ln2_g': out['ln2_g'], 'ln2_b': out['ln2_b'], 'ffn2_w_gu': out['ffn2_w_gu'], 'ffn2_w_down': out['ffn2_w_down'], 'ln3_g': out['ln3_g'], 'ln3_b': out['ln3_b'], 'loss_target': out['loss_target'], 'm_ffn1_w_gu': out['m_ffn1_w_gu'], 'm_ffn1_w_down': out['m_ffn1_w_down'], 'm_ln1_g': out['m_ln1_g'], 'm_ln1_b': out['m_ln1_b'], 'm_w_in': out['m_w_in'], 'm_conv_w': out['m_conv_w'], 'm_ssm_a_re': out['m_ssm_a_re'], 'm_ssm_a_im': out['m_ssm_a_im'], 'm_ssm_log_dt': out['m_ssm_log_dt'], 'm_ssm_b_re': out['m_ssm_b_re'], 'm_ssm_b_im': out['m_ssm_b_im'], 'm_ssm_c_re': out['m_ssm_c_re'], 'm_ssm_c_im': out['m_ssm_c_im'], 'm_ssm_d': out['m_ssm_d'], 'm_glu_w': out['m_glu_w'], 'm_glu_b': out['m_glu_b'], 'm_gdn_a_log': out['m_gdn_a_log'], 'm_gdn_dt_bias': out['m_gdn_dt_bias'], 'm_gdn_norm_w': out['m_gdn_norm_w'], 'm_w_br_ssm': out['m_w_br_ssm'], 'm_w_br_gdn': out['m_w_br_gdn'], 'm_w_out': out['m_w_out'], 'm_ln2_g': out['m_ln2_g'], 'm_ln2_b': out['m_ln2_b'], 'm_ffn2_w_gu': out['m_ffn2_w_gu'], 'm_ffn2_w_down': out['m_ffn2_w_down'], 'm_ln3_g': out['m_ln3_g'], 'm_ln3_b': out['m_ln3_b'], 'v_ffn1_w_gu': out['v_ffn1_w_gu'], 'v_ffn1_w_down': out['v_ffn1_w_down'], 'v_ln1_g': out['v_ln1_g'], 'v_ln1_b': out['v_ln1_b'], 'v_w_in': out['v_w_in'], 'v_conv_w': out['v_conv_w'], 'v_ssm_a_re': out['v_ssm_a_re'], 'v_ssm_a_im': out['v_ssm_a_im'], 'v_ssm_log_dt': out['v_ssm_log_dt'], 'v_ssm_b_re': out['v_ssm_b_re'], 'v_ssm_b_im': out['v_ssm_b_im'], 'v_ssm_c_re': out['v_ssm_c_re'], 'v_ssm_c_im': out['v_ssm_c_im'], 'v_ssm_d': out['v_ssm_d'], 'v_glu_w': out['v_glu_w'], 'v_glu_b': out['v_glu_b'], 'v_gdn_a_log': out['v_gdn_a_log'], 'v_gdn_dt_bias': out['v_gdn_dt_bias'], 'v_gdn_norm_w': out['v_gdn_norm_w'], 'v_w_br_ssm': out['v_w_br_ssm'], 'v_w_br_gdn': out['v_w_br_gdn'], 'v_w_out': out['v_w_out'], 'v_ln2_g': out['v_ln2_g'], 'v_ln2_b': out['v_ln2_b'], 'v_ffn2_w_gu': out['v_ffn2_w_gu'], 'v_ffn2_w_down': out['v_ffn2_w_down'], 'v_ln3_g': out['v_ln3_g'], 'v_ln3_b': out['v_ln3_b']}


def _loss(weights, diff, rest, loss_target):
    with _jax.named_scope("forward"):
        args = {**rest, TWIN_DIFF_INPUT: diff, **{k: w.astype(_WEIGHT_DTYPES[k]) for k, w in weights.items()}}
        y = _forward(args)
    with _jax.named_scope("loss_head"):
        err = _jnp.square(y.astype(_jnp.float32) - loss_target)
        return 0.5 * _jnp.sum(_jnp.mean(err, axis=-1)) if err.ndim else 0.5 * err


def _adamw(w, g, m, v):
    m = ADAM_B1 * m + (1.0 - ADAM_B1) * g
    v = ADAM_B2 * v + (1.0 - ADAM_B2) * _jnp.square(g)
    m_hat = m / (1.0 - ADAM_B1 ** ADAM_STEP)
    v_hat = v / (1.0 - ADAM_B2 ** ADAM_STEP)
    delta = -ADAM_LR * (m_hat / (_jnp.sqrt(v_hat) + ADAM_EPS) + ADAM_WD * w)
    return delta, m, v


def reference(x, ffn1_w_gu, ffn1_w_down, ln1_g, ln1_b, w_in, conv_w, ssm_a_re, ssm_a_im, ssm_log_dt, ssm_b_re, ssm_b_im, ssm_c_re, ssm_c_im, ssm_d, glu_w, glu_b, gdn_a_log, gdn_dt_bias, gdn_norm_w, w_br_ssm, w_br_gdn, w_out, ln2_g, ln2_b, ffn2_w_gu, ffn2_w_down, ln3_g, ln3_b, loss_target, m_ffn1_w_gu, m_ffn1_w_down, m_ln1_g, m_ln1_b, m_w_in, m_conv_w, m_ssm_a_re, m_ssm_a_im, m_ssm_log_dt, m_ssm_b_re, m_ssm_b_im, m_ssm_c_re, m_ssm_c_im, m_ssm_d, m_glu_w, m_glu_b, m_gdn_a_log, m_gdn_dt_bias, m_gdn_norm_w, m_w_br_ssm, m_w_br_gdn, m_w_out, m_ln2_g, m_ln2_b, m_ffn2_w_gu, m_ffn2_w_down, m_ln3_g, m_ln3_b, v_ffn1_w_gu, v_ffn1_w_down, v_ln1_g, v_ln1_b, v_w_in, v_conv_w, v_ssm_a_re, v_ssm_a_im, v_ssm_log_dt, v_ssm_b_re, v_ssm_b_im, v_ssm_c_re, v_ssm_c_im, v_ssm_d, v_glu_w, v_glu_b, v_gdn_a_log, v_gdn_dt_bias, v_gdn_norm_w, v_w_br_ssm, v_w_br_gdn, v_w_out, v_ln2_g, v_ln2_b, v_ffn2_w_gu, v_ffn2_w_down, v_ln3_g, v_ln3_b):
    given = dict(x=x, ffn1_w_gu=ffn1_w_gu, ffn1_w_down=ffn1_w_down, ln1_g=ln1_g, ln1_b=ln1_b, w_in=w_in, conv_w=conv_w, ssm_a_re=ssm_a_re, ssm_a_im=ssm_a_im, ssm_log_dt=ssm_log_dt, ssm_b_re=ssm_b_re, ssm_b_im=ssm_b_im, ssm_c_re=ssm_c_re, ssm_c_im=ssm_c_im, ssm_d=ssm_d, glu_w=glu_w, glu_b=glu_b, gdn_a_log=gdn_a_log, gdn_dt_bias=gdn_dt_bias, gdn_norm_w=gdn_norm_w, w_br_ssm=w_br_ssm, w_br_gdn=w_br_gdn, w_out=w_out, ln2_g=ln2_g, ln2_b=ln2_b, ffn2_w_gu=ffn2_w_gu, ffn2_w_down=ffn2_w_down, ln3_g=ln3_g, ln3_b=ln3_b, loss_target=loss_target, m_ffn1_w_gu=m_ffn1_w_gu, m_ffn1_w_down=m_ffn1_w_down, m_ln1_g=m_ln1_g, m_ln1_b=m_ln1_b, m_w_in=m_w_in, m_conv_w=m_conv_w, m_ssm_a_re=m_ssm_a_re, m_ssm_a_im=m_ssm_a_im, m_ssm_log_dt=m_ssm_log_dt, m_ssm_b_re=m_ssm_b_re, m_ssm_b_im=m_ssm_b_im, m_ssm_c_re=m_ssm_c_re, m_ssm_c_im=m_ssm_c_im, m_ssm_d=m_ssm_d, m_glu_w=m_glu_w, m_glu_b=m_glu_b, m_gdn_a_log=m_gdn_a_log, m_gdn_dt_bias=m_gdn_dt_bias, m_gdn_norm_w=m_gdn_norm_w, m_w_br_ssm=m_w_br_ssm, m_w_br_gdn=m_w_br_gdn, m_w_out=m_w_out, m_ln2_g=m_ln2_g, m_ln2_b=m_ln2_b, m_ffn2_w_gu=m_ffn2_w_gu, m_ffn2_w_down=m_ffn2_w_down, m_ln3_g=m_ln3_g, m_ln3_b=m_ln3_b, v_ffn1_w_gu=v_ffn1_w_gu, v_ffn1_w_down=v_ffn1_w_down, v_ln1_g=v_ln1_g, v_ln1_b=v_ln1_b, v_w_in=v_w_in, v_conv_w=v_conv_w, v_ssm_a_re=v_ssm_a_re, v_ssm_a_im=v_ssm_a_im, v_ssm_log_dt=v_ssm_log_dt, v_ssm_b_re=v_ssm_b_re, v_ssm_b_im=v_ssm_b_im, v_ssm_c_re=v_ssm_c_re, v_ssm_c_im=v_ssm_c_im, v_ssm_d=v_ssm_d, v_glu_w=v_glu_w, v_glu_b=v_glu_b, v_gdn_a_log=v_gdn_a_log, v_gdn_dt_bias=v_gdn_dt_bias, v_gdn_norm_w=v_gdn_norm_w, v_w_br_ssm=v_w_br_ssm, v_w_br_gdn=v_w_br_gdn, v_w_out=v_w_out, v_ln2_g=v_ln2_g, v_ln2_b=v_ln2_b, v_ffn2_w_gu=v_ffn2_w_gu, v_ffn2_w_down=v_ffn2_w_down, v_ln3_g=v_ln3_g, v_ln3_b=v_ln3_b)
    weights = {n: given[n] for n in TWIN_WEIGHTS}
    shared = {n: given[n] for n in SHARED_INPUTS}
    per_example = {n: given[n] for n in ['x']}
    grad_fn = _jax.value_and_grad(_loss, argnums=(0, 1))

    def one_microbatch(ex, loss_target):
        ex = dict(ex)
        diff = ex.pop(TWIN_DIFF_INPUT)
        return grad_fn(weights, diff, {**shared, **ex}, loss_target)

    if N_MICROBATCH == 1:
        loss, (grad_w, grad_x) = one_microbatch(per_example, given["loss_target"])
    else:
        def body(carry, xs):
            loss_sum, grad_sum = carry
            l_k, (gw_k, gx_k) = one_microbatch(xs[0], xs[1])
            with _jax.named_scope("update"):
                return (loss_sum + l_k, _jax.tree.map(_jnp.add, grad_sum, gw_k)), gx_k

        init = (_jnp.zeros((), _jnp.float32), _jax.tree.map(_jnp.zeros_like, weights))
        (loss, grad_w), grad_x = _jax.lax.scan(body, init, (per_example, given["loss_target"]))
    with _jax.named_scope("update"):
        delta_w, new_m, new_v = {}, {}, {}
        for n in TWIN_WEIGHTS:
            delta_w[n], new_m[n], new_v[n] = _adamw(weights[n], grad_w[n], given["m_" + n], given["v_" + n])
    return (loss, grad_x, *[grad_w[n] for n in TWIN_WEIGHTS], *[delta_w[n] for n in TWIN_WEIGHTS],
            *[new_m[n] for n in TWIN_WEIGHTS], *[new_v[n] for n in TWIN_WEIGHTS])
```

```python
import math

import jax
import jax.numpy as jnp
from jax import lax
from jax.experimental import pallas as pl
from jax.experimental.pallas import tpu as pltpu

F32 = jnp.float32
BF16 = jnp.bfloat16
HI = lax.Precision.HIGHEST
MESH = pl.DeviceIdType.MESH

N_CHIPS = 4
SSM_GROUP = 16
SSM_STATE = 64
GROUPS_PER_TILE = 8
HEAD_DIM = 128
CHUNK = 64
CONV_K = 4
LN_EPS = 1e-5
RMS_EPS = 1e-6
L2_EPS = 1e-6
SMALL_W = 512
ADAM_LR, ADAM_B1, ADAM_B2, ADAM_EPS, ADAM_WD, ADAM_STEP = 0.001, 0.9, 0.999, 1e-08, 0.01, 10
VMEM_LIMIT = 56 * 1024 * 1024

WEIGHT_NAMES = ['ffn1_w_gu', 'ffn1_w_down', 'ln1_g', 'ln1_b', 'w_in', 'conv_w', 'ssm_a_re', 'ssm_a_im', 'ssm_log_dt',
                'ssm_b_re', 'ssm_b_im', 'ssm_c_re', 'ssm_c_im', 'ssm_d', 'glu_w', 'glu_b', 'gdn_a_log', 'gdn_dt_bias',
                'gdn_norm_w', 'w_br_ssm', 'w_br_gdn', 'w_out', 'ln2_g', 'ln2_b', 'ffn2_w_gu', 'ffn2_w_down', 'ln3_g',
                'ln3_b']
BIG = ['ffn1_w_gu', 'ffn1_w_down', 'w_in', 'glu_w', 'w_br_ssm', 'w_br_gdn', 'w_out', 'ffn2_w_gu', 'ffn2_w_down']
SMALL = [n for n in WEIGHT_NAMES if n not in BIG]


def _tile(dim, prefs):
    for p in prefs:
        if dim % p == 0:
            return p
    return dim


def _cparams(sem):
    return pltpu.CompilerParams(dimension_semantics=sem, vmem_limit_bytes=VMEM_LIMIT)


def _ln(r, g, b):
    mu = jnp.mean(r, axis=-1, keepdims=True)
    xc = r - mu
    var = jnp.mean(xc * xc, axis=-1, keepdims=True)
    return xc * lax.rsqrt(var + LN_EPS) * g + b


def _lshape(x, nb):
    return (x.shape[0], x.shape[1]) if nb == 1 else (x.shape[1], x.shape[2] * nb)


def _cb_spec(x, nb, tr, tc, rc):
    if nb == 1:
        return pl.BlockSpec((tr, tc), lambda *g: rc(*g))
    cps = x.shape[2] // tc

    def imap(*g):
        r, c = rc(*g)
        return (c // cps, r, c % cps)
    return pl.BlockSpec((None, tr, tc), imap)


def mm(a, b, *, name, ta=False, tb=False, a_nb=1, b_nb=1, out_nb=1, out_dtype=F32, add=None, add_scale=1.0,
       out_scale=1.0):
    ar, ac = _lshape(a, a_nb)
    br, bc = _lshape(b, b_nb)
    m, k = (ac, ar) if ta else (ar, ac)
    k2, n = (bc, br) if tb else (br, bc)
    assert k == k2, (name, a.shape, b.shape)

    def lim(dim, *nbs):
        q = dim
        for nb in nbs:
            q = math.gcd(q, dim // nb)
        return q
    tm = _tile(lim(m, a_nb if ta else 1), (512, 256, 128))
    tn = _tile(lim(n, out_nb, 1 if tb else b_nb), (512, 256, 128))
    tk = _tile(lim(k, 1 if ta else a_nb, b_nb if tb else 1), (512, 256, 128))
    nk = k // tk
    dn = (((0 if ta else 1,), (1 if tb else 0,)), ((), ()))

    def body(*refs):
        if add is None:
            a_ref, b_ref, o_ref, acc = refs
        else:
            a_ref, b_ref, add_ref, o_ref, acc = refs
        kk = pl.program_id(2)

        @pl.when(kk == 0)
        def _():
            acc[...] = jnp.zeros_like(acc)

        acc[...] += lax.dot_general(a_ref[...].astype(BF16), b_ref[...].astype(BF16), dn, preferred_element_type=F32)

        @pl.when(kk == nk - 1)
        def _():
            r = acc[...]
            if out_scale != 1.0:
                r = r * out_scale
            if add is not None:
                r = r + add_scale * add_ref[...]
            o_ref[...] = r.astype(out_dtype)

    if ta:
        a_spec = _cb_spec(a, a_nb, tk, tm, lambda i, j, kk: (kk, i))
    else:
        a_spec = _cb_spec(a, a_nb, tm, tk, lambda i, j, kk: (i, kk))
    if tb:
        b_spec = _cb_spec(b, b_nb, tn, tk, lambda i, j, kk: (j, kk))
    else:
        b_spec = _cb_spec(b, b_nb, tk, tn, lambda i, j, kk: (kk, j))
    if out_nb == 1:
        out_shape = jax.ShapeDtypeStruct((m, n), out_dtype)
        out_spec = pl.BlockSpec((tm, tn), lambda i, j, kk: (i, j))
    else:
        out_shape = jax.ShapeDtypeStruct((out_nb, m, n // out_nb), out_dtype)
        out_spec = _cb_spec(out_shape, out_nb, tm, tn, lambda i, j, kk: (i, j))
    in_specs = [a_spec, b_spec]
    args = [a, b]
    if add is not None:
        in_specs.append(pl.BlockSpec((tm, tn), lambda i, j, kk: (i, j)))
        args.append(add)
    return pl.pallas_call(
        body, name=name, out_shape=out_shape, grid=(m // tm, n // tn, nk), in_specs=in_specs, out_specs=out_spec,
        scratch_shapes=[pltpu.VMEM((tm, tn), F32)],
        compiler_params=_cparams(("parallel", "parallel", "arbitrary")))(*args)


def down_res_ln(src, w, x, g, b, *, swiglu, alpha, scale, name):
    n_tok, d = x.shape
    kdim = w.shape[0]
    tm = _tile(n_tok, (256, 128))
    tk = _tile(kdim, (512, 256, 128))
    nk = kdim // tk

    def body(s_ref, w_ref, x_ref, g_ref, b_ref, r_ref, y_ref, acc):
        kk = pl.program_id(1)

        @pl.when(kk == 0)
        def _():
            acc[...] = jnp.zeros_like(acc)

        if swiglu:
            gate = s_ref[0]
            a = gate * jax.nn.sigmoid(gate) * s_ref[1]
        else:
            a = s_ref[...]
        acc[...] += jnp.dot(a.astype(BF16), w_ref[...], preferred_element_type=F32)

        @pl.when(kk == nk - 1)
        def _():
            r = alpha * x_ref[...] + scale * acc[...]
            r_ref[...] = r
            y_ref[...] = _ln(r, g_ref[...], b_ref[...])

    if swiglu:
        s_spec = pl.BlockSpec((2, tm, tk), lambda i, kk: (0, i, kk))
    else:
        s_spec = pl.BlockSpec((tm, tk), lambda i, kk: (i, kk))
    row = pl.BlockSpec((tm, d), lambda i, kk: (i, 0))
    vec = pl.BlockSpec((1, d), lambda i, kk: (0, 0))
    return pl.pallas_call(
        body, name=name, out_shape=[jax.ShapeDtypeStruct((n_tok, d), F32)] * 2, grid=(n_tok // tm, nk),
        in_specs=[s_spec, pl.BlockSpec((tk, d), lambda i, kk: (kk, 0)), row, vec, vec], out_specs=[row, row],
        scratch_shapes=[pltpu.VMEM((tm, d), F32)], compiler_params=_cparams(("parallel", "arbitrary")))(src, w, x, g, b)


def ln_bwd(r, g, dy, *, name):
    n_tok, d = r.shape
    tm = _tile(n_tok, (256, 128))

    def body(r_ref, g_ref, dy_ref, dr_ref, dg_ref, db_ref):
        i = pl.program_id(0)

        @pl.when(i == 0)
        def _():
            dg_ref[...] = jnp.zeros_like(dg_ref)
            db_ref[...] = jnp.zeros_like(db_ref)

        rv = r_ref[...]
        dyv = dy_ref[...]
        mu = jnp.mean(rv, axis=-1, keepdims=True)
        xc = rv - mu
        rstd = lax.rsqrt(jnp.mean(xc * xc, axis=-1, keepdims=True) + LN_EPS)
        xh = xc * rstd
        dxh = dyv * g_ref[...]
        dr_ref[...] = rstd * (dxh - jnp.mean(dxh, axis=-1, keepdims=True)
                              - xh * jnp.mean(dxh * xh, axis=-1, keepdims=True))
        dg_ref[...] += jnp.sum(dyv * xh, axis=0, keepdims=True)
        db_ref[...] += jnp.sum(dyv, axis=0, keepdims=True)

    row = pl.BlockSpec((tm, d), lambda i: (i, 0))
    vec = pl.BlockSpec((1, d), lambda i: (0, 0))
    return pl.pallas_call(
        body, name=name, out_shape=[jax.ShapeDtypeStruct((n_tok, d), F32), jax.ShapeDtypeStruct((1, d), F32),
                                    jax.ShapeDtypeStruct((1, d), F32)],
        grid=(n_tok // tm,), in_specs=[row, vec, row], out_specs=[row, vec, vec],
        compiler_params=_cparams(("arbitrary",)))(r, g, dy)


def loss_head(y, target, *, name):
    n_tok, d = y.shape
    tm = _tile(n_tok, (256, 128))

    def body(y_ref, t_ref, dy_ref, l_ref):
        i = pl.program_id(0)

        @pl.when(i == 0)
        def _():
            l_ref[...] = jnp.zeros_like(l_ref)

        e = y_ref[...] - t_ref[...]
        dy_ref[...] = e * (1.0 / d)
        s = jnp.sum(jnp.mean(e * e, axis=-1, keepdims=True), axis=0, keepdims=True)
        l_ref[...] += 0.5 * s

    row = pl.BlockSpec((tm, d), lambda i: (i, 0))
    return pl.pallas_call(
        body, name=name, out_shape=[jax.ShapeDtypeStruct((n_tok, d), F32), jax.ShapeDtypeStruct((8, 128), F32)],
        grid=(n_tok // tm,), in_specs=[row, row], out_specs=[row, pl.BlockSpec((8, 128), lambda i: (0, 0))],
        compiler_params=_cparams(("arbitrary",)))(y, target)


def ffn_bwd_mid(dr, wd, h, *, scale, name):
    n_tok, d = dr.shape
    f = wd.shape[0]
    tm = _tile(n_tok, (256, 128))
    tf = _tile(f, (512, 256, 128))

    def body(dr_ref, w_ref, h_ref, dh_ref, a_ref):
        dy = (scale * dr_ref[...]).astype(BF16)
        da = lax.dot_general(dy, w_ref[...], (((1,), (1,)), ((), ())), preferred_element_type=F32)
        gate = h_ref[0]
        up = h_ref[1]
        sg = jax.nn.sigmoid(gate)
        s = gate * sg
        a_ref[...] = (s * up).astype(BF16)
        dh_ref[0] = (da * up * (sg * (1.0 + gate * (1.0 - sg)))).astype(BF16)
        dh_ref[1] = (da * s).astype(BF16)

    return pl.pallas_call(
        body, name=name, out_shape=[jax.ShapeDtypeStruct((2, n_tok, f), BF16), jax.ShapeDtypeStruct((n_tok, f), BF16)],
        grid=(n_tok // tm, f // tf),
        in_specs=[pl.BlockSpec((tm, d), lambda i, j: (i, 0)), pl.BlockSpec((tf, d), lambda i, j: (j, 0)),
                  pl.BlockSpec((2, tm, tf), lambda i, j: (0, i, j))],
        out_specs=[pl.BlockSpec((2, tm, tf), lambda i, j: (0, i, j)), pl.BlockSpec((tm, tf), lambda i, j: (i, j))],
        compiler_params=_cparams(("parallel", "parallel")))(dr, wd, h)


def _cmul(ar, ai, br, bi):
    return ar * br - ai * bi, ar * bi + ai * br


def _scan_blocks(sr_ref, si_ref, lr, li, *, reverse):
    n_rows, width = sr_ref.shape
    n_blk = n_rows // 8
    row = lax.broadcasted_iota(jnp.int32, (8, width), 0)
    pr = jnp.broadcast_to(lr, (8, width))
    pi = jnp.broadcast_to(-li if reverse else li, (8, width))

    def shifted(v, dist, fill=0.0):
        if reverse:
            return jnp.where(row < 8 - dist, pltpu.roll(v, 8 - dist, 0), fill)
        return jnp.where(row >= dist, pltpu.roll(v, dist, 0), fill)

    p1 = (pr, pi)
    p2 = _cmul(*p1, *p1)
    p4 = _cmul(*p2, *p2)
    wr, wi = pr, pi
    for dist in (1, 2, 4):
        wr, wi = _cmul(wr, wi, shifted(wr, dist, 1.0), shifted(wi, dist, 0.0))
    edge = 0 if reverse else 7

    def step(i, carry):
        cr, ci = carry
        blk = (n_blk - 1 - i) if reverse else i
        r0 = pl.multiple_of(blk * 8, 8)
        xr = sr_ref[pl.ds(r0, 8), :]
        xi = si_ref[pl.ds(r0, 8), :]
        for dist, (qr, qi) in ((1, p1), (2, p2), (4, p4)):
            tr, ti = _cmul(qr, qi, shifted(xr, dist), shifted(xi, dist))
            xr, xi = xr + tr, xi + ti
        tr, ti = _cmul(wr, wi, cr, ci)
        xr, xi = xr + tr, xi + ti
        sr_ref[pl.ds(r0, 8), :] = xr
        si_ref[pl.ds(r0, 8), :] = xi
        br = jnp.where(row == edge, xr, 0.0)
        bi = jnp.where(row == edge, xi, 0.0)
        for dist in (1, 2, 4):
            br = br + pltpu.roll(br, dist, 0)
            bi = bi + pltpu.roll(bi, dist, 0)
        return br, bi

    zero = jnp.zeros((8, width), F32)
    lax.fori_loop(0, n_blk, step, (zero, zero), unroll=2)


def _s5_specs(n_tok, u_blk0):
    gw = GROUPS_PER_TILE * SSM_GROUP
    sw = GROUPS_PER_TILE * SSM_STATE
    u_spec = pl.BlockSpec((n_tok, gw), lambda t: (0, u_blk0 + t))
    col = pl.BlockSpec((n_tok, gw), lambda t: (0, t))
    bmat = pl.BlockSpec((None, gw, sw), lambda t: (t, 0, 0))
    cmat = pl.BlockSpec((None, sw, gw), lambda t: (t, 0, 0))
    lvec = pl.BlockSpec((1, sw), lambda t: (0, t))
    dvec = pl.BlockSpec((1, gw), lambda t: (0, t))
    return gw, sw, u_spec, col, bmat, cmat, lvec, dvec


def s5_fwd(proj, u_col0, bblk_r, bblk_i, cblk_r, cblk_i, lbar_r, lbar_i, dskip, *, name):
    n_tok = proj.shape[0]
    n_tiles = bblk_r.shape[0]
    gw, sw, u_spec, col, bmat, cmat, lvec, dvec = _s5_specs(n_tok, u_col0 // (GROUPS_PER_TILE * SSM_GROUP))

    def body(u_ref, br_ref, bi_ref, cr_ref, ci_ref, lr_ref, li_ref, d_ref, ypre_ref, y2_ref, sr, si):
        u = u_ref[...]
        ub = u.astype(BF16)
        sr[...] = jnp.dot(ub, br_ref[...].astype(BF16), preferred_element_type=F32)
        si[...] = jnp.dot(ub, bi_ref[...].astype(BF16), preferred_element_type=F32)
        _scan_blocks(sr, si, lr_ref[...], li_ref[...], reverse=False)
        y = (jnp.dot(sr[...].astype(BF16), cr_ref[...].astype(BF16), preferred_element_type=F32)
             - jnp.dot(si[...].astype(BF16), ci_ref[...].astype(BF16), preferred_element_type=F32)
             + d_ref[...] * u)
        ypre_ref[...] = y
        y2_ref[...] = jax.nn.gelu(y)

    width = n_tiles * gw
    return pl.pallas_call(
        body, name=name, out_shape=[jax.ShapeDtypeStruct((n_tok, width), F32)] * 2, grid=(n_tiles,),
        in_specs=[u_spec, bmat, bmat, cmat, cmat, lvec, lvec, dvec], out_specs=[col, col],
        scratch_shapes=[pltpu.VMEM((n_tok, sw), F32)] * 2,
        compiler_params=_cparams(("parallel",)))(proj, bblk_r, bblk_i, cblk_r, cblk_i, lbar_r, lbar_i, dskip)


def s5_bwd(proj, u_col0, ypre, dy2, bblk_r, bblk_i, cblk_r, cblk_i, lbar_r, lbar_i, dskip, *, name):
    n_tok = proj.shape[0]
    n_tiles = bblk_r.shape[0]
    gw, sw, u_spec, col, bmat, cmat, lvec, dvec = _s5_specs(n_tok, u_col0 // (GROUPS_PER_TILE * SSM_GROUP))
    rb = _tile(n_tok, (512, 256, 128))
    tn_dims = (((0,), (0,)), ((), ()))
    nt_dims = (((1,), (1,)), ((), ()))

    def body(u_ref, ypre_ref, dy2_ref, br_ref, bi_ref, cr_ref, ci_ref, lr_ref, li_ref, d_ref,
             du_ref, dbr_ref, dbi_ref, dcr_ref, dci_ref, dlr_ref, dli_ref, dd_ref, sr, si, gr, gi):
        u = u_ref[...]
        ub = u.astype(BF16)
        bmr = br_ref[...].astype(BF16)
        bmi = bi_ref[...].astype(BF16)
        cmr = cr_ref[...].astype(BF16)
        cmi = ci_ref[...].astype(BF16)
        lr = lr_ref[...]
        li = li_ref[...]
        _, gelu_vjp = jax.vjp(jax.nn.gelu, ypre_ref[...])
        dyp = gelu_vjp(dy2_ref[...])[0]
        dyb = dyp.astype(BF16)
        sr[...] = jnp.dot(ub, bmr, preferred_element_type=F32)
        si[...] = jnp.dot(ub, bmi, preferred_element_type=F32)
        _scan_blocks(sr, si, lr, li, reverse=False)
        gr[...] = lax.dot_general(dyb, cmr, nt_dims, preferred_element_type=F32)
        gi[...] = -lax.dot_general(dyb, cmi, nt_dims, preferred_element_type=F32)
        _scan_blocks(gr, gi, lr, li, reverse=True)
        srb = sr[...].astype(BF16)
        sib = si[...].astype(BF16)
        dcr_ref[...] = lax.dot_general(srb, dyb, tn_dims, preferred_element_type=F32)
        dci_ref[...] = -lax.dot_general(sib, dyb, tn_dims, preferred_element_type=F32)
        grb = gr[...].astype(BF16)
        gib = gi[...].astype(BF16)
        dbr_ref[...] = lax.dot_general(ub, grb, tn_dims, preferred_element_type=F32)
        dbi_ref[...] = lax.dot_general(ub, gib, tn_dims, preferred_element_type=F32)
        du_ref[...] = (lax.dot_general(grb, bmr, nt_dims, preferred_element_type=F32)
                       + lax.dot_general(gib, bmi, nt_dims, preferred_element_type=F32)
                       + d_ref[...] * dyp).astype(du_ref.dtype)
        dd_ref[...] = jnp.sum(dyp * u, axis=0, keepdims=True)
        inv = 1.0 / (lr * lr + li * li)
        qr = lr * inv
        qi = -li * inv
        acc_r = jnp.zeros((1, sw), F32)
        acc_i = jnp.zeros((1, sw), F32)
        for blk in range(n_tok // rb):
            rows = pl.ds(blk * rb, rb)
            ubb = u_ref[rows, :].astype(BF16)
            er = sr[rows, :] - jnp.dot(ubb, bmr, preferred_element_type=F32)
            ei = si[rows, :] - jnp.dot(ubb, bmi, preferred_element_type=F32)
            pr, pi = _cmul(er, ei, qr, qi)
            ar = gr[rows, :]
            ai = gi[rows, :]
            acc_r = acc_r + jnp.sum(ar * pr + ai * pi, axis=0, keepdims=True)
            acc_i = acc_i + jnp.sum(ai * pr - ar * pi, axis=0, keepdims=True)
        dlr_ref[...] = acc_r
        dli_ref[...] = acc_i

    width = n_tiles * gw
    out_shape = [jax.ShapeDtypeStruct((n_tok, width), BF16),
                 jax.ShapeDtypeStruct(bblk_r.shape, F32), jax.ShapeDtypeStruct(bblk_r.shape, F32),
                 jax.ShapeDtypeStruct(cblk_r.shape, F32), jax.ShapeDtypeStruct(cblk_r.shape, F32),
                 jax.ShapeDtypeStruct(lbar_r.shape, F32), jax.ShapeDtypeStruct(lbar_r.shape, F32),
                 jax.ShapeDtypeStruct(dskip.shape, F32)]
    return pl.pallas_call(
        body, name=name, out_shape=out_shape, grid=(n_tiles,),
        in_specs=[u_spec, col, col, bmat, bmat, cmat, cmat, lvec, lvec, dvec],
        out_specs=[col, bmat, bmat, cmat, cmat, lvec, lvec, dvec],
        scratch_shapes=[pltpu.VMEM((n_tok, sw), F32)] * 4,
        compiler_params=_cparams(("parallel",)))(proj, ypre, dy2, bblk_r, bblk_i, cblk_r, cblk_i, lbar_r, lbar_i, dskip)


CONV_ROWS = 256
CONV_COLS = 512


def _conv_pre(x_ref, w_ref, blk, n_blk):
    r0 = blk * CONV_ROWS
    if blk == 0:
        ext = jnp.concatenate([jnp.zeros((8, CONV_COLS), F32), x_ref[0:CONV_ROWS, :]], axis=0)
    else:
        ext = x_ref[r0 - 8:r0 + CONV_ROWS, :]
    taps = []
    c = None
    for j in range(CONV_K):
        s = CONV_K - 1 - j
        xs = ext[8:] if s == 0 else pltpu.roll(ext, s, 0)[8:]
        taps.append(xs)
        term = w_ref[j:j + 1, :] * xs
        c = term if c is None else c + term
    return c, taps


def conv_fwd(proj, col0, conv_w, *, name):
    n_tok = proj.shape[0]
    width = conv_w.shape[1]
    n_blk = n_tok // CONV_ROWS
    cb0 = col0 // CONV_COLS

    def body(x_ref, w_ref, o_ref):
        for blk in range(n_blk):
            c, _ = _conv_pre(x_ref, w_ref, blk, n_blk)
            o_ref[blk * CONV_ROWS:(blk + 1) * CONV_ROWS, :] = c * jax.nn.sigmoid(c)

    return pl.pallas_call(
        body, name=name, out_shape=jax.ShapeDtypeStruct((n_tok, width), F32), grid=(width // CONV_COLS,),
        in_specs=[pl.BlockSpec((n_tok, CONV_COLS), lambda j: (0, cb0 + j)),
                  pl.BlockSpec((CONV_K, CONV_COLS), lambda j: (0, j))],
        out_specs=pl.BlockSpec((n_tok, CONV_COLS), lambda j: (0, j)),
        compiler_params=_cparams(("parallel",)))(proj, conv_w)


def conv_bwd(proj, col0, conv_w, dout, *, name):
    n_tok = proj.shape[0]
    width = conv_w.shape[1]
    n_blk = n_tok // CONV_ROWS
    cb0 = col0 // CONV_COLS

    def body(x_ref, w_ref, do_ref, dx_ref, dw_ref, dc):
        dws = [jnp.zeros((1, CONV_COLS), F32) for _ in range(CONV_K)]
        for blk in range(n_blk):
            rows = slice(blk * CONV_ROWS, (blk + 1) * CONV_ROWS)
            c, taps = _conv_pre(x_ref, w_ref, blk, n_blk)
            sg = jax.nn.sigmoid(c)
            dcv = do_ref[rows, :] * (sg * (1.0 + c * (1.0 - sg)))
            dc[rows, :] = dcv
            for j in range(CONV_K):
                dws[j] = dws[j] + jnp.sum(dcv * taps[j], axis=0, keepdims=True)
        dc[n_tok:n_tok + 8, :] = jnp.zeros((8, CONV_COLS), F32)
        for j in range(CONV_K):
            dw_ref[j:j + 1, :] = dws[j]
        for blk in range(n_blk):
            r0 = blk * CONV_ROWS
            ext = dc[r0:r0 + CONV_ROWS + 8, :]
            dx = None
            for j in range(CONV_K):
                s = CONV_K - 1 - j
                sh = ext[:CONV_ROWS] if s == 0 else pltpu.roll(ext, CONV_ROWS + 8 - s, 0)[:CONV_ROWS]
                term = w_ref[j:j + 1, :] * sh
                dx = term if dx is None else dx + term
            dx_ref[r0:r0 + CONV_ROWS, :] = dx.astype(dx_ref.dtype)

    return pl.pallas_call(
        body, name=name, out_shape=[jax.ShapeDtypeStruct((n_tok, width), BF16), jax.ShapeDtypeStruct(conv_w.shape, F32)],
        grid=(width // CONV_COLS,),
        in_specs=[pl.BlockSpec((n_tok, CONV_COLS), lambda j: (0, cb0 + j)),
                  pl.BlockSpec((CONV_K, CONV_COLS), lambda j: (0, j)),
                  pl.BlockSpec((n_tok, CONV_COLS), lambda j: (0, j))],
        out_specs=[pl.BlockSpec((n_tok, CONV_COLS), lambda j: (0, j)), pl.BlockSpec((CONV_K, CONV_COLS), lambda j: (0, j))],
        scratch_shapes=[pltpu.VMEM((n_tok + 8, CONV_COLS), F32)],
        compiler_params=_cparams(("parallel",)))(proj, conv_w, dout)


def _gdn_chunk(head, n_heads, state, q, k, v, z, bsmall, alog_row, dtb_row, nw):
    c = CHUNK
    lane = lax.broadcasted_iota(jnp.int32, (c, HEAD_DIM), 1)
    lane1 = lax.broadcasted_iota(jnp.int32, (1, HEAD_DIM), 1)
    ri = lax.broadcasted_iota(jnp.int32, (c, c), 0)
    ci = lax.broadcasted_iota(jnp.int32, (c, c), 1)
    causal = ri >= ci
    strict = ri > ci
    tril = causal.astype(F32)
    triu = (ri <= ci).astype(F32)
    eye = (ri == ci).astype(F32)
    bl = jnp.sum(jnp.where(lane == head, bsmall, 0.0), axis=-1, keepdims=True)
    al = jnp.sum(jnp.where(lane == n_heads + head, bsmall, 0.0), axis=-1, keepdims=True)
    alog = jnp.sum(jnp.where(lane1 == head, alog_row, 0.0), axis=-1, keepdims=True)
    dtb = jnp.sum(jnp.where(lane1 == head, dtb_row, 0.0), axis=-1, keepdims=True)

    qn = q * lax.rsqrt(jnp.sum(q * q, axis=-1, keepdims=True) + L2_EPS) * (HEAD_DIM ** -0.5)
    kn = k * lax.rsqrt(jnp.sum(k * k, axis=-1, keepdims=True) + L2_EPS)
    beta = jax.nn.sigmoid(bl)
    xg = al + dtb
    g = -jnp.exp(alog) * (jnp.maximum(xg, 0.0) + jnp.log(1.0 + jnp.exp(-jnp.abs(xg))))
    g_sq = jnp.broadcast_to(g, (c, c))
    g_wide = jnp.broadcast_to(g, (c, HEAD_DIM))
    gc_rows = jnp.dot(tril, g_sq, precision=HI, preferred_element_type=F32)
    gc_cols = lax.dot_general(g_sq, triu, (((0,), (0,)), ((), ())), precision=HI, preferred_element_type=F32)
    gc = jnp.dot(tril, g_wide, precision=HI, preferred_element_type=F32)
    g_tot = jnp.sum(g, axis=0, keepdims=True)
    decay = jnp.exp(jnp.where(causal, gc_rows - gc_cols, -1e30))
    egc = jnp.exp(gc)
    kb = kn * beta
    knb = kn.astype(BF16)
    nt = (((1,), (1,)), ((), ()))
    lower = jnp.where(strict, lax.dot_general(kb.astype(BF16), knb, nt, preferred_element_type=F32) * decay, 0.0)
    xp = -lower
    tinv = eye + xp
    for _ in range(5):
        xp = jnp.dot(xp, xp, precision=HI, preferred_element_type=F32)
        tinv = tinv + jnp.dot(tinv, xp, precision=HI, preferred_element_type=F32)
    u_val = jnp.dot(tinv, v * beta, precision=HI, preferred_element_type=F32)
    w_key = jnp.dot(tinv, kb * egc, precision=HI, preferred_element_type=F32)
    attn = lax.dot_general(qn.astype(BF16), knb, nt, preferred_element_type=F32) * decay
    q_dec = qn * egc
    k_dec = kn * jnp.exp(g_tot - gc)
    sb = state.astype(BF16)
    v_new = u_val - jnp.dot(w_key.astype(BF16), sb, preferred_element_type=F32)
    vnb = v_new.astype(BF16)
    o = (jnp.dot(q_dec.astype(BF16), sb, preferred_element_type=F32)
         + jnp.dot(attn.astype(BF16), vnb, preferred_element_type=F32))
    new_state = state * jnp.exp(g_tot) + lax.dot_general(k_dec.astype(BF16), vnb, (((0,), (0,)), ((), ())),
                                                         preferred_element_type=F32)
    o = o * lax.rsqrt(jnp.mean(o * o, axis=-1, keepdims=True) + RMS_EPS) * nw
    o = o * (z * jax.nn.sigmoid(z))
    return o, new_state


def _gdn_in_specs(n_heads, qkv_width_blocks, z_blk, small_blk, rev, n_chunks):
    w = n_heads * HEAD_DIM

    def cidx(i):
        return (n_chunks - 1 - i) if rev else i
    qs = pl.BlockSpec((CHUNK, w), lambda i: (cidx(i), 0))
    ks = pl.BlockSpec((CHUNK, w), lambda i: (cidx(i), 1))
    vs = pl.BlockSpec((CHUNK, w), lambda i: (cidx(i), 2))
    zs = pl.BlockSpec((CHUNK, w), lambda i: (cidx(i), z_blk))
    bs = pl.BlockSpec((CHUNK, HEAD_DIM), lambda i: (cidx(i), small_blk))
    pv = pl.BlockSpec((1, HEAD_DIM), lambda i: (0, 0))
    return cidx, qs, ks, vs, zs, bs, pv


def gdn_fwd(qkv, proj, z_col0, small_col0, alog_row, dtb_row, nw_row, n_heads, *, name):
    n_tok = qkv.shape[0]
    w = n_heads * HEAD_DIM
    n_chunks = n_tok // CHUNK
    cidx, qs, ks, vs, zs, bs, pv = _gdn_in_specs(n_heads, 3, z_col0 // w, small_col0 // HEAD_DIM, False, n_chunks)

    def body(q_ref, k_ref, v_ref, z_ref, b_ref, al_ref, dt_ref, nw_ref, o_ref, s_ref, state):
        @pl.when(pl.program_id(0) == 0)
        def _():
            state[...] = jnp.zeros_like(state)

        bsm = b_ref[...]
        for h in range(n_heads):
            cols = slice(h * HEAD_DIM, (h + 1) * HEAD_DIM)
            st = state[h]
            s_ref[h] = st
            o, ns = _gdn_chunk(h, n_heads, st, q_ref[:, cols], k_ref[:, cols], v_ref[:, cols], z_ref[:, cols], bsm,
                               al_ref[...], dt_ref[...], nw_ref[...])
            o_ref[:, cols] = o
            state[h] = ns

    return pl.pallas_call(
        body, name=name,
        out_shape=[jax.ShapeDtypeStruct((n_tok, w), F32),
                   jax.ShapeDtypeStruct((n_chunks, n_heads, HEAD_DIM, HEAD_DIM), F32)],
        grid=(n_chunks,), in_specs=[qs, ks, vs, zs, bs, pv, pv, pv],
        out_specs=[pl.BlockSpec((CHUNK, w), lambda i: (i, 0)),
                   pl.BlockSpec((None, n_heads, HEAD_DIM, HEAD_DIM), lambda i: (i, 0, 0, 0))],
        scratch_shapes=[pltpu.VMEM((n_heads, HEAD_DIM, HEAD_DIM), F32)],
        compiler_params=_cparams(("arbitrary",)))(qkv, qkv, qkv, proj, proj, alog_row, dtb_row, nw_row)


def gdn_bwd(qkv, proj, z_col0, small_col0, alog_row, dtb_row, nw_row, states, dout, n_heads, *, name):
    n_tok = qkv.shape[0]
    w = n_heads * HEAD_DIM
    n_chunks = n_tok // CHUNK
    cidx, qs, ks, vs, zs, bs, pv = _gdn_in_specs(n_heads, 3, z_col0 // w, small_col0 // HEAD_DIM, True, n_chunks)

    def body(q_ref, k_ref, v_ref, z_ref, b_ref, al_ref, dt_ref, nw_ref, s_ref, do_ref,
             dq_ref, dk_ref, dv_ref, dz_ref, db_ref, dal_ref, ddt_ref, dnw_ref, dstate):
        @pl.when(pl.program_id(0) == 0)
        def _():
            dstate[...] = jnp.zeros_like(dstate)
            dal_ref[...] = jnp.zeros_like(dal_ref)
            ddt_ref[...] = jnp.zeros_like(ddt_ref)
            dnw_ref[...] = jnp.zeros_like(dnw_ref)

        bsm = b_ref[...]
        dbs = jnp.zeros((CHUNK, HEAD_DIM), F32)
        dal = jnp.zeros((1, HEAD_DIM), F32)
        ddt = jnp.zeros((1, HEAD_DIM), F32)
        dnw = jnp.zeros((1, HEAD_DIM), F32)
        for h in range(n_heads):
            cols = slice(h * HEAD_DIM, (h + 1) * HEAD_DIM)

            def f(st, q, k, v, z, bb, al, dt, nw, h=h):
                return _gdn_chunk(h, n_heads, st, q, k, v, z, bb, al, dt, nw)
            _, vjp = jax.vjp(f, s_ref[h], q_ref[:, cols], k_ref[:, cols], v_ref[:, cols], z_ref[:, cols], bsm,
                             al_ref[...], dt_ref[...], nw_ref[...])
            dst, dq, dk, dv, dz, dbb, da, dd, dn = vjp((do_ref[:, cols], dstate[h]))
            dstate[h] = dst
            dq_ref[:, cols] = dq
            dk_ref[:, cols] = dk
            dv_ref[:, cols] = dv
            dz_ref[:, cols] = dz.astype(dz_ref.dtype)
            dbs = dbs + dbb
            dal = dal + da
            ddt = ddt + dd
            dnw = dnw + dn
        db_ref[...] = dbs.astype(db_ref.dtype)
        dal_ref[...] += dal
        ddt_ref[...] += ddt
        dnw_ref[...] += dnw

    rowblk = pl.BlockSpec((CHUNK, w), lambda i: (cidx(i), 0))
    return pl.pallas_call(
        body, name=name,
        out_shape=[jax.ShapeDtypeStruct((n_tok, w), F32)] * 3 + [
            jax.ShapeDtypeStruct((n_tok, w), BF16), jax.ShapeDtypeStruct((n_tok, HEAD_DIM), BF16),
            jax.ShapeDtypeStruct((1, HEAD_DIM), F32), jax.ShapeDtypeStruct((1, HEAD_DIM), F32),
            jax.ShapeDtypeStruct((1, HEAD_DIM), F32)],
        grid=(n_chunks,),
        in_specs=[qs, ks, vs, zs, bs, pv, pv, pv,
                  pl.BlockSpec((None, n_heads, HEAD_DIM, HEAD_DIM), lambda i: (cidx(i), 0, 0, 0)), rowblk],
        out_specs=[rowblk, rowblk, rowblk, rowblk, pl.BlockSpec((CHUNK, HEAD_DIM), lambda i: (cidx(i), 0)), pv, pv, pv],
        scratch_shapes=[pltpu.VMEM((n_heads, HEAD_DIM, HEAD_DIM), F32)],
        compiler_params=_cparams(("arbitrary",)))(qkv, qkv, qkv, proj, proj, alog_row, dtb_row, nw_row, states, dout)


def glu_gate_fwd(y2, gl, bias, *, name):
    n_tok, w = y2.shape
    tm = _tile(n_tok, (256, 128))

    def body(y_ref, g_ref, b_ref, o_ref):
        o_ref[...] = y_ref[...] * jax.nn.sigmoid(g_ref[...] + b_ref[...])

    row = pl.BlockSpec((tm, w), lambda i: (i, 0))
    vec = pl.BlockSpec((1, w), lambda i: (0, 0))
    return pl.pallas_call(body, name=name, out_shape=jax.ShapeDtypeStruct((n_tok, w), F32), grid=(n_tok // tm,),
                          in_specs=[row, row, vec], out_specs=row, compiler_params=_cparams(("parallel",)))(y2, gl, bias)


def glu_gate_bwd(y2, gl, bias, dys, *, name):
    n_tok, w = y2.shape
    tm = _tile(n_tok, (256, 128))

    def body(y_ref, g_ref, b_ref, d_ref, dy_ref, dg_ref, db_ref):
        @pl.when(pl.program_id(0) == 0)
        def _():
            db_ref[...] = jnp.zeros_like(db_ref)

        sg = jax.nn.sigmoid(g_ref[...] + b_ref[...])
        d = d_ref[...]
        dy_ref[...] = d * sg
        dg = d * y_ref[...] * sg * (1.0 - sg)
        dg_ref[...] = dg.astype(dg_ref.dtype)
        db_ref[...] += jnp.sum(dg, axis=0, keepdims=True)

    row = pl.BlockSpec((tm, w), lambda i: (i, 0))
    vec = pl.BlockSpec((1, w), lambda i: (0, 0))
    return pl.pallas_call(
        body, name=name, out_shape=[jax.ShapeDtypeStruct((n_tok, w), F32), jax.ShapeDtypeStruct((n_tok, w), BF16),
                                    jax.ShapeDtypeStruct((1, w), F32)],
        grid=(n_tok // tm,), in_specs=[row, row, vec, row], out_specs=[row, row, vec],
        compiler_params=_cparams(("arbitrary",)))(y2, gl, bias, dys)


def merge_fwd(proj, bs, bd, *, name):
    n_tok, d = bs.shape
    tm = _tile(n_tok, (256, 128))

    def body(gs_ref, gd_ref, bs_ref, bd_ref, o_ref):
        o_ref[...] = jax.nn.sigmoid(gs_ref[...]) * bs_ref[...] + jax.nn.sigmoid(gd_ref[...]) * bd_ref[...]

    row = pl.BlockSpec((tm, d), lambda i: (i, 0))
    return pl.pallas_call(
        body, name=name, out_shape=jax.ShapeDtypeStruct((n_tok, d), F32), grid=(n_tok // tm,),
        in_specs=[row, pl.BlockSpec((tm, d), lambda i: (i, 1)), row, row], out_specs=row,
        compiler_params=_cparams(("parallel",)))(proj, proj, bs, bd)


def merge_bwd(proj, bs, bd, dm, *, name):
    n_tok, d = bs.shape
    tm = _tile(n_tok, (256, 128))

    def body(gs_ref, gd_ref, bs_ref, bd_ref, dm_ref, dbs_ref, dbd_ref, dgs_ref, dgd_ref):
        dmv = dm_ref[...]
        ss = jax.nn.sigmoid(gs_ref[...])
        sd = jax.nn.sigmoid(gd_ref[...])
        dbs_ref[...] = (ss * dmv).astype(BF16)
        dbd_ref[...] = (sd * dmv).astype(BF16)
        dgs_ref[...] = (dmv * bs_ref[...] * ss * (1.0 - ss)).astype(BF16)
        dgd_ref[...] = (dmv * bd_ref[...] * sd * (1.0 - sd)).astype(BF16)

    row = pl.BlockSpec((tm, d), lambda i: (i, 0))
    return pl.pallas_call(
        body, name=name, out_shape=[jax.ShapeDtypeStruct((n_tok, d), BF16)] * 4, grid=(n_tok // tm,),
        in_specs=[row, pl.BlockSpec((tm, d), lambda i: (i, 1)), row, row, row], out_specs=[row] * 4,
        compiler_params=_cparams(("parallel",)))(proj, proj, bs, bd, dm)


def add_pairs(mine, recv, out_dtype, *, name):
    outs = []
    for t, (a, b) in enumerate(zip(mine, recv)):
        rows = a.shape[0] * a.shape[1]
        cols = a.shape[2]
        a2 = a.reshape(rows, cols)
        b2 = b.reshape(rows, cols)
        tr = _tile(rows, (256, 128, 64, 32, 16))

        def body(a_ref, b_ref, o_ref):
            o_ref[...] = (a_ref[...].astype(F32) + b_ref[...].astype(F32)).astype(out_dtype)

        blk = pl.BlockSpec((tr, cols), lambda i: (i, 0))
        o = pl.pallas_call(body, name=f"{name}_{t}", out_shape=jax.ShapeDtypeStruct((rows, cols), out_dtype),
                           grid=(rows // tr,), in_specs=[blk, blk], out_specs=blk,
                           compiler_params=_cparams(("parallel",)))(a2, b2)
        outs.append(o.reshape(a.shape))
    return outs


def add_chips(parts, *, name):
    outs = []
    for t, p in enumerate(parts):
        _, h, cols = p.shape
        tr = _tile(h, (256, 128, 64, 32, 16))

        def body(p0, p1, p2, p3, o_ref):
            o_ref[...] = ((p0[...].astype(F32) + p1[...].astype(F32)) + p2[...].astype(F32)) + p3[...].astype(F32)

        specs = [pl.BlockSpec((None, tr, cols), lambda i, s=s: (s, i, 0)) for s in range(N_CHIPS)]
        outs.append(pl.pallas_call(body, name=f"{name}_{t}", out_shape=jax.ShapeDtypeStruct((h, cols), F32),
                                   grid=(h // tr,), in_specs=specs, out_specs=pl.BlockSpec((tr, cols), lambda i: (i, 0)),
                                   compiler_params=_cparams(("parallel",)))(p, p, p, p))
    return outs


def adamw(w, g, m, v, *, name):
    shape = w.shape
    cols = shape[-1]
    rows = w.size // cols
    tr = _tile(rows, (128, 64, 32, 16, 8))
    c1 = 1.0 / (1.0 - ADAM_B1 ** ADAM_STEP)
    c2 = 1.0 / (1.0 - ADAM_B2 ** ADAM_STEP)

    def body(w_ref, g_ref, m_ref, v_ref, d_ref, nm_ref, nv_ref):
        gv = g_ref[...]
        nm = ADAM_B1 * m_ref[...] + (1.0 - ADAM_B1) * gv
        nv = ADAM_B2 * v_ref[...] + (1.0 - ADAM_B2) * (gv * gv)
        d_ref[...] = -ADAM_LR * ((nm * c1) / (jnp.sqrt(nv * c2) + ADAM_EPS) + ADAM_WD * w_ref[...])
        nm_ref[...] = nm
        nv_ref[...] = nv

    blk = pl.BlockSpec((tr, cols), lambda i: (i, 0))
    outs = pl.pallas_call(body, name=name, out_shape=[jax.ShapeDtypeStruct((rows, cols), F32)] * 3, grid=(rows // tr,),
                          in_specs=[blk] * 4, out_specs=[blk] * 3, compiler_params=_cparams(("parallel",)))(
        w.reshape(rows, cols), g.reshape(rows, cols), m.reshape(rows, cols), v.reshape(rows, cols))
    return [o.reshape(shape) for o in outs]


def _place():
    return lax.axis_index("x"), lax.axis_index("y"), lax.axis_index("c")


def _other_chips(x, y):
    return [(1 - x, y), (x, 1 - y), (1 - x, 1 - y)]


ANY = pl.BlockSpec(memory_space=pl.ANY)


def gather_chips(blocks, halve, *, name):
    n = len(blocks)

    def body(*refs):
        ins, outs = refs[:n], refs[n:2 * n]
        send_sems, recv_sems, fwd_send, fwd_recv, local_sems = refs[2 * n:]
        x, y, c = _place()
        me = 2 * x + y
        chips = _other_chips(x, y)
        sibling = (x, y, 1 - c)
        locals_, sends, fwds = [], [], []
        for t in range(n):
            cp = pltpu.make_async_copy(ins[t], outs[t].at[me], local_sems.at[t])
            cp.start()
            locals_.append(cp)
            for j, (px, py) in enumerate(chips):
                if halve[t]:
                    h = ins[t].shape[0] // 2
                    rows = pl.ds(c * h, h)
                    src, dst = ins[t].at[rows], outs[t].at[me, rows]
                else:
                    src, dst = ins[t], outs[t].at[me]
                cp = pltpu.make_async_remote_copy(src_ref=src, dst_ref=dst, send_sem=send_sems.at[3 * t + j],
                                                  recv_sem=recv_sems.at[3 * t + j], device_id=(px, py, c),
                                                  device_id_type=MESH)
                cp.start()
                sends.append(cp)
        for t in range(n):
            for j, (px, py) in enumerate(chips):
                src_chip = 2 * px + py
                if halve[t]:
                    h = ins[t].shape[0] // 2
                    rows = pl.ds(c * h, h)
                    landed = outs[t].at[src_chip, rows]
                    pltpu.make_async_remote_copy(src_ref=landed, dst_ref=landed, send_sem=send_sems.at[3 * t + j],
                                                 recv_sem=recv_sems.at[3 * t + j], device_id=(px, py, c),
                                                 device_id_type=MESH).wait_recv()
                    cp = pltpu.make_async_remote_copy(src_ref=landed, dst_ref=landed, send_sem=fwd_send.at[3 * t + j],
                                                      recv_sem=fwd_recv.at[3 * t + j], device_id=sibling,
                                                      device_id_type=MESH)
                    cp.start()
                    fwds.append(cp)
                else:
                    landed = outs[t].at[src_chip]
                    pltpu.make_async_remote_copy(src_ref=landed, dst_ref=landed, send_sem=send_sems.at[3 * t + j],
                                                 recv_sem=recv_sems.at[3 * t + j], device_id=(px, py, c),
                                                 device_id_type=MESH).wait_recv()
        for t in range(n):
            if not halve[t]:
                continue
            h = ins[t].shape[0] // 2
            for j, (px, py) in enumerate(chips):
                theirs = outs[t].at[2 * px + py, pl.ds((1 - c) * h, h)]
                pltpu.make_async_remote_copy(src_ref=theirs, dst_ref=theirs, send_sem=fwd_send.at[3 * t + j],
                                             recv_sem=fwd_recv.at[3 * t + j], device_id=sibling,
                                             device_id_type=MESH).wait_recv()
        for cp in sends + fwds:
            cp.wait_send()
        for cp in locals_:
            cp.wait()

    return pl.pallas_call(
        body, name=name, out_shape=[jax.ShapeDtypeStruct((N_CHIPS,) + b.shape, b.dtype) for b in blocks],
        in_specs=[ANY] * n, out_specs=[ANY] * n,
        scratch_shapes=[pltpu.SemaphoreType.DMA((3 * n,))] * 4 + [pltpu.SemaphoreType.DMA((n,))],
        compiler_params=pltpu.CompilerParams(has_side_effects=True))(*blocks)


def pair_split(grads, *, name):
    n = len(grads)

    def body(*refs):
        ins, mine, recv = refs[:n], refs[n:2 * n], refs[2 * n:3 * n]
        send_sems, recv_sems, local_sems = refs[3 * n:]
        x, y, c = _place()
        sibling = (x, y, 1 - c)
        cps, lcs = [], []
        for t in range(n):
            h = ins[t].shape[1] // 2
            lc = pltpu.make_async_copy(ins[t].at[:, pl.ds(c * h, h)], mine[t], local_sems.at[t])
            lc.start()
            lcs.append(lc)
            cp = pltpu.make_async_remote_copy(src_ref=ins[t].at[:, pl.ds((1 - c) * h, h)], dst_ref=recv[t],
                                              send_sem=send_sems.at[t], recv_sem=recv_sems.at[t], device_id=sibling,
                                              device_id_type=MESH)
            cp.start()
            cps.append(cp)
        for cp in cps:
            cp.wait()
        for lc in lcs:
            lc.wait()

    half = [jax.ShapeDtypeStruct((g.shape[0], g.shape[1] // 2, g.shape[2]), g.dtype) for g in grads]
    outs = pl.pallas_call(
        body, name=name, out_shape=half + half, in_specs=[ANY] * n, out_specs=[ANY] * (2 * n),
        scratch_shapes=[pltpu.SemaphoreType.DMA((n,))] * 3,
        compiler_params=pltpu.CompilerParams(has_side_effects=True))(*grads)
    return outs[:n], outs[n:]


def chip_exchange(parts, *, name):
    n = len(parts)

    def body(*refs):
        ins, outs = refs[:n], refs[n:2 * n]
        send_sems, recv_sems, local_sems = refs[2 * n:]
        x, y, c = _place()
        me = 2 * x + y
        chips = _other_chips(x, y)
        cps, lcs = [], []
        for t in range(n):
            lc = pltpu.make_async_copy(ins[t].at[me], outs[t].at[me], local_sems.at[t])
            lc.start()
            lcs.append(lc)
            for j, (px, py) in enumerate(chips):
                cp = pltpu.make_async_remote_copy(src_ref=ins[t].at[2 * px + py], dst_ref=outs[t].at[me],
                                                  send_sem=send_sems.at[3 * t + j], recv_sem=recv_sems.at[3 * t + j],
                                                  device_id=(px, py, c), device_id_type=MESH)
                cp.start()
                cps.append(cp)
        for t in range(n):
            for j, (px, py) in enumerate(chips):
                landed = outs[t].at[2 * px + py]
                pltpu.make_async_remote_copy(src_ref=landed, dst_ref=landed, send_sem=send_sems.at[3 * t + j],
                                             recv_sem=recv_sems.at[3 * t + j], device_id=(px, py, c),
                                             device_id_type=MESH).wait_recv()
        for cp in cps:
            cp.wait_send()
        for lc in lcs:
            lc.wait()

    return pl.pallas_call(
        body, name=name, out_shape=[jax.ShapeDtypeStruct(p.shape, p.dtype) for p in parts],
        in_specs=[ANY] * n, out_specs=[ANY] * n,
        scratch_shapes=[pltpu.SemaphoreType.DMA((3 * n,))] * 2 + [pltpu.SemaphoreType.DMA((n,))],
        compiler_params=pltpu.CompilerParams(has_side_effects=True))(*parts)


def pair_join(halves, *, name):
    n = len(halves)

    def body(*refs):
        ins, outs = refs[:n], refs[n:2 * n]
        send_sems, recv_sems, local_sems = refs[2 * n:]
        x, y, c = _place()
        sibling = (x, y, 1 - c)
        cps, lcs = [], []
        for t in range(n):
            h = ins[t].shape[0]
            lc = pltpu.make_async_copy(ins[t], outs[t].at[pl.ds(c * h, h)], local_sems.at[t])
            lc.start()
            lcs.append(lc)
            cp = pltpu.make_async_remote_copy(src_ref=ins[t], dst_ref=outs[t].at[pl.ds(c * h, h)],
                                              send_sem=send_sems.at[t], recv_sem=recv_sems.at[t], device_id=sibling,
                                              device_id_type=MESH)
            cp.start()
            cps.append(cp)
        for t in range(n):
            h = ins[t].shape[0]
            theirs = outs[t].at[pl.ds((1 - c) * h, h)]
            pltpu.make_async_remote_copy(src_ref=theirs, dst_ref=theirs, send_sem=send_sems.at[t],
                                         recv_sem=recv_sems.at[t], device_id=sibling, device_id_type=MESH).wait_recv()
        for cp in cps:
            cp.wait_send()
        for lc in lcs:
            lc.wait()

    return pl.pallas_call(
        body, name=name, out_shape=[jax.ShapeDtypeStruct((2 * p.shape[0], p.shape[1]), p.dtype) for p in halves],
        in_specs=[ANY] * n, out_specs=[ANY] * n,
        scratch_shapes=[pltpu.SemaphoreType.DMA((n,))] * 3,
        compiler_params=pltpu.CompilerParams(has_side_effects=True))(*halves)


def reduce_scatter(grads, pay_dtype, *, name):
    mine, recv = pair_split(grads, name=name + "_split")
    part = add_pairs(mine, recv, pay_dtype, name=name + "_add2")
    got = chip_exchange(part, name=name + "_xchg")
    half = add_chips(got, name=name + "_add4")
    return pair_join(half, name=name + "_join")


def _in_sizes(d, w, n_heads):
    return (w, w, w, w, w, n_heads, n_heads, d, d)


def _wcat_from_gathered(wg, d, w, n_heads):
    full = jnp.concatenate([wg[s] for s in range(N_CHIPS)], axis=1)
    sizes = _in_sizes(d, w, n_heads)
    offs = [0]
    for s in sizes:
        offs.append(offs[-1] + s)
    pieces = [full[:, offs[i]:offs[i + 1]] for i in range(len(sizes))]
    u, q, k, v, z, beta, a, gs, gd = pieces
    pad = jnp.zeros((full.shape[0], SMALL_W - 2 * n_heads), full.dtype)
    return jnp.concatenate([gs, gd, u, q, k, v, z, beta, a, pad], axis=1)


def _wcat_grad_to_shards(dwcat, d, w, n_heads):
    gs, gd = dwcat[:, :d], dwcat[:, d:2 * d]
    o = 2 * d
    u, q, k, v, z = [dwcat[:, o + i * w:o + (i + 1) * w] for i in range(5)]
    o += 5 * w
    beta, a = dwcat[:, o:o + n_heads], dwcat[:, o + n_heads:o + 2 * n_heads]
    full = jnp.concatenate([u, q, k, v, z, beta, a, gs, gd], axis=1)
    return jnp.stack(jnp.split(full, N_CHIPS, axis=1))


def _s5_discretize(a_re, a_im, log_dt, b_re, b_im):
    dt = jnp.exp(log_dt)[:, None]
    mag = jnp.exp(a_re * dt)
    lbar_r, lbar_i = mag * jnp.cos(a_im * dt), mag * jnp.sin(a_im * dt)
    den = a_re * a_re + a_im * a_im
    zr, zi = _cmul(lbar_r - 1.0, lbar_i, a_re / den, -a_im / den)
    bbar_r, bbar_i = _cmul(zr[:, :, None], zi[:, :, None], b_re, b_im)
    return lbar_r, lbar_i, bbar_r, bbar_i


def _blockdiag_in(bbar):
    g, p, h = bbar.shape
    t = g // GROUPS_PER_TILE
    bb = bbar.reshape(t, GROUPS_PER_TILE, p, h).transpose(0, 1, 3, 2)
    eye = jnp.eye(GROUPS_PER_TILE, dtype=bbar.dtype)
    return jnp.einsum('tjhp,jk->tjhkp', bb, eye).reshape(t, GROUPS_PER_TILE * h, GROUPS_PER_TILE * p)


def _blockdiag_in_grad(dblk, g, p, h):
    t = g // GROUPS_PER_TILE
    d5 = dblk.reshape(t, GROUPS_PER_TILE, h, GROUPS_PER_TILE, p)
    eye = jnp.eye(GROUPS_PER_TILE, dtype=dblk.dtype)
    diag = jnp.einsum('tjhkp,jk->tjhp', d5, eye)
    return diag.transpose(0, 1, 3, 2).reshape(g, p, h)


def _blockdiag_out(cmat):
    g, h, p = cmat.shape
    t = g // GROUPS_PER_TILE
    cc = cmat.reshape(t, GROUPS_PER_TILE, h, p).transpose(0, 1, 3, 2)
    eye = jnp.eye(GROUPS_PER_TILE, dtype=cmat.dtype)
    return jnp.einsum('tjph,jk->tjpkh', cc, eye).reshape(t, GROUPS_PER_TILE * p, GROUPS_PER_TILE * h)


def _blockdiag_out_grad(dblk, g, h, p):
    t = g // GROUPS_PER_TILE
    d5 = dblk.reshape(t, GROUPS_PER_TILE, p, GROUPS_PER_TILE, h)
    eye = jnp.eye(GROUPS_PER_TILE, dtype=dblk.dtype)
    diag = jnp.einsum('tjpkh,jk->tjph', d5, eye)
    return diag.transpose(0, 1, 3, 2).reshape(g, h, p)


def _pad_row(v, width):
    return jnp.pad(v.reshape(1, -1), ((0, 0), (0, width - v.size)))


def _pack(arrs, rows_mult):
    flat = jnp.concatenate([a.reshape(-1) for a in arrs])
    per = 128 * rows_mult
    total = -(-flat.size // per) * per
    return jnp.pad(flat, (0, total - flat.size))


def _unpack(flat, like):
    out, o = [], 0
    for a in like:
        out.append(flat[o:o + a.size].reshape(a.shape))
        o += a.size
    return out


def kernel(x, ffn1_w_gu, ffn1_w_down, ln1_g, ln1_b, w_in, conv_w, ssm_a_re, ssm_a_im, ssm_log_dt, ssm_b_re, ssm_b_im, ssm_c_re, ssm_c_im, ssm_d, glu_w, glu_b, gdn_a_log, gdn_dt_bias, gdn_norm_w, w_br_ssm, w_br_gdn, w_out, ln2_g, ln2_b, ffn2_w_gu, ffn2_w_down, ln3_g, ln3_b, loss_target, m_ffn1_w_gu, m_ffn1_w_down, m_ln1_g, m_ln1_b, m_w_in, m_conv_w, m_ssm_a_re, m_ssm_a_im, m_ssm_log_dt, m_ssm_b_re, m_ssm_b_im, m_ssm_c_re, m_ssm_c_im, m_ssm_d, m_glu_w, m_glu_b, m_gdn_a_log, m_gdn_dt_bias, m_gdn_norm_w, m_w_br_ssm, m_w_br_gdn, m_w_out, m_ln2_g, m_ln2_b, m_ffn2_w_gu, m_ffn2_w_down, m_ln3_g, m_ln3_b, v_ffn1_w_gu, v_ffn1_w_down, v_ln1_g, v_ln1_b, v_w_in, v_conv_w, v_ssm_a_re, v_ssm_a_im, v_ssm_log_dt, v_ssm_b_re, v_ssm_b_im, v_ssm_c_re, v_ssm_c_im, v_ssm_d, v_glu_w, v_glu_b, v_gdn_a_log, v_gdn_dt_bias, v_gdn_norm_w, v_w_br_ssm, v_w_br_gdn, v_w_out, v_ln2_g, v_ln2_b, v_ffn2_w_gu, v_ffn2_w_down, v_ln3_g, v_ln3_b):
    args = locals()
    wts = {n: args[n] for n in WEIGHT_NAMES}
    mom = {n: args["m_" + n] for n in WEIGHT_NAMES}
    var = {n: args["v_" + n] for n in WEIGHT_NAMES}

    depth = ln1_g.shape[0]
    n_tok, d = x.shape[1], x.shape[2]
    w = glu_w.shape[-1]
    n_heads = gdn_a_log.shape[-1]
    n_groups, n_state, grp = ssm_b_re.shape[1], ssm_b_re.shape[2], ssm_b_re.shape[3]
    alpha = (2.0 * depth) ** 0.25
    u_col0 = 2 * d
    qkv_col0 = u_col0 + w
    z_col0 = u_col0 + 4 * w
    small_col0 = u_col0 + 5 * w
    x_idx, y_idx, _ = _place()
    chip = 2 * x_idx + y_idx

    xcur = x[0]
    saved = []
    for l in range(depth):
        shards = [wts[n][l].astype(BF16) for n in BIG] + [conv_w[l]]
        gathered = gather_chips(shards, [True] * len(BIG) + [False], name=f"gather_l{l}")
        gw = dict(zip(BIG, gathered[:-1]))
        conv_full = jnp.concatenate([gathered[-1][s] for s in range(N_CHIPS)], axis=1)
        wgu1, wgu2 = gw['ffn1_w_gu'], gw['ffn2_w_gu']
        wd1 = gw['ffn1_w_down'].reshape(-1, d)
        wd2 = gw['ffn2_w_down'].reshape(-1, d)
        wcat = _wcat_from_gathered(gw['w_in'], d, w, n_heads)
        wglu = gw['glu_w'].reshape(w, w)
        wbs, wbd = gw['w_br_ssm'], gw['w_br_gdn']
        wout = gw['w_out'].reshape(d, d)
        f = wd1.shape[0]

        vec = lambda v: v[l].reshape(1, -1)
        x0 = xcur
        h1 = mm(x0, wgu1, b_nb=N_CHIPS, out_nb=2, name=f"ffn_up")
        r1, x1 = down_res_ln(h1, wd1, x0, vec(ln1_g), vec(ln1_b), swiglu=True, alpha=alpha, scale=0.5, name="ffn_down")
        proj = mm(x1, wcat, name="in_proj")
        (lbar_r, lbar_i, bbar_r, bbar_i), disc_vjp = jax.vjp(
            _s5_discretize, ssm_a_re[l], ssm_a_im[l], ssm_log_dt[l], ssm_b_re[l], ssm_b_im[l])
        s5w = (_blockdiag_in(bbar_r), _blockdiag_in(bbar_i), _blockdiag_out(ssm_c_re[l]), _blockdiag_out(ssm_c_im[l]),
               lbar_r.reshape(1, -1), lbar_i.reshape(1, -1), ssm_d[l].reshape(1, -1))
        ypre, y2 = s5_fwd(proj, u_col0, *s5w, name="s5_fwd")
        gl = mm(y2, wglu, name="glu_proj")
        ys = glu_gate_fwd(y2, gl, vec(glu_b), name="glu_gate")
        qkv = conv_fwd(proj, qkv_col0, conv_full, name="conv_fwd")
        gdn_rows = (_pad_row(gdn_a_log[l], HEAD_DIM), _pad_row(gdn_dt_bias[l], HEAD_DIM), gdn_norm_w[l].reshape(1, -1))
        yg, states = gdn_fwd(qkv, proj, z_col0, small_col0, *gdn_rows, n_heads, name="gdn_fwd")
        bs = mm(ys, wbs, b_nb=N_CHIPS, name="br_ssm")
        bd = mm(yg, wbd, b_nb=N_CHIPS, name="br_gdn")
        mrg = merge_fwd(proj, bs, bd, name="merge")
        r2, x2 = down_res_ln(mrg, wout, x1, vec(ln2_g), vec(ln2_b), swiglu=False, alpha=alpha, scale=1.0, name="mix_out")
        h3 = mm(x2, wgu2, b_nb=N_CHIPS, out_nb=2, name="ffn_up")
        r3, x3 = down_res_ln(h3, wd2, x2, vec(ln3_g), vec(ln3_b), swiglu=True, alpha=alpha, scale=0.5, name="ffn_down")
        saved.append(dict(x0=x0, h1=h1, r1=r1, x1=x1, proj=proj, s5w=s5w, disc_vjp=disc_vjp, ypre=ypre, y2=y2, gl=gl,
                          ys=ys, qkv=qkv, gdn_rows=gdn_rows, yg=yg, states=states, bs=bs, bd=bd, mrg=mrg, r2=r2, x2=x2,
                          h3=h3, r3=r3, wgu1=wgu1, wgu2=wgu2, wd1=wd1, wd2=wd2, wcat=wcat, wglu=wglu, wbs=wbs,
                          wbd=wbd, wout=wout, conv_full=conv_full))
        xcur = x3

    dy, loss_blk = loss_head(xcur, loss_target[0], name="loss_head")
    loss = lax.psum(loss_blk[0, 0], ("x", "y", "c"))

    big_grads = {n: [None] * depth for n in BIG}
    small_grads = {n: [None] * depth for n in SMALL}
    for l in reversed(range(depth)):
        s = saved[l]
        vec = lambda v: v[l].reshape(1, -1)

        def ffn_back(dx_out, r, g_ln, xin, hh, wgu, wd):
            dr, dg, db = ln_bwd(r, g_ln, dx_out, name="ln_bwd")
            dh, act = ffn_bwd_mid(dr, wd, hh, scale=0.5, name="ffn_bwd_mid")
            dwd = mm(act, dr, ta=True, out_dtype=BF16, out_scale=0.5, name="ffn_dwd")
            dwgu = mm(xin, dh, ta=True, b_nb=2, out_nb=N_CHIPS, out_dtype=BF16, name="ffn_dwgu")
            dxin = mm(dh, wgu, tb=True, a_nb=2, b_nb=N_CHIPS, add=dr, add_scale=alpha, name="ffn_dx")
            return dxin, dg, db, dwgu, dwd.reshape(N_CHIPS, -1, d)

        dx2, dg3, db3, dwgu2, dwd2 = ffn_back(dy, s['r3'], vec(ln3_g), s['x2'], s['h3'], s['wgu2'], s['wd2'])
        dr2, dg2, db2 = ln_bwd(s['r2'], vec(ln2_g), dx2, name="ln_bwd")
        dmrg = mm(dr2, s['wout'], tb=True, name="mix_dm")
        dwout = mm(s['mrg'], dr2, ta=True, out_dtype=BF16, name="mix_dwout").reshape(N_CHIPS, -1, d)
        dbs, dbd, dgs, dgd = merge_bwd(s['proj'], s['bs'], s['bd'], dmrg, name="merge_bwd")
        dwbs = mm(s['ys'], dbs, ta=True, out_nb=N_CHIPS, out_dtype=BF16, name="br_dw")
        dwbd = mm(s['yg'], dbd, ta=True, out_nb=N_CHIPS, out_dtype=BF16, name="br_dw")
        dys = mm(dbs, s['wbs'], tb=True, b_nb=N_CHIPS, name="br_dx")
        dyg = mm(dbd, s['wbd'], tb=True, b_nb=N_CHIPS, name="br_dx")
        dy2a, dgl, dglu_b = glu_gate_bwd(s['y2'], s['gl'], vec(glu_b), dys, name="glu_gate_bwd")
        dwglu = mm(s['y2'], dgl, ta=True, out_dtype=BF16, name="glu_dw").reshape(N_CHIPS, -1, w)
        dy2 = mm(dgl, s['wglu'], tb=True, add=dy2a, name="glu_dx")
        du, dbr, dbi, dcr, dci, dlr, dli, dd = s5_bwd(s['proj'], u_col0, s['ypre'], dy2, *s['s5w'], name="s5_bwd")
        dq, dk, dv, dz, dsmall, dalog, ddtb, dnw = gdn_bwd(s['qkv'], s['proj'], z_col0, small_col0, *s['gdn_rows'],
                                                           s['states'], dyg, n_heads, name="gdn_bwd")
        dqkv, dconv = conv_bwd(s['proj'], qkv_col0, s['conv_full'], jnp.concatenate([dq, dk, dv], axis=1),
                               name="conv_bwd")
        dsmall_w = jnp.pad(dsmall, ((0, 0), (0, SMALL_W - HEAD_DIM)))
        dproj = jnp.concatenate([dgs, dgd, du, dqkv, dz, dsmall_w], axis=1)
        dwcat = mm(s['x1'], dproj, ta=True, out_dtype=BF16, name="in_dw")
        dx1 = mm(dproj, s['wcat'], tb=True, add=dr2, add_scale=alpha, name="in_dx")
        dx0, dg1, db1, dwgu1, dwd1 = ffn_back(dx1, s['r1'], vec(ln1_g), s['x0'], s['h1'], s['wgu1'], s['wd1'])
        dy = dx0

        da_re, da_im, dlog_dt, db_re, db_im = s['disc_vjp'](
            (dlr.reshape(n_groups, n_state), dli.reshape(n_groups, n_state),
             _blockdiag_in_grad(dbr, n_groups, n_state, grp), _blockdiag_in_grad(dbi, n_groups, n_state, grp)))
        sg = dict(ln1_g=dg1, ln1_b=db1, ln2_g=dg2, ln2_b=db2, ln3_g=dg3, ln3_b=db3, conv_w=dconv,
                  ssm_a_re=da_re, ssm_a_im=da_im, ssm_log_dt=dlog_dt, ssm_b_re=db_re, ssm_b_im=db_im,
                  ssm_c_re=_blockdiag_out_grad(dcr, n_groups, grp, n_state),
                  ssm_c_im=_blockdiag_out_grad(dci, n_groups, grp, n_state), ssm_d=dd, glu_b=dglu_b,
                  gdn_a_log=dalog[0, :n_heads], gdn_dt_bias=ddtb[0, :n_heads], gdn_norm_w=dnw)
        for n in SMALL:
            small_grads[n][l] = sg[n].reshape(-1)
        layer_grads = dict(ffn1_w_gu=dwgu1, ffn1_w_down=dwd1, w_in=_wcat_grad_to_shards(dwcat, d, w, n_heads),
                           glu_w=dwglu, w_br_ssm=dwbs, w_br_gdn=dwbd, w_out=dwout, ffn2_w_gu=dwgu2, ffn2_w_down=dwd2)
        reduced = reduce_scatter([layer_grads[n] for n in BIG], BF16, name="rs_big")
        for n, g in zip(BIG, reduced):
            big_grads[n][l] = g
    grad_x = dy[None]

    small_list = [jnp.stack(small_grads[n]) for n in SMALL]
    packed = _pack(small_list, 16 * N_CHIPS).reshape(N_CHIPS, -1, 128)
    red = reduce_scatter([packed], F32, name="rs_small")
    full = gather_chips(red, [True], name="gather_small")[0].reshape(-1)
    small_red = dict(zip(SMALL, _unpack(full, small_list)))
    cw_cols = conv_w.shape[-1]
    dconv_full = small_red['conv_w'].reshape(depth, CONV_K, N_CHIPS, cw_cols)
    small_red['conv_w'] = lax.dynamic_index_in_dim(dconv_full, chip, axis=2, keepdims=False)

    grads = {}
    for n in BIG:
        grads[n] = jnp.stack(big_grads[n]).reshape(wts[n].shape)
    for n in SMALL:
        grads[n] = small_red[n].reshape(wts[n].shape)

    delta, new_m, new_v = {}, {}, {}
    for n in BIG:
        delta[n], new_m[n], new_v[n] = adamw(wts[n], grads[n], mom[n], var[n], name="adamw_big")
    pk = lambda dct: _pack([dct[n] for n in SMALL], 8).reshape(-1, 128)
    sd, sm, sv = adamw(pk(wts), pk(grads), pk(mom), pk(var), name="adamw_small")
    like = [wts[n] for n in SMALL]
    for n, a, b, c in zip(SMALL, _unpack(sd.reshape(-1), like), _unpack(sm.reshape(-1), like),
                          _unpack(sv.reshape(-1), like)):
        delta[n], new_m[n], new_v[n] = a, b, c

    return (loss, grad_x, *[grads[n] for n in WEIGHT_NAMES], *[delta[n] for n in WEIGHT_NAMES],
            *[new_m[n] for n in WEIGHT_NAMES], *[new_v[n] for n in WEIGHT_NAMES])
```

```python
import math

import jax
import jax.numpy as jnp
from jax import lax
from jax.experimental import pallas as pl
from jax.experimental.pallas import tpu as pltpu

F32 = jnp.float32
BF16 = jnp.bfloat16
HI = lax.Precision.HIGHEST
MESH = pl.DeviceIdType.MESH

N_CHIPS = 4
SSM_GROUP = 16
SSM_STATE = 64
GROUPS_PER_TILE = 8
HEAD_DIM = 128
CHUNK = 64
CONV_K = 4
LN_EPS = 1e-5
RMS_EPS = 1e-6
L2_EPS = 1e-6
SMALL_W = 512
ADAM_LR, ADAM_B1, ADAM_B2, ADAM_EPS, ADAM_WD, ADAM_STEP = 0.001, 0.9, 0.999, 1e-08, 0.01, 10
VMEM_LIMIT = 56 * 1024 * 1024

WEIGHT_NAMES = ['ffn1_w_gu', 'ffn1_w_down', 'ln1_g', 'ln1_b', 'w_in', 'conv_w', 'ssm_a_re', 'ssm_a_im', 'ssm_log_dt',
                'ssm_b_re', 'ssm_b_im', 'ssm_c_re', 'ssm_c_im', 'ssm_d', 'glu_w', 'glu_b', 'gdn_a_log', 'gdn_dt_bias',
                'gdn_norm_w', 'w_br_ssm', 'w_br_gdn', 'w_out', 'ln2_g', 'ln2_b', 'ffn2_w_gu', 'ffn2_w_down', 'ln3_g',
                'ln3_b']
BIG = ['ffn1_w_gu', 'ffn1_w_down', 'w_in', 'glu_w', 'w_br_ssm', 'w_br_gdn', 'w_out', 'ffn2_w_gu', 'ffn2_w_down']
SMALL = [n for n in WEIGHT_NAMES if n not in BIG]


def _tile(dim, prefs):
    for p in prefs:
        if dim % p == 0:
            return p
    return dim


def _cparams(sem):
    return pltpu.CompilerParams(dimension_semantics=sem, vmem_limit_bytes=VMEM_LIMIT)


def _ln(r, g, b):
    mu = jnp.mean(r, axis=-1, keepdims=True)
    xc = r - mu
    var = jnp.mean(xc * xc, axis=-1, keepdims=True)
    return xc * lax.rsqrt(var + LN_EPS) * g + b


def _lshape(x, nb):
    return (x.shape[0], x.shape[1]) if nb == 1 else (x.shape[1], x.shape[2] * nb)


def _cb_spec(x, nb, tr, tc, rc):
    if nb == 1:
        return pl.BlockSpec((tr, tc), lambda *g: rc(*g))
    cps = x.shape[2] // tc

    def imap(*g):
        r, c = rc(*g)
        return (c // cps, r, c % cps)
    return pl.BlockSpec((None, tr, tc), imap)


MM_VMEM_BUDGET = 40 * 1024 * 1024
MM_TILES = (2048, 1024, 512, 256, 128)
MM_FULL_K = 2048


def mm(a, b, *, name, ta=False, tb=False, a_nb=1, b_nb=1, out_nb=1, out_dtype=F32, add=None, add_scale=1.0,
       out_scale=1.0):
    ar, ac = _lshape(a, a_nb)
    br, bc = _lshape(b, b_nb)
    m, k = (ac, ar) if ta else (ar, ac)
    k2, n = (bc, br) if tb else (br, bc)
    assert k == k2, (name, a.shape, b.shape)
    assert a.dtype == BF16 and b.dtype == BF16, name

    def lim(dim, *nbs):
        q = dim
        for nb in nbs:
            q = math.gcd(q, dim // nb)
        return q
    lm = lim(m, a_nb if ta else 1)
    ln = lim(n, out_nb, 1 if tb else b_nb)
    lk = lim(k, 1 if ta else a_nb, b_nb if tb else 1)
    tk = k if (k <= MM_FULL_K and lk == k) else _tile(lk, (512, 256, 128))
    nk = k // tk
    so = jnp.dtype(out_dtype).itemsize
    best = None
    for cm in MM_TILES:
        for cn in MM_TILES:
            if lm % cm or ln % cn:
                continue
            est = 2 * (cm * tk * 2 + tk * cn * 2 + cm * cn * so + (cm * cn * 4 if add is not None else 0))
            est += cm * cn * 4 * (2 if nk > 1 else 1) + (cm * tk * 2 + 512 * cm * 4 if ta else 0)
            if est <= MM_VMEM_BUDGET and (best is None or cm * cn > best[0] * best[1]):
                best = (cm, cn)
    tm, tn = best
    dn = (((1,), (1 if tb else 0,)), ((), ()))

    def body(*refs):
        a_ref, b_ref = refs[:2]
        add_ref = refs[2] if add is not None else None
        o_ref = refs[3 if add is not None else 2]
        scratch = refs[(4 if add is not None else 3):]
        acc = scratch[0] if nk > 1 else None
        kk = pl.program_id(2)

        if ta:
            at_ref = scratch[-1]

            def transpose_block():
                for r0 in range(0, tk, 512):
                    r1 = min(tk, r0 + 512)
                    at_ref[:, r0:r1] = a_ref[r0:r1, :].astype(F32).T.astype(BF16)
            if nk == 1:
                pl.when(pl.program_id(1) == 0)(transpose_block)
            else:
                transpose_block()
            av = at_ref[...]
        else:
            av = a_ref[...]
        part = lax.dot_general(av, b_ref[...], dn, preferred_element_type=F32)

        def finish(r):
            if out_scale != 1.0:
                r = r * out_scale
            if add is not None:
                r = r + add_scale * add_ref[...]
            o_ref[...] = r.astype(out_dtype)

        if nk == 1:
            finish(part)
        else:
            @pl.when(kk == 0)
            def _():
                acc[...] = part

            @pl.when(kk > 0)
            def _():
                acc[...] += part

            @pl.when(kk == nk - 1)
            def _():
                finish(acc[...])

    if ta:
        a_spec = _cb_spec(a, a_nb, tk, tm, lambda i, j, kk: (kk, i))
    else:
        a_spec = _cb_spec(a, a_nb, tm, tk, lambda i, j, kk: (i, kk))
    if tb:
        b_spec = _cb_spec(b, b_nb, tn, tk, lambda i, j, kk: (j, kk))
    else:
        b_spec = _cb_spec(b, b_nb, tk, tn, lambda i, j, kk: (kk, j))
    if out_nb == 1:
        out_shape = jax.ShapeDtypeStruct((m, n), out_dtype)
        out_spec = pl.BlockSpec((tm, tn), lambda i, j, kk: (i, j))
    else:
        out_shape = jax.ShapeDtypeStruct((out_nb, m, n // out_nb), out_dtype)
        out_spec = _cb_spec(out_shape, out_nb, tm, tn, lambda i, j, kk: (i, j))
    in_specs = [a_spec, b_spec]
    args = [a, b]
    if add is not None:
        in_specs.append(pl.BlockSpec((tm, tn), lambda i, j, kk: (i, j)))
        args.append(add)
    scratch = ([pltpu.VMEM((tm, tn), F32)] if nk > 1 else []) + ([pltpu.VMEM((tm, tk), BF16)] if ta else [])
    return pl.pallas_call(
        body, name=name, out_shape=out_shape, grid=(m // tm, n // tn, nk), in_specs=in_specs, out_specs=out_spec,
        scratch_shapes=scratch, compiler_params=_cparams(("parallel", "arbitrary", "arbitrary")))(*args)


def down_res_ln(src, w, x, g, b, *, swiglu, alpha, scale, name):
    n_tok, d = x.shape
    kdim = w.shape[0]
    tm = _tile(n_tok, (256, 128))
    tk = _tile(kdim, (512, 256, 128))
    nk = kdim // tk

    def body(s_ref, w_ref, x_ref, g_ref, b_ref, r_ref, y_ref, yb_ref, acc):
        kk = pl.program_id(1)

        @pl.when(kk == 0)
        def _():
            acc[...] = jnp.zeros_like(acc)

        if swiglu:
            gate = s_ref[0]
            a = gate * jax.nn.sigmoid(gate) * s_ref[1]
        else:
            a = s_ref[...]
        acc[...] += jnp.dot(a.astype(BF16), w_ref[...], preferred_element_type=F32)

        @pl.when(kk == nk - 1)
        def _():
            r = alpha * x_ref[...] + scale * acc[...]
            r_ref[...] = r
            y = _ln(r, g_ref[...], b_ref[...])
            y_ref[...] = y
            yb_ref[...] = y.astype(BF16)

    if swiglu:
        s_spec = pl.BlockSpec((2, tm, tk), lambda i, kk: (0, i, kk))
    else:
        s_spec = pl.BlockSpec((tm, tk), lambda i, kk: (i, kk))
    row = pl.BlockSpec((tm, d), lambda i, kk: (i, 0))
    vec = pl.BlockSpec((1, d), lambda i, kk: (0, 0))
    return pl.pallas_call(
        body, name=name, out_shape=[jax.ShapeDtypeStruct((n_tok, d), F32)] * 2 + [jax.ShapeDtypeStruct((n_tok, d), BF16)],
        grid=(n_tok // tm, nk),
        in_specs=[s_spec, pl.BlockSpec((tk, d), lambda i, kk: (kk, 0)), row, vec, vec], out_specs=[row, row, row],
        scratch_shapes=[pltpu.VMEM((tm, d), F32)], compiler_params=_cparams(("parallel", "arbitrary")))(src, w, x, g, b)


def ln_bwd(r, g, dy, *, name):
    n_tok, d = r.shape
    tm = _tile(n_tok, (256, 128))

    def body(r_ref, g_ref, dy_ref, dr_ref, drb_ref, dg_ref, db_ref):
        i = pl.program_id(0)

        @pl.when(i == 0)
        def _():
            dg_ref[...] = jnp.zeros_like(dg_ref)
            db_ref[...] = jnp.zeros_like(db_ref)

        rv = r_ref[...]
        dyv = dy_ref[...]
        mu = jnp.mean(rv, axis=-1, keepdims=True)
        xc = rv - mu
        rstd = lax.rsqrt(jnp.mean(xc * xc, axis=-1, keepdims=True) + LN_EPS)
        xh = xc * rstd
        dxh = dyv * g_ref[...]
        dr = rstd * (dxh - jnp.mean(dxh, axis=-1, keepdims=True) - xh * jnp.mean(dxh * xh, axis=-1, keepdims=True))
        dr_ref[...] = dr
        drb_ref[...] = dr.astype(BF16)
        dg_ref[...] += jnp.sum(dyv * xh, axis=0, keepdims=True)
        db_ref[...] += jnp.sum(dyv, axis=0, keepdims=True)

    row = pl.BlockSpec((tm, d), lambda i: (i, 0))
    vec = pl.BlockSpec((1, d), lambda i: (0, 0))
    return pl.pallas_call(
        body, name=name, out_shape=[jax.ShapeDtypeStruct((n_tok, d), F32), jax.ShapeDtypeStruct((n_tok, d), BF16),
                                    jax.ShapeDtypeStruct((1, d), F32), jax.ShapeDtypeStruct((1, d), F32)],
        grid=(n_tok // tm,), in_specs=[row, vec, row], out_specs=[row, row, vec, vec],
        compiler_params=_cparams(("arbitrary",)))(r, g, dy)


def loss_head(y, target, *, name):
    n_tok, d = y.shape
    tm = _tile(n_tok, (256, 128))

    def body(y_ref, t_ref, dy_ref, l_ref):
        i = pl.program_id(0)

        @pl.when(i == 0)
        def _():
            l_ref[...] = jnp.zeros_like(l_ref)

        e = y_ref[...] - t_ref[...]
        dy_ref[...] = e * (1.0 / d)
        s = jnp.sum(jnp.mean(e * e, axis=-1, keepdims=True), axis=0, keepdims=True)
        l_ref[...] += 0.5 * s

    row = pl.BlockSpec((tm, d), lambda i: (i, 0))
    return pl.pallas_call(
        body, name=name, out_shape=[jax.ShapeDtypeStruct((n_tok, d), F32), jax.ShapeDtypeStruct((8, 128), F32)],
        grid=(n_tok // tm,), in_specs=[row, row], out_specs=[row, pl.BlockSpec((8, 128), lambda i: (0, 0))],
        compiler_params=_cparams(("arbitrary",)))(y, target)


def ffn_bwd_mid(dr, wd, h, *, scale, name):
    n_tok, d = dr.shape
    f = wd.shape[0]
    tm = _tile(n_tok, (256, 128))
    tf = _tile(f, (512, 256, 128))

    def body(dr_ref, w_ref, h_ref, dh_ref, a_ref):
        dy = (scale * dr_ref[...]).astype(BF16)
        da = lax.dot_general(dy, w_ref[...], (((1,), (1,)), ((), ())), preferred_element_type=F32)
        gate = h_ref[0]
        up = h_ref[1]
        sg = jax.nn.sigmoid(gate)
        s = gate * sg
        a_ref[...] = (s * up).astype(BF16)
        dh_ref[0] = (da * up * (sg * (1.0 + gate * (1.0 - sg)))).astype(BF16)
        dh_ref[1] = (da * s).astype(BF16)

    return pl.pallas_call(
        body, name=name, out_shape=[jax.ShapeDtypeStruct((2, n_tok, f), BF16), jax.ShapeDtypeStruct((n_tok, f), BF16)],
        grid=(n_tok // tm, f // tf),
        in_specs=[pl.BlockSpec((tm, d), lambda i, j: (i, 0)), pl.BlockSpec((tf, d), lambda i, j: (j, 0)),
                  pl.BlockSpec((2, tm, tf), lambda i, j: (0, i, j))],
        out_specs=[pl.BlockSpec((2, tm, tf), lambda i, j: (0, i, j)), pl.BlockSpec((tm, tf), lambda i, j: (i, j))],
        compiler_params=_cparams(("parallel", "parallel")))(dr, wd, h)


def _cmul(ar, ai, br, bi):
    return ar * br - ai * bi, ar * bi + ai * br


def _scan_blocks(sr_ref, si_ref, lr, li, *, reverse):
    n_rows, width = sr_ref.shape
    n_blk = n_rows // 8
    row = lax.broadcasted_iota(jnp.int32, (8, width), 0)
    pr = jnp.broadcast_to(lr, (8, width))
    pi = jnp.broadcast_to(-li if reverse else li, (8, width))

    def shifted(v, dist, fill=0.0):
        if reverse:
            return jnp.where(row < 8 - dist, pltpu.roll(v, 8 - dist, 0), fill)
        return jnp.where(row >= dist, pltpu.roll(v, dist, 0), fill)

    p1 = (pr, pi)
    p2 = _cmul(*p1, *p1)
    p4 = _cmul(*p2, *p2)
    wr, wi = pr, pi
    for dist in (1, 2, 4):
        wr, wi = _cmul(wr, wi, shifted(wr, dist, 1.0), shifted(wi, dist, 0.0))
    edge = 0 if reverse else 7

    def step(i, carry):
        cr, ci = carry
        blk = (n_blk - 1 - i) if reverse else i
        r0 = pl.multiple_of(blk * 8, 8)
        xr = sr_ref[pl.ds(r0, 8), :]
        xi = si_ref[pl.ds(r0, 8), :]
        for dist, (qr, qi) in ((1, p1), (2, p2), (4, p4)):
            tr, ti = _cmul(qr, qi, shifted(xr, dist), shifted(xi, dist))
            xr, xi = xr + tr, xi + ti
        tr, ti = _cmul(wr, wi, cr, ci)
        xr, xi = xr + tr, xi + ti
        sr_ref[pl.ds(r0, 8), :] = xr
        si_ref[pl.ds(r0, 8), :] = xi
        br = jnp.where(row == edge, xr, 0.0)
        bi = jnp.where(row == edge, xi, 0.0)
        for dist in (1, 2, 4):
            br = br + pltpu.roll(br, dist, 0)
            bi = bi + pltpu.roll(bi, dist, 0)
        return br, bi

    zero = jnp.zeros((8, width), F32)
    lax.fori_loop(0, n_blk, step, (zero, zero), unroll=2)


def _s5_specs(n_tok, u_blk0):
    gw = GROUPS_PER_TILE * SSM_GROUP
    sw = GROUPS_PER_TILE * SSM_STATE
    u_spec = pl.BlockSpec((n_tok, gw), lambda t: (0, u_blk0 + t))
    col = pl.BlockSpec((n_tok, gw), lambda t: (0, t))
    bmat = pl.BlockSpec((None, gw, sw), lambda t: (t, 0, 0))
    cmat = pl.BlockSpec((None, sw, gw), lambda t: (t, 0, 0))
    lvec = pl.BlockSpec((1, sw), lambda t: (0, t))
    dvec = pl.BlockSpec((1, gw), lambda t: (0, t))
    return gw, sw, u_spec, col, bmat, cmat, lvec, dvec


def s5_fwd(proj, u_col0, bblk_r, bblk_i, cblk_r, cblk_i, lbar_r, lbar_i, dskip, *, name):
    n_tok = proj.shape[0]
    n_tiles = bblk_r.shape[0]
    gw, sw, u_spec, col, bmat, cmat, lvec, dvec = _s5_specs(n_tok, u_col0 // (GROUPS_PER_TILE * SSM_GROUP))

    def body(u_ref, br_ref, bi_ref, cr_ref, ci_ref, lr_ref, li_ref, d_ref, ypre_ref, y2_ref, y2b_ref, sr, si):
        u = u_ref[...]
        ub = u.astype(BF16)
        sr[...] = jnp.dot(ub, br_ref[...].astype(BF16), preferred_element_type=F32)
        si[...] = jnp.dot(ub, bi_ref[...].astype(BF16), preferred_element_type=F32)
        _scan_blocks(sr, si, lr_ref[...], li_ref[...], reverse=False)
        y = (jnp.dot(sr[...].astype(BF16), cr_ref[...].astype(BF16), preferred_element_type=F32)
             - jnp.dot(si[...].astype(BF16), ci_ref[...].astype(BF16), preferred_element_type=F32)
             + d_ref[...] * u)
        ypre_ref[...] = y
        y2 = jax.nn.gelu(y)
        y2_ref[...] = y2
        y2b_ref[...] = y2.astype(BF16)

    width = n_tiles * gw
    return pl.pallas_call(
        body, name=name,
        out_shape=[jax.ShapeDtypeStruct((n_tok, width), F32)] * 2 + [jax.ShapeDtypeStruct((n_tok, width), BF16)],
        grid=(n_tiles,), in_specs=[u_spec, bmat, bmat, cmat, cmat, lvec, lvec, dvec], out_specs=[col, col, col],
        scratch_shapes=[pltpu.VMEM((n_tok, sw), F32)] * 2,
        compiler_params=_cparams(("parallel",)))(proj, bblk_r, bblk_i, cblk_r, cblk_i, lbar_r, lbar_i, dskip)


def s5_bwd(proj, u_col0, ypre, dy2, bblk_r, bblk_i, cblk_r, cblk_i, lbar_r, lbar_i, dskip, *, name):
    n_tok = proj.shape[0]
    n_tiles = bblk_r.shape[0]
    gw, sw, u_spec, col, bmat, cmat, lvec, dvec = _s5_specs(n_tok, u_col0 // (GROUPS_PER_TILE * SSM_GROUP))
    rb = _tile(n_tok, (512, 256, 128))
    tn_dims = (((0,), (0,)), ((), ()))
    nt_dims = (((1,), (1,)), ((), ()))

    def body(u_ref, ypre_ref, dy2_ref, br_ref, bi_ref, cr_ref, ci_ref, lr_ref, li_ref, d_ref,
             du_ref, dbr_ref, dbi_ref, dcr_ref, dci_ref, dlr_ref, dli_ref, dd_ref, sr, si, gr, gi):
        u = u_ref[...]
        ub = u.astype(BF16)
        bmr = br_ref[...].astype(BF16)
        bmi = bi_ref[...].astype(BF16)
        cmr = cr_ref[...].astype(BF16)
        cmi = ci_ref[...].astype(BF16)
        lr = lr_ref[...]
        li = li_ref[...]
        _, gelu_vjp = jax.vjp(jax.nn.gelu, ypre_ref[...])
        dyp = gelu_vjp(dy2_ref[...])[0]
        dyb = dyp.astype(BF16)
        sr[...] = jnp.dot(ub, bmr, preferred_element_type=F32)
        si[...] = jnp.dot(ub, bmi, preferred_element_type=F32)
        _scan_blocks(sr, si, lr, li, reverse=False)
        gr[...] = lax.dot_general(dyb, cmr, nt_dims, preferred_element_type=F32)
        gi[...] = -lax.dot_general(dyb, cmi, nt_dims, preferred_element_type=F32)
        _scan_blocks(gr, gi, lr, li, reverse=True)
        srb = sr[...].astype(BF16)
        sib = si[...].astype(BF16)
        dcr_ref[...] = lax.dot_general(srb, dyb, tn_dims, preferred_element_type=F32)
        dci_ref[...] = -lax.dot_general(sib, dyb, tn_dims, preferred_element_type=F32)
        grb = gr[...].astype(BF16)
        gib = gi[...].astype(BF16)
        dbr_ref[...] = lax.dot_general(ub, grb, tn_dims, preferred_element_type=F32)
        dbi_ref[...] = lax.dot_general(ub, gib, tn_dims, preferred_element_type=F32)
        du_ref[...] = (lax.dot_general(grb, bmr, nt_dims, preferred_element_type=F32)
                       + lax.dot_general(gib, bmi, nt_dims, preferred_element_type=F32)
                       + d_ref[...] * dyp).astype(du_ref.dtype)
        dd_ref[...] = jnp.sum(dyp * u, axis=0, keepdims=True)
        inv = 1.0 / (lr * lr + li * li)
        qr = lr * inv
        qi = -li * inv
        acc_r = jnp.zeros((1, sw), F32)
        acc_i = jnp.zeros((1, sw), F32)
        for blk in range(n_tok // rb):
            rows = pl.ds(blk * rb, rb)
            ubb = u_ref[rows, :].astype(BF16)
            er = sr[rows, :] - jnp.dot(ubb, bmr, preferred_element_type=F32)
            ei = si[rows, :] - jnp.dot(ubb, bmi, preferred_element_type=F32)
            pr, pi = _cmul(er, ei, qr, qi)
            ar = gr[rows, :]
            ai = gi[rows, :]
            acc_r = acc_r + jnp.sum(ar * pr + ai * pi, axis=0, keepdims=True)
            acc_i = acc_i + jnp.sum(ai * pr - ar * pi, axis=0, keepdims=True)
        dlr_ref[...] = acc_r
        dli_ref[...] = acc_i

    width = n_tiles * gw
    out_shape = [jax.ShapeDtypeStruct((n_tok, width), BF16),
                 jax.ShapeDtypeStruct(bblk_r.shape, F32), jax.ShapeDtypeStruct(bblk_r.shape, F32),
                 jax.ShapeDtypeStruct(cblk_r.shape, F32), jax.ShapeDtypeStruct(cblk_r.shape, F32),
                 jax.ShapeDtypeStruct(lbar_r.shape, F32), jax.ShapeDtypeStruct(lbar_r.shape, F32),
                 jax.ShapeDtypeStruct(dskip.shape, F32)]
    return pl.pallas_call(
        body, name=name, out_shape=out_shape, grid=(n_tiles,),
        in_specs=[u_spec, col, col, bmat, bmat, cmat, cmat, lvec, lvec, dvec],
        out_specs=[col, bmat, bmat, cmat, cmat, lvec, lvec, dvec],
        scratch_shapes=[pltpu.VMEM((n_tok, sw), F32)] * 4,
        compiler_params=_cparams(("parallel",)))(proj, ypre, dy2, bblk_r, bblk_i, cblk_r, cblk_i, lbar_r, lbar_i, dskip)


CONV_ROWS = 256
CONV_COLS = 512


def _conv_pre(x_ref, w_ref, blk, n_blk):
    r0 = blk * CONV_ROWS
    if blk == 0:
        ext = jnp.concatenate([jnp.zeros((8, CONV_COLS), F32), x_ref[0:CONV_ROWS, :]], axis=0)
    else:
        ext = x_ref[r0 - 8:r0 + CONV_ROWS, :]
    taps = []
    c = None
    for j in range(CONV_K):
        s = CONV_K - 1 - j
        xs = ext[8:] if s == 0 else pltpu.roll(ext, s, 0)[8:]
        taps.append(xs)
        term = w_ref[j:j + 1, :] * xs
        c = term if c is None else c + term
    return c, taps


def conv_fwd(proj, col0, conv_w, *, name):
    n_tok = proj.shape[0]
    width = conv_w.shape[1]
    n_blk = n_tok // CONV_ROWS
    cb0 = col0 // CONV_COLS

    def body(x_ref, w_ref, o_ref):
        for blk in range(n_blk):
            c, _ = _conv_pre(x_ref, w_ref, blk, n_blk)
            o_ref[blk * CONV_ROWS:(blk + 1) * CONV_ROWS, :] = c * jax.nn.sigmoid(c)

    return pl.pallas_call(
        body, name=name, out_shape=jax.ShapeDtypeStruct((n_tok, width), F32), grid=(width // CONV_COLS,),
        in_specs=[pl.BlockSpec((n_tok, CONV_COLS), lambda j: (0, cb0 + j)),
                  pl.BlockSpec((CONV_K, CONV_COLS), lambda j: (0, j))],
        out_specs=pl.BlockSpec((n_tok, CONV_COLS), lambda j: (0, j)),
        compiler_params=_cparams(("parallel",)))(proj, conv_w)


def conv_bwd(proj, col0, conv_w, dout, *, name):
    n_tok = proj.shape[0]
    width = conv_w.shape[1]
    n_blk = n_tok // CONV_ROWS
    cb0 = col0 // CONV_COLS

    def body(x_ref, w_ref, do_ref, dx_ref, dw_ref, dc):
        dws = [jnp.zeros((1, CONV_COLS), F32) for _ in range(CONV_K)]
        for blk in range(n_blk):
            rows = slice(blk * CONV_ROWS, (blk + 1) * CONV_ROWS)
            c, taps = _conv_pre(x_ref, w_ref, blk, n_blk)
            sg = jax.nn.sigmoid(c)
            dcv = do_ref[rows, :] * (sg * (1.0 + c * (1.0 - sg)))
            dc[rows, :] = dcv
            for j in range(CONV_K):
                dws[j] = dws[j] + jnp.sum(dcv * taps[j], axis=0, keepdims=True)
        dc[n_tok:n_tok + 8, :] = jnp.zeros((8, CONV_COLS), F32)
        for j in range(CONV_K):
            dw_ref[j:j + 1, :] = dws[j]
        for blk in range(n_blk):
            r0 = blk * CONV_ROWS
            ext = dc[r0:r0 + CONV_ROWS + 8, :]
            dx = None
            for j in range(CONV_K):
                s = CONV_K - 1 - j
                sh = ext[:CONV_ROWS] if s == 0 else pltpu.roll(ext, CONV_ROWS + 8 - s, 0)[:CONV_ROWS]
                term = w_ref[j:j + 1, :] * sh
                dx = term if dx is None else dx + term
            dx_ref[r0:r0 + CONV_ROWS, :] = dx.astype(dx_ref.dtype)

    return pl.pallas_call(
        body, name=name, out_shape=[jax.ShapeDtypeStruct((n_tok, width), BF16), jax.ShapeDtypeStruct(conv_w.shape, F32)],
        grid=(width // CONV_COLS,),
        in_specs=[pl.BlockSpec((n_tok, CONV_COLS), lambda j: (0, cb0 + j)),
                  pl.BlockSpec((CONV_K, CONV_COLS), lambda j: (0, j)),
                  pl.BlockSpec((n_tok, CONV_COLS), lambda j: (0, j))],
        out_specs=[pl.BlockSpec((n_tok, CONV_COLS), lambda j: (0, j)), pl.BlockSpec((CONV_K, CONV_COLS), lambda j: (0, j))],
        scratch_shapes=[pltpu.VMEM((n_tok + 8, CONV_COLS), F32)],
        compiler_params=_cparams(("parallel",)))(proj, conv_w, dout)


def _gdn_chunk(head, n_heads, state, q, k, v, z, bsmall, alog_row, dtb_row, nw):
    c = CHUNK
    lane = lax.broadcasted_iota(jnp.int32, (c, HEAD_DIM), 1)
    lane1 = lax.broadcasted_iota(jnp.int32, (1, HEAD_DIM), 1)
    ri = lax.broadcasted_iota(jnp.int32, (c, c), 0)
    ci = lax.broadcasted_iota(jnp.int32, (c, c), 1)
    causal = ri >= ci
    strict = ri > ci
    tril = causal.astype(F32)
    triu = (ri <= ci).astype(F32)
    eye = (ri == ci).astype(F32)
    bl = jnp.sum(jnp.where(lane == head, bsmall, 0.0), axis=-1, keepdims=True)
    al = jnp.sum(jnp.where(lane == n_heads + head, bsmall, 0.0), axis=-1, keepdims=True)
    alog = jnp.sum(jnp.where(lane1 == head, alog_row, 0.0), axis=-1, keepdims=True)
    dtb = jnp.sum(jnp.where(lane1 == head, dtb_row, 0.0), axis=-1, keepdims=True)

    qn = q * lax.rsqrt(jnp.sum(q * q, axis=-1, keepdims=True) + L2_EPS) * (HEAD_DIM ** -0.5)
    kn = k * lax.rsqrt(jnp.sum(k * k, axis=-1, keepdims=True) + L2_EPS)
    beta = jax.nn.sigmoid(bl)
    xg = al + dtb
    g = -jnp.exp(alog) * (jnp.maximum(xg, 0.0) + jnp.log(1.0 + jnp.exp(-jnp.abs(xg))))
    g_sq = jnp.broadcast_to(g, (c, c))
    g_wide = jnp.broadcast_to(g, (c, HEAD_DIM))
    gc_rows = jnp.dot(tril, g_sq, precision=HI, preferred_element_type=F32)
    gc_cols = lax.dot_general(g_sq, triu, (((0,), (0,)), ((), ())), precision=HI, preferred_element_type=F32)
    gc = jnp.dot(tril, g_wide, precision=HI, preferred_element_type=F32)
    g_tot = jnp.sum(g, axis=0, keepdims=True)
    decay = jnp.exp(jnp.where(causal, gc_rows - gc_cols, -1e30))
    egc = jnp.exp(gc)
    kb = kn * beta
    knb = kn.astype(BF16)
    nt = (((1,), (1,)), ((), ()))
    lower = jnp.where(strict, lax.dot_general(kb.astype(BF16), knb, nt, preferred_element_type=F32) * decay, 0.0)
    xp = -lower
    tinv = eye + xp
    for _ in range(5):
        xp = jnp.dot(xp, xp, precision=HI, preferred_element_type=F32)
        tinv = tinv + jnp.dot(tinv, xp, precision=HI, preferred_element_type=F32)
    u_val = jnp.dot(tinv, v * beta, precision=HI, preferred_element_type=F32)
    w_key = jnp.dot(tinv, kb * egc, precision=HI, preferred_element_type=F32)
    attn = lax.dot_general(qn.astype(BF16), knb, nt, preferred_element_type=F32) * decay
    q_dec = qn * egc
    k_dec = kn * jnp.exp(g_tot - gc)
    sb = state.astype(BF16)
    v_new = u_val - jnp.dot(w_key.astype(BF16), sb, preferred_element_type=F32)
    vnb = v_new.astype(BF16)
    o = (jnp.dot(q_dec.astype(BF16), sb, preferred_element_type=F32)
         + jnp.dot(attn.astype(BF16), vnb, preferred_element_type=F32))
    new_state = state * jnp.exp(g_tot) + lax.dot_general(k_dec.astype(BF16), vnb, (((0,), (0,)), ((), ())),
                                                         preferred_element_type=F32)
    o = o * lax.rsqrt(jnp.mean(o * o, axis=-1, keepdims=True) + RMS_EPS) * nw
    o = o * (z * jax.nn.sigmoid(z))
    return o, new_state


def _gdn_in_specs(n_heads, qkv_width_blocks, z_blk, small_blk, rev, n_chunks):
    w = n_heads * HEAD_DIM

    def cidx(i):
        return (n_chunks - 1 - i) if rev else i
    qs = pl.BlockSpec((CHUNK, w), lambda i: (cidx(i), 0))
    ks = pl.BlockSpec((CHUNK, w), lambda i: (cidx(i), 1))
    vs = pl.BlockSpec((CHUNK, w), lambda i: (cidx(i), 2))
    zs = pl.BlockSpec((CHUNK, w), lambda i: (cidx(i), z_blk))
    bs = pl.BlockSpec((CHUNK, HEAD_DIM), lambda i: (cidx(i), small_blk))
    pv = pl.BlockSpec((1, HEAD_DIM), lambda i: (0, 0))
    return cidx, qs, ks, vs, zs, bs, pv


def gdn_fwd(qkv, proj, z_col0, small_col0, alog_row, dtb_row, nw_row, n_heads, *, name):
    n_tok = qkv.shape[0]
    w = n_heads * HEAD_DIM
    n_chunks = n_tok // CHUNK
    cidx, qs, ks, vs, zs, bs, pv = _gdn_in_specs(n_heads, 3, z_col0 // w, small_col0 // HEAD_DIM, False, n_chunks)

    def body(q_ref, k_ref, v_ref, z_ref, b_ref, al_ref, dt_ref, nw_ref, o_ref, s_ref, state):
        @pl.when(pl.program_id(0) == 0)
        def _():
            state[...] = jnp.zeros_like(state)

        bsm = b_ref[...]
        for h in range(n_heads):
            cols = slice(h * HEAD_DIM, (h + 1) * HEAD_DIM)
            st = state[h]
            s_ref[h] = st
            o, ns = _gdn_chunk(h, n_heads, st, q_ref[:, cols], k_ref[:, cols], v_ref[:, cols], z_ref[:, cols], bsm,
                               al_ref[...], dt_ref[...], nw_ref[...])
            o_ref[:, cols] = o.astype(BF16)
            state[h] = ns

    return pl.pallas_call(
        body, name=name,
        out_shape=[jax.ShapeDtypeStruct((n_tok, w), BF16),
                   jax.ShapeDtypeStruct((n_chunks, n_heads, HEAD_DIM, HEAD_DIM), F32)],
        grid=(n_chunks,), in_specs=[qs, ks, vs, zs, bs, pv, pv, pv],
        out_specs=[pl.BlockSpec((CHUNK, w), lambda i: (i, 0)),
                   pl.BlockSpec((None, n_heads, HEAD_DIM, HEAD_DIM), lambda i: (i, 0, 0, 0))],
        scratch_shapes=[pltpu.VMEM((n_heads, HEAD_DIM, HEAD_DIM), F32)],
        compiler_params=_cparams(("arbitrary",)))(qkv, qkv, qkv, proj, proj, alog_row, dtb_row, nw_row)


def gdn_bwd(qkv, proj, z_col0, small_col0, alog_row, dtb_row, nw_row, states, dout, n_heads, *, name):
    n_tok = qkv.shape[0]
    w = n_heads * HEAD_DIM
    n_chunks = n_tok // CHUNK
    cidx, qs, ks, vs, zs, bs, pv = _gdn_in_specs(n_heads, 3, z_col0 // w, small_col0 // HEAD_DIM, True, n_chunks)

    def body(q_ref, k_ref, v_ref, z_ref, b_ref, al_ref, dt_ref, nw_ref, s_ref, do_ref,
             dqkv_ref, dz_ref, db_ref, dal_ref, ddt_ref, dnw_ref, dstate):
        @pl.when(pl.program_id(0) == 0)
        def _():
            dstate[...] = jnp.zeros_like(dstate)
            dal_ref[...] = jnp.zeros_like(dal_ref)
            ddt_ref[...] = jnp.zeros_like(ddt_ref)
            dnw_ref[...] = jnp.zeros_like(dnw_ref)

        bsm = b_ref[...]
        dbs = jnp.zeros((CHUNK, HEAD_DIM), F32)
        dal = jnp.zeros((1, HEAD_DIM), F32)
        ddt = jnp.zeros((1, HEAD_DIM), F32)
        dnw = jnp.zeros((1, HEAD_DIM), F32)
        for h in range(n_heads):
            cols = slice(h * HEAD_DIM, (h + 1) * HEAD_DIM)

            def f(st, q, k, v, z, bb, al, dt, nw, h=h):
                return _gdn_chunk(h, n_heads, st, q, k, v, z, bb, al, dt, nw)
            _, vjp = jax.vjp(f, s_ref[h], q_ref[:, cols], k_ref[:, cols], v_ref[:, cols], z_ref[:, cols], bsm,
                             al_ref[...], dt_ref[...], nw_ref[...])
            dst, dq, dk, dv, dz, dbb, da, dd, dn = vjp((do_ref[:, cols], dstate[h]))
            dstate[h] = dst
            dqkv_ref[:, h * HEAD_DIM:(h + 1) * HEAD_DIM] = dq
            dqkv_ref[:, w + h * HEAD_DIM:w + (h + 1) * HEAD_DIM] = dk
            dqkv_ref[:, 2 * w + h * HEAD_DIM:2 * w + (h + 1) * HEAD_DIM] = dv
            dz_ref[:, cols] = dz.astype(dz_ref.dtype)
            dbs = dbs + dbb
            dal = dal + da
            ddt = ddt + dd
            dnw = dnw + dn
        db_ref[...] = dbs.astype(db_ref.dtype)
        dal_ref[...] += dal
        ddt_ref[...] += ddt
        dnw_ref[...] += dnw

    rowblk = pl.BlockSpec((CHUNK, w), lambda i: (cidx(i), 0))
    return pl.pallas_call(
        body, name=name,
        out_shape=[
            jax.ShapeDtypeStruct((n_tok, 3 * w), F32),
            jax.ShapeDtypeStruct((n_tok, w), BF16), jax.ShapeDtypeStruct((n_tok, HEAD_DIM), BF16),
            jax.ShapeDtypeStruct((1, HEAD_DIM), F32), jax.ShapeDtypeStruct((1, HEAD_DIM), F32),
            jax.ShapeDtypeStruct((1, HEAD_DIM), F32)],
        grid=(n_chunks,),
        in_specs=[qs, ks, vs, zs, bs, pv, pv, pv,
                  pl.BlockSpec((None, n_heads, HEAD_DIM, HEAD_DIM), lambda i: (cidx(i), 0, 0, 0)), rowblk],
        out_specs=[pl.BlockSpec((CHUNK, 3 * w), lambda i: (cidx(i), 0)), rowblk,
                   pl.BlockSpec((CHUNK, HEAD_DIM), lambda i: (cidx(i), 0)), pv, pv, pv],
        scratch_shapes=[pltpu.VMEM((n_heads, HEAD_DIM, HEAD_DIM), F32)],
        compiler_params=_cparams(("arbitrary",)))(qkv, qkv, qkv, proj, proj, alog_row, dtb_row, nw_row, states, dout)


def glu_gate_fwd(y2, gl, bias, *, name):
    n_tok, w = y2.shape
    tm = _tile(n_tok, (256, 128))

    def body(y_ref, g_ref, b_ref, o_ref):
        o_ref[...] = (y_ref[...] * jax.nn.sigmoid(g_ref[...] + b_ref[...])).astype(BF16)

    row = pl.BlockSpec((tm, w), lambda i: (i, 0))
    vec = pl.BlockSpec((1, w), lambda i: (0, 0))
    return pl.pallas_call(body, name=name, out_shape=jax.ShapeDtypeStruct((n_tok, w), BF16), grid=(n_tok // tm,),
                          in_specs=[row, row, vec], out_specs=row, compiler_params=_cparams(("parallel",)))(y2, gl, bias)


def glu_gate_bwd(y2, gl, bias, dys, *, name):
    n_tok, w = y2.shape
    tm = _tile(n_tok, (256, 128))

    def body(y_ref, g_ref, b_ref, d_ref, dy_ref, dg_ref, db_ref):
        @pl.when(pl.program_id(0) == 0)
        def _():
            db_ref[...] = jnp.zeros_like(db_ref)

        sg = jax.nn.sigmoid(g_ref[...] + b_ref[...])
        d = d_ref[...]
        dy_ref[...] = d * sg
        dg = d * y_ref[...] * sg * (1.0 - sg)
        dg_ref[...] = dg.astype(dg_ref.dtype)
        db_ref[...] += jnp.sum(dg, axis=0, keepdims=True)

    row = pl.BlockSpec((tm, w), lambda i: (i, 0))
    vec = pl.BlockSpec((1, w), lambda i: (0, 0))
    return pl.pallas_call(
        body, name=name, out_shape=[jax.ShapeDtypeStruct((n_tok, w), F32), jax.ShapeDtypeStruct((n_tok, w), BF16),
                                    jax.ShapeDtypeStruct((1, w), F32)],
        grid=(n_tok // tm,), in_specs=[row, row, vec, row], out_specs=[row, row, vec],
        compiler_params=_cparams(("arbitrary",)))(y2, gl, bias, dys)


def merge_fwd(proj, bs, bd, *, name):
    n_tok, d = bs.shape
    tm = _tile(n_tok, (256, 128))

    def body(gs_ref, gd_ref, bs_ref, bd_ref, o_ref):
        o_ref[...] = (jax.nn.sigmoid(gs_ref[...]) * bs_ref[...]
                      + jax.nn.sigmoid(gd_ref[...]) * bd_ref[...]).astype(BF16)

    row = pl.BlockSpec((tm, d), lambda i: (i, 0))
    return pl.pallas_call(
        body, name=name, out_shape=jax.ShapeDtypeStruct((n_tok, d), BF16), grid=(n_tok // tm,),
        in_specs=[row, pl.BlockSpec((tm, d), lambda i: (i, 1)), row, row], out_specs=row,
        compiler_params=_cparams(("parallel",)))(proj, proj, bs, bd)


def merge_bwd(proj, bs, bd, dm, *, name):
    n_tok, d = bs.shape
    tm = _tile(n_tok, (256, 128))

    def body(gs_ref, gd_ref, bs_ref, bd_ref, dm_ref, dbs_ref, dbd_ref, dgs_ref, dgd_ref):
        dmv = dm_ref[...]
        ss = jax.nn.sigmoid(gs_ref[...])
        sd = jax.nn.sigmoid(gd_ref[...])
        dbs_ref[...] = (ss * dmv).astype(BF16)
        dbd_ref[...] = (sd * dmv).astype(BF16)
        dgs_ref[...] = (dmv * bs_ref[...] * ss * (1.0 - ss)).astype(BF16)
        dgd_ref[...] = (dmv * bd_ref[...] * sd * (1.0 - sd)).astype(BF16)

    row = pl.BlockSpec((tm, d), lambda i: (i, 0))
    return pl.pallas_call(
        body, name=name, out_shape=[jax.ShapeDtypeStruct((n_tok, d), BF16)] * 4, grid=(n_tok // tm,),
        in_specs=[row, pl.BlockSpec((tm, d), lambda i: (i, 1)), row, row, row], out_specs=[row] * 4,
        compiler_params=_cparams(("parallel",)))(proj, proj, bs, bd, dm)


def add_pairs(grads, recv, out_dtype, *, name):
    core = jnp.reshape(lax.axis_index("c"), (1,)).astype(jnp.int32)
    outs = []
    for t, (a, b) in enumerate(zip(grads, recv)):
        n_sh, h, cols = b.shape
        tr = _tile(h, (256, 128, 64, 32, 16))
        nh = h // tr

        def body(c_ref, a_ref, b_ref, o_ref):
            o_ref[...] = (a_ref[...].astype(F32) + b_ref[...].astype(F32)).astype(out_dtype)

        grid_spec = pltpu.PrefetchScalarGridSpec(
            num_scalar_prefetch=1, grid=(n_sh, nh),
            in_specs=[pl.BlockSpec((None, tr, cols), lambda s, i, c_ref, nh=nh: (s, c_ref[0] * nh + i, 0)),
                      pl.BlockSpec((None, tr, cols), lambda s, i, c_ref: (s, i, 0))],
            out_specs=pl.BlockSpec((None, tr, cols), lambda s, i, c_ref: (s, i, 0)))
        outs.append(pl.pallas_call(body, name=f"{name}_{t}", out_shape=jax.ShapeDtypeStruct(b.shape, out_dtype),
                                   grid_spec=grid_spec, compiler_params=_cparams(("parallel", "parallel")))(core, a, b))
    return outs


def add_chips(parts, *, name):
    outs = []
    for t, p in enumerate(parts):
        _, h, cols = p.shape
        tr = _tile(h, (256, 128, 64, 32, 16))

        def body(p0, p1, p2, p3, o_ref):
            o_ref[...] = ((p0[...].astype(F32) + p1[...].astype(F32)) + p2[...].astype(F32)) + p3[...].astype(F32)

        specs = [pl.BlockSpec((None, tr, cols), lambda i, s=s: (s, i, 0)) for s in range(N_CHIPS)]
        outs.append(pl.pallas_call(body, name=f"{name}_{t}", out_shape=jax.ShapeDtypeStruct((h, cols), F32),
                                   grid=(h // tr,), in_specs=specs, out_specs=pl.BlockSpec((tr, cols), lambda i: (i, 0)),
                                   compiler_params=_cparams(("parallel",)))(p, p, p, p))
    return outs


ADAMW_BLOCK_BYTES = 3 * 512 * 1024


def adamw(w, g, m, v, *, name):
    shape = w.shape
    cols = shape[-1]
    rows = w.size // cols
    tr = _tile(rows, tuple(t for t in (1024, 512, 256, 128, 64, 32, 16, 8) if t * cols * 4 <= ADAMW_BLOCK_BYTES))
    c1 = 1.0 / (1.0 - ADAM_B1 ** ADAM_STEP)
    c2 = 1.0 / (1.0 - ADAM_B2 ** ADAM_STEP)

    def body(w_ref, g_ref, m_ref, v_ref, d_ref, nm_ref, nv_ref):
        gv = g_ref[...]
        nm = ADAM_B1 * m_ref[...] + (1.0 - ADAM_B1) * gv
        nv = ADAM_B2 * v_ref[...] + (1.0 - ADAM_B2) * (gv * gv)
        d_ref[...] = -ADAM_LR * ((nm * c1) / (jnp.sqrt(nv * c2) + ADAM_EPS) + ADAM_WD * w_ref[...])
        nm_ref[...] = nm
        nv_ref[...] = nv

    blk = pl.BlockSpec((tr, cols), lambda i: (i, 0))
    outs = pl.pallas_call(body, name=name, out_shape=[jax.ShapeDtypeStruct((rows, cols), F32)] * 3, grid=(rows // tr,),
                          in_specs=[blk] * 4, out_specs=[blk] * 3, compiler_params=_cparams(("parallel",)))(
        w.reshape(rows, cols), g.reshape(rows, cols), m.reshape(rows, cols), v.reshape(rows, cols))
    return [o.reshape(shape) for o in outs]


def _place():
    return lax.axis_index("x"), lax.axis_index("y"), lax.axis_index("c")


def _other_chips(x, y):
    return [(1 - x, y), (x, 1 - y), (1 - x, 1 - y)]


ANY = pl.BlockSpec(memory_space=pl.ANY)
STAGE_BYTES = 1 << 20


def _stage_shape(rows, cols, dtype):
    mult = 32 // jnp.dtype(dtype).itemsize
    per_row = (-(-cols // 128) * 128) * jnp.dtype(dtype).itemsize
    chunk = max(mult, STAGE_BYTES // per_row // mult * mult)
    return pltpu.VMEM((2, min(chunk, rows), cols), dtype)


def _staged_copy(src, dst, buf, sem_in, sem_out, k):
    rows, chunk = src.shape[0], buf.shape[1]
    pending = []
    for i, r0 in enumerate(range(0, rows, chunk)):
        sz = min(chunk, rows - r0)
        slot = i % 2
        if i >= 2:
            pending[i - 2].wait()
        stage = buf.at[slot, pl.ds(0, sz)]
        cin = pltpu.make_async_copy(src.at[pl.ds(r0, sz)], stage, sem_in.at[2 * k + slot])
        cin.start()
        cin.wait()
        cout = pltpu.make_async_copy(stage, dst.at[pl.ds(r0, sz)], sem_out.at[2 * k + slot])
        cout.start()
        pending.append(cout)
    for cp in pending[max(0, len(pending) - 2):]:
        cp.wait()


def gather_chips(blocks, halve, *, name):
    n = len(blocks)

    def body(*refs):
        ins, outs = refs[:n], refs[n:2 * n]
        send_sems, recv_sems, fwd_send, fwd_recv, stage_in, stage_out = refs[2 * n:2 * n + 6]
        bufs = refs[2 * n + 6:]
        x, y, c = _place()
        me = 2 * x + y
        chips = _other_chips(x, y)
        sibling = (x, y, 1 - c)
        sends, fwds = [], []
        for t in range(n):
            for j, (px, py) in enumerate(chips):
                if halve[t]:
                    h = ins[t].shape[0] // 2
                    rows = pl.ds(c * h, h)
                    src, dst = ins[t].at[rows], outs[t].at[me, rows]
                else:
                    src, dst = ins[t], outs[t].at[me]
                cp = pltpu.make_async_remote_copy(src_ref=src, dst_ref=dst, send_sem=send_sems.at[3 * t + j],
                                                  recv_sem=recv_sems.at[3 * t + j], device_id=(px, py, c),
                                                  device_id_type=MESH)
                cp.start()
                sends.append(cp)
        for t in range(n):
            _staged_copy(ins[t], outs[t].at[me], bufs[t], stage_in, stage_out, t)
        for t in range(n):
            for j, (px, py) in enumerate(chips):
                src_chip = 2 * px + py
                if halve[t]:
                    h = ins[t].shape[0] // 2
                    rows = pl.ds(c * h, h)
                    landed = outs[t].at[src_chip, rows]
                    pltpu.make_async_remote_copy(src_ref=landed, dst_ref=landed, send_sem=send_sems.at[3 * t + j],
                                                 recv_sem=recv_sems.at[3 * t + j], device_id=(px, py, c),
                                                 device_id_type=MESH).wait_recv()
                    cp = pltpu.make_async_remote_copy(src_ref=landed, dst_ref=landed, send_sem=fwd_send.at[3 * t + j],
                                                      recv_sem=fwd_recv.at[3 * t + j], device_id=sibling,
                                                      device_id_type=MESH)
                    cp.start()
                    fwds.append(cp)
                else:
                    landed = outs[t].at[src_chip]
                    pltpu.make_async_remote_copy(src_ref=landed, dst_ref=landed, send_sem=send_sems.at[3 * t + j],
                                                 recv_sem=recv_sems.at[3 * t + j], device_id=(px, py, c),
                                                 device_id_type=MESH).wait_recv()
        for t in range(n):
            if not halve[t]:
                continue
            h = ins[t].shape[0] // 2
            for j, (px, py) in enumerate(chips):
                theirs = outs[t].at[2 * px + py, pl.ds((1 - c) * h, h)]
                pltpu.make_async_remote_copy(src_ref=theirs, dst_ref=theirs, send_sem=fwd_send.at[3 * t + j],
                                             recv_sem=fwd_recv.at[3 * t + j], device_id=sibling,
                                             device_id_type=MESH).wait_recv()
        for cp in sends + fwds:
            cp.wait_send()

    return pl.pallas_call(
        body, name=name, out_shape=[jax.ShapeDtypeStruct((N_CHIPS,) + b.shape, b.dtype) for b in blocks],
        in_specs=[ANY] * n, out_specs=[ANY] * n,
        scratch_shapes=[pltpu.SemaphoreType.DMA((3 * n,))] * 4 + [pltpu.SemaphoreType.DMA((2 * n,))] * 2
        + [_stage_shape(b.shape[0], b.shape[1], b.dtype) for b in blocks],
        compiler_params=pltpu.CompilerParams(has_side_effects=True, vmem_limit_bytes=VMEM_LIMIT))(*blocks)


def pair_split(grads, *, name):
    n = len(grads)

    def body(*refs):
        ins, recv = refs[:n], refs[n:2 * n]
        send_sems, recv_sems = refs[2 * n:]
        x, y, c = _place()
        sibling = (x, y, 1 - c)
        cps = []
        for t in range(n):
            h = ins[t].shape[1] // 2
            cp = pltpu.make_async_remote_copy(src_ref=ins[t].at[:, pl.ds((1 - c) * h, h)], dst_ref=recv[t],
                                              send_sem=send_sems.at[t], recv_sem=recv_sems.at[t], device_id=sibling,
                                              device_id_type=MESH)
            cp.start()
            cps.append(cp)
        for cp in cps:
            cp.wait()

    half = [jax.ShapeDtypeStruct((g.shape[0], g.shape[1] // 2, g.shape[2]), g.dtype) for g in grads]
    return pl.pallas_call(
        body, name=name, out_shape=half, in_specs=[ANY] * n, out_specs=[ANY] * n,
        scratch_shapes=[pltpu.SemaphoreType.DMA((n,))] * 2,
        compiler_params=pltpu.CompilerParams(has_side_effects=True))(*grads)


def chip_exchange(parts, *, name):
    n = len(parts)

    def body(*refs):
        ins, outs = refs[:n], refs[n:2 * n]
        send_sems, recv_sems, stage_in, stage_out = refs[2 * n:2 * n + 4]
        bufs = refs[2 * n + 4:]
        x, y, c = _place()
        me = 2 * x + y
        chips = _other_chips(x, y)
        cps = []
        for t in range(n):
            for j, (px, py) in enumerate(chips):
                cp = pltpu.make_async_remote_copy(src_ref=ins[t].at[2 * px + py], dst_ref=outs[t].at[me],
                                                  send_sem=send_sems.at[3 * t + j], recv_sem=recv_sems.at[3 * t + j],
                                                  device_id=(px, py, c), device_id_type=MESH)
                cp.start()
                cps.append(cp)
        for t in range(n):
            _staged_copy(ins[t].at[me], outs[t].at[me], bufs[t], stage_in, stage_out, t)
        for t in range(n):
            for j, (px, py) in enumerate(chips):
                landed = outs[t].at[2 * px + py]
                pltpu.make_async_remote_copy(src_ref=landed, dst_ref=landed, send_sem=send_sems.at[3 * t + j],
                                             recv_sem=recv_sems.at[3 * t + j], device_id=(px, py, c),
                                             device_id_type=MESH).wait_recv()
        for cp in cps:
            cp.wait_send()

    return pl.pallas_call(
        body, name=name, out_shape=[jax.ShapeDtypeStruct(p.shape, p.dtype) for p in parts],
        in_specs=[ANY] * n, out_specs=[ANY] * n,
        scratch_shapes=[pltpu.SemaphoreType.DMA((3 * n,))] * 2 + [pltpu.SemaphoreType.DMA((2 * n,))] * 2
        + [_stage_shape(p.shape[1], p.shape[2], p.dtype) for p in parts],
        compiler_params=pltpu.CompilerParams(has_side_effects=True, vmem_limit_bytes=VMEM_LIMIT))(*parts)


def pair_join(halves, *, name):
    n = len(halves)

    def body(*refs):
        ins, outs = refs[:n], refs[n:2 * n]
        send_sems, recv_sems, stage_in, stage_out = refs[2 * n:2 * n + 4]
        bufs = refs[2 * n + 4:]
        x, y, c = _place()
        sibling = (x, y, 1 - c)
        cps = []
        for t in range(n):
            h = ins[t].shape[0]
            cp = pltpu.make_async_remote_copy(src_ref=ins[t], dst_ref=outs[t].at[pl.ds(c * h, h)],
                                              send_sem=send_sems.at[t], recv_sem=recv_sems.at[t], device_id=sibling,
                                              device_id_type=MESH)
            cp.start()
            cps.append(cp)
        for t in range(n):
            h = ins[t].shape[0]
            _staged_copy(ins[t], outs[t].at[pl.ds(c * h, h)], bufs[t], stage_in, stage_out, t)
        for t in range(n):
            h = ins[t].shape[0]
            theirs = outs[t].at[pl.ds((1 - c) * h, h)]
            pltpu.make_async_remote_copy(src_ref=theirs, dst_ref=theirs, send_sem=send_sems.at[t],
                                         recv_sem=recv_sems.at[t], device_id=sibling, device_id_type=MESH).wait_recv()
        for cp in cps:
            cp.wait_send()

    return pl.pallas_call(
        body, name=name, out_shape=[jax.ShapeDtypeStruct((2 * p.shape[0], p.shape[1]), p.dtype) for p in halves],
        in_specs=[ANY] * n, out_specs=[ANY] * n,
        scratch_shapes=[pltpu.SemaphoreType.DMA((n,))] * 2 + [pltpu.SemaphoreType.DMA((2 * n,))] * 2
        + [_stage_shape(p.shape[0], p.shape[1], p.dtype) for p in halves],
        compiler_params=pltpu.CompilerParams(has_side_effects=True, vmem_limit_bytes=VMEM_LIMIT))(*halves)


def reduce_scatter(grads, pay_dtype, *, name):
    recv = pair_split(grads, name=name + "_split")
    part = add_pairs(grads, recv, pay_dtype, name=name + "_add2")
    got = chip_exchange(part, name=name + "_xchg")
    half = add_chips(got, name=name + "_add4")
    return pair_join(half, name=name + "_join")


def _in_sizes(d, w, n_heads):
    return (w, w, w, w, w, n_heads, n_heads, d, d)


def _wcat_from_gathered(wg, d, w, n_heads):
    full = jnp.concatenate([wg[s] for s in range(N_CHIPS)], axis=1)
    sizes = _in_sizes(d, w, n_heads)
    offs = [0]
    for s in sizes:
        offs.append(offs[-1] + s)
    pieces = [full[:, offs[i]:offs[i + 1]] for i in range(len(sizes))]
    u, q, k, v, z, beta, a, gs, gd = pieces
    pad = jnp.zeros((full.shape[0], SMALL_W - 2 * n_heads), full.dtype)
    return jnp.concatenate([gs, gd, u, q, k, v, z, beta, a, pad], axis=1)


def _wcat_grad_to_shards(dwcat, d, w, n_heads):
    gs, gd = dwcat[:, :d], dwcat[:, d:2 * d]
    o = 2 * d
    u, q, k, v, z = [dwcat[:, o + i * w:o + (i + 1) * w] for i in range(5)]
    o += 5 * w
    beta, a = dwcat[:, o:o + n_heads], dwcat[:, o + n_heads:o + 2 * n_heads]
    full = jnp.concatenate([u, q, k, v, z, beta, a, gs, gd], axis=1)
    return jnp.stack(jnp.split(full, N_CHIPS, axis=1))


def _s5_discretize(a_re, a_im, log_dt, b_re, b_im):
    dt = jnp.exp(log_dt)[:, None]
    mag = jnp.exp(a_re * dt)
    lbar_r, lbar_i = mag * jnp.cos(a_im * dt), mag * jnp.sin(a_im * dt)
    den = a_re * a_re + a_im * a_im
    zr, zi = _cmul(lbar_r - 1.0, lbar_i, a_re / den, -a_im / den)
    bbar_r, bbar_i = _cmul(zr[:, :, None], zi[:, :, None], b_re, b_im)
    return lbar_r, lbar_i, bbar_r, bbar_i


def _blockdiag_in(bbar):
    g, p, h = bbar.shape
    t = g // GROUPS_PER_TILE
    bb = bbar.reshape(t, GROUPS_PER_TILE, p, h).transpose(0, 1, 3, 2)
    eye = jnp.eye(GROUPS_PER_TILE, dtype=bbar.dtype)
    return jnp.einsum('tjhp,jk->tjhkp', bb, eye).reshape(t, GROUPS_PER_TILE * h, GROUPS_PER_TILE * p)


def _blockdiag_in_grad(dblk, g, p, h):
    t = g // GROUPS_PER_TILE
    d5 = dblk.reshape(t, GROUPS_PER_TILE, h, GROUPS_PER_TILE, p)
    eye = jnp.eye(GROUPS_PER_TILE, dtype=dblk.dtype)
    diag = jnp.einsum('tjhkp,jk->tjhp', d5, eye)
    return diag.transpose(0, 1, 3, 2).reshape(g, p, h)


def _blockdiag_out(cmat):
    g, h, p = cmat.shape
    t = g // GROUPS_PER_TILE
    cc = cmat.reshape(t, GROUPS_PER_TILE, h, p).transpose(0, 1, 3, 2)
    eye = jnp.eye(GROUPS_PER_TILE, dtype=cmat.dtype)
    return jnp.einsum('tjph,jk->tjpkh', cc, eye).reshape(t, GROUPS_PER_TILE * p, GROUPS_PER_TILE * h)


def _blockdiag_out_grad(dblk, g, h, p):
    t = g // GROUPS_PER_TILE
    d5 = dblk.reshape(t, GROUPS_PER_TILE, p, GROUPS_PER_TILE, h)
    eye = jnp.eye(GROUPS_PER_TILE, dtype=dblk.dtype)
    diag = jnp.einsum('tjpkh,jk->tjph', d5, eye)
    return diag.transpose(0, 1, 3, 2).reshape(g, h, p)


def _pad_row(v, width):
    return jnp.pad(v.reshape(1, -1), ((0, 0), (0, width - v.size)))


def _pack(arrs, rows_mult):
    flat = jnp.concatenate([a.reshape(-1) for a in arrs])
    per = 128 * rows_mult
    total = -(-flat.size // per) * per
    return jnp.pad(flat, (0, total - flat.size))


def _unpack(flat, like):
    out, o = [], 0
    for a in like:
        out.append(flat[o:o + a.size].reshape(a.shape))
        o += a.size
    return out


def kernel(x, ffn1_w_gu, ffn1_w_down, ln1_g, ln1_b, w_in, conv_w, ssm_a_re, ssm_a_im, ssm_log_dt, ssm_b_re, ssm_b_im, ssm_c_re, ssm_c_im, ssm_d, glu_w, glu_b, gdn_a_log, gdn_dt_bias, gdn_norm_w, w_br_ssm, w_br_gdn, w_out, ln2_g, ln2_b, ffn2_w_gu, ffn2_w_down, ln3_g, ln3_b, loss_target, m_ffn1_w_gu, m_ffn1_w_down, m_ln1_g, m_ln1_b, m_w_in, m_conv_w, m_ssm_a_re, m_ssm_a_im, m_ssm_log_dt, m_ssm_b_re, m_ssm_b_im, m_ssm_c_re, m_ssm_c_im, m_ssm_d, m_glu_w, m_glu_b, m_gdn_a_log, m_gdn_dt_bias, m_gdn_norm_w, m_w_br_ssm, m_w_br_gdn, m_w_out, m_ln2_g, m_ln2_b, m_ffn2_w_gu, m_ffn2_w_down, m_ln3_g, m_ln3_b, v_ffn1_w_gu, v_ffn1_w_down, v_ln1_g, v_ln1_b, v_w_in, v_conv_w, v_ssm_a_re, v_ssm_a_im, v_ssm_log_dt, v_ssm_b_re, v_ssm_b_im, v_ssm_c_re, v_ssm_c_im, v_ssm_d, v_glu_w, v_glu_b, v_gdn_a_log, v_gdn_dt_bias, v_gdn_norm_w, v_w_br_ssm, v_w_br_gdn, v_w_out, v_ln2_g, v_ln2_b, v_ffn2_w_gu, v_ffn2_w_down, v_ln3_g, v_ln3_b):
    args = locals()
    wts = {n: args[n] for n in WEIGHT_NAMES}
    mom = {n: args["m_" + n] for n in WEIGHT_NAMES}
    var = {n: args["v_" + n] for n in WEIGHT_NAMES}

    depth = ln1_g.shape[0]
    n_tok, d = x.shape[1], x.shape[2]
    w = glu_w.shape[-1]
    n_heads = gdn_a_log.shape[-1]
    n_groups, n_state, grp = ssm_b_re.shape[1], ssm_b_re.shape[2], ssm_b_re.shape[3]
    alpha = (2.0 * depth) ** 0.25
    u_col0 = 2 * d
    qkv_col0 = u_col0 + w
    z_col0 = u_col0 + 4 * w
    small_col0 = u_col0 + 5 * w
    x_idx, y_idx, _ = _place()
    chip = 2 * x_idx + y_idx

    xcur = x[0]
    xcur_b = xcur.astype(BF16)
    saved = []
    for l in range(depth):
        shards = [wts[n][l].astype(BF16) for n in BIG] + [conv_w[l]]
        gathered = gather_chips(shards, [True] * len(BIG) + [False], name=f"gather_l{l}")
        gw = dict(zip(BIG, gathered[:-1]))
        conv_full = jnp.concatenate([gathered[-1][s] for s in range(N_CHIPS)], axis=1)
        wgu1, wgu2 = gw['ffn1_w_gu'], gw['ffn2_w_gu']
        wd1 = gw['ffn1_w_down'].reshape(-1, d)
        wd2 = gw['ffn2_w_down'].reshape(-1, d)
        wcat = _wcat_from_gathered(gw['w_in'], d, w, n_heads)
        wglu = gw['glu_w'].reshape(w, w)
        wbs, wbd = gw['w_br_ssm'], gw['w_br_gdn']
        wout = gw['w_out'].reshape(d, d)
        f = wd1.shape[0]

        vec = lambda v: v[l].reshape(1, -1)
        x0, x0b = xcur, xcur_b
        h1 = mm(x0b, wgu1, b_nb=N_CHIPS, out_nb=2, name=f"ffn_up")
        r1, x1, x1b = down_res_ln(h1, wd1, x0, vec(ln1_g), vec(ln1_b), swiglu=True, alpha=alpha, scale=0.5,
                                  name="ffn_down")
        proj = mm(x1b, wcat, name="in_proj")
        (lbar_r, lbar_i, bbar_r, bbar_i), disc_vjp = jax.vjp(
            _s5_discretize, ssm_a_re[l], ssm_a_im[l], ssm_log_dt[l], ssm_b_re[l], ssm_b_im[l])
        s5w = (_blockdiag_in(bbar_r), _blockdiag_in(bbar_i), _blockdiag_out(ssm_c_re[l]), _blockdiag_out(ssm_c_im[l]),
               lbar_r.reshape(1, -1), lbar_i.reshape(1, -1), ssm_d[l].reshape(1, -1))
        ypre, y2, y2b = s5_fwd(proj, u_col0, *s5w, name="s5_fwd")
        gl = mm(y2b, wglu, name="glu_proj")
        ys = glu_gate_fwd(y2, gl, vec(glu_b), name="glu_gate")
        qkv = conv_fwd(proj, qkv_col0, conv_full, name="conv_fwd")
        gdn_rows = (_pad_row(gdn_a_log[l], HEAD_DIM), _pad_row(gdn_dt_bias[l], HEAD_DIM), gdn_norm_w[l].reshape(1, -1))
        yg, states = gdn_fwd(qkv, proj, z_col0, small_col0, *gdn_rows, n_heads, name="gdn_fwd")
        bs = mm(ys, wbs, b_nb=N_CHIPS, name="br_ssm")
        bd = mm(yg, wbd, b_nb=N_CHIPS, name="br_gdn")
        mrg = merge_fwd(proj, bs, bd, name="merge")
        r2, x2, x2b = down_res_ln(mrg, wout, x1, vec(ln2_g), vec(ln2_b), swiglu=False, alpha=alpha, scale=1.0,
                                  name="mix_out")
        h3 = mm(x2b, wgu2, b_nb=N_CHIPS, out_nb=2, name="ffn_up")
        r3, x3, x3b = down_res_ln(h3, wd2, x2, vec(ln3_g), vec(ln3_b), swiglu=True, alpha=alpha, scale=0.5,
                                  name="ffn_down")
        saved.append(dict(x0b=x0b, h1=h1, r1=r1, x1b=x1b, proj=proj, s5w=s5w, disc_vjp=disc_vjp, ypre=ypre, y2=y2,
                          y2b=y2b, gl=gl, ys=ys, qkv=qkv, gdn_rows=gdn_rows, yg=yg, states=states, bs=bs, bd=bd, mrg=mrg,
                          r2=r2, x2b=x2b, h3=h3, r3=r3, wgu1=wgu1, wgu2=wgu2, wd1=wd1, wd2=wd2, wcat=wcat, wglu=wglu,
                          wbs=wbs, wbd=wbd, wout=wout, conv_full=conv_full))
        xcur, xcur_b = x3, x3b

    dy, loss_blk = loss_head(xcur, loss_target[0], name="loss_head")
    loss = lax.psum(loss_blk[0, 0], ("x", "y", "c"))

    big_grads = {n: [None] * depth for n in BIG}
    small_grads = {n: [None] * depth for n in SMALL}
    for l in reversed(range(depth)):
        s = saved[l]
        vec = lambda v: v[l].reshape(1, -1)

        def ffn_back(dx_out, r, g_ln, xin, hh, wgu, wd):
            dr, drb, dg, db = ln_bwd(r, g_ln, dx_out, name="ln_bwd")
            dh, act = ffn_bwd_mid(dr, wd, hh, scale=0.5, name="ffn_bwd_mid")
            dwd = mm(act, drb, ta=True, out_dtype=BF16, out_scale=0.5, name="ffn_dwd")
            dwgu = mm(xin, dh, ta=True, b_nb=2, out_nb=N_CHIPS, out_dtype=BF16, name="ffn_dwgu")
            dxin = mm(dh, wgu, tb=True, a_nb=2, b_nb=N_CHIPS, add=dr, add_scale=alpha, name="ffn_dx")
            return dxin, dg, db, dwgu, dwd.reshape(N_CHIPS, -1, d)

        dx2, dg3, db3, dwgu2, dwd2 = ffn_back(dy, s['r3'], vec(ln3_g), s['x2b'], s['h3'], s['wgu2'], s['wd2'])
        dr2, dr2b, dg2, db2 = ln_bwd(s['r2'], vec(ln2_g), dx2, name="ln_bwd")
        dmrg = mm(dr2b, s['wout'], tb=True, name="mix_dm")
        dwout = mm(s['mrg'], dr2b, ta=True, out_dtype=BF16, name="mix_dwout").reshape(N_CHIPS, -1, d)
        dbs, dbd, dgs, dgd = merge_bwd(s['proj'], s['bs'], s['bd'], dmrg, name="merge_bwd")
        dwbs = mm(s['ys'], dbs, ta=True, out_nb=N_CHIPS, out_dtype=BF16, name="br_dw")
        dwbd = mm(s['yg'], dbd, ta=True, out_nb=N_CHIPS, out_dtype=BF16, name="br_dw")
        dys = mm(dbs, s['wbs'], tb=True, b_nb=N_CHIPS, name="br_dx")
        dyg = mm(dbd, s['wbd'], tb=True, b_nb=N_CHIPS, name="br_dx")
        dy2a, dgl, dglu_b = glu_gate_bwd(s['y2'], s['gl'], vec(glu_b), dys, name="glu_gate_bwd")
        dwglu = mm(s['y2b'], dgl, ta=True, out_dtype=BF16, name="glu_dw").reshape(N_CHIPS, -1, w)
        dy2 = mm(dgl, s['wglu'], tb=True, add=dy2a, name="glu_dx")
        du, dbr, dbi, dcr, dci, dlr, dli, dd = s5_bwd(s['proj'], u_col0, s['ypre'], dy2, *s['s5w'], name="s5_bwd")
        dqkv_act, dz, dsmall, dalog, ddtb, dnw = gdn_bwd(s['qkv'], s['proj'], z_col0, small_col0, *s['gdn_rows'],
                                                           s['states'], dyg, n_heads, name="gdn_bwd")
        dqkv, dconv = conv_bwd(s['proj'], qkv_col0, s['conv_full'], dqkv_act, name="conv_bwd")
        dsmall_w = jnp.pad(dsmall, ((0, 0), (0, SMALL_W - HEAD_DIM)))
        dproj = jnp.concatenate([dgs, dgd, du, dqkv, dz, dsmall_w], axis=1)
        dwcat = mm(s['x1b'], dproj, ta=True, out_dtype=BF16, name="in_dw")
        dx1 = mm(dproj, s['wcat'], tb=True, add=dr2, add_scale=alpha, name="in_dx")
        dx0, dg1, db1, dwgu1, dwd1 = ffn_back(dx1, s['r1'], vec(ln1_g), s['x0b'], s['h1'], s['wgu1'], s['wd1'])
        dy = dx0

        da_re, da_im, dlog_dt, db_re, db_im = s['disc_vjp'](
            (dlr.reshape(n_groups, n_state), dli.reshape(n_groups, n_state),
             _blockdiag_in_grad(dbr, n_groups, n_state, grp), _blockdiag_in_grad(dbi, n_groups, n_state, grp)))
        sg = dict(ln1_g=dg1, ln1_b=db1, ln2_g=dg2, ln2_b=db2, ln3_g=dg3, ln3_b=db3, conv_w=dconv,
                  ssm_a_re=da_re, ssm_a_im=da_im, ssm_log_dt=dlog_dt, ssm_b_re=db_re, ssm_b_im=db_im,
                  ssm_c_re=_blockdiag_out_grad(dcr, n_groups, grp, n_state),
                  ssm_c_im=_blockdiag_out_grad(dci, n_groups, grp, n_state), ssm_d=dd, glu_b=dglu_b,
                  gdn_a_log=dalog[0, :n_heads], gdn_dt_bias=ddtb[0, :n_heads], gdn_norm_w=dnw)
        for n in SMALL:
            small_grads[n][l] = sg[n].reshape(-1)
        layer_grads = dict(ffn1_w_gu=dwgu1, ffn1_w_down=dwd1, w_in=_wcat_grad_to_shards(dwcat, d, w, n_heads),
                           glu_w=dwglu, w_br_ssm=dwbs, w_br_gdn=dwbd, w_out=dwout, ffn2_w_gu=dwgu2, ffn2_w_down=dwd2)
        reduced = reduce_scatter([layer_grads[n] for n in BIG], BF16, name="rs_big")
        for n, g in zip(BIG, reduced):
            big_grads[n][l] = g
    grad_x = dy[None]

    small_list = [jnp.stack(small_grads[n]) for n in SMALL]
    packed = _pack(small_list, 16 * N_CHIPS).reshape(N_CHIPS, -1, 128)
    red = reduce_scatter([packed], F32, name="rs_small")
    full = gather_chips(red, [True], name="gather_small")[0].reshape(-1)
    small_red = dict(zip(SMALL, _unpack(full, small_list)))
    cw_cols = conv_w.shape[-1]
    dconv_full = small_red['conv_w'].reshape(depth, CONV_K, N_CHIPS, cw_cols)
    small_red['conv_w'] = lax.dynamic_index_in_dim(dconv_full, chip, axis=2, keepdims=False)

    grads = {}
    for n in BIG:
        grads[n] = jnp.stack(big_grads[n]).reshape(wts[n].shape)
    for n in SMALL:
        grads[n] = small_red[n].reshape(wts[n].shape)

    delta, new_m, new_v = {}, {}, {}
    for n in BIG:
        delta[n], new_m[n], new_v[n] = adamw(wts[n], grads[n], mom[n], var[n], name="adamw_big")
    pk = lambda dct: _pack([dct[n] for n in SMALL], 1024).reshape(-1, 128)
    sd, sm, sv = adamw(pk(wts), pk(grads), pk(mom), pk(var), name="adamw_small")
    like = [wts[n] for n in SMALL]
    for n, a, b, c in zip(SMALL, _unpack(sd.reshape(-1), like), _unpack(sm.reshape(-1), like),
                          _unpack(sv.reshape(-1), like)):
        delta[n], new_m[n], new_v[n] = a, b, c

    return (loss, grad_x, *[grads[n] for n in WEIGHT_NAMES], *[delta[n] for n in WEIGHT_NAMES],
            *[new_m[n] for n in WEIGHT_NAMES], *[new_v[n] for n in WEIGHT_NAMES])
```

```python
import math

import jax
import jax.numpy as jnp
from jax import lax
from jax.experimental import pallas as pl
from jax.experimental.pallas import tpu as pltpu

F32 = jnp.float32
BF16 = jnp.bfloat16
HI = lax.Precision.HIGHEST
MESH = pl.DeviceIdType.MESH

N_CHIPS = 4
SSM_GROUP = 16
SSM_STATE = 64
GROUPS_PER_TILE = 8
HEAD_DIM = 128
CHUNK = 64
CONV_K = 4
LN_EPS = 1e-5
RMS_EPS = 1e-6
L2_EPS = 1e-6
SMALL_W = 512
ADAM_LR, ADAM_B1, ADAM_B2, ADAM_EPS, ADAM_WD, ADAM_STEP = 0.001, 0.9, 0.999, 1e-08, 0.01, 10
VMEM_LIMIT = 56 * 1024 * 1024

WEIGHT_NAMES = ['ffn1_w_gu', 'ffn1_w_down', 'ln1_g', 'ln1_b', 'w_in', 'conv_w', 'ssm_a_re', 'ssm_a_im', 'ssm_log_dt',
                'ssm_b_re', 'ssm_b_im', 'ssm_c_re', 'ssm_c_im', 'ssm_d', 'glu_w', 'glu_b', 'gdn_a_log', 'gdn_dt_bias',
                'gdn_norm_w', 'w_br_ssm', 'w_br_gdn', 'w_out', 'ln2_g', 'ln2_b', 'ffn2_w_gu', 'ffn2_w_down', 'ln3_g',
                'ln3_b']
BIG = ['ffn1_w_gu', 'ffn1_w_down', 'w_in', 'glu_w', 'w_br_ssm', 'w_br_gdn', 'w_out', 'ffn2_w_gu', 'ffn2_w_down']
SMALL = [n for n in WEIGHT_NAMES if n not in BIG]


def _tile(dim, prefs):
    for p in prefs:
        if dim % p == 0:
            return p
    return dim


def _cparams(sem):
    return pltpu.CompilerParams(dimension_semantics=sem, vmem_limit_bytes=VMEM_LIMIT)


def _ln(r, g, b):
    mu = jnp.mean(r, axis=-1, keepdims=True)
    xc = r - mu
    var = jnp.mean(xc * xc, axis=-1, keepdims=True)
    return xc * lax.rsqrt(var + LN_EPS) * g + b


def _lshape(x, nb):
    return (x.shape[0], x.shape[1]) if nb == 1 else (x.shape[1], x.shape[2] * nb)


def _cb_spec(x, nb, tr, tc, rc):
    if nb == 1:
        return pl.BlockSpec((tr, tc), lambda *g: rc(*g))
    cps = x.shape[2] // tc

    def imap(*g):
        r, c = rc(*g)
        return (c // cps, r, c % cps)
    return pl.BlockSpec((None, tr, tc), imap)


MM_VMEM_BUDGET = 40 * 1024 * 1024
MM_TILES = (2048, 1024, 512, 256, 128)
MM_FULL_K = 2048


def mm(a, b, *, name, ta=False, tb=False, a_nb=1, b_nb=1, out_nb=1, out_dtype=F32, add=None, add_scale=1.0,
       out_scale=1.0):
    ar, ac = _lshape(a, a_nb)
    br, bc = _lshape(b, b_nb)
    m, k = (ac, ar) if ta else (ar, ac)
    k2, n = (bc, br) if tb else (br, bc)
    assert k == k2, (name, a.shape, b.shape)
    assert a.dtype == BF16 and b.dtype == BF16, name

    def lim(dim, *nbs):
        q = dim
        for nb in nbs:
            q = math.gcd(q, dim // nb)
        return q
    lm = lim(m, a_nb if ta else 1)
    ln = lim(n, out_nb, 1 if tb else b_nb)
    lk = lim(k, 1 if ta else a_nb, b_nb if tb else 1)
    so = jnp.dtype(out_dtype).itemsize
    if k <= MM_FULL_K and lk == k:
        tks = [k]
    else:
        tks = [t for t in range(MM_FULL_K, 127, -128) if lk % t == 0]
    best = None
    for ck in tks:
        for cm in MM_TILES:
            for cn in MM_TILES:
                if lm % cm or ln % cn:
                    continue
                est = 2 * (cm * ck * 2 + ck * cn * 2 + cm * cn * so + (cm * cn * 4 if add is not None else 0))
                est += cm * cn * 4 * (2 if k > ck else 1) + (cm * ck * 2 + 512 * cm * 4 if ta else 0)
                score = min(cm, 512) * cn * ck
                if est <= MM_VMEM_BUDGET and (best is None or score > best[0]):
                    best = (score, cm, cn, ck)
    _, tm, tn, tk = best
    nk = k // tk
    dn = (((1,), (1 if tb else 0,)), ((), ()))

    def body(*refs):
        a_ref, b_ref = refs[:2]
        add_ref = refs[2] if add is not None else None
        o_ref = refs[3 if add is not None else 2]
        scratch = refs[(4 if add is not None else 3):]
        acc = scratch[0] if nk > 1 else None
        kk = pl.program_id(2)

        if ta:
            at_ref = scratch[-1]

            def transpose_block():
                for r0 in range(0, tk, 512):
                    r1 = min(tk, r0 + 512)
                    at_ref[:, r0:r1] = a_ref[r0:r1, :].astype(F32).T.astype(BF16)
            if nk == 1:
                pl.when(pl.program_id(1) == 0)(transpose_block)
            else:
                transpose_block()
            av = at_ref[...]
        else:
            av = a_ref[...]
        part = lax.dot_general(av, b_ref[...], dn, preferred_element_type=F32)

        def finish(r):
            if out_scale != 1.0:
                r = r * out_scale
            if add is not None:
                r = r + add_scale * add_ref[...]
            o_ref[...] = r.astype(out_dtype)

        if nk == 1:
            finish(part)
        else:
            @pl.when(kk == 0)
            def _():
                acc[...] = part

            @pl.when(kk > 0)
            def _():
                acc[...] += part

            @pl.when(kk == nk - 1)
            def _():
                finish(acc[...])

    if ta:
        a_spec = _cb_spec(a, a_nb, tk, tm, lambda i, j, kk: (kk, i))
    else:
        a_spec = _cb_spec(a, a_nb, tm, tk, lambda i, j, kk: (i, kk))
    if tb:
        b_spec = _cb_spec(b, b_nb, tn, tk, lambda i, j, kk: (j, kk))
    else:
        b_spec = _cb_spec(b, b_nb, tk, tn, lambda i, j, kk: (kk, j))
    if out_nb == 1:
        out_shape = jax.ShapeDtypeStruct((m, n), out_dtype)
        out_spec = pl.BlockSpec((tm, tn), lambda i, j, kk: (i, j))
    else:
        out_shape = jax.ShapeDtypeStruct((out_nb, m, n // out_nb), out_dtype)
        out_spec = _cb_spec(out_shape, out_nb, tm, tn, lambda i, j, kk: (i, j))
    in_specs = [a_spec, b_spec]
    args = [a, b]
    if add is not None:
        in_specs.append(pl.BlockSpec((tm, tn), lambda i, j, kk: (i, j)))
        args.append(add)
    scratch = ([pltpu.VMEM((tm, tn), F32)] if nk > 1 else []) + ([pltpu.VMEM((tm, tk), BF16)] if ta else [])
    return pl.pallas_call(
        body, name=name, out_shape=out_shape, grid=(m // tm, n // tn, nk), in_specs=in_specs, out_specs=out_spec,
        scratch_shapes=scratch, compiler_params=_cparams(("parallel", "arbitrary", "arbitrary")))(*args)


def down_res_ln(src, w, x, g, b, *, swiglu, alpha, scale, name):
    n_tok, d = x.shape
    kdim = w.shape[0]
    tm = _tile(n_tok, (512, 256, 128))
    tk = _tile(kdim, (512, 256, 128))
    nk = kdim // tk

    def body(s_ref, w_ref, x_ref, g_ref, b_ref, r_ref, y_ref, yb_ref, acc):
        kk = pl.program_id(1)

        @pl.when(kk == 0)
        def _():
            acc[...] = jnp.zeros_like(acc)

        if swiglu:
            gate = s_ref[0]
            a = gate * jax.nn.sigmoid(gate) * s_ref[1]
        else:
            a = s_ref[...]
        acc[...] += jnp.dot(a.astype(BF16), w_ref[...], preferred_element_type=F32)

        @pl.when(kk == nk - 1)
        def _():
            r = alpha * x_ref[...] + scale * acc[...]
            r_ref[...] = r
            y = _ln(r, g_ref[...], b_ref[...])
            y_ref[...] = y
            yb_ref[...] = y.astype(BF16)

    if swiglu:
        s_spec = pl.BlockSpec((2, tm, tk), lambda i, kk: (0, i, kk))
    else:
        s_spec = pl.BlockSpec((tm, tk), lambda i, kk: (i, kk))
    row = pl.BlockSpec((tm, d), lambda i, kk: (i, 0))
    vec = pl.BlockSpec((1, d), lambda i, kk: (0, 0))
    return pl.pallas_call(
        body, name=name, out_shape=[jax.ShapeDtypeStruct((n_tok, d), F32)] * 2 + [jax.ShapeDtypeStruct((n_tok, d), BF16)],
        grid=(n_tok // tm, nk),
        in_specs=[s_spec, pl.BlockSpec((tk, d), lambda i, kk: (kk, 0)), row, vec, vec], out_specs=[row, row, row],
        scratch_shapes=[pltpu.VMEM((tm, d), F32)], compiler_params=_cparams(("parallel", "arbitrary")))(src, w, x, g, b)


def ln_bwd(r, g, dy, *, name):
    n_tok, d = r.shape
    tm = _tile(n_tok, (256, 128))

    def body(r_ref, g_ref, dy_ref, dr_ref, drb_ref, dg_ref, db_ref):
        i = pl.program_id(0)

        @pl.when(i == 0)
        def _():
            dg_ref[...] = jnp.zeros_like(dg_ref)
            db_ref[...] = jnp.zeros_like(db_ref)

        rv = r_ref[...]
        dyv = dy_ref[...]
        mu = jnp.mean(rv, axis=-1, keepdims=True)
        xc = rv - mu
        rstd = lax.rsqrt(jnp.mean(xc * xc, axis=-1, keepdims=True) + LN_EPS)
        xh = xc * rstd
        dxh = dyv * g_ref[...]
        dr = rstd * (dxh - jnp.mean(dxh, axis=-1, keepdims=True) - xh * jnp.mean(dxh * xh, axis=-1, keepdims=True))
        dr_ref[...] = dr
        drb_ref[...] = dr.astype(BF16)
        dg_ref[...] += jnp.sum(dyv * xh, axis=0, keepdims=True)
        db_ref[...] += jnp.sum(dyv, axis=0, keepdims=True)

    row = pl.BlockSpec((tm, d), lambda i: (i, 0))
    vec = pl.BlockSpec((1, d), lambda i: (0, 0))
    return pl.pallas_call(
        body, name=name, out_shape=[jax.ShapeDtypeStruct((n_tok, d), F32), jax.ShapeDtypeStruct((n_tok, d), BF16),
                                    jax.ShapeDtypeStruct((1, d), F32), jax.ShapeDtypeStruct((1, d), F32)],
        grid=(n_tok // tm,), in_specs=[row, vec, row], out_specs=[row, row, vec, vec],
        compiler_params=_cparams(("arbitrary",)))(r, g, dy)


def loss_head(y, target, *, name):
    n_tok, d = y.shape
    tm = _tile(n_tok, (256, 128))

    def body(y_ref, t_ref, dy_ref, l_ref):
        i = pl.program_id(0)

        @pl.when(i == 0)
        def _():
            l_ref[...] = jnp.zeros_like(l_ref)

        e = y_ref[...] - t_ref[...]
        dy_ref[...] = e * (1.0 / d)
        s = jnp.sum(jnp.mean(e * e, axis=-1, keepdims=True), axis=0, keepdims=True)
        l_ref[...] += 0.5 * s

    row = pl.BlockSpec((tm, d), lambda i: (i, 0))
    return pl.pallas_call(
        body, name=name, out_shape=[jax.ShapeDtypeStruct((n_tok, d), F32), jax.ShapeDtypeStruct((8, 128), F32)],
        grid=(n_tok // tm,), in_specs=[row, row], out_specs=[row, pl.BlockSpec((8, 128), lambda i: (0, 0))],
        compiler_params=_cparams(("arbitrary",)))(y, target)


def ffn_bwd_mid(dr, wd, h, *, scale, name):
    n_tok, d = dr.shape
    f = wd.shape[0]
    tm = _tile(n_tok, (512, 256, 128))
    tf = _tile(f, (512, 256, 128))

    def body(dr_ref, w_ref, h_ref, dh_ref, a_ref):
        dy = (scale * dr_ref[...]).astype(BF16)
        da = lax.dot_general(dy, w_ref[...], (((1,), (1,)), ((), ())), preferred_element_type=F32)
        gate = h_ref[0]
        up = h_ref[1]
        sg = jax.nn.sigmoid(gate)
        s = gate * sg
        a_ref[...] = (s * up).astype(BF16)
        dh_ref[0] = (da * up * (sg * (1.0 + gate * (1.0 - sg)))).astype(BF16)
        dh_ref[1] = (da * s).astype(BF16)

    return pl.pallas_call(
        body, name=name, out_shape=[jax.ShapeDtypeStruct((2, n_tok, f), BF16), jax.ShapeDtypeStruct((n_tok, f), BF16)],
        grid=(n_tok // tm, f // tf),
        in_specs=[pl.BlockSpec((tm, d), lambda i, j: (i, 0)), pl.BlockSpec((tf, d), lambda i, j: (j, 0)),
                  pl.BlockSpec((2, tm, tf), lambda i, j: (0, i, j))],
        out_specs=[pl.BlockSpec((2, tm, tf), lambda i, j: (0, i, j)), pl.BlockSpec((tm, tf), lambda i, j: (i, j))],
        compiler_params=_cparams(("parallel", "parallel")))(dr, wd, h)


def _cmul(ar, ai, br, bi):
    return ar * br - ai * bi, ar * bi + ai * br


def _scan_blocks(sr_ref, si_ref, lr, li, *, reverse):
    n_rows, width = sr_ref.shape
    n_blk = n_rows // 8
    row = lax.broadcasted_iota(jnp.int32, (8, width), 0)
    pr = jnp.broadcast_to(lr, (8, width))
    pi = jnp.broadcast_to(-li if reverse else li, (8, width))

    def shifted(v, dist, fill=0.0):
        if reverse:
            return jnp.where(row < 8 - dist, pltpu.roll(v, 8 - dist, 0), fill)
        return jnp.where(row >= dist, pltpu.roll(v, dist, 0), fill)

    p1 = (pr, pi)
    p2 = _cmul(*p1, *p1)
    p4 = _cmul(*p2, *p2)
    wr, wi = pr, pi
    for dist in (1, 2, 4):
        wr, wi = _cmul(wr, wi, shifted(wr, dist, 1.0), shifted(wi, dist, 0.0))
    edge = 0 if reverse else 7

    def step(i, carry):
        cr, ci = carry
        blk = (n_blk - 1 - i) if reverse else i
        r0 = pl.multiple_of(blk * 8, 8)
        xr = sr_ref[pl.ds(r0, 8), :]
        xi = si_ref[pl.ds(r0, 8), :]
        for dist, (qr, qi) in ((1, p1), (2, p2), (4, p4)):
            tr, ti = _cmul(qr, qi, shifted(xr, dist), shifted(xi, dist))
            xr, xi = xr + tr, xi + ti
        tr, ti = _cmul(wr, wi, cr, ci)
        xr, xi = xr + tr, xi + ti
        sr_ref[pl.ds(r0, 8), :] = xr
        si_ref[pl.ds(r0, 8), :] = xi
        br = jnp.where(row == edge, xr, 0.0)
        bi = jnp.where(row == edge, xi, 0.0)
        for dist in (1, 2, 4):
            br = br + pltpu.roll(br, dist, 0)
            bi = bi + pltpu.roll(bi, dist, 0)
        return br, bi

    zero = jnp.zeros((8, width), F32)
    lax.fori_loop(0, n_blk, step, (zero, zero), unroll=2)


def _s5_specs(n_tok, u_blk0):
    gw = GROUPS_PER_TILE * SSM_GROUP
    sw = GROUPS_PER_TILE * SSM_STATE
    u_spec = pl.BlockSpec((n_tok, gw), lambda t: (0, u_blk0 + t))
    col = pl.BlockSpec((n_tok, gw), lambda t: (0, t))
    bmat = pl.BlockSpec((None, gw, sw), lambda t: (t, 0, 0))
    cmat = pl.BlockSpec((None, sw, gw), lambda t: (t, 0, 0))
    lvec = pl.BlockSpec((1, sw), lambda t: (0, t))
    dvec = pl.BlockSpec((1, gw), lambda t: (0, t))
    return gw, sw, u_spec, col, bmat, cmat, lvec, dvec


def s5_fwd(proj, u_col0, bblk_r, bblk_i, cblk_r, cblk_i, lbar_r, lbar_i, dskip, *, name):
    n_tok = proj.shape[0]
    n_tiles = bblk_r.shape[0]
    gw, sw, u_spec, col, bmat, cmat, lvec, dvec = _s5_specs(n_tok, u_col0 // (GROUPS_PER_TILE * SSM_GROUP))

    def body(u_ref, br_ref, bi_ref, cr_ref, ci_ref, lr_ref, li_ref, d_ref, ypre_ref, y2_ref, y2b_ref, sr, si):
        u = u_ref[...]
        ub = u.astype(BF16)
        sr[...] = jnp.dot(ub, br_ref[...].astype(BF16), preferred_element_type=F32)
        si[...] = jnp.dot(ub, bi_ref[...].astype(BF16), preferred_element_type=F32)
        _scan_blocks(sr, si, lr_ref[...], li_ref[...], reverse=False)
        y = (jnp.dot(sr[...].astype(BF16), cr_ref[...].astype(BF16), preferred_element_type=F32)
             - jnp.dot(si[...].astype(BF16), ci_ref[...].astype(BF16), preferred_element_type=F32)
             + d_ref[...] * u)
        ypre_ref[...] = y
        y2 = jax.nn.gelu(y)
        y2_ref[...] = y2
        y2b_ref[...] = y2.astype(BF16)

    width = n_tiles * gw
    return pl.pallas_call(
        body, name=name,
        out_shape=[jax.ShapeDtypeStruct((n_tok, width), F32)] * 2 + [jax.ShapeDtypeStruct((n_tok, width), BF16)],
        grid=(n_tiles,), in_specs=[u_spec, bmat, bmat, cmat, cmat, lvec, lvec, dvec], out_specs=[col, col, col],
        scratch_shapes=[pltpu.VMEM((n_tok, sw), F32)] * 2,
        compiler_params=_cparams(("parallel",)))(proj, bblk_r, bblk_i, cblk_r, cblk_i, lbar_r, lbar_i, dskip)


def s5_bwd(proj, u_col0, ypre, dy2, bblk_r, bblk_i, cblk_r, cblk_i, lbar_r, lbar_i, dskip, *, name):
    n_tok = proj.shape[0]
    n_tiles = bblk_r.shape[0]
    gw, sw, u_spec, col, bmat, cmat, lvec, dvec = _s5_specs(n_tok, u_col0 // (GROUPS_PER_TILE * SSM_GROUP))
    rb = _tile(n_tok, (512, 256, 128))
    tn_dims = (((0,), (0,)), ((), ()))
    nt_dims = (((1,), (1,)), ((), ()))

    def body(u_ref, ypre_ref, dy2_ref, br_ref, bi_ref, cr_ref, ci_ref, lr_ref, li_ref, d_ref,
             du_ref, dbr_ref, dbi_ref, dcr_ref, dci_ref, dlr_ref, dli_ref, dd_ref, sr, si, gr, gi):
        u = u_ref[...]
        ub = u.astype(BF16)
        bmr = br_ref[...].astype(BF16)
        bmi = bi_ref[...].astype(BF16)
        cmr = cr_ref[...].astype(BF16)
        cmi = ci_ref[...].astype(BF16)
        lr = lr_ref[...]
        li = li_ref[...]
        _, gelu_vjp = jax.vjp(jax.nn.gelu, ypre_ref[...])
        dyp = gelu_vjp(dy2_ref[...])[0]
        dyb = dyp.astype(BF16)
        sr[...] = jnp.dot(ub, bmr, preferred_element_type=F32)
        si[...] = jnp.dot(ub, bmi, preferred_element_type=F32)
        _scan_blocks(sr, si, lr, li, reverse=False)
        gr[...] = lax.dot_general(dyb, cmr, nt_dims, preferred_element_type=F32)
        gi[...] = -lax.dot_general(dyb, cmi, nt_dims, preferred_element_type=F32)
        _scan_blocks(gr, gi, lr, li, reverse=True)
        srb = sr[...].astype(BF16)
        sib = si[...].astype(BF16)
        dcr_ref[...] = lax.dot_general(srb, dyb, tn_dims, preferred_element_type=F32)
        dci_ref[...] = -lax.dot_general(sib, dyb, tn_dims, preferred_element_type=F32)
        grb = gr[...].astype(BF16)
        gib = gi[...].astype(BF16)
        dbr_ref[...] = lax.dot_general(ub, grb, tn_dims, preferred_element_type=F32)
        dbi_ref[...] = lax.dot_general(ub, gib, tn_dims, preferred_element_type=F32)
        du_ref[...] = (lax.dot_general(grb, bmr, nt_dims, preferred_element_type=F32)
                       + lax.dot_general(gib, bmi, nt_dims, preferred_element_type=F32)
                       + d_ref[...] * dyp).astype(du_ref.dtype)
        dd_ref[...] = jnp.sum(dyp * u, axis=0, keepdims=True)
        inv = 1.0 / (lr * lr + li * li)
        qr = lr * inv
        qi = -li * inv
        acc_r = jnp.zeros((1, sw), F32)
        acc_i = jnp.zeros((1, sw), F32)
        for blk in range(n_tok // rb):
            rows = pl.ds(blk * rb, rb)
            ubb = u_ref[rows, :].astype(BF16)
            er = sr[rows, :] - jnp.dot(ubb, bmr, preferred_element_type=F32)
            ei = si[rows, :] - jnp.dot(ubb, bmi, preferred_element_type=F32)
            pr, pi = _cmul(er, ei, qr, qi)
            ar = gr[rows, :]
            ai = gi[rows, :]
            acc_r = acc_r + jnp.sum(ar * pr + ai * pi, axis=0, keepdims=True)
            acc_i = acc_i + jnp.sum(ai * pr - ar * pi, axis=0, keepdims=True)
        dlr_ref[...] = acc_r
        dli_ref[...] = acc_i

    width = n_tiles * gw
    out_shape = [jax.ShapeDtypeStruct((n_tok, width), BF16),
                 jax.ShapeDtypeStruct(bblk_r.shape, F32), jax.ShapeDtypeStruct(bblk_r.shape, F32),
                 jax.ShapeDtypeStruct(cblk_r.shape, F32), jax.ShapeDtypeStruct(cblk_r.shape, F32),
                 jax.ShapeDtypeStruct(lbar_r.shape, F32), jax.ShapeDtypeStruct(lbar_r.shape, F32),
                 jax.ShapeDtypeStruct(dskip.shape, F32)]
    return pl.pallas_call(
        body, name=name, out_shape=out_shape, grid=(n_tiles,),
        in_specs=[u_spec, col, col, bmat, bmat, cmat, cmat, lvec, lvec, dvec],
        out_specs=[col, bmat, bmat, cmat, cmat, lvec, lvec, dvec],
        scratch_shapes=[pltpu.VMEM((n_tok, sw), F32)] * 4,
        compiler_params=_cparams(("parallel",)))(proj, ypre, dy2, bblk_r, bblk_i, cblk_r, cblk_i, lbar_r, lbar_i, dskip)


CONV_ROWS = 256
CONV_COLS = 512


def _conv_pre(x_ref, w_ref, blk, n_blk):
    r0 = blk * CONV_ROWS
    if blk == 0:
        ext = jnp.concatenate([jnp.zeros((8, CONV_COLS), F32), x_ref[0:CONV_ROWS, :]], axis=0)
    else:
        ext = x_ref[r0 - 8:r0 + CONV_ROWS, :]
    taps = []
    c = None
    for j in range(CONV_K):
        s = CONV_K - 1 - j
        xs = ext[8:] if s == 0 else pltpu.roll(ext, s, 0)[8:]
        taps.append(xs)
        term = w_ref[j:j + 1, :] * xs
        c = term if c is None else c + term
    return c, taps


def conv_fwd(proj, col0, conv_w, *, name):
    n_tok = proj.shape[0]
    width = conv_w.shape[1]
    n_blk = n_tok // CONV_ROWS
    cb0 = col0 // CONV_COLS

    def body(x_ref, w_ref, o_ref):
        for blk in range(n_blk):
            c, _ = _conv_pre(x_ref, w_ref, blk, n_blk)
            o_ref[blk * CONV_ROWS:(blk + 1) * CONV_ROWS, :] = c * jax.nn.sigmoid(c)

    return pl.pallas_call(
        body, name=name, out_shape=jax.ShapeDtypeStruct((n_tok, width), F32), grid=(width // CONV_COLS,),
        in_specs=[pl.BlockSpec((n_tok, CONV_COLS), lambda j: (0, cb0 + j)),
                  pl.BlockSpec((CONV_K, CONV_COLS), lambda j: (0, j))],
        out_specs=pl.BlockSpec((n_tok, CONV_COLS), lambda j: (0, j)),
        compiler_params=_cparams(("parallel",)))(proj, conv_w)


def conv_bwd(proj, col0, conv_w, dout, *, name):
    n_tok = proj.shape[0]
    width = conv_w.shape[1]
    n_blk = n_tok // CONV_ROWS
    cb0 = col0 // CONV_COLS

    def body(x_ref, w_ref, do_ref, dx_ref, dw_ref, dc):
        dws = [jnp.zeros((1, CONV_COLS), F32) for _ in range(CONV_K)]
        for blk in range(n_blk):
            rows = slice(blk * CONV_ROWS, (blk + 1) * CONV_ROWS)
            c, taps = _conv_pre(x_ref, w_ref, blk, n_blk)
            sg = jax.nn.sigmoid(c)
            dcv = do_ref[rows, :] * (sg * (1.0 + c * (1.0 - sg)))
            dc[rows, :] = dcv
            for j in range(CONV_K):
                dws[j] = dws[j] + jnp.sum(dcv * taps[j], axis=0, keepdims=True)
        dc[n_tok:n_tok + 8, :] = jnp.zeros((8, CONV_COLS), F32)
        for j in range(CONV_K):
            dw_ref[j:j + 1, :] = dws[j]
        for blk in range(n_blk):
            r0 = blk * CONV_ROWS
            ext = dc[r0:r0 + CONV_ROWS + 8, :]
            dx = None
            for j in range(CONV_K):
                s = CONV_K - 1 - j
                sh = ext[:CONV_ROWS] if s == 0 else pltpu.roll(ext, CONV_ROWS + 8 - s, 0)[:CONV_ROWS]
                term = w_ref[j:j + 1, :] * sh
                dx = term if dx is None else dx + term
            dx_ref[r0:r0 + CONV_ROWS, :] = dx.astype(dx_ref.dtype)

    return pl.pallas_call(
        body, name=name, out_shape=[jax.ShapeDtypeStruct((n_tok, width), BF16), jax.ShapeDtypeStruct(conv_w.shape, F32)],
        grid=(width // CONV_COLS,),
        in_specs=[pl.BlockSpec((n_tok, CONV_COLS), lambda j: (0, cb0 + j)),
                  pl.BlockSpec((CONV_K, CONV_COLS), lambda j: (0, j)),
                  pl.BlockSpec((n_tok, CONV_COLS), lambda j: (0, j))],
        out_specs=[pl.BlockSpec((n_tok, CONV_COLS), lambda j: (0, j)), pl.BlockSpec((CONV_K, CONV_COLS), lambda j: (0, j))],
        scratch_shapes=[pltpu.VMEM((n_tok + 8, CONV_COLS), F32)],
        compiler_params=_cparams(("parallel",)))(proj, conv_w, dout)


GDN_PREC = lax.Precision.HIGH


def _neumann_inverse(lower):
    n = lower.shape[0]
    eye = (lax.broadcasted_iota(jnp.int32, (n, n), 0) == lax.broadcasted_iota(jnp.int32, (n, n), 1)).astype(F32)
    xp = -lower
    tinv = eye + xp
    power = 2
    while power < n:
        xp = jnp.dot(xp, xp, precision=GDN_PREC, preferred_element_type=F32)
        tinv = tinv + jnp.dot(tinv, xp, precision=GDN_PREC, preferred_element_type=F32)
        power *= 2
    return tinv


@jax.custom_vjp
def _unit_lower_inverse(lower):
    return _neumann_inverse(lower)


def _unit_lower_inverse_fwd(lower):
    tinv = _neumann_inverse(lower)
    return tinv, tinv


def _unit_lower_inverse_bwd(tinv, ct):
    left = lax.dot_general(tinv, ct, (((0,), (0,)), ((), ())), precision=GDN_PREC, preferred_element_type=F32)
    return (-lax.dot_general(left, tinv, (((1,), (1,)), ((), ())), precision=GDN_PREC, preferred_element_type=F32),)


_unit_lower_inverse.defvjp(_unit_lower_inverse_fwd, _unit_lower_inverse_bwd)


def _gdn_chunk(head, n_heads, state, q, k, v, z, bsmall, alog_row, dtb_row, nw):
    c = CHUNK
    lane = lax.broadcasted_iota(jnp.int32, (c, HEAD_DIM), 1)
    lane1 = lax.broadcasted_iota(jnp.int32, (1, HEAD_DIM), 1)
    ri = lax.broadcasted_iota(jnp.int32, (c, c), 0)
    ci = lax.broadcasted_iota(jnp.int32, (c, c), 1)
    causal = ri >= ci
    strict = ri > ci
    tril = causal.astype(F32)
    bl = jnp.sum(jnp.where(lane == head, bsmall, 0.0), axis=-1, keepdims=True)
    al = jnp.sum(jnp.where(lane == n_heads + head, bsmall, 0.0), axis=-1, keepdims=True)
    alog = jnp.sum(jnp.where(lane1 == head, alog_row, 0.0), axis=-1, keepdims=True)
    dtb = jnp.sum(jnp.where(lane1 == head, dtb_row, 0.0), axis=-1, keepdims=True)

    qn = q * lax.rsqrt(jnp.sum(q * q, axis=-1, keepdims=True) + L2_EPS) * (HEAD_DIM ** -0.5)
    kn = k * lax.rsqrt(jnp.sum(k * k, axis=-1, keepdims=True) + L2_EPS)
    beta = jax.nn.sigmoid(bl)
    xg = al + dtb
    g = -jnp.exp(alog) * (jnp.maximum(xg, 0.0) + jnp.log(1.0 + jnp.exp(-jnp.abs(xg))))
    g_wide = jnp.broadcast_to(g, (c, HEAD_DIM))
    gc = jnp.dot(tril, g_wide, precision=HI, preferred_element_type=F32)
    gc_rows = jnp.broadcast_to(jnp.mean(gc, axis=-1, keepdims=True), (c, c))
    gc_cols = gc.T[:c, :]
    g_tot = jnp.sum(g, axis=0, keepdims=True)
    decay = jnp.exp(jnp.where(causal, gc_rows - gc_cols, -1e30))
    egc = jnp.exp(gc)
    kb = kn * beta
    knb = kn.astype(BF16)
    nt = (((1,), (1,)), ((), ()))
    lower = jnp.where(strict, lax.dot_general(kb.astype(BF16), knb, nt, preferred_element_type=F32) * decay, 0.0)
    tinv = _unit_lower_inverse(lower)
    u_val = jnp.dot(tinv, v * beta, precision=GDN_PREC, preferred_element_type=F32)
    w_key = jnp.dot(tinv, kb * egc, precision=GDN_PREC, preferred_element_type=F32)
    attn = lax.dot_general(qn.astype(BF16), knb, nt, preferred_element_type=F32) * decay
    q_dec = qn * egc
    k_dec = kn * jnp.exp(g_tot - gc)
    sb = state.astype(BF16)
    v_new = u_val - jnp.dot(w_key.astype(BF16), sb, preferred_element_type=F32)
    vnb = v_new.astype(BF16)
    o = (jnp.dot(q_dec.astype(BF16), sb, preferred_element_type=F32)
         + jnp.dot(attn.astype(BF16), vnb, preferred_element_type=F32))
    new_state = state * jnp.exp(g_tot) + lax.dot_general(k_dec.astype(BF16), vnb, (((0,), (0,)), ((), ())),
                                                         preferred_element_type=F32)
    o = o * lax.rsqrt(jnp.mean(o * o, axis=-1, keepdims=True) + RMS_EPS) * nw
    o = o * (z * jax.nn.sigmoid(z))
    return o, new_state


def _gdn_in_specs(n_heads, qkv_width_blocks, z_blk, small_blk, rev, n_chunks):
    w = n_heads * HEAD_DIM

    def cidx(i):
        return (n_chunks - 1 - i) if rev else i
    qs = pl.BlockSpec((CHUNK, w), lambda i: (cidx(i), 0))
    ks = pl.BlockSpec((CHUNK, w), lambda i: (cidx(i), 1))
    vs = pl.BlockSpec((CHUNK, w), lambda i: (cidx(i), 2))
    zs = pl.BlockSpec((CHUNK, w), lambda i: (cidx(i), z_blk))
    bs = pl.BlockSpec((CHUNK, HEAD_DIM), lambda i: (cidx(i), small_blk))
    pv = pl.BlockSpec((1, HEAD_DIM), lambda i: (0, 0))
    return cidx, qs, ks, vs, zs, bs, pv


def gdn_fwd(qkv, proj, z_col0, small_col0, alog_row, dtb_row, nw_row, n_heads, *, name):
    n_tok = qkv.shape[0]
    w = n_heads * HEAD_DIM
    n_chunks = n_tok // CHUNK
    cidx, qs, ks, vs, zs, bs, pv = _gdn_in_specs(n_heads, 3, z_col0 // w, small_col0 // HEAD_DIM, False, n_chunks)

    def body(q_ref, k_ref, v_ref, z_ref, b_ref, al_ref, dt_ref, nw_ref, o_ref, s_ref, state):
        @pl.when(pl.program_id(0) == 0)
        def _():
            state[...] = jnp.zeros_like(state)

        bsm = b_ref[...]
        for h in range(n_heads):
            cols = slice(h * HEAD_DIM, (h + 1) * HEAD_DIM)
            st = state[h]
            s_ref[h] = st
            o, ns = _gdn_chunk(h, n_heads, st, q_ref[:, cols], k_ref[:, cols], v_ref[:, cols], z_ref[:, cols], bsm,
                               al_ref[...], dt_ref[...], nw_ref[...])
            o_ref[:, cols] = o.astype(BF16)
            state[h] = ns

    return pl.pallas_call(
        body, name=name,
        out_shape=[jax.ShapeDtypeStruct((n_tok, w), BF16),
                   jax.ShapeDtypeStruct((n_chunks, n_heads, HEAD_DIM, HEAD_DIM), F32)],
        grid=(n_chunks,), in_specs=[qs, ks, vs, zs, bs, pv, pv, pv],
        out_specs=[pl.BlockSpec((CHUNK, w), lambda i: (i, 0)),
                   pl.BlockSpec((None, n_heads, HEAD_DIM, HEAD_DIM), lambda i: (i, 0, 0, 0))],
        scratch_shapes=[pltpu.VMEM((n_heads, HEAD_DIM, HEAD_DIM), F32)],
        compiler_params=_cparams(("arbitrary",)))(qkv, qkv, qkv, proj, proj, alog_row, dtb_row, nw_row)


def gdn_bwd(qkv, proj, z_col0, small_col0, alog_row, dtb_row, nw_row, states, dout, n_heads, *, name):
    n_tok = qkv.shape[0]
    w = n_heads * HEAD_DIM
    n_chunks = n_tok // CHUNK
    cidx, qs, ks, vs, zs, bs, pv = _gdn_in_specs(n_heads, 3, z_col0 // w, small_col0 // HEAD_DIM, True, n_chunks)

    def body(q_ref, k_ref, v_ref, z_ref, b_ref, al_ref, dt_ref, nw_ref, s_ref, do_ref,
             dqkv_ref, dz_ref, db_ref, dal_ref, ddt_ref, dnw_ref, dstate):
        @pl.when(pl.program_id(0) == 0)
        def _():
            dstate[...] = jnp.zeros_like(dstate)
            dal_ref[...] = jnp.zeros_like(dal_ref)
            ddt_ref[...] = jnp.zeros_like(ddt_ref)
            dnw_ref[...] = jnp.zeros_like(dnw_ref)

        bsm = b_ref[...]
        dbs = jnp.zeros((CHUNK, HEAD_DIM), F32)
        dal = jnp.zeros((1, HEAD_DIM), F32)
        ddt = jnp.zeros((1, HEAD_DIM), F32)
        dnw = jnp.zeros((1, HEAD_DIM), F32)
        for h in range(n_heads):
            cols = slice(h * HEAD_DIM, (h + 1) * HEAD_DIM)

            def f(st, q, k, v, z, bb, al, dt, nw, h=h):
                return _gdn_chunk(h, n_heads, st, q, k, v, z, bb, al, dt, nw)
            _, vjp = jax.vjp(f, s_ref[h], q_ref[:, cols], k_ref[:, cols], v_ref[:, cols], z_ref[:, cols], bsm,
                             al_ref[...], dt_ref[...], nw_ref[...])
            dst, dq, dk, dv, dz, dbb, da, dd, dn = vjp((do_ref[:, cols], dstate[h]))
            dstate[h] = dst
            dqkv_ref[:, h * HEAD_DIM:(h + 1) * HEAD_DIM] = dq
            dqkv_ref[:, w + h * HEAD_DIM:w + (h + 1) * HEAD_DIM] = dk
            dqkv_ref[:, 2 * w + h * HEAD_DIM:2 * w + (h + 1) * HEAD_DIM] = dv
            dz_ref[:, cols] = dz.astype(dz_ref.dtype)
            dbs = dbs + dbb
            dal = dal + da
            ddt = ddt + dd
            dnw = dnw + dn
        db_ref[...] = dbs.astype(db_ref.dtype)
        dal_ref[...] += dal
        ddt_ref[...] += ddt
        dnw_ref[...] += dnw

    rowblk = pl.BlockSpec((CHUNK, w), lambda i: (cidx(i), 0))
    return pl.pallas_call(
        body, name=name,
        out_shape=[
            jax.ShapeDtypeStruct((n_tok, 3 * w), F32),
            jax.ShapeDtypeStruct((n_tok, w), BF16), jax.ShapeDtypeStruct((n_tok, HEAD_DIM), BF16),
            jax.ShapeDtypeStruct((1, HEAD_DIM), F32), jax.ShapeDtypeStruct((1, HEAD_DIM), F32),
            jax.ShapeDtypeStruct((1, HEAD_DIM), F32)],
        grid=(n_chunks,),
        in_specs=[qs, ks, vs, zs, bs, pv, pv, pv,
                  pl.BlockSpec((None, n_heads, HEAD_DIM, HEAD_DIM), lambda i: (cidx(i), 0, 0, 0)), rowblk],
        out_specs=[pl.BlockSpec((CHUNK, 3 * w), lambda i: (cidx(i), 0)), rowblk,
                   pl.BlockSpec((CHUNK, HEAD_DIM), lambda i: (cidx(i), 0)), pv, pv, pv],
        scratch_shapes=[pltpu.VMEM((n_heads, HEAD_DIM, HEAD_DIM), F32)],
        compiler_params=_cparams(("arbitrary",)))(qkv, qkv, qkv, proj, proj, alog_row, dtb_row, nw_row, states, dout)


def glu_gate_fwd(y2, gl, bias, *, name):
    n_tok, w = y2.shape
    tm = _tile(n_tok, (256, 128))

    def body(y_ref, g_ref, b_ref, o_ref):
        o_ref[...] = (y_ref[...] * jax.nn.sigmoid(g_ref[...] + b_ref[...])).astype(BF16)

    row = pl.BlockSpec((tm, w), lambda i: (i, 0))
    vec = pl.BlockSpec((1, w), lambda i: (0, 0))
    return pl.pallas_call(body, name=name, out_shape=jax.ShapeDtypeStruct((n_tok, w), BF16), grid=(n_tok // tm,),
                          in_specs=[row, row, vec], out_specs=row, compiler_params=_cparams(("parallel",)))(y2, gl, bias)


def glu_gate_bwd(y2, gl, bias, dys, *, name):
    n_tok, w = y2.shape
    tm = _tile(n_tok, (256, 128))

    def body(y_ref, g_ref, b_ref, d_ref, dy_ref, dg_ref, db_ref):
        @pl.when(pl.program_id(0) == 0)
        def _():
            db_ref[...] = jnp.zeros_like(db_ref)

        sg = jax.nn.sigmoid(g_ref[...] + b_ref[...])
        d = d_ref[...]
        dy_ref[...] = d * sg
        dg = d * y_ref[...] * sg * (1.0 - sg)
        dg_ref[...] = dg.astype(dg_ref.dtype)
        db_ref[...] += jnp.sum(dg, axis=0, keepdims=True)

    row = pl.BlockSpec((tm, w), lambda i: (i, 0))
    vec = pl.BlockSpec((1, w), lambda i: (0, 0))
    return pl.pallas_call(
        body, name=name, out_shape=[jax.ShapeDtypeStruct((n_tok, w), F32), jax.ShapeDtypeStruct((n_tok, w), BF16),
                                    jax.ShapeDtypeStruct((1, w), F32)],
        grid=(n_tok // tm,), in_specs=[row, row, vec, row], out_specs=[row, row, vec],
        compiler_params=_cparams(("arbitrary",)))(y2, gl, bias, dys)


def merge_fwd(proj, bs, bd, *, name):
    n_tok, d = bs.shape
    tm = _tile(n_tok, (256, 128))

    def body(gs_ref, gd_ref, bs_ref, bd_ref, o_ref):
        o_ref[...] = (jax.nn.sigmoid(gs_ref[...]) * bs_ref[...]
                      + jax.nn.sigmoid(gd_ref[...]) * bd_ref[...]).astype(BF16)

    row = pl.BlockSpec((tm, d), lambda i: (i, 0))
    return pl.pallas_call(
        body, name=name, out_shape=jax.ShapeDtypeStruct((n_tok, d), BF16), grid=(n_tok // tm,),
        in_specs=[row, pl.BlockSpec((tm, d), lambda i: (i, 1)), row, row], out_specs=row,
        compiler_params=_cparams(("parallel",)))(proj, proj, bs, bd)


def merge_bwd(proj, bs, bd, dm, *, name):
    n_tok, d = bs.shape
    tm = _tile(n_tok, (256, 128))

    def body(gs_ref, gd_ref, bs_ref, bd_ref, dm_ref, dbs_ref, dbd_ref, dgs_ref, dgd_ref):
        dmv = dm_ref[...]
        ss = jax.nn.sigmoid(gs_ref[...])
        sd = jax.nn.sigmoid(gd_ref[...])
        dbs_ref[...] = (ss * dmv).astype(BF16)
        dbd_ref[...] = (sd * dmv).astype(BF16)
        dgs_ref[...] = (dmv * bs_ref[...] * ss * (1.0 - ss)).astype(BF16)
        dgd_ref[...] = (dmv * bd_ref[...] * sd * (1.0 - sd)).astype(BF16)

    row = pl.BlockSpec((tm, d), lambda i: (i, 0))
    return pl.pallas_call(
        body, name=name, out_shape=[jax.ShapeDtypeStruct((n_tok, d), BF16)] * 4, grid=(n_tok // tm,),
        in_specs=[row, pl.BlockSpec((tm, d), lambda i: (i, 1)), row, row, row], out_specs=[row] * 4,
        compiler_params=_cparams(("parallel",)))(proj, proj, bs, bd, dm)


def add_pairs(grads, recv, out_dtype, *, name):
    core = jnp.reshape(lax.axis_index("c"), (1,)).astype(jnp.int32)
    outs = []
    for t, (a, b) in enumerate(zip(grads, recv)):
        n_sh, h, cols = b.shape
        tr = _tile(h, (256, 128, 64, 32, 16))
        nh = h // tr

        def body(c_ref, a_ref, b_ref, o_ref):
            o_ref[...] = (a_ref[...].astype(F32) + b_ref[...].astype(F32)).astype(out_dtype)

        grid_spec = pltpu.PrefetchScalarGridSpec(
            num_scalar_prefetch=1, grid=(n_sh, nh),
            in_specs=[pl.BlockSpec((None, tr, cols), lambda s, i, c_ref, nh=nh: (s, c_ref[0] * nh + i, 0)),
                      pl.BlockSpec((None, tr, cols), lambda s, i, c_ref: (s, i, 0))],
            out_specs=pl.BlockSpec((None, tr, cols), lambda s, i, c_ref: (s, i, 0)))
        outs.append(pl.pallas_call(body, name=f"{name}_{t}", out_shape=jax.ShapeDtypeStruct(b.shape, out_dtype),
                                   grid_spec=grid_spec, compiler_params=_cparams(("parallel", "parallel")))(core, a, b))
    return outs


def add_chips(parts, *, name):
    outs = []
    for t, p in enumerate(parts):
        _, h, cols = p.shape
        tr = _tile(h, (256, 128, 64, 32, 16))

        def body(p0, p1, p2, p3, o_ref):
            o_ref[...] = ((p0[...].astype(F32) + p1[...].astype(F32)) + p2[...].astype(F32)) + p3[...].astype(F32)

        specs = [pl.BlockSpec((None, tr, cols), lambda i, s=s: (s, i, 0)) for s in range(N_CHIPS)]
        outs.append(pl.pallas_call(body, name=f"{name}_{t}", out_shape=jax.ShapeDtypeStruct((h, cols), F32),
                                   grid=(h // tr,), in_specs=specs, out_specs=pl.BlockSpec((tr, cols), lambda i: (i, 0)),
                                   compiler_params=_cparams(("parallel",)))(p, p, p, p))
    return outs


ADAMW_BLOCK_BYTES = 3 * 512 * 1024


def adamw(w, g, m, v, *, name):
    shape = w.shape
    cols = shape[-1]
    rows = w.size // cols
    tr = _tile(rows, tuple(t for t in (1024, 512, 256, 128, 64, 32, 16, 8) if t * cols * 4 <= ADAMW_BLOCK_BYTES))
    c1 = 1.0 / (1.0 - ADAM_B1 ** ADAM_STEP)
    c2 = 1.0 / (1.0 - ADAM_B2 ** ADAM_STEP)

    def body(w_ref, g_ref, m_ref, v_ref, d_ref, nm_ref, nv_ref):
        gv = g_ref[...]
        nm = ADAM_B1 * m_ref[...] + (1.0 - ADAM_B1) * gv
        nv = ADAM_B2 * v_ref[...] + (1.0 - ADAM_B2) * (gv * gv)
        d_ref[...] = -ADAM_LR * ((nm * c1) / (jnp.sqrt(nv * c2) + ADAM_EPS) + ADAM_WD * w_ref[...])
        nm_ref[...] = nm
        nv_ref[...] = nv

    blk = pl.BlockSpec((tr, cols), lambda i: (i, 0))
    outs = pl.pallas_call(body, name=name, out_shape=[jax.ShapeDtypeStruct((rows, cols), F32)] * 3, grid=(rows // tr,),
                          in_specs=[blk] * 4, out_specs=[blk] * 3, compiler_params=_cparams(("parallel",)))(
        w.reshape(rows, cols), g.reshape(rows, cols), m.reshape(rows, cols), v.reshape(rows, cols))
    return [o.reshape(shape) for o in outs]


def _place():
    return lax.axis_index("x"), lax.axis_index("y"), lax.axis_index("c")


def _other_chips(x, y):
    return [(1 - x, y), (x, 1 - y), (1 - x, 1 - y)]


ANY = pl.BlockSpec(memory_space=pl.ANY)
STAGE_BYTES = 1 << 20


def _stage_shape(rows, cols, dtype):
    mult = 32 // jnp.dtype(dtype).itemsize
    per_row = (-(-cols // 128) * 128) * jnp.dtype(dtype).itemsize
    chunk = max(mult, STAGE_BYTES // per_row // mult * mult)
    return pltpu.VMEM((2, min(chunk, rows), cols), dtype)


def _staged_copy(src, dst, buf, sem_in, sem_out, k):
    rows, chunk = src.shape[0], buf.shape[1]
    pending = []
    for i, r0 in enumerate(range(0, rows, chunk)):
        sz = min(chunk, rows - r0)
        slot = i % 2
        if i >= 2:
            pending[i - 2].wait()
        stage = buf.at[slot, pl.ds(0, sz)]
        cin = pltpu.make_async_copy(src.at[pl.ds(r0, sz)], stage, sem_in.at[2 * k + slot])
        cin.start()
        cin.wait()
        cout = pltpu.make_async_copy(stage, dst.at[pl.ds(r0, sz)], sem_out.at[2 * k + slot])
        cout.start()
        pending.append(cout)
    for cp in pending[max(0, len(pending) - 2):]:
        cp.wait()


def gather_chips(blocks, halve, *, name):
    n = len(blocks)

    def body(*refs):
        ins, outs = refs[:n], refs[n:2 * n]
        send_sems, recv_sems, fwd_send, fwd_recv, stage_in, stage_out = refs[2 * n:2 * n + 6]
        bufs = refs[2 * n + 6:]
        x, y, c = _place()
        me = 2 * x + y
        chips = _other_chips(x, y)
        sibling = (x, y, 1 - c)
        sends, fwds = [], []
        for t in range(n):
            for j, (px, py) in enumerate(chips):
                if halve[t]:
                    h = ins[t].shape[0] // 2
                    rows = pl.ds(c * h, h)
                    src, dst = ins[t].at[rows], outs[t].at[me, rows]
                else:
                    src, dst = ins[t], outs[t].at[me]
                cp = pltpu.make_async_remote_copy(src_ref=src, dst_ref=dst, send_sem=send_sems.at[3 * t + j],
                                                  recv_sem=recv_sems.at[3 * t + j], device_id=(px, py, c),
                                                  device_id_type=MESH)
                cp.start()
                sends.append(cp)
        for t in range(n):
            _staged_copy(ins[t], outs[t].at[me], bufs[t], stage_in, stage_out, t)
        for t in range(n):
            for j, (px, py) in enumerate(chips):
                src_chip = 2 * px + py
                if halve[t]:
                    h = ins[t].shape[0] // 2
                    rows = pl.ds(c * h, h)
                    landed = outs[t].at[src_chip, rows]
                    pltpu.make_async_remote_copy(src_ref=landed, dst_ref=landed, send_sem=send_sems.at[3 * t + j],
                                                 recv_sem=recv_sems.at[3 * t + j], device_id=(px, py, c),
                                                 device_id_type=MESH).wait_recv()
                    cp = pltpu.make_async_remote_copy(src_ref=landed, dst_ref=landed, send_sem=fwd_send.at[3 * t + j],
                                                      recv_sem=fwd_recv.at[3 * t + j], device_id=sibling,
                                                      device_id_type=MESH)
                    cp.start()
                    fwds.append(cp)
                else:
                    landed = outs[t].at[src_chip]
                    pltpu.make_async_remote_copy(src_ref=landed, dst_ref=landed, send_sem=send_sems.at[3 * t + j],
                                                 recv_sem=recv_sems.at[3 * t + j], device_id=(px, py, c),
                                                 device_id_type=MESH).wait_recv()
        for t in range(n):
            if not halve[t]:
                continue
            h = ins[t].shape[0] // 2
            for j, (px, py) in enumerate(chips):
                theirs = outs[t].at[2 * px + py, pl.ds((1 - c) * h, h)]
                pltpu.make_async_remote_copy(src_ref=theirs, dst_ref=theirs, send_sem=fwd_send.at[3 * t + j],
                                             recv_sem=fwd_recv.at[3 * t + j], device_id=sibling,
                                             device_id_type=MESH).wait_recv()
        for cp in sends + fwds:
            cp.wait_send()

    return pl.pallas_call(
        body, name=name, out_shape=[jax.ShapeDtypeStruct((N_CHIPS,) + b.shape, b.dtype) for b in blocks],
        in_specs=[ANY] * n, out_specs=[ANY] * n,
        scratch_shapes=[pltpu.SemaphoreType.DMA((3 * n,))] * 4 + [pltpu.SemaphoreType.DMA((2 * n,))] * 2
        + [_stage_shape(b.shape[0], b.shape[1], b.dtype) for b in blocks],
        compiler_params=pltpu.CompilerParams(has_side_effects=True, vmem_limit_bytes=VMEM_LIMIT))(*blocks)


def pair_split(grads, *, name):
    n = len(grads)

    def body(*refs):
        ins, recv = refs[:n], refs[n:2 * n]
        send_sems, recv_sems = refs[2 * n:]
        x, y, c = _place()
        sibling = (x, y, 1 - c)
        cps = []
        for t in range(n):
            h = ins[t].shape[1] // 2
            cp = pltpu.make_async_remote_copy(src_ref=ins[t].at[:, pl.ds((1 - c) * h, h)], dst_ref=recv[t],
                                              send_sem=send_sems.at[t], recv_sem=recv_sems.at[t], device_id=sibling,
                                              device_id_type=MESH)
            cp.start()
            cps.append(cp)
        for cp in cps:
            cp.wait()

    half = [jax.ShapeDtypeStruct((g.shape[0], g.shape[1] // 2, g.shape[2]), g.dtype) for g in grads]
    return pl.pallas_call(
        body, name=name, out_shape=half, in_specs=[ANY] * n, out_specs=[ANY] * n,
        scratch_shapes=[pltpu.SemaphoreType.DMA((n,))] * 2,
        compiler_params=pltpu.CompilerParams(has_side_effects=True))(*grads)


def chip_exchange(parts, *, name):
    n = len(parts)

    def body(*refs):
        ins, outs = refs[:n], refs[n:2 * n]
        send_sems, recv_sems, stage_in, stage_out = refs[2 * n:2 * n + 4]
        bufs = refs[2 * n + 4:]
        x, y, c = _place()
        me = 2 * x + y
        chips = _other_chips(x, y)
        cps = []
        for t in range(n):
            for j, (px, py) in enumerate(chips):
                cp = pltpu.make_async_remote_copy(src_ref=ins[t].at[2 * px + py], dst_ref=outs[t].at[me],
                                                  send_sem=send_sems.at[3 * t + j], recv_sem=recv_sems.at[3 * t + j],
                                                  device_id=(px, py, c), device_id_type=MESH)
                cp.start()
                cps.append(cp)
        for t in range(n):
            _staged_copy(ins[t].at[me], outs[t].at[me], bufs[t], stage_in, stage_out, t)
        for t in range(n):
            for j, (px, py) in enumerate(chips):
                landed = outs[t].at[2 * px + py]
                pltpu.make_async_remote_copy(src_ref=landed, dst_ref=landed, send_sem=send_sems.at[3 * t + j],
                                             recv_sem=recv_sems.at[3 * t + j], device_id=(px, py, c),
                                             device_id_type=MESH).wait_recv()
        for cp in cps:
            cp.wait_send()

    return pl.pallas_call(
        body, name=name, out_shape=[jax.ShapeDtypeStruct(p.shape, p.dtype) for p in parts],
        in_specs=[ANY] * n, out_specs=[ANY] * n,
        scratch_shapes=[pltpu.SemaphoreType.DMA((3 * n,))] * 2 + [pltpu.SemaphoreType.DMA((2 * n,))] * 2
        + [_stage_shape(p.shape[1], p.shape[2], p.dtype) for p in parts],
        compiler_params=pltpu.CompilerParams(has_side_effects=True, vmem_limit_bytes=VMEM_LIMIT))(*parts)


def pair_join(halves, *, name):
    n = len(halves)

    def body(*refs):
        ins, outs = refs[:n], refs[n:2 * n]
        send_sems, recv_sems, stage_in, stage_out = refs[2 * n:2 * n + 4]
        bufs = refs[2 * n + 4:]
        x, y, c = _place()
        sibling = (x, y, 1 - c)
        cps = []
        for t in range(n):
            h = ins[t].shape[0]
            cp = pltpu.make_async_remote_copy(src_ref=ins[t], dst_ref=outs[t].at[pl.ds(c * h, h)],
                                              send_sem=send_sems.at[t], recv_sem=recv_sems.at[t], device_id=sibling,
                                              device_id_type=MESH)
            cp.start()
            cps.append(cp)
        for t in range(n):
            h = ins[t].shape[0]
            _staged_copy(ins[t], outs[t].at[pl.ds(c * h, h)], bufs[t], stage_in, stage_out, t)
        for t in range(n):
            h = ins[t].shape[0]
            theirs = outs[t].at[pl.ds((1 - c) * h, h)]
            pltpu.make_async_remote_copy(src_ref=theirs, dst_ref=theirs, send_sem=send_sems.at[t],
                                         recv_sem=recv_sems.at[t], device_id=sibling, device_id_type=MESH).wait_recv()
        for cp in cps:
            cp.wait_send()

    return pl.pallas_call(
        body, name=name, out_shape=[jax.ShapeDtypeStruct((2 * p.shape[0], p.shape[1]), p.dtype) for p in halves],
        in_specs=[ANY] * n, out_specs=[ANY] * n,
        scratch_shapes=[pltpu.SemaphoreType.DMA((n,))] * 2 + [pltpu.SemaphoreType.DMA((2 * n,))] * 2
        + [_stage_shape(p.shape[0], p.shape[1], p.dtype) for p in halves],
        compiler_params=pltpu.CompilerParams(has_side_effects=True, vmem_limit_bytes=VMEM_LIMIT))(*halves)


def reduce_scatter(grads, pay_dtype, *, name):
    recv = pair_split(grads, name=name + "_split")
    part = add_pairs(grads, recv, pay_dtype, name=name + "_add2")
    got = chip_exchange(part, name=name + "_xchg")
    half = add_chips(got, name=name + "_add4")
    return pair_join(half, name=name + "_join")


def _in_sizes(d, w, n_heads):
    return (w, w, w, w, w, n_heads, n_heads, d, d)


def _wcat_from_gathered(wg, d, w, n_heads):
    full = jnp.concatenate([wg[s] for s in range(N_CHIPS)], axis=1)
    sizes = _in_sizes(d, w, n_heads)
    offs = [0]
    for s in sizes:
        offs.append(offs[-1] + s)
    pieces = [full[:, offs[i]:offs[i + 1]] for i in range(len(sizes))]
    u, q, k, v, z, beta, a, gs, gd = pieces
    pad = jnp.zeros((full.shape[0], SMALL_W - 2 * n_heads), full.dtype)
    return jnp.concatenate([gs, gd, u, q, k, v, z, beta, a, pad], axis=1)


def _wcat_grad_to_shards(dwcat, d, w, n_heads):
    gs, gd = dwcat[:, :d], dwcat[:, d:2 * d]
    o = 2 * d
    u, q, k, v, z = [dwcat[:, o + i * w:o + (i + 1) * w] for i in range(5)]
    o += 5 * w
    beta, a = dwcat[:, o:o + n_heads], dwcat[:, o + n_heads:o + 2 * n_heads]
    full = jnp.concatenate([u, q, k, v, z, beta, a, gs, gd], axis=1)
    return jnp.stack(jnp.split(full, N_CHIPS, axis=1))


def _s5_discretize(a_re, a_im, log_dt, b_re, b_im):
    dt = jnp.exp(log_dt)[:, None]
    mag = jnp.exp(a_re * dt)
    lbar_r, lbar_i = mag * jnp.cos(a_im * dt), mag * jnp.sin(a_im * dt)
    den = a_re * a_re + a_im * a_im
    zr, zi = _cmul(lbar_r - 1.0, lbar_i, a_re / den, -a_im / den)
    bbar_r, bbar_i = _cmul(zr[:, :, None], zi[:, :, None], b_re, b_im)
    return lbar_r, lbar_i, bbar_r, bbar_i


def _blockdiag_in(bbar):
    g, p, h = bbar.shape
    t = g // GROUPS_PER_TILE
    bb = bbar.reshape(t, GROUPS_PER_TILE, p, h).transpose(0, 1, 3, 2)
    eye = jnp.eye(GROUPS_PER_TILE, dtype=bbar.dtype)
    return jnp.einsum('tjhp,jk->tjhkp', bb, eye).reshape(t, GROUPS_PER_TILE * h, GROUPS_PER_TILE * p)


def _blockdiag_in_grad(dblk, g, p, h):
    t = g // GROUPS_PER_TILE
    d5 = dblk.reshape(t, GROUPS_PER_TILE, h, GROUPS_PER_TILE, p)
    eye = jnp.eye(GROUPS_PER_TILE, dtype=dblk.dtype)
    diag = jnp.einsum('tjhkp,jk->tjhp', d5, eye)
    return diag.transpose(0, 1, 3, 2).reshape(g, p, h)


def _blockdiag_out(cmat):
    g, h, p = cmat.shape
    t = g // GROUPS_PER_TILE
    cc = cmat.reshape(t, GROUPS_PER_TILE, h, p).transpose(0, 1, 3, 2)
    eye = jnp.eye(GROUPS_PER_TILE, dtype=cmat.dtype)
    return jnp.einsum('tjph,jk->tjpkh', cc, eye).reshape(t, GROUPS_PER_TILE * p, GROUPS_PER_TILE * h)


def _blockdiag_out_grad(dblk, g, h, p):
    t = g // GROUPS_PER_TILE
    d5 = dblk.reshape(t, GROUPS_PER_TILE, p, GROUPS_PER_TILE, h)
    eye = jnp.eye(GROUPS_PER_TILE, dtype=dblk.dtype)
    diag = jnp.einsum('tjpkh,jk->tjph', d5, eye)
    return diag.transpose(0, 1, 3, 2).reshape(g, h, p)


def _pad_row(v, width):
    return jnp.pad(v.reshape(1, -1), ((0, 0), (0, width - v.size)))


def _pack(arrs, rows_mult):
    flat = jnp.concatenate([a.reshape(-1) for a in arrs])
    per = 128 * rows_mult
    total = -(-flat.size // per) * per
    return jnp.pad(flat, (0, total - flat.size))


def _unpack(flat, like):
    out, o = [], 0
    for a in like:
        out.append(flat[o:o + a.size].reshape(a.shape))
        o += a.size
    return out


def kernel(x, ffn1_w_gu, ffn1_w_down, ln1_g, ln1_b, w_in, conv_w, ssm_a_re, ssm_a_im, ssm_log_dt, ssm_b_re, ssm_b_im, ssm_c_re, ssm_c_im, ssm_d, glu_w, glu_b, gdn_a_log, gdn_dt_bias, gdn_norm_w, w_br_ssm, w_br_gdn, w_out, ln2_g, ln2_b, ffn2_w_gu, ffn2_w_down, ln3_g, ln3_b, loss_target, m_ffn1_w_gu, m_ffn1_w_down, m_ln1_g, m_ln1_b, m_w_in, m_conv_w, m_ssm_a_re, m_ssm_a_im, m_ssm_log_dt, m_ssm_b_re, m_ssm_b_im, m_ssm_c_re, m_ssm_c_im, m_ssm_d, m_glu_w, m_glu_b, m_gdn_a_log, m_gdn_dt_bias, m_gdn_norm_w, m_w_br_ssm, m_w_br_gdn, m_w_out, m_ln2_g, m_ln2_b, m_ffn2_w_gu, m_ffn2_w_down, m_ln3_g, m_ln3_b, v_ffn1_w_gu, v_ffn1_w_down, v_ln1_g, v_ln1_b, v_w_in, v_conv_w, v_ssm_a_re, v_ssm_a_im, v_ssm_log_dt, v_ssm_b_re, v_ssm_b_im, v_ssm_c_re, v_ssm_c_im, v_ssm_d, v_glu_w, v_glu_b, v_gdn_a_log, v_gdn_dt_bias, v_gdn_norm_w, v_w_br_ssm, v_w_br_gdn, v_w_out, v_ln2_g, v_ln2_b, v_ffn2_w_gu, v_ffn2_w_down, v_ln3_g, v_ln3_b):
    args = locals()
    wts = {n: args[n] for n in WEIGHT_NAMES}
    mom = {n: args["m_" + n] for n in WEIGHT_NAMES}
    var = {n: args["v_" + n] for n in WEIGHT_NAMES}

    depth = ln1_g.shape[0]
    n_tok, d = x.shape[1], x.shape[2]
    w = glu_w.shape[-1]
    n_heads = gdn_a_log.shape[-1]
    n_groups, n_state, grp = ssm_b_re.shape[1], ssm_b_re.shape[2], ssm_b_re.shape[3]
    alpha = (2.0 * depth) ** 0.25
    u_col0 = 2 * d
    qkv_col0 = u_col0 + w
    z_col0 = u_col0 + 4 * w
    small_col0 = u_col0 + 5 * w
    x_idx, y_idx, _ = _place()
    chip = 2 * x_idx + y_idx

    xcur = x[0]
    xcur_b = xcur.astype(BF16)
    saved = []
    for l in range(depth):
        shards = [wts[n][l].astype(BF16) for n in BIG] + [conv_w[l]]
        gathered = gather_chips(shards, [True] * len(BIG) + [False], name=f"gather_l{l}")
        gw = dict(zip(BIG, gathered[:-1]))
        conv_full = jnp.concatenate([gathered[-1][s] for s in range(N_CHIPS)], axis=1)
        wgu1, wgu2 = gw['ffn1_w_gu'], gw['ffn2_w_gu']
        wd1 = gw['ffn1_w_down'].reshape(-1, d)
        wd2 = gw['ffn2_w_down'].reshape(-1, d)
        wcat = _wcat_from_gathered(gw['w_in'], d, w, n_heads)
        wglu = gw['glu_w'].reshape(w, w)
        wbs, wbd = gw['w_br_ssm'], gw['w_br_gdn']
        wout = gw['w_out'].reshape(d, d)
        f = wd1.shape[0]

        vec = lambda v: v[l].reshape(1, -1)
        x0, x0b = xcur, xcur_b
        h1 = mm(x0b, wgu1, b_nb=N_CHIPS, out_nb=2, name=f"ffn_up")
        r1, x1, x1b = down_res_ln(h1, wd1, x0, vec(ln1_g), vec(ln1_b), swiglu=True, alpha=alpha, scale=0.5,
                                  name="ffn_down")
        proj = mm(x1b, wcat, name="in_proj")
        (lbar_r, lbar_i, bbar_r, bbar_i), disc_vjp = jax.vjp(
            _s5_discretize, ssm_a_re[l], ssm_a_im[l], ssm_log_dt[l], ssm_b_re[l], ssm_b_im[l])
        s5w = (_blockdiag_in(bbar_r), _blockdiag_in(bbar_i), _blockdiag_out(ssm_c_re[l]), _blockdiag_out(ssm_c_im[l]),
               lbar_r.reshape(1, -1), lbar_i.reshape(1, -1), ssm_d[l].reshape(1, -1))
        ypre, y2, y2b = s5_fwd(proj, u_col0, *s5w, name="s5_fwd")
        gl = mm(y2b, wglu, name="glu_proj")
        ys = glu_gate_fwd(y2, gl, vec(glu_b), name="glu_gate")
        qkv = conv_fwd(proj, qkv_col0, conv_full, name="conv_fwd")
        gdn_rows = (_pad_row(gdn_a_log[l], HEAD_DIM), _pad_row(gdn_dt_bias[l], HEAD_DIM), gdn_norm_w[l].reshape(1, -1))
        yg, states = gdn_fwd(qkv, proj, z_col0, small_col0, *gdn_rows, n_heads, name="gdn_fwd")
        bs = mm(ys, wbs, b_nb=N_CHIPS, name="br_ssm")
        bd = mm(yg, wbd, b_nb=N_CHIPS, name="br_gdn")
        mrg = merge_fwd(proj, bs, bd, name="merge")
        r2, x2, x2b = down_res_ln(mrg, wout, x1, vec(ln2_g), vec(ln2_b), swiglu=False, alpha=alpha, scale=1.0,
                                  name="mix_out")
        h3 = mm(x2b, wgu2, b_nb=N_CHIPS, out_nb=2, name="ffn_up")
        r3, x3, x3b = down_res_ln(h3, wd2, x2, vec(ln3_g), vec(ln3_b), swiglu=True, alpha=alpha, scale=0.5,
                                  name="ffn_down")
        saved.append(dict(x0b=x0b, h1=h1, r1=r1, x1b=x1b, proj=proj, s5w=s5w, disc_vjp=disc_vjp, ypre=ypre, y2=y2,
                          y2b=y2b, gl=gl, ys=ys, qkv=qkv, gdn_rows=gdn_rows, yg=yg, states=states, bs=bs, bd=bd, mrg=mrg,
                          r2=r2, x2b=x2b, h3=h3, r3=r3, wgu1=wgu1, wgu2=wgu2, wd1=wd1, wd2=wd2, wcat=wcat, wglu=wglu,
                          wbs=wbs, wbd=wbd, wout=wout, conv_full=conv_full))
        xcur, xcur_b = x3, x3b

    dy, loss_blk = loss_head(xcur, loss_target[0], name="loss_head")
    loss = lax.psum(loss_blk[0, 0], ("x", "y", "c"))

    big_grads = {n: [None] * depth for n in BIG}
    small_grads = {n: [None] * depth for n in SMALL}
    for l in reversed(range(depth)):
        s = saved[l]
        vec = lambda v: v[l].reshape(1, -1)

        def ffn_back(dx_out, r, g_ln, xin, hh, wgu, wd):
            dr, drb, dg, db = ln_bwd(r, g_ln, dx_out, name="ln_bwd")
            dh, act = ffn_bwd_mid(dr, wd, hh, scale=0.5, name="ffn_bwd_mid")
            dwd = mm(act, drb, ta=True, out_dtype=BF16, out_scale=0.5, name="ffn_dwd")
            dwgu = mm(xin, dh, ta=True, b_nb=2, out_nb=N_CHIPS, out_dtype=BF16, name="ffn_dwgu")
            dxin = mm(dh, wgu, tb=True, a_nb=2, b_nb=N_CHIPS, add=dr, add_scale=alpha, name="ffn_dx")
            return dxin, dg, db, dwgu, dwd.reshape(N_CHIPS, -1, d)

        dx2, dg3, db3, dwgu2, dwd2 = ffn_back(dy, s['r3'], vec(ln3_g), s['x2b'], s['h3'], s['wgu2'], s['wd2'])
        dr2, dr2b, dg2, db2 = ln_bwd(s['r2'], vec(ln2_g), dx2, name="ln_bwd")
        dmrg = mm(dr2b, s['wout'], tb=True, name="mix_dm")
        dwout = mm(s['mrg'], dr2b, ta=True, out_dtype=BF16, name="mix_dwout").reshape(N_CHIPS, -1, d)
        dbs, dbd, dgs, dgd = merge_bwd(s['proj'], s['bs'], s['bd'], dmrg, name="merge_bwd")
        dwbs = mm(s['ys'], dbs, ta=True, out_nb=N_CHIPS, out_dtype=BF16, name="br_dw")
        dwbd = mm(s['yg'], dbd, ta=True, out_nb=N_CHIPS, out_dtype=BF16, name="br_dw")
        dys = mm(dbs, s['wbs'], tb=True, b_nb=N_CHIPS, name="br_dx")
        dyg = mm(dbd, s['wbd'], tb=True, b_nb=N_CHIPS, name="br_dx")
        dy2a, dgl, dglu_b = glu_gate_bwd(s['y2'], s['gl'], vec(glu_b), dys, name="glu_gate_bwd")
        dwglu = mm(s['y2b'], dgl, ta=True, out_dtype=BF16, name="glu_dw").reshape(N_CHIPS, -1, w)
        dy2 = mm(dgl, s['wglu'], tb=True, add=dy2a, name="glu_dx")
        du, dbr, dbi, dcr, dci, dlr, dli, dd = s5_bwd(s['proj'], u_col0, s['ypre'], dy2, *s['s5w'], name="s5_bwd")
        dqkv_act, dz, dsmall, dalog, ddtb, dnw = gdn_bwd(s['qkv'], s['proj'], z_col0, small_col0, *s['gdn_rows'],
                                                           s['states'], dyg, n_heads, name="gdn_bwd")
        dqkv, dconv = conv_bwd(s['proj'], qkv_col0, s['conv_full'], dqkv_act, name="conv_bwd")
        dsmall_w = jnp.pad(dsmall, ((0, 0), (0, SMALL_W - HEAD_DIM)))
        dproj = jnp.concatenate([dgs, dgd, du, dqkv, dz, dsmall_w], axis=1)
        dwcat = mm(s['x1b'], dproj, ta=True, out_dtype=BF16, name="in_dw")
        dx1 = mm(dproj, s['wcat'], tb=True, add=dr2, add_scale=alpha, name="in_dx")
        dx0, dg1, db1, dwgu1, dwd1 = ffn_back(dx1, s['r1'], vec(ln1_g), s['x0b'], s['h1'], s['wgu1'], s['wd1'])
        dy = dx0

        da_re, da_im, dlog_dt, db_re, db_im = s['disc_vjp'](
            (dlr.reshape(n_groups, n_state), dli.reshape(n_groups, n_state),
             _blockdiag_in_grad(dbr, n_groups, n_state, grp), _blockdiag_in_grad(dbi, n_groups, n_state, grp)))
        sg = dict(ln1_g=dg1, ln1_b=db1, ln2_g=dg2, ln2_b=db2, ln3_g=dg3, ln3_b=db3, conv_w=dconv,
                  ssm_a_re=da_re, ssm_a_im=da_im, ssm_log_dt=dlog_dt, ssm_b_re=db_re, ssm_b_im=db_im,
                  ssm_c_re=_blockdiag_out_grad(dcr, n_groups, grp, n_state),
                  ssm_c_im=_blockdiag_out_grad(dci, n_groups, grp, n_state), ssm_d=dd, glu_b=dglu_b,
                  gdn_a_log=dalog[0, :n_heads], gdn_dt_bias=ddtb[0, :n_heads], gdn_norm_w=dnw)
        for n in SMALL:
            small_grads[n][l] = sg[n].reshape(-1)
        layer_grads = dict(ffn1_w_gu=dwgu1, ffn1_w_down=dwd1, w_in=_wcat_grad_to_shards(dwcat, d, w, n_heads),
                           glu_w=dwglu, w_br_ssm=dwbs, w_br_gdn=dwbd, w_out=dwout, ffn2_w_gu=dwgu2, ffn2_w_down=dwd2)
        reduced = reduce_scatter([layer_grads[n] for n in BIG], BF16, name="rs_big")
        for n, g in zip(BIG, reduced):
            big_grads[n][l] = g
    grad_x = dy[None]

    small_list = [jnp.stack(small_grads[n]) for n in SMALL]
    packed = _pack(small_list, 16 * N_CHIPS).reshape(N_CHIPS, -1, 128)
    red = reduce_scatter([packed], F32, name="rs_small")
    full = gather_chips(red, [True], name="gather_small")[0].reshape(-1)
    small_red = dict(zip(SMALL, _unpack(full, small_list)))
    cw_cols = conv_w.shape[-1]
    dconv_full = small_red['conv_w'].reshape(depth, CONV_K, N_CHIPS, cw_cols)
    small_red['conv_w'] = lax.dynamic_index_in_dim(dconv_full, chip, axis=2, keepdims=False)

    grads = {}
    for n in BIG:
        grads[n] = jnp.stack(big_grads[n]).reshape(wts[n].shape)
    for n in SMALL:
        grads[n] = small_red[n].reshape(wts[n].shape)

    delta, new_m, new_v = {}, {}, {}
    for n in BIG:
        delta[n], new_m[n], new_v[n] = adamw(wts[n], grads[n], mom[n], var[n], name="adamw_big")
    pk = lambda dct: _pack([dct[n] for n in SMALL], 1024).reshape(-1, 128)
    sd, sm, sv = adamw(pk(wts), pk(grads), pk(mom), pk(var), name="adamw_small")
    like = [wts[n] for n in SMALL]
    for n, a, b, c in zip(SMALL, _unpack(sd.reshape(-1), like), _unpack(sm.reshape(-1), like),
                          _unpack(sv.reshape(-1), like)):
        delta[n], new_m[n], new_v[n] = a, b, c

    return (loss, grad_x, *[grads[n] for n in WEIGHT_NAMES], *[delta[n] for n in WEIGHT_NAMES],
            *[new_m[n] for n in WEIGHT_NAMES], *[new_v[n] for n in WEIGHT_NAMES])
```

```python
import math

import jax
import jax.numpy as jnp
from jax import lax
from jax.experimental import pallas as pl
from jax.experimental.pallas import tpu as pltpu

F32 = jnp.float32
BF16 = jnp.bfloat16
HI = lax.Precision.HIGHEST
MESH = pl.DeviceIdType.MESH

N_CHIPS = 4
SSM_GROUP = 16
SSM_STATE = 64
GROUPS_PER_TILE = 8
HEAD_DIM = 128
CHUNK = 64
CONV_K = 4
LN_EPS = 1e-5
RMS_EPS = 1e-6
L2_EPS = 1e-6
SMALL_W = 512
ADAM_LR, ADAM_B1, ADAM_B2, ADAM_EPS, ADAM_WD, ADAM_STEP = 0.001, 0.9, 0.999, 1e-08, 0.01, 10
VMEM_LIMIT = 56 * 1024 * 1024

WEIGHT_NAMES = ['ffn1_w_gu', 'ffn1_w_down', 'ln1_g', 'ln1_b', 'w_in', 'conv_w', 'ssm_a_re', 'ssm_a_im', 'ssm_log_dt',
                'ssm_b_re', 'ssm_b_im', 'ssm_c_re', 'ssm_c_im', 'ssm_d', 'glu_w', 'glu_b', 'gdn_a_log', 'gdn_dt_bias',
                'gdn_norm_w', 'w_br_ssm', 'w_br_gdn', 'w_out', 'ln2_g', 'ln2_b', 'ffn2_w_gu', 'ffn2_w_down', 'ln3_g',
                'ln3_b']
BIG = ['ffn1_w_gu', 'ffn1_w_down', 'w_in', 'glu_w', 'w_br_ssm', 'w_br_gdn', 'w_out', 'ffn2_w_gu', 'ffn2_w_down']
SMALL = [n for n in WEIGHT_NAMES if n not in BIG]


def _tile(dim, prefs):
    for p in prefs:
        if dim % p == 0:
            return p
    return dim


def _cparams(sem):
    return pltpu.CompilerParams(dimension_semantics=sem, vmem_limit_bytes=VMEM_LIMIT)


def _ln(r, g, b):
    mu = jnp.mean(r, axis=-1, keepdims=True)
    xc = r - mu
    var = jnp.mean(xc * xc, axis=-1, keepdims=True)
    return xc * lax.rsqrt(var + LN_EPS) * g + b


def _lshape(x, nb):
    return (x.shape[0], x.shape[1]) if nb == 1 else (x.shape[1], x.shape[2] * nb)


def _cb_spec(x, nb, tr, tc, rc):
    if nb == 1:
        return pl.BlockSpec((tr, tc), lambda *g: rc(*g))
    cps = x.shape[2] // tc

    def imap(*g):
        r, c = rc(*g)
        return (c // cps, r, c % cps)
    return pl.BlockSpec((None, tr, tc), imap)


MM_VMEM_BUDGET = 40 * 1024 * 1024
MM_TILES = (2048, 1024, 512, 256, 128)
MM_FULL_K = 2048


def mm(a, b, *, name, ta=False, tb=False, a_nb=1, b_nb=1, out_nb=1, out_dtype=F32, add=None, add_scale=1.0,
       out_scale=1.0):
    ar, ac = _lshape(a, a_nb)
    br, bc = _lshape(b, b_nb)
    m, k = (ac, ar) if ta else (ar, ac)
    k2, n = (bc, br) if tb else (br, bc)
    assert k == k2, (name, a.shape, b.shape)
    assert a.dtype == BF16 and b.dtype == BF16, name

    def lim(dim, *nbs):
        q = dim
        for nb in nbs:
            q = math.gcd(q, dim // nb)
        return q
    lm = lim(m, a_nb if ta else 1)
    ln = lim(n, out_nb, 1 if tb else b_nb)
    lk = lim(k, 1 if ta else a_nb, b_nb if tb else 1)
    so = jnp.dtype(out_dtype).itemsize
    if k <= MM_FULL_K and lk == k:
        tks = [k]
    else:
        tks = [t for t in range(MM_FULL_K, 127, -128) if lk % t == 0]
    best = None
    for ck in tks:
        for cm in MM_TILES:
            for cn in MM_TILES:
                if lm % cm or ln % cn:
                    continue
                est = 2 * (cm * ck * 2 + ck * cn * 2 + cm * cn * so + (cm * cn * 4 if add is not None else 0))
                est += cm * cn * 4 * (2 if k > ck else 1) + (cm * ck * 2 + 512 * cm * 4 if ta else 0)
                score = min(cm, 512) * cn * ck
                if est <= MM_VMEM_BUDGET and (best is None or score > best[0]):
                    best = (score, cm, cn, ck)
    _, tm, tn, tk = best
    nk = k // tk
    dn = (((1,), (1 if tb else 0,)), ((), ()))

    def body(*refs):
        a_ref, b_ref = refs[:2]
        add_ref = refs[2] if add is not None else None
        o_ref = refs[3 if add is not None else 2]
        scratch = refs[(4 if add is not None else 3):]
        acc = scratch[0] if nk > 1 else None
        kk = pl.program_id(2)

        if ta:
            at_ref = scratch[-1]

            def transpose_block():
                for r0 in range(0, tk, 512):
                    r1 = min(tk, r0 + 512)
                    at_ref[:, r0:r1] = a_ref[r0:r1, :].astype(F32).T.astype(BF16)
            if nk == 1:
                pl.when(pl.program_id(1) == 0)(transpose_block)
            else:
                transpose_block()
            av = at_ref[...]
        else:
            av = a_ref[...]
        part = lax.dot_general(av, b_ref[...], dn, preferred_element_type=F32)

        def finish(r):
            if out_scale != 1.0:
                r = r * out_scale
            if add is not None:
                r = r + add_scale * add_ref[...]
            o_ref[...] = r.astype(out_dtype)

        if nk == 1:
            finish(part)
        else:
            @pl.when(kk == 0)
            def _():
                acc[...] = part

            @pl.when(kk > 0)
            def _():
                acc[...] += part

            @pl.when(kk == nk - 1)
            def _():
                finish(acc[...])

    if ta:
        a_spec = _cb_spec(a, a_nb, tk, tm, lambda i, j, kk: (kk, i))
    else:
        a_spec = _cb_spec(a, a_nb, tm, tk, lambda i, j, kk: (i, kk))
    if tb:
        b_spec = _cb_spec(b, b_nb, tn, tk, lambda i, j, kk: (j, kk))
    else:
        b_spec = _cb_spec(b, b_nb, tk, tn, lambda i, j, kk: (kk, j))
    if out_nb == 1:
        out_shape = jax.ShapeDtypeStruct((m, n), out_dtype)
        out_spec = pl.BlockSpec((tm, tn), lambda i, j, kk: (i, j))
    else:
        out_shape = jax.ShapeDtypeStruct((out_nb, m, n // out_nb), out_dtype)
        out_spec = _cb_spec(out_shape, out_nb, tm, tn, lambda i, j, kk: (i, j))
    in_specs = [a_spec, b_spec]
    args = [a, b]
    if add is not None:
        in_specs.append(pl.BlockSpec((tm, tn), lambda i, j, kk: (i, j)))
        args.append(add)
    scratch = ([pltpu.VMEM((tm, tn), F32)] if nk > 1 else []) + ([pltpu.VMEM((tm, tk), BF16)] if ta else [])
    return pl.pallas_call(
        body, name=name, out_shape=out_shape, grid=(m // tm, n // tn, nk), in_specs=in_specs, out_specs=out_spec,
        scratch_shapes=scratch, compiler_params=_cparams(("parallel", "arbitrary", "arbitrary")))(*args)


def down_res_ln(src, w, x, g, b, *, swiglu, alpha, scale, name):
    n_tok, d = x.shape
    kdim = w.shape[0]
    tm = _tile(n_tok, (512, 256, 128))
    tk = _tile(kdim, (512, 256, 128))
    nk = kdim // tk

    def body(s_ref, w_ref, x_ref, g_ref, b_ref, r_ref, y_ref, yb_ref, acc):
        kk = pl.program_id(1)

        @pl.when(kk == 0)
        def _():
            acc[...] = jnp.zeros_like(acc)

        if swiglu:
            gate = s_ref[0]
            a = gate * jax.nn.sigmoid(gate) * s_ref[1]
        else:
            a = s_ref[...]
        acc[...] += jnp.dot(a.astype(BF16), w_ref[...], preferred_element_type=F32)

        @pl.when(kk == nk - 1)
        def _():
            r = alpha * x_ref[...] + scale * acc[...]
            r_ref[...] = r
            y = _ln(r, g_ref[...], b_ref[...])
            y_ref[...] = y
            yb_ref[...] = y.astype(BF16)

    if swiglu:
        s_spec = pl.BlockSpec((2, tm, tk), lambda i, kk: (0, i, kk))
    else:
        s_spec = pl.BlockSpec((tm, tk), lambda i, kk: (i, kk))
    row = pl.BlockSpec((tm, d), lambda i, kk: (i, 0))
    vec = pl.BlockSpec((1, d), lambda i, kk: (0, 0))
    return pl.pallas_call(
        body, name=name, out_shape=[jax.ShapeDtypeStruct((n_tok, d), F32)] * 2 + [jax.ShapeDtypeStruct((n_tok, d), BF16)],
        grid=(n_tok // tm, nk),
        in_specs=[s_spec, pl.BlockSpec((tk, d), lambda i, kk: (kk, 0)), row, vec, vec], out_specs=[row, row, row],
        scratch_shapes=[pltpu.VMEM((tm, d), F32)], compiler_params=_cparams(("parallel", "arbitrary")))(src, w, x, g, b)


def ln_bwd(r, g, dy, *, name):
    n_tok, d = r.shape
    tm = _tile(n_tok, (256, 128))

    def body(r_ref, g_ref, dy_ref, dr_ref, drb_ref, dg_ref, db_ref):
        i = pl.program_id(0)

        @pl.when(i == 0)
        def _():
            dg_ref[...] = jnp.zeros_like(dg_ref)
            db_ref[...] = jnp.zeros_like(db_ref)

        rv = r_ref[...]
        dyv = dy_ref[...]
        mu = jnp.mean(rv, axis=-1, keepdims=True)
        xc = rv - mu
        rstd = lax.rsqrt(jnp.mean(xc * xc, axis=-1, keepdims=True) + LN_EPS)
        xh = xc * rstd
        dxh = dyv * g_ref[...]
        dr = rstd * (dxh - jnp.mean(dxh, axis=-1, keepdims=True) - xh * jnp.mean(dxh * xh, axis=-1, keepdims=True))
        dr_ref[...] = dr
        drb_ref[...] = dr.astype(BF16)
        dg_ref[...] += jnp.sum(dyv * xh, axis=0, keepdims=True)
        db_ref[...] += jnp.sum(dyv, axis=0, keepdims=True)

    row = pl.BlockSpec((tm, d), lambda i: (i, 0))
    vec = pl.BlockSpec((1, d), lambda i: (0, 0))
    return pl.pallas_call(
        body, name=name, out_shape=[jax.ShapeDtypeStruct((n_tok, d), F32), jax.ShapeDtypeStruct((n_tok, d), BF16),
                                    jax.ShapeDtypeStruct((1, d), F32), jax.ShapeDtypeStruct((1, d), F32)],
        grid=(n_tok // tm,), in_specs=[row, vec, row], out_specs=[row, row, vec, vec],
        compiler_params=_cparams(("arbitrary",)))(r, g, dy)


def loss_head(y, target, *, name):
    n_tok, d = y.shape
    tm = _tile(n_tok, (256, 128))

    def body(y_ref, t_ref, dy_ref, l_ref):
        i = pl.program_id(0)

        @pl.when(i == 0)
        def _():
            l_ref[...] = jnp.zeros_like(l_ref)

        e = y_ref[...] - t_ref[...]
        dy_ref[...] = e * (1.0 / d)
        s = jnp.sum(jnp.mean(e * e, axis=-1, keepdims=True), axis=0, keepdims=True)
        l_ref[...] += 0.5 * s

    row = pl.BlockSpec((tm, d), lambda i: (i, 0))
    return pl.pallas_call(
        body, name=name, out_shape=[jax.ShapeDtypeStruct((n_tok, d), F32), jax.ShapeDtypeStruct((8, 128), F32)],
        grid=(n_tok // tm,), in_specs=[row, row], out_specs=[row, pl.BlockSpec((8, 128), lambda i: (0, 0))],
        compiler_params=_cparams(("arbitrary",)))(y, target)


def ffn_bwd_mid(dr, wd, h, *, scale, name):
    n_tok, d = dr.shape
    f = wd.shape[0]
    tm = _tile(n_tok, (512, 256, 128))
    tf = _tile(f, (512, 256, 128))

    def body(dr_ref, w_ref, h_ref, dh_ref, a_ref):
        dy = (scale * dr_ref[...]).astype(BF16)
        da = lax.dot_general(dy, w_ref[...], (((1,), (1,)), ((), ())), preferred_element_type=F32)
        gate = h_ref[0]
        up = h_ref[1]
        sg = jax.nn.sigmoid(gate)
        s = gate * sg
        a_ref[...] = (s * up).astype(BF16)
        dh_ref[0] = (da * up * (sg * (1.0 + gate * (1.0 - sg)))).astype(BF16)
        dh_ref[1] = (da * s).astype(BF16)

    return pl.pallas_call(
        body, name=name, out_shape=[jax.ShapeDtypeStruct((2, n_tok, f), BF16), jax.ShapeDtypeStruct((n_tok, f), BF16)],
        grid=(n_tok // tm, f // tf),
        in_specs=[pl.BlockSpec((tm, d), lambda i, j: (i, 0)), pl.BlockSpec((tf, d), lambda i, j: (j, 0)),
                  pl.BlockSpec((2, tm, tf), lambda i, j: (0, i, j))],
        out_specs=[pl.BlockSpec((2, tm, tf), lambda i, j: (0, i, j)), pl.BlockSpec((tm, tf), lambda i, j: (i, j))],
        compiler_params=_cparams(("parallel", "parallel")))(dr, wd, h)


def _cmul(ar, ai, br, bi):
    return ar * br - ai * bi, ar * bi + ai * br


def _scan_blocks(sr_ref, si_ref, lr, li, *, reverse):
    n_rows, width = sr_ref.shape
    n_blk = n_rows // 8
    row = lax.broadcasted_iota(jnp.int32, (8, width), 0)
    pr = jnp.broadcast_to(lr, (8, width))
    pi = jnp.broadcast_to(-li if reverse else li, (8, width))

    def shifted(v, dist, fill=0.0):
        if reverse:
            return jnp.where(row < 8 - dist, pltpu.roll(v, 8 - dist, 0), fill)
        return jnp.where(row >= dist, pltpu.roll(v, dist, 0), fill)

    p1 = (pr, pi)
    p2 = _cmul(*p1, *p1)
    p4 = _cmul(*p2, *p2)
    wr, wi = pr, pi
    for dist in (1, 2, 4):
        wr, wi = _cmul(wr, wi, shifted(wr, dist, 1.0), shifted(wi, dist, 0.0))
    edge = 0 if reverse else 7

    def step(i, carry):
        cr, ci = carry
        blk = (n_blk - 1 - i) if reverse else i
        r0 = pl.multiple_of(blk * 8, 8)
        xr = sr_ref[pl.ds(r0, 8), :]
        xi = si_ref[pl.ds(r0, 8), :]
        for dist, (qr, qi) in ((1, p1), (2, p2), (4, p4)):
            tr, ti = _cmul(qr, qi, shifted(xr, dist), shifted(xi, dist))
            xr, xi = xr + tr, xi + ti
        tr, ti = _cmul(wr, wi, cr, ci)
        xr, xi = xr + tr, xi + ti
        sr_ref[pl.ds(r0, 8), :] = xr
        si_ref[pl.ds(r0, 8), :] = xi
        br = jnp.where(row == edge, xr, 0.0)
        bi = jnp.where(row == edge, xi, 0.0)
        for dist in (1, 2, 4):
            br = br + pltpu.roll(br, dist, 0)
            bi = bi + pltpu.roll(bi, dist, 0)
        return br, bi

    zero = jnp.zeros((8, width), F32)
    lax.fori_loop(0, n_blk, step, (zero, zero), unroll=2)


def _s5_specs(n_tok, u_blk0):
    gw = GROUPS_PER_TILE * SSM_GROUP
    sw = GROUPS_PER_TILE * SSM_STATE
    u_spec = pl.BlockSpec((n_tok, gw), lambda t: (0, u_blk0 + t))
    col = pl.BlockSpec((n_tok, gw), lambda t: (0, t))
    bmat = pl.BlockSpec((None, gw, sw), lambda t: (t, 0, 0))
    cmat = pl.BlockSpec((None, sw, gw), lambda t: (t, 0, 0))
    lvec = pl.BlockSpec((1, sw), lambda t: (0, t))
    dvec = pl.BlockSpec((1, gw), lambda t: (0, t))
    return gw, sw, u_spec, col, bmat, cmat, lvec, dvec


def s5_fwd(proj, u_col0, bblk_r, bblk_i, cblk_r, cblk_i, lbar_r, lbar_i, dskip, *, name):
    n_tok = proj.shape[0]
    n_tiles = bblk_r.shape[0]
    gw, sw, u_spec, col, bmat, cmat, lvec, dvec = _s5_specs(n_tok, u_col0 // (GROUPS_PER_TILE * SSM_GROUP))

    def body(u_ref, br_ref, bi_ref, cr_ref, ci_ref, lr_ref, li_ref, d_ref, ypre_ref, y2_ref, y2b_ref, sr, si):
        u = u_ref[...]
        ub = u.astype(BF16)
        sr[...] = jnp.dot(ub, br_ref[...].astype(BF16), preferred_element_type=F32)
        si[...] = jnp.dot(ub, bi_ref[...].astype(BF16), preferred_element_type=F32)
        _scan_blocks(sr, si, lr_ref[...], li_ref[...], reverse=False)
        y = (jnp.dot(sr[...].astype(BF16), cr_ref[...].astype(BF16), preferred_element_type=F32)
             - jnp.dot(si[...].astype(BF16), ci_ref[...].astype(BF16), preferred_element_type=F32)
             + d_ref[...] * u)
        ypre_ref[...] = y
        y2 = jax.nn.gelu(y)
        y2_ref[...] = y2
        y2b_ref[...] = y2.astype(BF16)

    width = n_tiles * gw
    return pl.pallas_call(
        body, name=name,
        out_shape=[jax.ShapeDtypeStruct((n_tok, width), F32)] * 2 + [jax.ShapeDtypeStruct((n_tok, width), BF16)],
        grid=(n_tiles,), in_specs=[u_spec, bmat, bmat, cmat, cmat, lvec, lvec, dvec], out_specs=[col, col, col],
        scratch_shapes=[pltpu.VMEM((n_tok, sw), F32)] * 2,
        compiler_params=_cparams(("parallel",)))(proj, bblk_r, bblk_i, cblk_r, cblk_i, lbar_r, lbar_i, dskip)


def s5_bwd(proj, u_col0, ypre, dy2, bblk_r, bblk_i, cblk_r, cblk_i, lbar_r, lbar_i, dskip, *, name):
    n_tok = proj.shape[0]
    n_tiles = bblk_r.shape[0]
    gw, sw, u_spec, col, bmat, cmat, lvec, dvec = _s5_specs(n_tok, u_col0 // (GROUPS_PER_TILE * SSM_GROUP))
    rb = _tile(n_tok, (512, 256, 128))
    tn_dims = (((0,), (0,)), ((), ()))
    nt_dims = (((1,), (1,)), ((), ()))

    def body(u_ref, ypre_ref, dy2_ref, br_ref, bi_ref, cr_ref, ci_ref, lr_ref, li_ref, d_ref,
             du_ref, dbr_ref, dbi_ref, dcr_ref, dci_ref, dlr_ref, dli_ref, dd_ref, sr, si, gr, gi):
        u = u_ref[...]
        ub = u.astype(BF16)
        bmr = br_ref[...].astype(BF16)
        bmi = bi_ref[...].astype(BF16)
        cmr = cr_ref[...].astype(BF16)
        cmi = ci_ref[...].astype(BF16)
        lr = lr_ref[...]
        li = li_ref[...]
        _, gelu_vjp = jax.vjp(jax.nn.gelu, ypre_ref[...])
        dyp = gelu_vjp(dy2_ref[...])[0]
        dyb = dyp.astype(BF16)
        sr[...] = jnp.dot(ub, bmr, preferred_element_type=F32)
        si[...] = jnp.dot(ub, bmi, preferred_element_type=F32)
        _scan_blocks(sr, si, lr, li, reverse=False)
        gr[...] = lax.dot_general(dyb, cmr, nt_dims, preferred_element_type=F32)
        gi[...] = -lax.dot_general(dyb, cmi, nt_dims, preferred_element_type=F32)
        _scan_blocks(gr, gi, lr, li, reverse=True)
        srb = sr[...].astype(BF16)
        sib = si[...].astype(BF16)
        dcr_ref[...] = lax.dot_general(srb, dyb, tn_dims, preferred_element_type=F32)
        dci_ref[...] = -lax.dot_general(sib, dyb, tn_dims, preferred_element_type=F32)
        grb = gr[...].astype(BF16)
        gib = gi[...].astype(BF16)
        dbr_ref[...] = lax.dot_general(ub, grb, tn_dims, preferred_element_type=F32)
        dbi_ref[...] = lax.dot_general(ub, gib, tn_dims, preferred_element_type=F32)
        du_ref[...] = (lax.dot_general(grb, bmr, nt_dims, preferred_element_type=F32)
                       + lax.dot_general(gib, bmi, nt_dims, preferred_element_type=F32)
                       + d_ref[...] * dyp).astype(du_ref.dtype)
        dd_ref[...] = jnp.sum(dyp * u, axis=0, keepdims=True)
        inv = 1.0 / (lr * lr + li * li)
        qr = lr * inv
        qi = -li * inv
        acc_r = jnp.zeros((1, sw), F32)
        acc_i = jnp.zeros((1, sw), F32)
        for blk in range(n_tok // rb):
            rows = pl.ds(blk * rb, rb)
            ubb = u_ref[rows, :].astype(BF16)
            er = sr[rows, :] - jnp.dot(ubb, bmr, preferred_element_type=F32)
            ei = si[rows, :] - jnp.dot(ubb, bmi, preferred_element_type=F32)
            pr, pi = _cmul(er, ei, qr, qi)
            ar = gr[rows, :]
            ai = gi[rows, :]
            acc_r = acc_r + jnp.sum(ar * pr + ai * pi, axis=0, keepdims=True)
            acc_i = acc_i + jnp.sum(ai * pr - ar * pi, axis=0, keepdims=True)
        dlr_ref[...] = acc_r
        dli_ref[...] = acc_i

    width = n_tiles * gw
    out_shape = [jax.ShapeDtypeStruct((n_tok, width), BF16),
                 jax.ShapeDtypeStruct(bblk_r.shape, F32), jax.ShapeDtypeStruct(bblk_r.shape, F32),
                 jax.ShapeDtypeStruct(cblk_r.shape, F32), jax.ShapeDtypeStruct(cblk_r.shape, F32),
                 jax.ShapeDtypeStruct(lbar_r.shape, F32), jax.ShapeDtypeStruct(lbar_r.shape, F32),
                 jax.ShapeDtypeStruct(dskip.shape, F32)]
    return pl.pallas_call(
        body, name=name, out_shape=out_shape, grid=(n_tiles,),
        in_specs=[u_spec, col, col, bmat, bmat, cmat, cmat, lvec, lvec, dvec],
        out_specs=[col, bmat, bmat, cmat, cmat, lvec, lvec, dvec],
        scratch_shapes=[pltpu.VMEM((n_tok, sw), F32)] * 4,
        compiler_params=_cparams(("parallel",)))(proj, ypre, dy2, bblk_r, bblk_i, cblk_r, cblk_i, lbar_r, lbar_i, dskip)


CONV_ROWS = 256
CONV_COLS = 512


def _conv_pre(x_ref, w_ref, blk, n_blk):
    r0 = blk * CONV_ROWS
    if blk == 0:
        ext = jnp.concatenate([jnp.zeros((8, CONV_COLS), F32), x_ref[0:CONV_ROWS, :]], axis=0)
    else:
        ext = x_ref[r0 - 8:r0 + CONV_ROWS, :]
    taps = []
    c = None
    for j in range(CONV_K):
        s = CONV_K - 1 - j
        xs = ext[8:] if s == 0 else pltpu.roll(ext, s, 0)[8:]
        taps.append(xs)
        term = w_ref[j:j + 1, :] * xs
        c = term if c is None else c + term
    return c, taps


def conv_fwd(proj, col0, conv_w, *, name):
    n_tok = proj.shape[0]
    width = conv_w.shape[1]
    n_blk = n_tok // CONV_ROWS
    cb0 = col0 // CONV_COLS

    def body(x_ref, w_ref, o_ref):
        for blk in range(n_blk):
            c, _ = _conv_pre(x_ref, w_ref, blk, n_blk)
            o_ref[blk * CONV_ROWS:(blk + 1) * CONV_ROWS, :] = c * jax.nn.sigmoid(c)

    return pl.pallas_call(
        body, name=name, out_shape=jax.ShapeDtypeStruct((n_tok, width), F32), grid=(width // CONV_COLS,),
        in_specs=[pl.BlockSpec((n_tok, CONV_COLS), lambda j: (0, cb0 + j)),
                  pl.BlockSpec((CONV_K, CONV_COLS), lambda j: (0, j))],
        out_specs=pl.BlockSpec((n_tok, CONV_COLS), lambda j: (0, j)),
        compiler_params=_cparams(("parallel",)))(proj, conv_w)


def conv_bwd(proj, col0, conv_w, dout, *, name):
    n_tok = proj.shape[0]
    width = conv_w.shape[1]
    n_blk = n_tok // CONV_ROWS
    cb0 = col0 // CONV_COLS

    def body(x_ref, w_ref, do_ref, dx_ref, dw_ref, dc):
        dws = [jnp.zeros((1, CONV_COLS), F32) for _ in range(CONV_K)]
        for blk in range(n_blk):
            rows = slice(blk * CONV_ROWS, (blk + 1) * CONV_ROWS)
            c, taps = _conv_pre(x_ref, w_ref, blk, n_blk)
            sg = jax.nn.sigmoid(c)
            dcv = do_ref[rows, :] * (sg * (1.0 + c * (1.0 - sg)))
            dc[rows, :] = dcv
            for j in range(CONV_K):
                dws[j] = dws[j] + jnp.sum(dcv * taps[j], axis=0, keepdims=True)
        dc[n_tok:n_tok + 8, :] = jnp.zeros((8, CONV_COLS), F32)
        for j in range(CONV_K):
            dw_ref[j:j + 1, :] = dws[j]
        for blk in range(n_blk):
            r0 = blk * CONV_ROWS
            ext = dc[r0:r0 + CONV_ROWS + 8, :]
            dx = None
            for j in range(CONV_K):
                s = CONV_K - 1 - j
                sh = ext[:CONV_ROWS] if s == 0 else pltpu.roll(ext, CONV_ROWS + 8 - s, 0)[:CONV_ROWS]
                term = w_ref[j:j + 1, :] * sh
                dx = term if dx is None else dx + term
            dx_ref[r0:r0 + CONV_ROWS, :] = dx.astype(dx_ref.dtype)

    return pl.pallas_call(
        body, name=name, out_shape=[jax.ShapeDtypeStruct((n_tok, width), BF16), jax.ShapeDtypeStruct(conv_w.shape, F32)],
        grid=(width // CONV_COLS,),
        in_specs=[pl.BlockSpec((n_tok, CONV_COLS), lambda j: (0, cb0 + j)),
                  pl.BlockSpec((CONV_K, CONV_COLS), lambda j: (0, j)),
                  pl.BlockSpec((n_tok, CONV_COLS), lambda j: (0, j))],
        out_specs=[pl.BlockSpec((n_tok, CONV_COLS), lambda j: (0, j)), pl.BlockSpec((CONV_K, CONV_COLS), lambda j: (0, j))],
        scratch_shapes=[pltpu.VMEM((n_tok + 8, CONV_COLS), F32)],
        compiler_params=_cparams(("parallel",)))(proj, conv_w, dout)


GDN_PREC = lax.Precision.HIGH


def _neumann_inverse(lower):
    n = lower.shape[0]
    eye = (lax.broadcasted_iota(jnp.int32, (n, n), 0) == lax.broadcasted_iota(jnp.int32, (n, n), 1)).astype(F32)
    xp = -lower
    tinv = eye + xp
    power = 2
    while power < n:
        xp = jnp.dot(xp, xp, precision=GDN_PREC, preferred_element_type=F32)
        tinv = tinv + jnp.dot(tinv, xp, precision=GDN_PREC, preferred_element_type=F32)
        power *= 2
    return tinv


@jax.custom_vjp
def _unit_lower_inverse(lower):
    return _neumann_inverse(lower)


def _unit_lower_inverse_fwd(lower):
    tinv = _neumann_inverse(lower)
    return tinv, tinv


def _unit_lower_inverse_bwd(tinv, ct):
    left = lax.dot_general(tinv, ct, (((0,), (0,)), ((), ())), precision=GDN_PREC, preferred_element_type=F32)
    return (-lax.dot_general(left, tinv, (((1,), (1,)), ((), ())), precision=GDN_PREC, preferred_element_type=F32),)


_unit_lower_inverse.defvjp(_unit_lower_inverse_fwd, _unit_lower_inverse_bwd)


def _gdn_chunk(head, n_heads, state, q, k, v, z, bsmall, alog_row, dtb_row, nw):
    c = CHUNK
    lane = lax.broadcasted_iota(jnp.int32, (c, HEAD_DIM), 1)
    lane1 = lax.broadcasted_iota(jnp.int32, (1, HEAD_DIM), 1)
    ri = lax.broadcasted_iota(jnp.int32, (c, c), 0)
    ci = lax.broadcasted_iota(jnp.int32, (c, c), 1)
    causal = ri >= ci
    strict = ri > ci
    tril = causal.astype(F32)
    bl = jnp.sum(jnp.where(lane == head, bsmall, 0.0), axis=-1, keepdims=True)
    al = jnp.sum(jnp.where(lane == n_heads + head, bsmall, 0.0), axis=-1, keepdims=True)
    alog = jnp.sum(jnp.where(lane1 == head, alog_row, 0.0), axis=-1, keepdims=True)
    dtb = jnp.sum(jnp.where(lane1 == head, dtb_row, 0.0), axis=-1, keepdims=True)

    qn = q * lax.rsqrt(jnp.sum(q * q, axis=-1, keepdims=True) + L2_EPS) * (HEAD_DIM ** -0.5)
    kn = k * lax.rsqrt(jnp.sum(k * k, axis=-1, keepdims=True) + L2_EPS)
    beta = jax.nn.sigmoid(bl)
    xg = al + dtb
    g = -jnp.exp(alog) * (jnp.maximum(xg, 0.0) + jnp.log(1.0 + jnp.exp(-jnp.abs(xg))))
    g_wide = jnp.broadcast_to(g, (c, HEAD_DIM))
    gc = jnp.dot(tril, g_wide, precision=HI, preferred_element_type=F32)
    gc_rows = jnp.broadcast_to(jnp.mean(gc, axis=-1, keepdims=True), (c, c))
    gc_cols = gc.T[:c, :]
    g_tot = jnp.sum(g, axis=0, keepdims=True)
    decay = jnp.exp(jnp.where(causal, gc_rows - gc_cols, -1e30))
    egc = jnp.exp(gc)
    kb = kn * beta
    knb = kn.astype(BF16)
    nt = (((1,), (1,)), ((), ()))
    lower = jnp.where(strict, lax.dot_general(kb.astype(BF16), knb, nt, preferred_element_type=F32) * decay, 0.0)
    tinv = _unit_lower_inverse(lower)
    u_val = jnp.dot(tinv, v * beta, precision=GDN_PREC, preferred_element_type=F32)
    w_key = jnp.dot(tinv, kb * egc, precision=GDN_PREC, preferred_element_type=F32)
    attn = lax.dot_general(qn.astype(BF16), knb, nt, preferred_element_type=F32) * decay
    q_dec = qn * egc
    k_dec = kn * jnp.exp(g_tot - gc)
    sb = state.astype(BF16)
    v_new = u_val - jnp.dot(w_key.astype(BF16), sb, preferred_element_type=F32)
    vnb = v_new.astype(BF16)
    o = (jnp.dot(q_dec.astype(BF16), sb, preferred_element_type=F32)
         + jnp.dot(attn.astype(BF16), vnb, preferred_element_type=F32))
    new_state = state * jnp.exp(g_tot) + lax.dot_general(k_dec.astype(BF16), vnb, (((0,), (0,)), ((), ())),
                                                         preferred_element_type=F32)
    o = o * lax.rsqrt(jnp.mean(o * o, axis=-1, keepdims=True) + RMS_EPS) * nw
    o = o * (z * jax.nn.sigmoid(z))
    return o, new_state


def _gdn_in_specs(n_heads, qkv_width_blocks, z_blk, small_blk, rev, n_chunks):
    w = n_heads * HEAD_DIM

    def cidx(i):
        return (n_chunks - 1 - i) if rev else i
    qs = pl.BlockSpec((CHUNK, w), lambda i: (cidx(i), 0))
    ks = pl.BlockSpec((CHUNK, w), lambda i: (cidx(i), 1))
    vs = pl.BlockSpec((CHUNK, w), lambda i: (cidx(i), 2))
    zs = pl.BlockSpec((CHUNK, w), lambda i: (cidx(i), z_blk))
    bs = pl.BlockSpec((CHUNK, HEAD_DIM), lambda i: (cidx(i), small_blk))
    pv = pl.BlockSpec((1, HEAD_DIM), lambda i: (0, 0))
    return cidx, qs, ks, vs, zs, bs, pv


def gdn_fwd(qkv, proj, z_col0, small_col0, alog_row, dtb_row, nw_row, n_heads, *, name):
    n_tok = qkv.shape[0]
    w = n_heads * HEAD_DIM
    n_chunks = n_tok // CHUNK
    cidx, qs, ks, vs, zs, bs, pv = _gdn_in_specs(n_heads, 3, z_col0 // w, small_col0 // HEAD_DIM, False, n_chunks)

    def body(q_ref, k_ref, v_ref, z_ref, b_ref, al_ref, dt_ref, nw_ref, o_ref, s_ref, state):
        @pl.when(pl.program_id(0) == 0)
        def _():
            state[...] = jnp.zeros_like(state)

        bsm = b_ref[...]
        for h in range(n_heads):
            cols = slice(h * HEAD_DIM, (h + 1) * HEAD_DIM)
            st = state[h]
            s_ref[h] = st
            o, ns = _gdn_chunk(h, n_heads, st, q_ref[:, cols], k_ref[:, cols], v_ref[:, cols], z_ref[:, cols], bsm,
                               al_ref[...], dt_ref[...], nw_ref[...])
            o_ref[:, cols] = o.astype(BF16)
            state[h] = ns

    return pl.pallas_call(
        body, name=name,
        out_shape=[jax.ShapeDtypeStruct((n_tok, w), BF16),
                   jax.ShapeDtypeStruct((n_chunks, n_heads, HEAD_DIM, HEAD_DIM), F32)],
        grid=(n_chunks,), in_specs=[qs, ks, vs, zs, bs, pv, pv, pv],
        out_specs=[pl.BlockSpec((CHUNK, w), lambda i: (i, 0)),
                   pl.BlockSpec((None, n_heads, HEAD_DIM, HEAD_DIM), lambda i: (i, 0, 0, 0))],
        scratch_shapes=[pltpu.VMEM((n_heads, HEAD_DIM, HEAD_DIM), F32)],
        compiler_params=_cparams(("arbitrary",)))(qkv, qkv, qkv, proj, proj, alog_row, dtb_row, nw_row)


def gdn_bwd(qkv, proj, z_col0, small_col0, alog_row, dtb_row, nw_row, states, dout, n_heads, *, name):
    n_tok = qkv.shape[0]
    w = n_heads * HEAD_DIM
    n_chunks = n_tok // CHUNK
    cidx, qs, ks, vs, zs, bs, pv = _gdn_in_specs(n_heads, 3, z_col0 // w, small_col0 // HEAD_DIM, True, n_chunks)

    def body(q_ref, k_ref, v_ref, z_ref, b_ref, al_ref, dt_ref, nw_ref, s_ref, do_ref,
             dqkv_ref, dz_ref, db_ref, dal_ref, ddt_ref, dnw_ref, dstate):
        @pl.when(pl.program_id(0) == 0)
        def _():
            dstate[...] = jnp.zeros_like(dstate)
            dal_ref[...] = jnp.zeros_like(dal_ref)
            ddt_ref[...] = jnp.zeros_like(ddt_ref)
            dnw_ref[...] = jnp.zeros_like(dnw_ref)

        bsm = b_ref[...]
        dbs = jnp.zeros((CHUNK, HEAD_DIM), F32)
        dal = jnp.zeros((1, HEAD_DIM), F32)
        ddt = jnp.zeros((1, HEAD_DIM), F32)
        dnw = jnp.zeros((1, HEAD_DIM), F32)
        for h in range(n_heads):
            cols = slice(h * HEAD_DIM, (h + 1) * HEAD_DIM)

            def f(st, q, k, v, z, bb, al, dt, nw, h=h):
                return _gdn_chunk(h, n_heads, st, q, k, v, z, bb, al, dt, nw)
            _, vjp = jax.vjp(f, s_ref[h], q_ref[:, cols], k_ref[:, cols], v_ref[:, cols], z_ref[:, cols], bsm,
                             al_ref[...], dt_ref[...], nw_ref[...])
            dst, dq, dk, dv, dz, dbb, da, dd, dn = vjp((do_ref[:, cols], dstate[h]))
            dstate[h] = dst
            dqkv_ref[:, h * HEAD_DIM:(h + 1) * HEAD_DIM] = dq
            dqkv_ref[:, w + h * HEAD_DIM:w + (h + 1) * HEAD_DIM] = dk
            dqkv_ref[:, 2 * w + h * HEAD_DIM:2 * w + (h + 1) * HEAD_DIM] = dv
            dz_ref[:, cols] = dz.astype(dz_ref.dtype)
            dbs = dbs + dbb
            dal = dal + da
            ddt = ddt + dd
            dnw = dnw + dn
        db_ref[...] = dbs.astype(db_ref.dtype)
        dal_ref[...] += dal
        ddt_ref[...] += ddt
        dnw_ref[...] += dnw

    rowblk = pl.BlockSpec((CHUNK, w), lambda i: (cidx(i), 0))
    return pl.pallas_call(
        body, name=name,
        out_shape=[
            jax.ShapeDtypeStruct((n_tok, 3 * w), F32),
            jax.ShapeDtypeStruct((n_tok, w), BF16), jax.ShapeDtypeStruct((n_tok, HEAD_DIM), BF16),
            jax.ShapeDtypeStruct((1, HEAD_DIM), F32), jax.ShapeDtypeStruct((1, HEAD_DIM), F32),
            jax.ShapeDtypeStruct((1, HEAD_DIM), F32)],
        grid=(n_chunks,),
        in_specs=[qs, ks, vs, zs, bs, pv, pv, pv,
                  pl.BlockSpec((None, n_heads, HEAD_DIM, HEAD_DIM), lambda i: (cidx(i), 0, 0, 0)), rowblk],
        out_specs=[pl.BlockSpec((CHUNK, 3 * w), lambda i: (cidx(i), 0)), rowblk,
                   pl.BlockSpec((CHUNK, HEAD_DIM), lambda i: (cidx(i), 0)), pv, pv, pv],
        scratch_shapes=[pltpu.VMEM((n_heads, HEAD_DIM, HEAD_DIM), F32)],
        compiler_params=_cparams(("arbitrary",)))(qkv, qkv, qkv, proj, proj, alog_row, dtb_row, nw_row, states, dout)


def glu_gate_fwd(y2, gl, bias, *, name):
    n_tok, w = y2.shape
    tm = _tile(n_tok, (256, 128))

    def body(y_ref, g_ref, b_ref, o_ref):
        o_ref[...] = (y_ref[...] * jax.nn.sigmoid(g_ref[...] + b_ref[...])).astype(BF16)

    row = pl.BlockSpec((tm, w), lambda i: (i, 0))
    vec = pl.BlockSpec((1, w), lambda i: (0, 0))
    return pl.pallas_call(body, name=name, out_shape=jax.ShapeDtypeStruct((n_tok, w), BF16), grid=(n_tok // tm,),
                          in_specs=[row, row, vec], out_specs=row, compiler_params=_cparams(("parallel",)))(y2, gl, bias)


def glu_gate_bwd(y2, gl, bias, dys, *, name):
    n_tok, w = y2.shape
    tm = _tile(n_tok, (256, 128))

    def body(y_ref, g_ref, b_ref, d_ref, dy_ref, dg_ref, db_ref):
        @pl.when(pl.program_id(0) == 0)
        def _():
            db_ref[...] = jnp.zeros_like(db_ref)

        sg = jax.nn.sigmoid(g_ref[...] + b_ref[...])
        d = d_ref[...]
        dy_ref[...] = d * sg
        dg = d * y_ref[...] * sg * (1.0 - sg)
        dg_ref[...] = dg.astype(dg_ref.dtype)
        db_ref[...] += jnp.sum(dg, axis=0, keepdims=True)

    row = pl.BlockSpec((tm, w), lambda i: (i, 0))
    vec = pl.BlockSpec((1, w), lambda i: (0, 0))
    return pl.pallas_call(
        body, name=name, out_shape=[jax.ShapeDtypeStruct((n_tok, w), F32), jax.ShapeDtypeStruct((n_tok, w), BF16),
                                    jax.ShapeDtypeStruct((1, w), F32)],
        grid=(n_tok // tm,), in_specs=[row, row, vec, row], out_specs=[row, row, vec],
        compiler_params=_cparams(("arbitrary",)))(y2, gl, bias, dys)


def merge_fwd(proj, bs, bd, *, name):
    n_tok, d = bs.shape
    tm = _tile(n_tok, (256, 128))

    def body(gs_ref, gd_ref, bs_ref, bd_ref, o_ref):
        o_ref[...] = (jax.nn.sigmoid(gs_ref[...]) * bs_ref[...]
                      + jax.nn.sigmoid(gd_ref[...]) * bd_ref[...]).astype(BF16)

    row = pl.BlockSpec((tm, d), lambda i: (i, 0))
    return pl.pallas_call(
        body, name=name, out_shape=jax.ShapeDtypeStruct((n_tok, d), BF16), grid=(n_tok // tm,),
        in_specs=[row, pl.BlockSpec((tm, d), lambda i: (i, 1)), row, row], out_specs=row,
        compiler_params=_cparams(("parallel",)))(proj, proj, bs, bd)


def merge_bwd(proj, bs, bd, dm, *, name):
    n_tok, d = bs.shape
    tm = _tile(n_tok, (256, 128))

    def body(gs_ref, gd_ref, bs_ref, bd_ref, dm_ref, dbs_ref, dbd_ref, dgs_ref, dgd_ref):
        dmv = dm_ref[...]
        ss = jax.nn.sigmoid(gs_ref[...])
        sd = jax.nn.sigmoid(gd_ref[...])
        dbs_ref[...] = (ss * dmv).astype(BF16)
        dbd_ref[...] = (sd * dmv).astype(BF16)
        dgs_ref[...] = (dmv * bs_ref[...] * ss * (1.0 - ss)).astype(BF16)
        dgd_ref[...] = (dmv * bd_ref[...] * sd * (1.0 - sd)).astype(BF16)

    row = pl.BlockSpec((tm, d), lambda i: (i, 0))
    return pl.pallas_call(
        body, name=name, out_shape=[jax.ShapeDtypeStruct((n_tok, d), BF16)] * 4, grid=(n_tok // tm,),
        in_specs=[row, pl.BlockSpec((tm, d), lambda i: (i, 1)), row, row, row], out_specs=[row] * 4,
        compiler_params=_cparams(("parallel",)))(proj, proj, bs, bd, dm)


def add_pairs(grads, recv, out_dtype, *, name):
    core = jnp.reshape(lax.axis_index("c"), (1,)).astype(jnp.int32)
    outs = []
    for t, (a, b) in enumerate(zip(grads, recv)):
        n_sh, h, cols = b.shape
        tr = _tile(h, (256, 128, 64, 32, 16))
        nh = h // tr

        def body(c_ref, a_ref, b_ref, o_ref):
            o_ref[...] = (a_ref[...].astype(F32) + b_ref[...].astype(F32)).astype(out_dtype)

        grid_spec = pltpu.PrefetchScalarGridSpec(
            num_scalar_prefetch=1, grid=(n_sh, nh),
            in_specs=[pl.BlockSpec((None, tr, cols), lambda s, i, c_ref, nh=nh: (s, c_ref[0] * nh + i, 0)),
                      pl.BlockSpec((None, tr, cols), lambda s, i, c_ref: (s, i, 0))],
            out_specs=pl.BlockSpec((None, tr, cols), lambda s, i, c_ref: (s, i, 0)))
        outs.append(pl.pallas_call(body, name=f"{name}_{t}", out_shape=jax.ShapeDtypeStruct(b.shape, out_dtype),
                                   grid_spec=grid_spec, compiler_params=_cparams(("parallel", "parallel")))(core, a, b))
    return outs


def add_chips(parts, *, name):
    outs = []
    for t, p in enumerate(parts):
        _, h, cols = p.shape
        tr = _tile(h, (256, 128, 64, 32, 16))

        def body(p0, p1, p2, p3, o_ref):
            o_ref[...] = ((p0[...].astype(F32) + p1[...].astype(F32)) + p2[...].astype(F32)) + p3[...].astype(F32)

        specs = [pl.BlockSpec((None, tr, cols), lambda i, s=s: (s, i, 0)) for s in range(N_CHIPS)]
        outs.append(pl.pallas_call(body, name=f"{name}_{t}", out_shape=jax.ShapeDtypeStruct((h, cols), F32),
                                   grid=(h // tr,), in_specs=specs, out_specs=pl.BlockSpec((tr, cols), lambda i: (i, 0)),
                                   compiler_params=_cparams(("parallel",)))(p, p, p, p))
    return outs


ADAMW_BLOCK_BYTES = 3 * 512 * 1024


def adamw(w, g, m, v, *, name):
    shape = w.shape
    cols = shape[-1]
    rows = w.size // cols
    tr = _tile(rows, tuple(t for t in (1024, 512, 256, 128, 64, 32, 16, 8) if t * cols * 4 <= ADAMW_BLOCK_BYTES))
    c1 = 1.0 / (1.0 - ADAM_B1 ** ADAM_STEP)
    c2 = 1.0 / (1.0 - ADAM_B2 ** ADAM_STEP)

    def body(w_ref, g_ref, m_ref, v_ref, d_ref, nm_ref, nv_ref):
        gv = g_ref[...]
        nm = ADAM_B1 * m_ref[...] + (1.0 - ADAM_B1) * gv
        nv = ADAM_B2 * v_ref[...] + (1.0 - ADAM_B2) * (gv * gv)
        d_ref[...] = -ADAM_LR * ((nm * c1) / (jnp.sqrt(nv * c2) + ADAM_EPS) + ADAM_WD * w_ref[...])
        nm_ref[...] = nm
        nv_ref[...] = nv

    blk = pl.BlockSpec((tr, cols), lambda i: (i, 0))
    outs = pl.pallas_call(body, name=name, out_shape=[jax.ShapeDtypeStruct((rows, cols), F32)] * 3, grid=(rows // tr,),
                          in_specs=[blk] * 4, out_specs=[blk] * 3, compiler_params=_cparams(("parallel",)))(
        w.reshape(rows, cols), g.reshape(rows, cols), m.reshape(rows, cols), v.reshape(rows, cols))
    return [o.reshape(shape) for o in outs]


def _place():
    return lax.axis_index("x"), lax.axis_index("y"), lax.axis_index("c")


def _other_chips(x, y):
    return [(1 - x, y), (x, 1 - y), (1 - x, 1 - y)]


ANY = pl.BlockSpec(memory_space=pl.ANY)
STAGE_BYTES = 1 << 20


def _stage_shape(rows, cols, dtype):
    mult = 32 // jnp.dtype(dtype).itemsize
    per_row = (-(-cols // 128) * 128) * jnp.dtype(dtype).itemsize
    chunk = max(mult, STAGE_BYTES // per_row // mult * mult)
    return pltpu.VMEM((2, min(chunk, rows), cols), dtype)


def _staged_copy(src, dst, buf, sem_in, sem_out, k):
    rows, chunk = src.shape[0], buf.shape[1]
    pending = []
    for i, r0 in enumerate(range(0, rows, chunk)):
        sz = min(chunk, rows - r0)
        slot = i % 2
        if i >= 2:
            pending[i - 2].wait()
        stage = buf.at[slot, pl.ds(0, sz)]
        cin = pltpu.make_async_copy(src.at[pl.ds(r0, sz)], stage, sem_in.at[2 * k + slot])
        cin.start()
        cin.wait()
        cout = pltpu.make_async_copy(stage, dst.at[pl.ds(r0, sz)], sem_out.at[2 * k + slot])
        cout.start()
        pending.append(cout)
    for cp in pending[max(0, len(pending) - 2):]:
        cp.wait()


def gather_chips(blocks, halve, *, name):
    n = len(blocks)

    def body(*refs):
        ins, outs = refs[:n], refs[n:2 * n]
        send_sems, recv_sems, fwd_send, fwd_recv, stage_in, stage_out = refs[2 * n:2 * n + 6]
        bufs = refs[2 * n + 6:]
        x, y, c = _place()
        me = 2 * x + y
        chips = _other_chips(x, y)
        sibling = (x, y, 1 - c)
        sends, fwds = [], []
        for t in range(n):
            for j, (px, py) in enumerate(chips):
                if halve[t]:
                    h = ins[t].shape[0] // 2
                    rows = pl.ds(c * h, h)
                    src, dst = ins[t].at[rows], outs[t].at[me, rows]
                else:
                    src, dst = ins[t], outs[t].at[me]
                cp = pltpu.make_async_remote_copy(src_ref=src, dst_ref=dst, send_sem=send_sems.at[3 * t + j],
                                                  recv_sem=recv_sems.at[3 * t + j], device_id=(px, py, c),
                                                  device_id_type=MESH)
                cp.start()
                sends.append(cp)
        for t in range(n):
            _staged_copy(ins[t], outs[t].at[me], bufs[t], stage_in, stage_out, t)
        for t in range(n):
            for j, (px, py) in enumerate(chips):
                src_chip = 2 * px + py
                if halve[t]:
                    h = ins[t].shape[0] // 2
                    rows = pl.ds(c * h, h)
                    landed = outs[t].at[src_chip, rows]
                    pltpu.make_async_remote_copy(src_ref=landed, dst_ref=landed, send_sem=send_sems.at[3 * t + j],
                                                 recv_sem=recv_sems.at[3 * t + j], device_id=(px, py, c),
                                                 device_id_type=MESH).wait_recv()
                    cp = pltpu.make_async_remote_copy(src_ref=landed, dst_ref=landed, send_sem=fwd_send.at[3 * t + j],
                                                      recv_sem=fwd_recv.at[3 * t + j], device_id=sibling,
                                                      device_id_type=MESH)
                    cp.start()
                    fwds.append(cp)
                else:
                    landed = outs[t].at[src_chip]
                    pltpu.make_async_remote_copy(src_ref=landed, dst_ref=landed, send_sem=send_sems.at[3 * t + j],
                                                 recv_sem=recv_sems.at[3 * t + j], device_id=(px, py, c),
                                                 device_id_type=MESH).wait_recv()
        for t in range(n):
            if not halve[t]:
                continue
            h = ins[t].shape[0] // 2
            for j, (px, py) in enumerate(chips):
                theirs = outs[t].at[2 * px + py, pl.ds((1 - c) * h, h)]
                pltpu.make_async_remote_copy(src_ref=theirs, dst_ref=theirs, send_sem=fwd_send.at[3 * t + j],
                                             recv_sem=fwd_recv.at[3 * t + j], device_id=sibling,
                                             device_id_type=MESH).wait_recv()
        for cp in sends + fwds:
            cp.wait_send()

    return pl.pallas_call(
        body, name=name, out_shape=[jax.ShapeDtypeStruct((N_CHIPS,) + b.shape, b.dtype) for b in blocks],
        in_specs=[ANY] * n, out_specs=[ANY] * n,
        scratch_shapes=[pltpu.SemaphoreType.DMA((3 * n,))] * 4 + [pltpu.SemaphoreType.DMA((2 * n,))] * 2
        + [_stage_shape(b.shape[0], b.shape[1], b.dtype) for b in blocks],
        compiler_params=pltpu.CompilerParams(has_side_effects=True, vmem_limit_bytes=VMEM_LIMIT))(*blocks)


HBM_SPEC = pl.BlockSpec(memory_space=pltpu.HBM)
SEM_SPEC = pl.BlockSpec(memory_space=pltpu.SEMAPHORE)
DATAFLOW = pltpu.SideEffectType.DATAFLOW_SIDE_EFFECTING


def _in_hbm(v):
    return pltpu.with_memory_space_constraint(v, pltpu.HBM)


def _split_start(srcs, land_shapes, make_copies, *, name):
    n = len(srcs)
    lands = [_in_hbm(lax.empty(s.shape, s.dtype)) for s in land_shapes]

    def body(*refs):
        ins, lands_in = refs[:n], refs[n:2 * n]
        send_sems, recv_sems = refs[2 * n], refs[2 * n + 1]
        token = refs[-1]
        for cp in make_copies(ins, lands_in, send_sems, recv_sems, False):
            cp.start()
        token[...] = jnp.zeros_like(token)

    outs = pl.pallas_call(
        body, name=name,
        out_shape=(pltpu.SemaphoreType.DMA((3 * n,)), pltpu.SemaphoreType.DMA((3 * n,)),
                   *[pltpu.HBM(s.shape, s.dtype) for s in srcs], *[pltpu.HBM(s.shape, s.dtype) for s in land_shapes],
                   jax.ShapeDtypeStruct((8, 128), F32)),
        in_specs=[HBM_SPEC] * (2 * n),
        out_specs=(SEM_SPEC, SEM_SPEC, *([HBM_SPEC] * (2 * n)), pl.BlockSpec(memory_space=pltpu.VMEM)),
        input_output_aliases={i: 2 + i for i in range(2 * n)},
        compiler_params=pltpu.CompilerParams(has_side_effects=DATAFLOW))(*[_in_hbm(s) for s in srcs], *lands)
    return outs[0], outs[1], list(outs[2:2 + n]), list(outs[2 + n:2 + 2 * n]), outs[-1]


def _split_wait(send_sems, recv_sems, srcs, lands, after, make_copies, *, name):
    n = len(srcs)

    def body(*refs):
        ins, lands_in = refs[:n], refs[n:2 * n]
        s_sems, r_sems = refs[2 * n], refs[2 * n + 1]
        token = refs[-1]
        for cp in make_copies(ins, lands_in, s_sems, r_sems, False):
            cp.wait_send()
        for cp in make_copies(ins, lands_in, s_sems, r_sems, True):
            cp.wait_recv()
        token[...] = jnp.zeros_like(token)

    outs = pl.pallas_call(
        body, name=name,
        out_shape=(*[pltpu.HBM(s.shape, s.dtype) for s in srcs], *[pltpu.HBM(s.shape, s.dtype) for s in lands],
                   jax.ShapeDtypeStruct((8, 128), F32)),
        in_specs=[HBM_SPEC] * (2 * n) + [SEM_SPEC, SEM_SPEC, ANY],
        out_specs=(*([HBM_SPEC] * (2 * n)), pl.BlockSpec(memory_space=pltpu.VMEM)),
        input_output_aliases={i: i for i in range(2 * n)},
        compiler_params=pltpu.CompilerParams(has_side_effects=DATAFLOW))(*srcs, *lands, send_sems, recv_sems, after)
    return list(outs[:n]), list(outs[n:2 * n]), outs[-1]


def _gather_copies(halve):
    def make(ins, lands, send_sems, recv_sems, arrivals):
        x, y, c = _place()
        me = 2 * x + y
        cps = []
        for t in range(len(ins)):
            for j, (px, py) in enumerate(_other_chips(x, y)):
                if halve[t]:
                    h = ins[t].shape[0] // 2
                    rows = pl.ds(c * h, h)
                    src, dst, landed = ins[t].at[rows], lands[t].at[me, rows], lands[t].at[2 * px + py, rows]
                else:
                    src, dst, landed = ins[t], lands[t].at[me], lands[t].at[2 * px + py]
                sem = dict(send_sem=send_sems.at[3 * t + j], recv_sem=recv_sems.at[3 * t + j], device_id=(px, py, c),
                           device_id_type=MESH)
                if arrivals:
                    cps.append(pltpu.make_async_remote_copy(src_ref=landed, dst_ref=landed, **sem))
                else:
                    cps.append(pltpu.make_async_remote_copy(src_ref=src, dst_ref=dst, **sem))
        return cps
    return make


def gather_start(blocks, halve, *, name):
    shapes = [jax.ShapeDtypeStruct((N_CHIPS,) + b.shape, b.dtype) for b in blocks]
    return _split_start(blocks, shapes, _gather_copies(halve), name=name)


def gather_wait(started, halve, after, *, name):
    send_sems, recv_sems, srcs, lands, _ = started
    return _split_wait(send_sems, recv_sems, srcs, lands, after, _gather_copies(halve), name=name)


def gather_finish(blocks, lands, halve, token, *, name):
    n = len(blocks)

    def body(*refs):
        ins, outs = refs[:n], refs[2 * n + 1:3 * n + 1]
        fwd_send, fwd_recv, stage_in, stage_out = refs[3 * n + 1:3 * n + 5]
        bufs = refs[3 * n + 5:]
        x, y, c = _place()
        me = 2 * x + y
        chips = _other_chips(x, y)
        sibling = (x, y, 1 - c)
        fwds = []
        for t in range(n):
            if not halve[t]:
                continue
            h = ins[t].shape[0] // 2
            for j, (px, py) in enumerate(chips):
                landed = outs[t].at[2 * px + py, pl.ds(c * h, h)]
                cp = pltpu.make_async_remote_copy(src_ref=landed, dst_ref=landed, send_sem=fwd_send.at[3 * t + j],
                                                  recv_sem=fwd_recv.at[3 * t + j], device_id=sibling, device_id_type=MESH)
                cp.start()
                fwds.append(cp)
        for t in range(n):
            _staged_copy(ins[t], outs[t].at[me], bufs[t], stage_in, stage_out, t)
        for t in range(n):
            if not halve[t]:
                continue
            h = ins[t].shape[0] // 2
            for j, (px, py) in enumerate(chips):
                theirs = outs[t].at[2 * px + py, pl.ds((1 - c) * h, h)]
                pltpu.make_async_remote_copy(src_ref=theirs, dst_ref=theirs, send_sem=fwd_send.at[3 * t + j],
                                             recv_sem=fwd_recv.at[3 * t + j], device_id=sibling,
                                             device_id_type=MESH).wait_recv()
        for cp in fwds:
            cp.wait_send()

    return pl.pallas_call(
        body, name=name, out_shape=[jax.ShapeDtypeStruct(v.shape, v.dtype) for v in lands],
        in_specs=[ANY] * (2 * n) + [pl.BlockSpec(memory_space=pltpu.VMEM)], out_specs=[ANY] * n,
        input_output_aliases={n + i: i for i in range(n)},
        scratch_shapes=[pltpu.SemaphoreType.DMA((3 * n,))] * 2 + [pltpu.SemaphoreType.DMA((2 * n,))] * 2
        + [_stage_shape(b.shape[0], b.shape[1], b.dtype) for b in blocks],
        compiler_params=pltpu.CompilerParams(has_side_effects=True, vmem_limit_bytes=VMEM_LIMIT))(*blocks, *lands, token)


def _xchg_copies(ins, lands, send_sems, recv_sems, arrivals):
    x, y, c = _place()
    me = 2 * x + y
    cps = []
    for t in range(len(ins)):
        for j, (px, py) in enumerate(_other_chips(x, y)):
            landed = lands[t].at[2 * px + py]
            sem = dict(send_sem=send_sems.at[3 * t + j], recv_sem=recv_sems.at[3 * t + j], device_id=(px, py, c),
                       device_id_type=MESH)
            if arrivals:
                cps.append(pltpu.make_async_remote_copy(src_ref=landed, dst_ref=landed, **sem))
            else:
                cps.append(pltpu.make_async_remote_copy(src_ref=ins[t].at[2 * px + py], dst_ref=lands[t].at[me], **sem))
    return cps


def xchg_start(parts, *, name):
    return _split_start(parts, [jax.ShapeDtypeStruct(p.shape, p.dtype) for p in parts], _xchg_copies, name=name)


def xchg_wait(started, after, *, name):
    send_sems, recv_sems, srcs, lands, _ = started
    return _split_wait(send_sems, recv_sems, srcs, lands, after, _xchg_copies, name=name)


def xchg_finish(parts, lands, *, name):
    n = len(parts)

    def body(*refs):
        ins, outs = refs[:n], refs[2 * n:3 * n]
        stage_in, stage_out = refs[3 * n:3 * n + 2]
        bufs = refs[3 * n + 2:]
        x, y, _ = _place()
        me = 2 * x + y
        for t in range(n):
            _staged_copy(ins[t].at[me], outs[t].at[me], bufs[t], stage_in, stage_out, t)

    return pl.pallas_call(
        body, name=name, out_shape=[jax.ShapeDtypeStruct(v.shape, v.dtype) for v in lands],
        in_specs=[ANY] * (2 * n), out_specs=[ANY] * n, input_output_aliases={n + i: i for i in range(n)},
        scratch_shapes=[pltpu.SemaphoreType.DMA((2 * n,))] * 2
        + [_stage_shape(p.shape[1], p.shape[2], p.dtype) for p in parts],
        compiler_params=pltpu.CompilerParams(has_side_effects=True, vmem_limit_bytes=VMEM_LIMIT))(*parts, *lands)


def pair_split(grads, *, name):
    n = len(grads)

    def body(*refs):
        ins, recv = refs[:n], refs[n:2 * n]
        send_sems, recv_sems = refs[2 * n:]
        x, y, c = _place()
        sibling = (x, y, 1 - c)
        cps = []
        for t in range(n):
            h = ins[t].shape[1] // 2
            cp = pltpu.make_async_remote_copy(src_ref=ins[t].at[:, pl.ds((1 - c) * h, h)], dst_ref=recv[t],
                                              send_sem=send_sems.at[t], recv_sem=recv_sems.at[t], device_id=sibling,
                                              device_id_type=MESH)
            cp.start()
            cps.append(cp)
        for cp in cps:
            cp.wait()

    half = [jax.ShapeDtypeStruct((g.shape[0], g.shape[1] // 2, g.shape[2]), g.dtype) for g in grads]
    return pl.pallas_call(
        body, name=name, out_shape=half, in_specs=[ANY] * n, out_specs=[ANY] * n,
        scratch_shapes=[pltpu.SemaphoreType.DMA((n,))] * 2,
        compiler_params=pltpu.CompilerParams(has_side_effects=True))(*grads)


def chip_exchange(parts, *, name):
    n = len(parts)

    def body(*refs):
        ins, outs = refs[:n], refs[n:2 * n]
        send_sems, recv_sems, stage_in, stage_out = refs[2 * n:2 * n + 4]
        bufs = refs[2 * n + 4:]
        x, y, c = _place()
        me = 2 * x + y
        chips = _other_chips(x, y)
        cps = []
        for t in range(n):
            for j, (px, py) in enumerate(chips):
                cp = pltpu.make_async_remote_copy(src_ref=ins[t].at[2 * px + py], dst_ref=outs[t].at[me],
                                                  send_sem=send_sems.at[3 * t + j], recv_sem=recv_sems.at[3 * t + j],
                                                  device_id=(px, py, c), device_id_type=MESH)
                cp.start()
                cps.append(cp)
        for t in range(n):
            _staged_copy(ins[t].at[me], outs[t].at[me], bufs[t], stage_in, stage_out, t)
        for t in range(n):
            for j, (px, py) in enumerate(chips):
                landed = outs[t].at[2 * px + py]
                pltpu.make_async_remote_copy(src_ref=landed, dst_ref=landed, send_sem=send_sems.at[3 * t + j],
                                             recv_sem=recv_sems.at[3 * t + j], device_id=(px, py, c),
                                             device_id_type=MESH).wait_recv()
        for cp in cps:
            cp.wait_send()

    return pl.pallas_call(
        body, name=name, out_shape=[jax.ShapeDtypeStruct(p.shape, p.dtype) for p in parts],
        in_specs=[ANY] * n, out_specs=[ANY] * n,
        scratch_shapes=[pltpu.SemaphoreType.DMA((3 * n,))] * 2 + [pltpu.SemaphoreType.DMA((2 * n,))] * 2
        + [_stage_shape(p.shape[1], p.shape[2], p.dtype) for p in parts],
        compiler_params=pltpu.CompilerParams(has_side_effects=True, vmem_limit_bytes=VMEM_LIMIT))(*parts)


def pair_join(halves, *, name):
    n = len(halves)

    def body(*refs):
        ins, outs = refs[:n], refs[n:2 * n]
        send_sems, recv_sems, stage_in, stage_out = refs[2 * n:2 * n + 4]
        bufs = refs[2 * n + 4:]
        x, y, c = _place()
        sibling = (x, y, 1 - c)
        cps = []
        for t in range(n):
            h = ins[t].shape[0]
            cp = pltpu.make_async_remote_copy(src_ref=ins[t], dst_ref=outs[t].at[pl.ds(c * h, h)],
                                              send_sem=send_sems.at[t], recv_sem=recv_sems.at[t], device_id=sibling,
                                              device_id_type=MESH)
            cp.start()
            cps.append(cp)
        for t in range(n):
            h = ins[t].shape[0]
            _staged_copy(ins[t], outs[t].at[pl.ds(c * h, h)], bufs[t], stage_in, stage_out, t)
        for t in range(n):
            h = ins[t].shape[0]
            theirs = outs[t].at[pl.ds((1 - c) * h, h)]
            pltpu.make_async_remote_copy(src_ref=theirs, dst_ref=theirs, send_sem=send_sems.at[t],
                                         recv_sem=recv_sems.at[t], device_id=sibling, device_id_type=MESH).wait_recv()
        for cp in cps:
            cp.wait_send()

    return pl.pallas_call(
        body, name=name, out_shape=[jax.ShapeDtypeStruct((2 * p.shape[0], p.shape[1]), p.dtype) for p in halves],
        in_specs=[ANY] * n, out_specs=[ANY] * n,
        scratch_shapes=[pltpu.SemaphoreType.DMA((n,))] * 2 + [pltpu.SemaphoreType.DMA((2 * n,))] * 2
        + [_stage_shape(p.shape[0], p.shape[1], p.dtype) for p in halves],
        compiler_params=pltpu.CompilerParams(has_side_effects=True, vmem_limit_bytes=VMEM_LIMIT))(*halves)


def reduce_scatter(grads, pay_dtype, *, name):
    recv = pair_split(grads, name=name + "_split")
    part = add_pairs(grads, recv, pay_dtype, name=name + "_add2")
    got = chip_exchange(part, name=name + "_xchg")
    half = add_chips(got, name=name + "_add4")
    return pair_join(half, name=name + "_join")


def _in_sizes(d, w, n_heads):
    return (w, w, w, w, w, n_heads, n_heads, d, d)


def _wcat_from_gathered(wg, d, w, n_heads):
    full = jnp.concatenate([wg[s] for s in range(N_CHIPS)], axis=1)
    sizes = _in_sizes(d, w, n_heads)
    offs = [0]
    for s in sizes:
        offs.append(offs[-1] + s)
    pieces = [full[:, offs[i]:offs[i + 1]] for i in range(len(sizes))]
    u, q, k, v, z, beta, a, gs, gd = pieces
    pad = jnp.zeros((full.shape[0], SMALL_W - 2 * n_heads), full.dtype)
    return jnp.concatenate([gs, gd, u, q, k, v, z, beta, a, pad], axis=1)


def _wcat_grad_to_shards(dwcat, d, w, n_heads):
    gs, gd = dwcat[:, :d], dwcat[:, d:2 * d]
    o = 2 * d
    u, q, k, v, z = [dwcat[:, o + i * w:o + (i + 1) * w] for i in range(5)]
    o += 5 * w
    beta, a = dwcat[:, o:o + n_heads], dwcat[:, o + n_heads:o + 2 * n_heads]
    full = jnp.concatenate([u, q, k, v, z, beta, a, gs, gd], axis=1)
    return jnp.stack(jnp.split(full, N_CHIPS, axis=1))


def _s5_discretize(a_re, a_im, log_dt, b_re, b_im):
    dt = jnp.exp(log_dt)[:, None]
    mag = jnp.exp(a_re * dt)
    lbar_r, lbar_i = mag * jnp.cos(a_im * dt), mag * jnp.sin(a_im * dt)
    den = a_re * a_re + a_im * a_im
    zr, zi = _cmul(lbar_r - 1.0, lbar_i, a_re / den, -a_im / den)
    bbar_r, bbar_i = _cmul(zr[:, :, None], zi[:, :, None], b_re, b_im)
    return lbar_r, lbar_i, bbar_r, bbar_i


def _blockdiag_in(bbar):
    g, p, h = bbar.shape
    t = g // GROUPS_PER_TILE
    bb = bbar.reshape(t, GROUPS_PER_TILE, p, h).transpose(0, 1, 3, 2)
    eye = jnp.eye(GROUPS_PER_TILE, dtype=bbar.dtype)
    return jnp.einsum('tjhp,jk->tjhkp', bb, eye).reshape(t, GROUPS_PER_TILE * h, GROUPS_PER_TILE * p)


def _blockdiag_in_grad(dblk, g, p, h):
    t = g // GROUPS_PER_TILE
    d5 = dblk.reshape(t, GROUPS_PER_TILE, h, GROUPS_PER_TILE, p)
    eye = jnp.eye(GROUPS_PER_TILE, dtype=dblk.dtype)
    diag = jnp.einsum('tjhkp,jk->tjhp', d5, eye)
    return diag.transpose(0, 1, 3, 2).reshape(g, p, h)


def _blockdiag_out(cmat):
    g, h, p = cmat.shape
    t = g // GROUPS_PER_TILE
    cc = cmat.reshape(t, GROUPS_PER_TILE, h, p).transpose(0, 1, 3, 2)
    eye = jnp.eye(GROUPS_PER_TILE, dtype=cmat.dtype)
    return jnp.einsum('tjph,jk->tjpkh', cc, eye).reshape(t, GROUPS_PER_TILE * p, GROUPS_PER_TILE * h)


def _blockdiag_out_grad(dblk, g, h, p):
    t = g // GROUPS_PER_TILE
    d5 = dblk.reshape(t, GROUPS_PER_TILE, p, GROUPS_PER_TILE, h)
    eye = jnp.eye(GROUPS_PER_TILE, dtype=dblk.dtype)
    diag = jnp.einsum('tjpkh,jk->tjph', d5, eye)
    return diag.transpose(0, 1, 3, 2).reshape(g, h, p)


def _pad_row(v, width):
    return jnp.pad(v.reshape(1, -1), ((0, 0), (0, width - v.size)))


def _pack(arrs, rows_mult):
    flat = jnp.concatenate([a.reshape(-1) for a in arrs])
    per = 128 * rows_mult
    total = -(-flat.size // per) * per
    return jnp.pad(flat, (0, total - flat.size))


def _unpack(flat, like):
    out, o = [], 0
    for a in like:
        out.append(flat[o:o + a.size].reshape(a.shape))
        o += a.size
    return out


def kernel(x, ffn1_w_gu, ffn1_w_down, ln1_g, ln1_b, w_in, conv_w, ssm_a_re, ssm_a_im, ssm_log_dt, ssm_b_re, ssm_b_im, ssm_c_re, ssm_c_im, ssm_d, glu_w, glu_b, gdn_a_log, gdn_dt_bias, gdn_norm_w, w_br_ssm, w_br_gdn, w_out, ln2_g, ln2_b, ffn2_w_gu, ffn2_w_down, ln3_g, ln3_b, loss_target, m_ffn1_w_gu, m_ffn1_w_down, m_ln1_g, m_ln1_b, m_w_in, m_conv_w, m_ssm_a_re, m_ssm_a_im, m_ssm_log_dt, m_ssm_b_re, m_ssm_b_im, m_ssm_c_re, m_ssm_c_im, m_ssm_d, m_glu_w, m_glu_b, m_gdn_a_log, m_gdn_dt_bias, m_gdn_norm_w, m_w_br_ssm, m_w_br_gdn, m_w_out, m_ln2_g, m_ln2_b, m_ffn2_w_gu, m_ffn2_w_down, m_ln3_g, m_ln3_b, v_ffn1_w_gu, v_ffn1_w_down, v_ln1_g, v_ln1_b, v_w_in, v_conv_w, v_ssm_a_re, v_ssm_a_im, v_ssm_log_dt, v_ssm_b_re, v_ssm_b_im, v_ssm_c_re, v_ssm_c_im, v_ssm_d, v_glu_w, v_glu_b, v_gdn_a_log, v_gdn_dt_bias, v_gdn_norm_w, v_w_br_ssm, v_w_br_gdn, v_w_out, v_ln2_g, v_ln2_b, v_ffn2_w_gu, v_ffn2_w_down, v_ln3_g, v_ln3_b):
    args = locals()
    wts = {n: args[n] for n in WEIGHT_NAMES}
    mom = {n: args["m_" + n] for n in WEIGHT_NAMES}
    var = {n: args["v_" + n] for n in WEIGHT_NAMES}

    depth = ln1_g.shape[0]
    n_tok, d = x.shape[1], x.shape[2]
    w = glu_w.shape[-1]
    n_heads = gdn_a_log.shape[-1]
    n_groups, n_state, grp = ssm_b_re.shape[1], ssm_b_re.shape[2], ssm_b_re.shape[3]
    alpha = (2.0 * depth) ** 0.25
    u_col0 = 2 * d
    qkv_col0 = u_col0 + w
    z_col0 = u_col0 + 4 * w
    small_col0 = u_col0 + 5 * w
    x_idx, y_idx, _ = _place()
    chip = 2 * x_idx + y_idx

    xcur = x[0]
    xcur_b = xcur.astype(BF16)
    saved = []
    halve = [True] * len(BIG) + [False]

    def start_gather(layer, order_token):
        shards = [wts[n][layer].astype(BF16) for n in BIG] + [conv_w[layer] + order_token[0, 0]]
        return gather_start(shards, halve, name=f"gather_start_l{layer}")

    started = start_gather(0, jnp.zeros((1, 1), F32))
    after = xcur
    for l in range(depth):
        shards, lands, waited = gather_wait(started, halve, after, name=f"gather_wait_l{l}")
        started = start_gather(l + 1, waited) if l + 1 < depth else started
        gathered = gather_finish(shards, lands, halve, started[4], name="gather_finish")
        gw = dict(zip(BIG, gathered[:-1]))
        conv_full = jnp.concatenate([gathered[-1][s] for s in range(N_CHIPS)], axis=1)
        wgu1, wgu2 = gw['ffn1_w_gu'], gw['ffn2_w_gu']
        wd1 = gw['ffn1_w_down'].reshape(-1, d)
        wd2 = gw['ffn2_w_down'].reshape(-1, d)
        wcat = _wcat_from_gathered(gw['w_in'], d, w, n_heads)
        wglu = gw['glu_w'].reshape(w, w)
        wbs, wbd = gw['w_br_ssm'], gw['w_br_gdn']
        wout = gw['w_out'].reshape(d, d)
        f = wd1.shape[0]

        vec = lambda v: v[l].reshape(1, -1)
        x0, x0b = xcur, xcur_b
        h1 = mm(x0b, wgu1, b_nb=N_CHIPS, out_nb=2, name=f"ffn_up")
        r1, x1, x1b = down_res_ln(h1, wd1, x0, vec(ln1_g), vec(ln1_b), swiglu=True, alpha=alpha, scale=0.5,
                                  name="ffn_down")
        proj = mm(x1b, wcat, name="in_proj")
        (lbar_r, lbar_i, bbar_r, bbar_i), disc_vjp = jax.vjp(
            _s5_discretize, ssm_a_re[l], ssm_a_im[l], ssm_log_dt[l], ssm_b_re[l], ssm_b_im[l])
        s5w = (_blockdiag_in(bbar_r), _blockdiag_in(bbar_i), _blockdiag_out(ssm_c_re[l]), _blockdiag_out(ssm_c_im[l]),
               lbar_r.reshape(1, -1), lbar_i.reshape(1, -1), ssm_d[l].reshape(1, -1))
        ypre, y2, y2b = s5_fwd(proj, u_col0, *s5w, name="s5_fwd")
        gl = mm(y2b, wglu, name="glu_proj")
        ys = glu_gate_fwd(y2, gl, vec(glu_b), name="glu_gate")
        qkv = conv_fwd(proj, qkv_col0, conv_full, name="conv_fwd")
        gdn_rows = (_pad_row(gdn_a_log[l], HEAD_DIM), _pad_row(gdn_dt_bias[l], HEAD_DIM), gdn_norm_w[l].reshape(1, -1))
        yg, states = gdn_fwd(qkv, proj, z_col0, small_col0, *gdn_rows, n_heads, name="gdn_fwd")
        bs = mm(ys, wbs, b_nb=N_CHIPS, name="br_ssm")
        bd = mm(yg, wbd, b_nb=N_CHIPS, name="br_gdn")
        mrg = merge_fwd(proj, bs, bd, name="merge")
        r2, x2, x2b = down_res_ln(mrg, wout, x1, vec(ln2_g), vec(ln2_b), swiglu=False, alpha=alpha, scale=1.0,
                                  name="mix_out")
        h3 = mm(x2b, wgu2, b_nb=N_CHIPS, out_nb=2, name="ffn_up")
        r3, x3, x3b = down_res_ln(h3, wd2, x2, vec(ln3_g), vec(ln3_b), swiglu=True, alpha=alpha, scale=0.5,
                                  name="ffn_down")
        saved.append(dict(x0b=x0b, h1=h1, r1=r1, x1b=x1b, proj=proj, s5w=s5w, disc_vjp=disc_vjp, ypre=ypre, y2=y2,
                          y2b=y2b, gl=gl, ys=ys, qkv=qkv, gdn_rows=gdn_rows, yg=yg, states=states, bs=bs, bd=bd, mrg=mrg,
                          r2=r2, x2b=x2b, h3=h3, r3=r3, wgu1=wgu1, wgu2=wgu2, wd1=wd1, wd2=wd2, wcat=wcat, wglu=wglu,
                          wbs=wbs, wbd=wbd, wout=wout, conv_full=conv_full))
        xcur, xcur_b = x3, x3b
        after = x3

    dy, loss_blk = loss_head(xcur, loss_target[0], name="loss_head")
    loss = lax.psum(loss_blk[0, 0], ("x", "y", "c"))

    big_grads = {n: [None] * depth for n in BIG}
    small_grads = {n: [None] * depth for n in SMALL}
    def finish_reduction(layer, exchange, after_arr):
        parts, lands, _ = xchg_wait(exchange, after_arr, name=f"xchg_wait_l{layer}")
        half = add_chips(xchg_finish(parts, lands, name="xchg_finish"), name="rs_big_add4")
        for n, g in zip(BIG, pair_join(half, name="rs_big_join")):
            big_grads[n][layer] = g

    pending = None
    for l in reversed(range(depth)):
        s = saved[l]
        order = pending[1][4][0, 0] if pending is not None else 0.0
        vec = lambda v: v[l].reshape(1, -1)

        def ffn_back(dx_out, r, g_ln, xin, hh, wgu, wd):
            dr, drb, dg, db = ln_bwd(r, g_ln, dx_out, name="ln_bwd")
            dh, act = ffn_bwd_mid(dr, wd, hh, scale=0.5, name="ffn_bwd_mid")
            dwd = mm(act, drb, ta=True, out_dtype=BF16, out_scale=0.5, name="ffn_dwd")
            dwgu = mm(xin, dh, ta=True, b_nb=2, out_nb=N_CHIPS, out_dtype=BF16, name="ffn_dwgu")
            dxin = mm(dh, wgu, tb=True, a_nb=2, b_nb=N_CHIPS, add=dr, add_scale=alpha, name="ffn_dx")
            return dxin, dg, db, dwgu, dwd.reshape(N_CHIPS, -1, d)

        dx2, dg3, db3, dwgu2, dwd2 = ffn_back(dy, s['r3'], vec(ln3_g) + order, s['x2b'], s['h3'], s['wgu2'], s['wd2'])
        dr2, dr2b, dg2, db2 = ln_bwd(s['r2'], vec(ln2_g), dx2, name="ln_bwd")
        dmrg = mm(dr2b, s['wout'], tb=True, name="mix_dm")
        dwout = mm(s['mrg'], dr2b, ta=True, out_dtype=BF16, name="mix_dwout").reshape(N_CHIPS, -1, d)
        dbs, dbd, dgs, dgd = merge_bwd(s['proj'], s['bs'], s['bd'], dmrg, name="merge_bwd")
        dwbs = mm(s['ys'], dbs, ta=True, out_nb=N_CHIPS, out_dtype=BF16, name="br_dw")
        dwbd = mm(s['yg'], dbd, ta=True, out_nb=N_CHIPS, out_dtype=BF16, name="br_dw")
        dys = mm(dbs, s['wbs'], tb=True, b_nb=N_CHIPS, name="br_dx")
        dyg = mm(dbd, s['wbd'], tb=True, b_nb=N_CHIPS, name="br_dx")
        dy2a, dgl, dglu_b = glu_gate_bwd(s['y2'], s['gl'], vec(glu_b), dys, name="glu_gate_bwd")
        dwglu = mm(s['y2b'], dgl, ta=True, out_dtype=BF16, name="glu_dw").reshape(N_CHIPS, -1, w)
        dy2 = mm(dgl, s['wglu'], tb=True, add=dy2a, name="glu_dx")
        du, dbr, dbi, dcr, dci, dlr, dli, dd = s5_bwd(s['proj'], u_col0, s['ypre'], dy2, *s['s5w'], name="s5_bwd")
        dqkv_act, dz, dsmall, dalog, ddtb, dnw = gdn_bwd(s['qkv'], s['proj'], z_col0, small_col0, *s['gdn_rows'],
                                                           s['states'], dyg, n_heads, name="gdn_bwd")
        dqkv, dconv = conv_bwd(s['proj'], qkv_col0, s['conv_full'], dqkv_act, name="conv_bwd")
        dsmall_w = jnp.pad(dsmall, ((0, 0), (0, SMALL_W - HEAD_DIM)))
        dproj = jnp.concatenate([dgs, dgd, du, dqkv, dz, dsmall_w], axis=1)
        dwcat = mm(s['x1b'], dproj, ta=True, out_dtype=BF16, name="in_dw")
        dx1 = mm(dproj, s['wcat'], tb=True, add=dr2, add_scale=alpha, name="in_dx")
        dx0, dg1, db1, dwgu1, dwd1 = ffn_back(dx1, s['r1'], vec(ln1_g), s['x0b'], s['h1'], s['wgu1'], s['wd1'])
        dy = dx0

        da_re, da_im, dlog_dt, db_re, db_im = s['disc_vjp'](
            (dlr.reshape(n_groups, n_state), dli.reshape(n_groups, n_state),
             _blockdiag_in_grad(dbr, n_groups, n_state, grp), _blockdiag_in_grad(dbi, n_groups, n_state, grp)))
        sg = dict(ln1_g=dg1, ln1_b=db1, ln2_g=dg2, ln2_b=db2, ln3_g=dg3, ln3_b=db3, conv_w=dconv,
                  ssm_a_re=da_re, ssm_a_im=da_im, ssm_log_dt=dlog_dt, ssm_b_re=db_re, ssm_b_im=db_im,
                  ssm_c_re=_blockdiag_out_grad(dcr, n_groups, grp, n_state),
                  ssm_c_im=_blockdiag_out_grad(dci, n_groups, grp, n_state), ssm_d=dd, glu_b=dglu_b,
                  gdn_a_log=dalog[0, :n_heads], gdn_dt_bias=ddtb[0, :n_heads], gdn_norm_w=dnw)
        for n in SMALL:
            small_grads[n][l] = sg[n].reshape(-1)
        layer_grads = dict(ffn1_w_gu=dwgu1, ffn1_w_down=dwd1, w_in=_wcat_grad_to_shards(dwcat, d, w, n_heads),
                           glu_w=dwglu, w_br_ssm=dwbs, w_br_gdn=dwbd, w_out=dwout, ffn2_w_gu=dwgu2, ffn2_w_down=dwd2)
        if pending is not None:
            finish_reduction(pending[0], pending[1], dy)
        grads_l = [layer_grads[n] for n in BIG]
        part = add_pairs(grads_l, pair_split(grads_l, name="rs_big_split"), BF16, name="rs_big_add2")
        pending = (l, xchg_start(part, name=f"xchg_start_l{l}"))
    finish_reduction(pending[0], pending[1], dy)
    grad_x = dy[None]

    small_list = [jnp.stack(small_grads[n]) for n in SMALL]
    packed = _pack(small_list, 16 * N_CHIPS).reshape(N_CHIPS, -1, 128)
    red = reduce_scatter([packed], F32, name="rs_small")
    full = gather_chips(red, [True], name="gather_small")[0].reshape(-1)
    small_red = dict(zip(SMALL, _unpack(full, small_list)))
    cw_cols = conv_w.shape[-1]
    dconv_full = small_red['conv_w'].reshape(depth, CONV_K, N_CHIPS, cw_cols)
    small_red['conv_w'] = lax.dynamic_index_in_dim(dconv_full, chip, axis=2, keepdims=False)

    grads = {}
    for n in BIG:
        grads[n] = jnp.stack(big_grads[n]).reshape(wts[n].shape)
    for n in SMALL:
        grads[n] = small_red[n].reshape(wts[n].shape)

    delta, new_m, new_v = {}, {}, {}
    for n in BIG:
        delta[n], new_m[n], new_v[n] = adamw(wts[n], grads[n], mom[n], var[n], name="adamw_big")
    pk = lambda dct: _pack([dct[n] for n in SMALL], 1024).reshape(-1, 128)
    sd, sm, sv = adamw(pk(wts), pk(grads), pk(mom), pk(var), name="adamw_small")
    like = [wts[n] for n in SMALL]
    for n, a, b, c in zip(SMALL, _unpack(sd.reshape(-1), like), _unpack(sm.reshape(-1), like),
                          _unpack(sv.reshape(-1), like)):
        delta[n], new_m[n], new_v[n] = a, b, c

    return (loss, grad_x, *[grads[n] for n in WEIGHT_NAMES], *[delta[n] for n in WEIGHT_NAMES],
            *[new_m[n] for n in WEIGHT_NAMES], *[new_v[n] for n in WEIGHT_NAMES])
```

```python
import math

import jax
import jax.numpy as jnp
from jax import lax
from jax.experimental import pallas as pl
from jax.experimental.pallas import tpu as pltpu

F32 = jnp.float32
BF16 = jnp.bfloat16
HI = lax.Precision.HIGHEST
MESH = pl.DeviceIdType.MESH

N_CHIPS = 4
SSM_GROUP = 16
SSM_STATE = 64
GROUPS_PER_TILE = 8
HEAD_DIM = 128
CHUNK = 64
CONV_K = 4
LN_EPS = 1e-5
RMS_EPS = 1e-6
L2_EPS = 1e-6
SMALL_W = 512
ADAM_LR, ADAM_B1, ADAM_B2, ADAM_EPS, ADAM_WD, ADAM_STEP = 0.001, 0.9, 0.999, 1e-08, 0.01, 10
VMEM_LIMIT = 56 * 1024 * 1024

WEIGHT_NAMES = ['ffn1_w_gu', 'ffn1_w_down', 'ln1_g', 'ln1_b', 'w_in', 'conv_w', 'ssm_a_re', 'ssm_a_im', 'ssm_log_dt',
                'ssm_b_re', 'ssm_b_im', 'ssm_c_re', 'ssm_c_im', 'ssm_d', 'glu_w', 'glu_b', 'gdn_a_log', 'gdn_dt_bias',
                'gdn_norm_w', 'w_br_ssm', 'w_br_gdn', 'w_out', 'ln2_g', 'ln2_b', 'ffn2_w_gu', 'ffn2_w_down', 'ln3_g',
                'ln3_b']
BIG = ['ffn1_w_gu', 'ffn1_w_down', 'w_in', 'glu_w', 'w_br_ssm', 'w_br_gdn', 'w_out', 'ffn2_w_gu', 'ffn2_w_down']
SMALL = [n for n in WEIGHT_NAMES if n not in BIG]


def _tile(dim, prefs):
    for p in prefs:
        if dim % p == 0:
            return p
    return dim


def _cparams(sem):
    return pltpu.CompilerParams(dimension_semantics=sem, vmem_limit_bytes=VMEM_LIMIT)


def _ln(r, g, b):
    mu = jnp.mean(r, axis=-1, keepdims=True)
    xc = r - mu
    var = jnp.mean(xc * xc, axis=-1, keepdims=True)
    return xc * lax.rsqrt(var + LN_EPS) * g + b


def _lshape(x, nb):
    return (x.shape[0], x.shape[1]) if nb == 1 else (x.shape[1], x.shape[2] * nb)


def _cb_spec(x, nb, tr, tc, rc):
    if nb == 1:
        return pl.BlockSpec((tr, tc), lambda *g: rc(*g))
    cps = x.shape[2] // tc

    def imap(*g):
        r, c = rc(*g)
        return (c // cps, r, c % cps)
    return pl.BlockSpec((None, tr, tc), imap)


MM_VMEM_BUDGET = 40 * 1024 * 1024
MM_TILES = (2048, 1024, 512, 256, 128)
MM_FULL_K = 2048


def mm(a, b, *, name, ta=False, tb=False, a_nb=1, b_nb=1, out_nb=1, out_dtype=F32, add=None, add_scale=1.0,
       out_scale=1.0):
    ar, ac = _lshape(a, a_nb)
    br, bc = _lshape(b, b_nb)
    m, k = (ac, ar) if ta else (ar, ac)
    k2, n = (bc, br) if tb else (br, bc)
    assert k == k2, (name, a.shape, b.shape)
    assert a.dtype == BF16 and b.dtype == BF16, name

    def lim(dim, *nbs):
        q = dim
        for nb in nbs:
            q = math.gcd(q, dim // nb)
        return q
    lm = lim(m, a_nb if ta else 1)
    ln = lim(n, out_nb, 1 if tb else b_nb)
    lk = lim(k, 1 if ta else a_nb, b_nb if tb else 1)
    so = jnp.dtype(out_dtype).itemsize
    if k <= MM_FULL_K and lk == k:
        tks = [k]
    else:
        tks = [t for t in range(MM_FULL_K, 127, -128) if lk % t == 0]
    best = None
    for ck in tks:
        for cm in MM_TILES:
            for cn in MM_TILES:
                if lm % cm or ln % cn:
                    continue
                est = 2 * (cm * ck * 2 + ck * cn * 2 + cm * cn * so + (cm * cn * 4 if add is not None else 0))
                est += cm * cn * 4 * (2 if k > ck else 1) + (cm * ck * 2 + 512 * cm * 4 if ta else 0)
                score = min(cm, 512) * cn * ck
                if est <= MM_VMEM_BUDGET and (best is None or score > best[0]):
                    best = (score, cm, cn, ck)
    _, tm, tn, tk = best
    nk = k // tk
    dn = (((1,), (1 if tb else 0,)), ((), ()))

    def body(*refs):
        a_ref, b_ref = refs[:2]
        add_ref = refs[2] if add is not None else None
        o_ref = refs[3 if add is not None else 2]
        scratch = refs[(4 if add is not None else 3):]
        acc = scratch[0] if nk > 1 else None
        kk = pl.program_id(2)

        if ta:
            at_ref = scratch[-1]

            def transpose_block():
                for r0 in range(0, tk, 512):
                    r1 = min(tk, r0 + 512)
                    at_ref[:, r0:r1] = a_ref[r0:r1, :].astype(F32).T.astype(BF16)
            if nk == 1:
                pl.when(pl.program_id(1) == 0)(transpose_block)
            else:
                transpose_block()
            av = at_ref[...]
        else:
            av = a_ref[...]
        part = lax.dot_general(av, b_ref[...], dn, preferred_element_type=F32)

        def finish(r):
            if out_scale != 1.0:
                r = r * out_scale
            if add is not None:
                r = r + add_scale * add_ref[...]
            o_ref[...] = r.astype(out_dtype)

        if nk == 1:
            finish(part)
        else:
            @pl.when(kk == 0)
            def _():
                acc[...] = part

            @pl.when(kk > 0)
            def _():
                acc[...] += part

            @pl.when(kk == nk - 1)
            def _():
                finish(acc[...])

    if ta:
        a_spec = _cb_spec(a, a_nb, tk, tm, lambda i, j, kk: (kk, i))
    else:
        a_spec = _cb_spec(a, a_nb, tm, tk, lambda i, j, kk: (i, kk))
    if tb:
        b_spec = _cb_spec(b, b_nb, tn, tk, lambda i, j, kk: (j, kk))
    else:
        b_spec = _cb_spec(b, b_nb, tk, tn, lambda i, j, kk: (kk, j))
    if out_nb == 1:
        out_shape = jax.ShapeDtypeStruct((m, n), out_dtype)
        out_spec = pl.BlockSpec((tm, tn), lambda i, j, kk: (i, j))
    else:
        out_shape = jax.ShapeDtypeStruct((out_nb, m, n // out_nb), out_dtype)
        out_spec = _cb_spec(out_shape, out_nb, tm, tn, lambda i, j, kk: (i, j))
    in_specs = [a_spec, b_spec]
    args = [a, b]
    if add is not None:
        in_specs.append(pl.BlockSpec((tm, tn), lambda i, j, kk: (i, j)))
        args.append(add)
    scratch = ([pltpu.VMEM((tm, tn), F32)] if nk > 1 else []) + ([pltpu.VMEM((tm, tk), BF16)] if ta else [])
    return pl.pallas_call(
        body, name=name, out_shape=out_shape, grid=(m // tm, n // tn, nk), in_specs=in_specs, out_specs=out_spec,
        scratch_shapes=scratch, compiler_params=_cparams(("parallel", "arbitrary", "arbitrary")))(*args)


def down_res_ln(src, w, x, g, b, *, swiglu, alpha, scale, name):
    n_tok, d = x.shape
    kdim = w.shape[0]
    tm = _tile(n_tok, (512, 256, 128))
    tk = _tile(kdim, (512, 256, 128))
    nk = kdim // tk

    def body(s_ref, w_ref, x_ref, g_ref, b_ref, r_ref, y_ref, yb_ref, acc):
        kk = pl.program_id(1)

        @pl.when(kk == 0)
        def _():
            acc[...] = jnp.zeros_like(acc)

        if swiglu:
            gate = s_ref[0]
            a = gate * jax.nn.sigmoid(gate) * s_ref[1]
        else:
            a = s_ref[...]
        acc[...] += jnp.dot(a.astype(BF16), w_ref[...], preferred_element_type=F32)

        @pl.when(kk == nk - 1)
        def _():
            r = alpha * x_ref[...] + scale * acc[...]
            r_ref[...] = r
            y = _ln(r, g_ref[...], b_ref[...])
            y_ref[...] = y
            yb_ref[...] = y.astype(BF16)

    if swiglu:
        s_spec = pl.BlockSpec((2, tm, tk), lambda i, kk: (0, i, kk))
    else:
        s_spec = pl.BlockSpec((tm, tk), lambda i, kk: (i, kk))
    row = pl.BlockSpec((tm, d), lambda i, kk: (i, 0))
    vec = pl.BlockSpec((1, d), lambda i, kk: (0, 0))
    return pl.pallas_call(
        body, name=name, out_shape=[jax.ShapeDtypeStruct((n_tok, d), F32)] * 2 + [jax.ShapeDtypeStruct((n_tok, d), BF16)],
        grid=(n_tok // tm, nk),
        in_specs=[s_spec, pl.BlockSpec((tk, d), lambda i, kk: (kk, 0)), row, vec, vec], out_specs=[row, row, row],
        scratch_shapes=[pltpu.VMEM((tm, d), F32)], compiler_params=_cparams(("parallel", "arbitrary")))(src, w, x, g, b)


def ln_bwd(r, g, dy, *, name):
    n_tok, d = r.shape
    tm = _tile(n_tok, (256, 128))

    def body(r_ref, g_ref, dy_ref, dr_ref, drb_ref, dg_ref, db_ref):
        i = pl.program_id(0)

        @pl.when(i == 0)
        def _():
            dg_ref[...] = jnp.zeros_like(dg_ref)
            db_ref[...] = jnp.zeros_like(db_ref)

        rv = r_ref[...]
        dyv = dy_ref[...]
        mu = jnp.mean(rv, axis=-1, keepdims=True)
        xc = rv - mu
        rstd = lax.rsqrt(jnp.mean(xc * xc, axis=-1, keepdims=True) + LN_EPS)
        xh = xc * rstd
        dxh = dyv * g_ref[...]
        dr = rstd * (dxh - jnp.mean(dxh, axis=-1, keepdims=True) - xh * jnp.mean(dxh * xh, axis=-1, keepdims=True))
        dr_ref[...] = dr
        drb_ref[...] = dr.astype(BF16)
        dg_ref[...] += jnp.sum(dyv * xh, axis=0, keepdims=True)
        db_ref[...] += jnp.sum(dyv, axis=0, keepdims=True)

    row = pl.BlockSpec((tm, d), lambda i: (i, 0))
    vec = pl.BlockSpec((1, d), lambda i: (0, 0))
    return pl.pallas_call(
        body, name=name, out_shape=[jax.ShapeDtypeStruct((n_tok, d), F32), jax.ShapeDtypeStruct((n_tok, d), BF16),
                                    jax.ShapeDtypeStruct((1, d), F32), jax.ShapeDtypeStruct((1, d), F32)],
        grid=(n_tok // tm,), in_specs=[row, vec, row], out_specs=[row, row, vec, vec],
        compiler_params=_cparams(("arbitrary",)))(r, g, dy)


def loss_head(y, target, *, name):
    n_tok, d = y.shape
    tm = _tile(n_tok, (256, 128))

    def body(y_ref, t_ref, dy_ref, l_ref):
        i = pl.program_id(0)

        @pl.when(i == 0)
        def _():
            l_ref[...] = jnp.zeros_like(l_ref)

        e = y_ref[...] - t_ref[...]
        dy_ref[...] = e * (1.0 / d)
        s = jnp.sum(jnp.mean(e * e, axis=-1, keepdims=True), axis=0, keepdims=True)
        l_ref[...] += 0.5 * s

    row = pl.BlockSpec((tm, d), lambda i: (i, 0))
    return pl.pallas_call(
        body, name=name, out_shape=[jax.ShapeDtypeStruct((n_tok, d), F32), jax.ShapeDtypeStruct((8, 128), F32)],
        grid=(n_tok // tm,), in_specs=[row, row], out_specs=[row, pl.BlockSpec((8, 128), lambda i: (0, 0))],
        compiler_params=_cparams(("arbitrary",)))(y, target)


def ffn_bwd_mid(dr, wd, h, *, scale, name):
    n_tok, d = dr.shape
    f = wd.shape[0]
    tm = _tile(n_tok, (512, 256, 128))
    tf = _tile(f, (512, 256, 128))

    def body(dr_ref, w_ref, h_ref, dh_ref, a_ref):
        dy = (scale * dr_ref[...]).astype(BF16)
        da = lax.dot_general(dy, w_ref[...], (((1,), (1,)), ((), ())), preferred_element_type=F32)
        gate = h_ref[0]
        up = h_ref[1]
        sg = jax.nn.sigmoid(gate)
        s = gate * sg
        a_ref[...] = (s * up).astype(BF16)
        dh_ref[0] = (da * up * (sg * (1.0 + gate * (1.0 - sg)))).astype(BF16)
        dh_ref[1] = (da * s).astype(BF16)

    return pl.pallas_call(
        body, name=name, out_shape=[jax.ShapeDtypeStruct((2, n_tok, f), BF16), jax.ShapeDtypeStruct((n_tok, f), BF16)],
        grid=(n_tok // tm, f // tf),
        in_specs=[pl.BlockSpec((tm, d), lambda i, j: (i, 0)), pl.BlockSpec((tf, d), lambda i, j: (j, 0)),
                  pl.BlockSpec((2, tm, tf), lambda i, j: (0, i, j))],
        out_specs=[pl.BlockSpec((2, tm, tf), lambda i, j: (0, i, j)), pl.BlockSpec((tm, tf), lambda i, j: (i, j))],
        compiler_params=_cparams(("parallel", "parallel")))(dr, wd, h)


def _cmul(ar, ai, br, bi):
    return ar * br - ai * bi, ar * bi + ai * br


def _scan_blocks(sr_ref, si_ref, lr, li, *, reverse):
    n_rows, width = sr_ref.shape
    n_blk = n_rows // 8
    row = lax.broadcasted_iota(jnp.int32, (8, width), 0)
    pr = jnp.broadcast_to(lr, (8, width))
    pi = jnp.broadcast_to(-li if reverse else li, (8, width))

    def shifted(v, dist, fill=0.0):
        if reverse:
            return jnp.where(row < 8 - dist, pltpu.roll(v, 8 - dist, 0), fill)
        return jnp.where(row >= dist, pltpu.roll(v, dist, 0), fill)

    p1 = (pr, pi)
    p2 = _cmul(*p1, *p1)
    p4 = _cmul(*p2, *p2)
    wr, wi = pr, pi
    for dist in (1, 2, 4):
        wr, wi = _cmul(wr, wi, shifted(wr, dist, 1.0), shifted(wi, dist, 0.0))
    edge = 0 if reverse else 7

    def step(i, carry):
        cr, ci = carry
        blk = (n_blk - 1 - i) if reverse else i
        r0 = pl.multiple_of(blk * 8, 8)
        xr = sr_ref[pl.ds(r0, 8), :]
        xi = si_ref[pl.ds(r0, 8), :]
        for dist, (qr, qi) in ((1, p1), (2, p2), (4, p4)):
            tr, ti = _cmul(qr, qi, shifted(xr, dist), shifted(xi, dist))
            xr, xi = xr + tr, xi + ti
        tr, ti = _cmul(wr, wi, cr, ci)
        xr, xi = xr + tr, xi + ti
        sr_ref[pl.ds(r0, 8), :] = xr
        si_ref[pl.ds(r0, 8), :] = xi
        br = jnp.where(row == edge, xr, 0.0)
        bi = jnp.where(row == edge, xi, 0.0)
        for dist in (1, 2, 4):
            br = br + pltpu.roll(br, dist, 0)
            bi = bi + pltpu.roll(bi, dist, 0)
        return br, bi

    zero = jnp.zeros((8, width), F32)
    lax.fori_loop(0, n_blk, step, (zero, zero), unroll=2)


def _s5_specs(n_tok, u_blk0):
    gw = GROUPS_PER_TILE * SSM_GROUP
    sw = GROUPS_PER_TILE * SSM_STATE
    u_spec = pl.BlockSpec((n_tok, gw), lambda t: (0, u_blk0 + t))
    col = pl.BlockSpec((n_tok, gw), lambda t: (0, t))
    bmat = pl.BlockSpec((None, gw, sw), lambda t: (t, 0, 0))
    cmat = pl.BlockSpec((None, sw, gw), lambda t: (t, 0, 0))
    lvec = pl.BlockSpec((1, sw), lambda t: (0, t))
    dvec = pl.BlockSpec((1, gw), lambda t: (0, t))
    return gw, sw, u_spec, col, bmat, cmat, lvec, dvec


def s5_fwd(proj, u_col0, bblk_r, bblk_i, cblk_r, cblk_i, lbar_r, lbar_i, dskip, *, name):
    n_tok = proj.shape[0]
    n_tiles = bblk_r.shape[0]
    gw, sw, u_spec, col, bmat, cmat, lvec, dvec = _s5_specs(n_tok, u_col0 // (GROUPS_PER_TILE * SSM_GROUP))

    def body(u_ref, br_ref, bi_ref, cr_ref, ci_ref, lr_ref, li_ref, d_ref, ypre_ref, y2_ref, y2b_ref, sr, si):
        u = u_ref[...]
        ub = u.astype(BF16)
        sr[...] = jnp.dot(ub, br_ref[...].astype(BF16), preferred_element_type=F32)
        si[...] = jnp.dot(ub, bi_ref[...].astype(BF16), preferred_element_type=F32)
        _scan_blocks(sr, si, lr_ref[...], li_ref[...], reverse=False)
        y = (jnp.dot(sr[...].astype(BF16), cr_ref[...].astype(BF16), preferred_element_type=F32)
             - jnp.dot(si[...].astype(BF16), ci_ref[...].astype(BF16), preferred_element_type=F32)
             + d_ref[...] * u)
        ypre_ref[...] = y
        y2 = jax.nn.gelu(y)
        y2_ref[...] = y2
        y2b_ref[...] = y2.astype(BF16)

    width = n_tiles * gw
    return pl.pallas_call(
        body, name=name,
        out_shape=[jax.ShapeDtypeStruct((n_tok, width), F32)] * 2 + [jax.ShapeDtypeStruct((n_tok, width), BF16)],
        grid=(n_tiles,), in_specs=[u_spec, bmat, bmat, cmat, cmat, lvec, lvec, dvec], out_specs=[col, col, col],
        scratch_shapes=[pltpu.VMEM((n_tok, sw), F32)] * 2,
        compiler_params=_cparams(("parallel",)))(proj, bblk_r, bblk_i, cblk_r, cblk_i, lbar_r, lbar_i, dskip)


def s5_bwd(proj, u_col0, ypre, dy2, bblk_r, bblk_i, cblk_r, cblk_i, lbar_r, lbar_i, dskip, *, name):
    n_tok = proj.shape[0]
    n_tiles = bblk_r.shape[0]
    gw, sw, u_spec, col, bmat, cmat, lvec, dvec = _s5_specs(n_tok, u_col0 // (GROUPS_PER_TILE * SSM_GROUP))
    rb = _tile(n_tok, (512, 256, 128))
    tn_dims = (((0,), (0,)), ((), ()))
    nt_dims = (((1,), (1,)), ((), ()))

    def body(u_ref, ypre_ref, dy2_ref, br_ref, bi_ref, cr_ref, ci_ref, lr_ref, li_ref, d_ref,
             du_ref, dbr_ref, dbi_ref, dcr_ref, dci_ref, dlr_ref, dli_ref, dd_ref, sr, si, gr, gi):
        u = u_ref[...]
        ub = u.astype(BF16)
        bmr = br_ref[...].astype(BF16)
        bmi = bi_ref[...].astype(BF16)
        cmr = cr_ref[...].astype(BF16)
        cmi = ci_ref[...].astype(BF16)
        lr = lr_ref[...]
        li = li_ref[...]
        _, gelu_vjp = jax.vjp(jax.nn.gelu, ypre_ref[...])
        dyp = gelu_vjp(dy2_ref[...])[0]
        dyb = dyp.astype(BF16)
        sr[...] = jnp.dot(ub, bmr, preferred_element_type=F32)
        si[...] = jnp.dot(ub, bmi, preferred_element_type=F32)
        _scan_blocks(sr, si, lr, li, reverse=False)
        gr[...] = lax.dot_general(dyb, cmr, nt_dims, preferred_element_type=F32)
        gi[...] = -lax.dot_general(dyb, cmi, nt_dims, preferred_element_type=F32)
        _scan_blocks(gr, gi, lr, li, reverse=True)
        srb = sr[...].astype(BF16)
        sib = si[...].astype(BF16)
        dcr_ref[...] = lax.dot_general(srb, dyb, tn_dims, preferred_element_type=F32)
        dci_ref[...] = -lax.dot_general(sib, dyb, tn_dims, preferred_element_type=F32)
        grb = gr[...].astype(BF16)
        gib = gi[...].astype(BF16)
        dbr_ref[...] = lax.dot_general(ub, grb, tn_dims, preferred_element_type=F32)
        dbi_ref[...] = lax.dot_general(ub, gib, tn_dims, preferred_element_type=F32)
        du_ref[...] = (lax.dot_general(grb, bmr, nt_dims, preferred_element_type=F32)
                       + lax.dot_general(gib, bmi, nt_dims, preferred_element_type=F32)
                       + d_ref[...] * dyp).astype(du_ref.dtype)
        dd_ref[...] = jnp.sum(dyp * u, axis=0, keepdims=True)
        inv = 1.0 / (lr * lr + li * li)
        qr = lr * inv
        qi = -li * inv
        acc_r = jnp.zeros((1, sw), F32)
        acc_i = jnp.zeros((1, sw), F32)
        for blk in range(n_tok // rb):
            rows = pl.ds(blk * rb, rb)
            ubb = u_ref[rows, :].astype(BF16)
            er = sr[rows, :] - jnp.dot(ubb, bmr, preferred_element_type=F32)
            ei = si[rows, :] - jnp.dot(ubb, bmi, preferred_element_type=F32)
            pr, pi = _cmul(er, ei, qr, qi)
            ar = gr[rows, :]
            ai = gi[rows, :]
            acc_r = acc_r + jnp.sum(ar * pr + ai * pi, axis=0, keepdims=True)
            acc_i = acc_i + jnp.sum(ai * pr - ar * pi, axis=0, keepdims=True)
        dlr_ref[...] = acc_r
        dli_ref[...] = acc_i

    width = n_tiles * gw
    out_shape = [jax.ShapeDtypeStruct((n_tok, width), BF16),
                 jax.ShapeDtypeStruct(bblk_r.shape, F32), jax.ShapeDtypeStruct(bblk_r.shape, F32),
                 jax.ShapeDtypeStruct(cblk_r.shape, F32), jax.ShapeDtypeStruct(cblk_r.shape, F32),
                 jax.ShapeDtypeStruct(lbar_r.shape, F32), jax.ShapeDtypeStruct(lbar_r.shape, F32),
                 jax.ShapeDtypeStruct(dskip.shape, F32)]
    return pl.pallas_call(
        body, name=name, out_shape=out_shape, grid=(n_tiles,),
        in_specs=[u_spec, col, col, bmat, bmat, cmat, cmat, lvec, lvec, dvec],
        out_specs=[col, bmat, bmat, cmat, cmat, lvec, lvec, dvec],
        scratch_shapes=[pltpu.VMEM((n_tok, sw), F32)] * 4,
        compiler_params=_cparams(("parallel",)))(proj, ypre, dy2, bblk_r, bblk_i, cblk_r, cblk_i, lbar_r, lbar_i, dskip)


CONV_ROWS = 256
CONV_COLS = 512


def _conv_pre(x_ref, w_ref, blk, n_blk):
    r0 = blk * CONV_ROWS
    if blk == 0:
        ext = jnp.concatenate([jnp.zeros((8, CONV_COLS), F32), x_ref[0:CONV_ROWS, :]], axis=0)
    else:
        ext = x_ref[r0 - 8:r0 + CONV_ROWS, :]
    taps = []
    c = None
    for j in range(CONV_K):
        s = CONV_K - 1 - j
        xs = ext[8:] if s == 0 else pltpu.roll(ext, s, 0)[8:]
        taps.append(xs)
        term = w_ref[j:j + 1, :] * xs
        c = term if c is None else c + term
    return c, taps


def conv_fwd(proj, col0, conv_w, *, name):
    n_tok = proj.shape[0]
    width = conv_w.shape[1]
    n_blk = n_tok // CONV_ROWS
    cb0 = col0 // CONV_COLS

    def body(x_ref, w_ref, o_ref):
        for blk in range(n_blk):
            c, _ = _conv_pre(x_ref, w_ref, blk, n_blk)
            o_ref[blk * CONV_ROWS:(blk + 1) * CONV_ROWS, :] = c * jax.nn.sigmoid(c)

    return pl.pallas_call(
        body, name=name, out_shape=jax.ShapeDtypeStruct((n_tok, width), F32), grid=(width // CONV_COLS,),
        in_specs=[pl.BlockSpec((n_tok, CONV_COLS), lambda j: (0, cb0 + j)),
                  pl.BlockSpec((CONV_K, CONV_COLS), lambda j: (0, j))],
        out_specs=pl.BlockSpec((n_tok, CONV_COLS), lambda j: (0, j)),
        compiler_params=_cparams(("parallel",)))(proj, conv_w)


def conv_bwd(proj, col0, conv_w, dout, *, name):
    n_tok = proj.shape[0]
    width = conv_w.shape[1]
    n_blk = n_tok // CONV_ROWS
    cb0 = col0 // CONV_COLS

    def body(x_ref, w_ref, do_ref, dx_ref, dw_ref, dc):
        dws = [jnp.zeros((1, CONV_COLS), F32) for _ in range(CONV_K)]
        for blk in range(n_blk):
            rows = slice(blk * CONV_ROWS, (blk + 1) * CONV_ROWS)
            c, taps = _conv_pre(x_ref, w_ref, blk, n_blk)
            sg = jax.nn.sigmoid(c)
            dcv = do_ref[rows, :] * (sg * (1.0 + c * (1.0 - sg)))
            dc[rows, :] = dcv
            for j in range(CONV_K):
                dws[j] = dws[j] + jnp.sum(dcv * taps[j], axis=0, keepdims=True)
        dc[n_tok:n_tok + 8, :] = jnp.zeros((8, CONV_COLS), F32)
        for j in range(CONV_K):
            dw_ref[j:j + 1, :] = dws[j]
        for blk in range(n_blk):
            r0 = blk * CONV_ROWS
            ext = dc[r0:r0 + CONV_ROWS + 8, :]
            dx = None
            for j in range(CONV_K):
                s = CONV_K - 1 - j
                sh = ext[:CONV_ROWS] if s == 0 else pltpu.roll(ext, CONV_ROWS + 8 - s, 0)[:CONV_ROWS]
                term = w_ref[j:j + 1, :] * sh
                dx = term if dx is None else dx + term
            dx_ref[r0:r0 + CONV_ROWS, :] = dx.astype(dx_ref.dtype)

    return pl.pallas_call(
        body, name=name, out_shape=[jax.ShapeDtypeStruct((n_tok, width), BF16), jax.ShapeDtypeStruct(conv_w.shape, F32)],
        grid=(width // CONV_COLS,),
        in_specs=[pl.BlockSpec((n_tok, CONV_COLS), lambda j: (0, cb0 + j)),
                  pl.BlockSpec((CONV_K, CONV_COLS), lambda j: (0, j)),
                  pl.BlockSpec((n_tok, CONV_COLS), lambda j: (0, j))],
        out_specs=[pl.BlockSpec((n_tok, CONV_COLS), lambda j: (0, j)), pl.BlockSpec((CONV_K, CONV_COLS), lambda j: (0, j))],
        scratch_shapes=[pltpu.VMEM((n_tok + 8, CONV_COLS), F32)],
        compiler_params=_cparams(("parallel",)))(proj, conv_w, dout)


GDN_PREC = lax.Precision.HIGH


def _neumann_inverse(lowers):
    n = lowers[0].shape[0]
    eye = (lax.broadcasted_iota(jnp.int32, (n, n), 0) == lax.broadcasted_iota(jnp.int32, (n, n), 1)).astype(F32)
    xs = [-l for l in lowers]
    ts = [eye + x for x in xs]
    power = 2
    while power < n:
        xs = [jnp.dot(x, x, precision=GDN_PREC, preferred_element_type=F32) for x in xs]
        ts = [t + jnp.dot(t, x, precision=GDN_PREC, preferred_element_type=F32) for t, x in zip(ts, xs)]
        power *= 2
    return tuple(ts)


@jax.custom_vjp
def _unit_lower_inverse(lowers):
    return _neumann_inverse(lowers)


def _unit_lower_inverse_fwd(lowers):
    ts = _neumann_inverse(lowers)
    return ts, ts


def _unit_lower_inverse_bwd(ts, cts):
    tn = (((0,), (0,)), ((), ()))
    nt = (((1,), (1,)), ((), ()))
    lefts = [lax.dot_general(t, ct, tn, precision=GDN_PREC, preferred_element_type=F32) for t, ct in zip(ts, cts)]
    return (tuple(-lax.dot_general(l, t, nt, precision=GDN_PREC, preferred_element_type=F32)
                  for l, t in zip(lefts, ts)),)


_unit_lower_inverse.defvjp(_unit_lower_inverse_fwd, _unit_lower_inverse_bwd)


def _gdn_head(head, n_heads, state, q, k, v, z, bsmall, alog_row, dtb_row, nw):
    c = CHUNK
    lane = lax.broadcasted_iota(jnp.int32, (c, HEAD_DIM), 1)
    lane1 = lax.broadcasted_iota(jnp.int32, (1, HEAD_DIM), 1)
    ri = lax.broadcasted_iota(jnp.int32, (c, c), 0)
    ci = lax.broadcasted_iota(jnp.int32, (c, c), 1)
    causal = ri >= ci
    strict = ri > ci
    tril = causal.astype(F32)
    bl = jnp.sum(jnp.where(lane == head, bsmall, 0.0), axis=-1, keepdims=True)
    al = jnp.sum(jnp.where(lane == n_heads + head, bsmall, 0.0), axis=-1, keepdims=True)
    alog = jnp.sum(jnp.where(lane1 == head, alog_row, 0.0), axis=-1, keepdims=True)
    dtb = jnp.sum(jnp.where(lane1 == head, dtb_row, 0.0), axis=-1, keepdims=True)

    qn = q * lax.rsqrt(jnp.sum(q * q, axis=-1, keepdims=True) + L2_EPS) * (HEAD_DIM ** -0.5)
    kn = k * lax.rsqrt(jnp.sum(k * k, axis=-1, keepdims=True) + L2_EPS)
    beta = jax.nn.sigmoid(bl)
    xg = al + dtb
    g = -jnp.exp(alog) * (jnp.maximum(xg, 0.0) + jnp.log(1.0 + jnp.exp(-jnp.abs(xg))))
    g_wide = jnp.broadcast_to(g, (c, HEAD_DIM))
    yield None
    gc = jnp.dot(tril, g_wide, precision=HI, preferred_element_type=F32)
    yield None
    gc_rows = jnp.broadcast_to(jnp.mean(gc, axis=-1, keepdims=True), (c, c))
    gc_cols = gc.T[:c, :]
    g_tot = jnp.sum(g, axis=0, keepdims=True)
    decay = jnp.exp(jnp.where(causal, gc_rows - gc_cols, -1e30))
    egc = jnp.exp(gc)
    kb = kn * beta
    knb = kn.astype(BF16)
    nt = (((1,), (1,)), ((), ()))
    yield None
    lower = jnp.where(strict, lax.dot_general(kb.astype(BF16), knb, nt, preferred_element_type=F32) * decay, 0.0)
    tinv = yield lower
    u_val = jnp.dot(tinv, v * beta, precision=GDN_PREC, preferred_element_type=F32)
    yield None
    w_key = jnp.dot(tinv, kb * egc, precision=GDN_PREC, preferred_element_type=F32)
    yield None
    attn = lax.dot_general(qn.astype(BF16), knb, nt, preferred_element_type=F32) * decay
    q_dec = qn * egc
    k_dec = kn * jnp.exp(g_tot - gc)
    sb = state.astype(BF16)
    yield None
    v_new = u_val - jnp.dot(w_key.astype(BF16), sb, preferred_element_type=F32)
    vnb = v_new.astype(BF16)
    yield None
    o = (jnp.dot(q_dec.astype(BF16), sb, preferred_element_type=F32)
         + jnp.dot(attn.astype(BF16), vnb, preferred_element_type=F32))
    yield None
    new_state = state * jnp.exp(g_tot) + lax.dot_general(k_dec.astype(BF16), vnb, (((0,), (0,)), ((), ())),
                                                         preferred_element_type=F32)
    yield None
    o = o * lax.rsqrt(jnp.mean(o * o, axis=-1, keepdims=True) + RMS_EPS) * nw
    o = o * (z * jax.nn.sigmoid(z))
    return o, new_state


def _gdn_step(n_heads, states, qs, ks, vs, zs, bsmall, alog_row, dtb_row, nw):
    gens = [_gdn_head(h, n_heads, states[h], qs[h], ks[h], vs[h], zs[h], bsmall, alog_row, dtb_row, nw)
            for h in range(n_heads)]
    lowers = [None] * n_heads
    while any(m is None for m in lowers):
        for h in range(n_heads):
            lowers[h] = next(gens[h])
    tinvs = _unit_lower_inverse(tuple(lowers))
    results = [None] * n_heads
    first = True
    while any(r is None for r in results):
        for h in range(n_heads):
            try:
                if first:
                    gens[h].send(tinvs[h])
                else:
                    next(gens[h])
            except StopIteration as stop:
                results[h] = stop.value
        first = False
    return tuple(r[0] for r in results), tuple(r[1] for r in results)


def _gdn_in_specs(n_heads, qkv_width_blocks, z_blk, small_blk, rev, n_chunks):
    w = n_heads * HEAD_DIM

    def cidx(i):
        return (n_chunks - 1 - i) if rev else i
    qs = pl.BlockSpec((CHUNK, w), lambda i: (cidx(i), 0))
    ks = pl.BlockSpec((CHUNK, w), lambda i: (cidx(i), 1))
    vs = pl.BlockSpec((CHUNK, w), lambda i: (cidx(i), 2))
    zs = pl.BlockSpec((CHUNK, w), lambda i: (cidx(i), z_blk))
    bs = pl.BlockSpec((CHUNK, HEAD_DIM), lambda i: (cidx(i), small_blk))
    pv = pl.BlockSpec((1, HEAD_DIM), lambda i: (0, 0))
    return cidx, qs, ks, vs, zs, bs, pv


def gdn_fwd(qkv, proj, z_col0, small_col0, alog_row, dtb_row, nw_row, n_heads, *, name):
    n_tok = qkv.shape[0]
    w = n_heads * HEAD_DIM
    n_chunks = n_tok // CHUNK
    cidx, qs, ks, vs, zs, bs, pv = _gdn_in_specs(n_heads, 3, z_col0 // w, small_col0 // HEAD_DIM, False, n_chunks)

    def body(q_ref, k_ref, v_ref, z_ref, b_ref, al_ref, dt_ref, nw_ref, o_ref, s_ref, state):
        @pl.when(pl.program_id(0) == 0)
        def _():
            state[...] = jnp.zeros_like(state)

        heads = range(n_heads)
        cols = [slice(h * HEAD_DIM, (h + 1) * HEAD_DIM) for h in heads]
        states = [state[h] for h in heads]
        for h in heads:
            s_ref[h] = states[h]
        outs, new_states = _gdn_step(n_heads, states, [q_ref[:, c] for c in cols], [k_ref[:, c] for c in cols],
                                     [v_ref[:, c] for c in cols], [z_ref[:, c] for c in cols], b_ref[...],
                                     al_ref[...], dt_ref[...], nw_ref[...])
        for h in heads:
            o_ref[:, cols[h]] = outs[h].astype(BF16)
            state[h] = new_states[h]

    return pl.pallas_call(
        body, name=name,
        out_shape=[jax.ShapeDtypeStruct((n_tok, w), BF16),
                   jax.ShapeDtypeStruct((n_chunks, n_heads, HEAD_DIM, HEAD_DIM), F32)],
        grid=(n_chunks,), in_specs=[qs, ks, vs, zs, bs, pv, pv, pv],
        out_specs=[pl.BlockSpec((CHUNK, w), lambda i: (i, 0)),
                   pl.BlockSpec((None, n_heads, HEAD_DIM, HEAD_DIM), lambda i: (i, 0, 0, 0))],
        scratch_shapes=[pltpu.VMEM((n_heads, HEAD_DIM, HEAD_DIM), F32)],
        compiler_params=_cparams(("arbitrary",)))(qkv, qkv, qkv, proj, proj, alog_row, dtb_row, nw_row)


def gdn_bwd(qkv, proj, z_col0, small_col0, alog_row, dtb_row, nw_row, states, dout, n_heads, *, name):
    n_tok = qkv.shape[0]
    w = n_heads * HEAD_DIM
    n_chunks = n_tok // CHUNK
    cidx, qs, ks, vs, zs, bs, pv = _gdn_in_specs(n_heads, 3, z_col0 // w, small_col0 // HEAD_DIM, True, n_chunks)

    def body(q_ref, k_ref, v_ref, z_ref, b_ref, al_ref, dt_ref, nw_ref, s_ref, do_ref,
             dqkv_ref, dz_ref, db_ref, dal_ref, ddt_ref, dnw_ref, dstate):
        @pl.when(pl.program_id(0) == 0)
        def _():
            dstate[...] = jnp.zeros_like(dstate)
            dal_ref[...] = jnp.zeros_like(dal_ref)
            ddt_ref[...] = jnp.zeros_like(ddt_ref)
            dnw_ref[...] = jnp.zeros_like(dnw_ref)

        heads = range(n_heads)
        cols = [slice(h * HEAD_DIM, (h + 1) * HEAD_DIM) for h in heads]

        def f(sts, q, k, v, z, bb, al, dt, nw):
            return _gdn_step(n_heads, sts, q, k, v, z, bb, al, dt, nw)
        _, vjp = jax.vjp(f, tuple(s_ref[h] for h in heads), tuple(q_ref[:, c] for c in cols),
                         tuple(k_ref[:, c] for c in cols), tuple(v_ref[:, c] for c in cols),
                         tuple(z_ref[:, c] for c in cols), b_ref[...], al_ref[...], dt_ref[...], nw_ref[...])
        dsts, dqs, dks, dvs, dzs, dbb, da, dd, dn = vjp((tuple(do_ref[:, c] for c in cols),
                                                         tuple(dstate[h] for h in heads)))
        for h in heads:
            dstate[h] = dsts[h]
            dqkv_ref[:, h * HEAD_DIM:(h + 1) * HEAD_DIM] = dqs[h]
            dqkv_ref[:, w + h * HEAD_DIM:w + (h + 1) * HEAD_DIM] = dks[h]
            dqkv_ref[:, 2 * w + h * HEAD_DIM:2 * w + (h + 1) * HEAD_DIM] = dvs[h]
            dz_ref[:, cols[h]] = dzs[h].astype(dz_ref.dtype)
        db_ref[...] = dbb.astype(db_ref.dtype)
        dal_ref[...] += da
        ddt_ref[...] += dd
        dnw_ref[...] += dn

    rowblk = pl.BlockSpec((CHUNK, w), lambda i: (cidx(i), 0))
    return pl.pallas_call(
        body, name=name,
        out_shape=[
            jax.ShapeDtypeStruct((n_tok, 3 * w), F32),
            jax.ShapeDtypeStruct((n_tok, w), BF16), jax.ShapeDtypeStruct((n_tok, HEAD_DIM), BF16),
            jax.ShapeDtypeStruct((1, HEAD_DIM), F32), jax.ShapeDtypeStruct((1, HEAD_DIM), F32),
            jax.ShapeDtypeStruct((1, HEAD_DIM), F32)],
        grid=(n_chunks,),
        in_specs=[qs, ks, vs, zs, bs, pv, pv, pv,
                  pl.BlockSpec((None, n_heads, HEAD_DIM, HEAD_DIM), lambda i: (cidx(i), 0, 0, 0)), rowblk],
        out_specs=[pl.BlockSpec((CHUNK, 3 * w), lambda i: (cidx(i), 0)), rowblk,
                   pl.BlockSpec((CHUNK, HEAD_DIM), lambda i: (cidx(i), 0)), pv, pv, pv],
        scratch_shapes=[pltpu.VMEM((n_heads, HEAD_DIM, HEAD_DIM), F32)],
        compiler_params=_cparams(("arbitrary",)))(qkv, qkv, qkv, proj, proj, alog_row, dtb_row, nw_row, states, dout)


def glu_gate_fwd(y2, gl, bias, *, name):
    n_tok, w = y2.shape
    tm = _tile(n_tok, (256, 128))

    def body(y_ref, g_ref, b_ref, o_ref):
        o_ref[...] = (y_ref[...] * jax.nn.sigmoid(g_ref[...] + b_ref[...])).astype(BF16)

    row = pl.BlockSpec((tm, w), lambda i: (i, 0))
    vec = pl.BlockSpec((1, w), lambda i: (0, 0))
    return pl.pallas_call(body, name=name, out_shape=jax.ShapeDtypeStruct((n_tok, w), BF16), grid=(n_tok // tm,),
                          in_specs=[row, row, vec], out_specs=row, compiler_params=_cparams(("parallel",)))(y2, gl, bias)


def glu_gate_bwd(y2, gl, bias, dys, *, name):
    n_tok, w = y2.shape
    tm = _tile(n_tok, (256, 128))

    def body(y_ref, g_ref, b_ref, d_ref, dy_ref, dg_ref, db_ref):
        @pl.when(pl.program_id(0) == 0)
        def _():
            db_ref[...] = jnp.zeros_like(db_ref)

        sg = jax.nn.sigmoid(g_ref[...] + b_ref[...])
        d = d_ref[...]
        dy_ref[...] = d * sg
        dg = d * y_ref[...] * sg * (1.0 - sg)
        dg_ref[...] = dg.astype(dg_ref.dtype)
        db_ref[...] += jnp.sum(dg, axis=0, keepdims=True)

    row = pl.BlockSpec((tm, w), lambda i: (i, 0))
    vec = pl.BlockSpec((1, w), lambda i: (0, 0))
    return pl.pallas_call(
        body, name=name, out_shape=[jax.ShapeDtypeStruct((n_tok, w), F32), jax.ShapeDtypeStruct((n_tok, w), BF16),
                                    jax.ShapeDtypeStruct((1, w), F32)],
        grid=(n_tok // tm,), in_specs=[row, row, vec, row], out_specs=[row, row, vec],
        compiler_params=_cparams(("arbitrary",)))(y2, gl, bias, dys)


def merge_fwd(proj, bs, bd, *, name):
    n_tok, d = bs.shape
    tm = _tile(n_tok, (256, 128))

    def body(gs_ref, gd_ref, bs_ref, bd_ref, o_ref):
        o_ref[...] = (jax.nn.sigmoid(gs_ref[...]) * bs_ref[...]
                      + jax.nn.sigmoid(gd_ref[...]) * bd_ref[...]).astype(BF16)

    row = pl.BlockSpec((tm, d), lambda i: (i, 0))
    return pl.pallas_call(
        body, name=name, out_shape=jax.ShapeDtypeStruct((n_tok, d), BF16), grid=(n_tok // tm,),
        in_specs=[row, pl.BlockSpec((tm, d), lambda i: (i, 1)), row, row], out_specs=row,
        compiler_params=_cparams(("parallel",)))(proj, proj, bs, bd)


def merge_bwd(proj, bs, bd, dm, *, name):
    n_tok, d = bs.shape
    tm = _tile(n_tok, (256, 128))

    def body(gs_ref, gd_ref, bs_ref, bd_ref, dm_ref, dbs_ref, dbd_ref, dgs_ref, dgd_ref):
        dmv = dm_ref[...]
        ss = jax.nn.sigmoid(gs_ref[...])
        sd = jax.nn.sigmoid(gd_ref[...])
        dbs_ref[...] = (ss * dmv).astype(BF16)
        dbd_ref[...] = (sd * dmv).astype(BF16)
        dgs_ref[...] = (dmv * bs_ref[...] * ss * (1.0 - ss)).astype(BF16)
        dgd_ref[...] = (dmv * bd_ref[...] * sd * (1.0 - sd)).astype(BF16)

    row = pl.BlockSpec((tm, d), lambda i: (i, 0))
    return pl.pallas_call(
        body, name=name, out_shape=[jax.ShapeDtypeStruct((n_tok, d), BF16)] * 4, grid=(n_tok // tm,),
        in_specs=[row, pl.BlockSpec((tm, d), lambda i: (i, 1)), row, row, row], out_specs=[row] * 4,
        compiler_params=_cparams(("parallel",)))(proj, proj, bs, bd, dm)


def add_pairs(grads, recv, out_dtype, *, name):
    core = jnp.reshape(lax.axis_index("c"), (1,)).astype(jnp.int32)
    outs = []
    for t, (a, b) in enumerate(zip(grads, recv)):
        n_sh, h, cols = b.shape
        tr = _tile(h, (256, 128, 64, 32, 16))
        nh = h // tr

        def body(c_ref, a_ref, b_ref, o_ref):
            o_ref[...] = (a_ref[...].astype(F32) + b_ref[...].astype(F32)).astype(out_dtype)

        grid_spec = pltpu.PrefetchScalarGridSpec(
            num_scalar_prefetch=1, grid=(n_sh, nh),
            in_specs=[pl.BlockSpec((None, tr, cols), lambda s, i, c_ref, nh=nh: (s, c_ref[0] * nh + i, 0)),
                      pl.BlockSpec((None, tr, cols), lambda s, i, c_ref: (s, i, 0))],
            out_specs=pl.BlockSpec((None, tr, cols), lambda s, i, c_ref: (s, i, 0)))
        outs.append(pl.pallas_call(body, name=f"{name}_{t}", out_shape=jax.ShapeDtypeStruct(b.shape, out_dtype),
                                   grid_spec=grid_spec, compiler_params=_cparams(("parallel", "parallel")))(core, a, b))
    return outs


def add_chips(parts, *, name):
    outs = []
    for t, p in enumerate(parts):
        _, h, cols = p.shape
        tr = _tile(h, (256, 128, 64, 32, 16))

        def body(p0, p1, p2, p3, o_ref):
            o_ref[...] = ((p0[...].astype(F32) + p1[...].astype(F32)) + p2[...].astype(F32)) + p3[...].astype(F32)

        specs = [pl.BlockSpec((None, tr, cols), lambda i, s=s: (s, i, 0)) for s in range(N_CHIPS)]
        outs.append(pl.pallas_call(body, name=f"{name}_{t}", out_shape=jax.ShapeDtypeStruct((h, cols), F32),
                                   grid=(h // tr,), in_specs=specs, out_specs=pl.BlockSpec((tr, cols), lambda i: (i, 0)),
                                   compiler_params=_cparams(("parallel",)))(p, p, p, p))
    return outs


ADAMW_BLOCK_BYTES = 3 * 512 * 1024


def adamw(w, g, m, v, *, name):
    shape = w.shape
    cols = shape[-1]
    rows = w.size // cols
    tr = _tile(rows, tuple(t for t in (1024, 512, 256, 128, 64, 32, 16, 8) if t * cols * 4 <= ADAMW_BLOCK_BYTES))
    c1 = 1.0 / (1.0 - ADAM_B1 ** ADAM_STEP)
    c2 = 1.0 / (1.0 - ADAM_B2 ** ADAM_STEP)

    def body(w_ref, g_ref, m_ref, v_ref, d_ref, nm_ref, nv_ref):
        gv = g_ref[...]
        nm = ADAM_B1 * m_ref[...] + (1.0 - ADAM_B1) * gv
        nv = ADAM_B2 * v_ref[...] + (1.0 - ADAM_B2) * (gv * gv)
        d_ref[...] = -ADAM_LR * ((nm * c1) / (jnp.sqrt(nv * c2) + ADAM_EPS) + ADAM_WD * w_ref[...])
        nm_ref[...] = nm
        nv_ref[...] = nv

    blk = pl.BlockSpec((tr, cols), lambda i: (i, 0))
    outs = pl.pallas_call(body, name=name, out_shape=[jax.ShapeDtypeStruct((rows, cols), F32)] * 3, grid=(rows // tr,),
                          in_specs=[blk] * 4, out_specs=[blk] * 3, compiler_params=_cparams(("parallel",)))(
        w.reshape(rows, cols), g.reshape(rows, cols), m.reshape(rows, cols), v.reshape(rows, cols))
    return [o.reshape(shape) for o in outs]


def _place():
    return lax.axis_index("x"), lax.axis_index("y"), lax.axis_index("c")


def _other_chips(x, y):
    return [(1 - x, y), (x, 1 - y), (1 - x, 1 - y)]


ANY = pl.BlockSpec(memory_space=pl.ANY)
STAGE_BYTES = 1 << 20


def _stage_shape(rows, cols, dtype):
    mult = 32 // jnp.dtype(dtype).itemsize
    per_row = (-(-cols // 128) * 128) * jnp.dtype(dtype).itemsize
    chunk = max(mult, STAGE_BYTES // per_row // mult * mult)
    return pltpu.VMEM((2, min(chunk, rows), cols), dtype)


def _staged_copy(src, dst, buf, sem_in, sem_out, k):
    rows, chunk = src.shape[0], buf.shape[1]
    pending = []
    for i, r0 in enumerate(range(0, rows, chunk)):
        sz = min(chunk, rows - r0)
        slot = i % 2
        if i >= 2:
            pending[i - 2].wait()
        stage = buf.at[slot, pl.ds(0, sz)]
        cin = pltpu.make_async_copy(src.at[pl.ds(r0, sz)], stage, sem_in.at[2 * k + slot])
        cin.start()
        cin.wait()
        cout = pltpu.make_async_copy(stage, dst.at[pl.ds(r0, sz)], sem_out.at[2 * k + slot])
        cout.start()
        pending.append(cout)
    for cp in pending[max(0, len(pending) - 2):]:
        cp.wait()


def gather_chips(blocks, halve, *, name):
    n = len(blocks)

    def body(*refs):
        ins, outs = refs[:n], refs[n:2 * n]
        send_sems, recv_sems, fwd_send, fwd_recv, stage_in, stage_out = refs[2 * n:2 * n + 6]
        bufs = refs[2 * n + 6:]
        x, y, c = _place()
        me = 2 * x + y
        chips = _other_chips(x, y)
        sibling = (x, y, 1 - c)
        sends, fwds = [], []
        for t in range(n):
            for j, (px, py) in enumerate(chips):
                if halve[t]:
                    h = ins[t].shape[0] // 2
                    rows = pl.ds(c * h, h)
                    src, dst = ins[t].at[rows], outs[t].at[me, rows]
                else:
                    src, dst = ins[t], outs[t].at[me]
                cp = pltpu.make_async_remote_copy(src_ref=src, dst_ref=dst, send_sem=send_sems.at[3 * t + j],
                                                  recv_sem=recv_sems.at[3 * t + j], device_id=(px, py, c),
                                                  device_id_type=MESH)
                cp.start()
                sends.append(cp)
        for t in range(n):
            _staged_copy(ins[t], outs[t].at[me], bufs[t], stage_in, stage_out, t)
        for t in range(n):
            for j, (px, py) in enumerate(chips):
                src_chip = 2 * px + py
                if halve[t]:
                    h = ins[t].shape[0] // 2
                    rows = pl.ds(c * h, h)
                    landed = outs[t].at[src_chip, rows]
                    pltpu.make_async_remote_copy(src_ref=landed, dst_ref=landed, send_sem=send_sems.at[3 * t + j],
                                                 recv_sem=recv_sems.at[3 * t + j], device_id=(px, py, c),
                                                 device_id_type=MESH).wait_recv()
                    cp = pltpu.make_async_remote_copy(src_ref=landed, dst_ref=landed, send_sem=fwd_send.at[3 * t + j],
                                                      recv_sem=fwd_recv.at[3 * t + j], device_id=sibling,
                                                      device_id_type=MESH)
                    cp.start()
                    fwds.append(cp)
                else:
                    landed = outs[t].at[src_chip]
                    pltpu.make_async_remote_copy(src_ref=landed, dst_ref=landed, send_sem=send_sems.at[3 * t + j],
                                                 recv_sem=recv_sems.at[3 * t + j], device_id=(px, py, c),
                                                 device_id_type=MESH).wait_recv()
        for t in range(n):
            if not halve[t]:
                continue
            h = ins[t].shape[0] // 2
            for j, (px, py) in enumerate(chips):
                theirs = outs[t].at[2 * px + py, pl.ds((1 - c) * h, h)]
                pltpu.make_async_remote_copy(src_ref=theirs, dst_ref=theirs, send_sem=fwd_send.at[3 * t + j],
                                             recv_sem=fwd_recv.at[3 * t + j], device_id=sibling,
                                             device_id_type=MESH).wait_recv()
        for cp in sends + fwds:
            cp.wait_send()

    return pl.pallas_call(
        body, name=name, out_shape=[jax.ShapeDtypeStruct((N_CHIPS,) + b.shape, b.dtype) for b in blocks],
        in_specs=[ANY] * n, out_specs=[ANY] * n,
        scratch_shapes=[pltpu.SemaphoreType.DMA((3 * n,))] * 4 + [pltpu.SemaphoreType.DMA((2 * n,))] * 2
        + [_stage_shape(b.shape[0], b.shape[1], b.dtype) for b in blocks],
        compiler_params=pltpu.CompilerParams(has_side_effects=True, vmem_limit_bytes=VMEM_LIMIT))(*blocks)


HBM_SPEC = pl.BlockSpec(memory_space=pltpu.HBM)
SEM_SPEC = pl.BlockSpec(memory_space=pltpu.SEMAPHORE)
DATAFLOW = pltpu.SideEffectType.DATAFLOW_SIDE_EFFECTING


def _in_hbm(v):
    return pltpu.with_memory_space_constraint(v, pltpu.HBM)


def _split_start(srcs, land_shapes, make_copies, *, name):
    n = len(srcs)
    lands = [_in_hbm(lax.empty(s.shape, s.dtype)) for s in land_shapes]

    def body(*refs):
        ins, lands_in = refs[:n], refs[n:2 * n]
        send_sems, recv_sems = refs[2 * n], refs[2 * n + 1]
        token = refs[-1]
        for cp in make_copies(ins, lands_in, send_sems, recv_sems, False):
            cp.start()
        token[...] = jnp.zeros_like(token)

    outs = pl.pallas_call(
        body, name=name,
        out_shape=(pltpu.SemaphoreType.DMA((3 * n,)), pltpu.SemaphoreType.DMA((3 * n,)),
                   *[pltpu.HBM(s.shape, s.dtype) for s in srcs], *[pltpu.HBM(s.shape, s.dtype) for s in land_shapes],
                   jax.ShapeDtypeStruct((8, 128), F32)),
        in_specs=[HBM_SPEC] * (2 * n),
        out_specs=(SEM_SPEC, SEM_SPEC, *([HBM_SPEC] * (2 * n)), pl.BlockSpec(memory_space=pltpu.VMEM)),
        input_output_aliases={i: 2 + i for i in range(2 * n)},
        compiler_params=pltpu.CompilerParams(has_side_effects=DATAFLOW))(*[_in_hbm(s) for s in srcs], *lands)
    return outs[0], outs[1], list(outs[2:2 + n]), list(outs[2 + n:2 + 2 * n]), outs[-1]


def _split_wait(send_sems, recv_sems, srcs, lands, after, make_copies, *, name):
    n = len(srcs)

    def body(*refs):
        ins, lands_in = refs[:n], refs[n:2 * n]
        s_sems, r_sems = refs[2 * n], refs[2 * n + 1]
        token = refs[-1]
        for cp in make_copies(ins, lands_in, s_sems, r_sems, False):
            cp.wait_send()
        for cp in make_copies(ins, lands_in, s_sems, r_sems, True):
            cp.wait_recv()
        token[...] = jnp.zeros_like(token)

    outs = pl.pallas_call(
        body, name=name,
        out_shape=(*[pltpu.HBM(s.shape, s.dtype) for s in srcs], *[pltpu.HBM(s.shape, s.dtype) for s in lands],
                   jax.ShapeDtypeStruct((8, 128), F32)),
        in_specs=[HBM_SPEC] * (2 * n) + [SEM_SPEC, SEM_SPEC, ANY],
        out_specs=(*([HBM_SPEC] * (2 * n)), pl.BlockSpec(memory_space=pltpu.VMEM)),
        input_output_aliases={i: i for i in range(2 * n)},
        compiler_params=pltpu.CompilerParams(has_side_effects=DATAFLOW))(*srcs, *lands, send_sems, recv_sems, after)
    return list(outs[:n]), list(outs[n:2 * n]), outs[-1]


def _gather_copies(halve):
    def make(ins, lands, send_sems, recv_sems, arrivals):
        x, y, c = _place()
        me = 2 * x + y
        cps = []
        for t in range(len(ins)):
            for j, (px, py) in enumerate(_other_chips(x, y)):
                if halve[t]:
                    h = ins[t].shape[0] // 2
                    rows = pl.ds(c * h, h)
                    src, dst, landed = ins[t].at[rows], lands[t].at[me, rows], lands[t].at[2 * px + py, rows]
                else:
                    src, dst, landed = ins[t], lands[t].at[me], lands[t].at[2 * px + py]
                sem = dict(send_sem=send_sems.at[3 * t + j], recv_sem=recv_sems.at[3 * t + j], device_id=(px, py, c),
                           device_id_type=MESH)
                if arrivals:
                    cps.append(pltpu.make_async_remote_copy(src_ref=landed, dst_ref=landed, **sem))
                else:
                    cps.append(pltpu.make_async_remote_copy(src_ref=src, dst_ref=dst, **sem))
        return cps
    return make


def gather_start(blocks, halve, *, name):
    shapes = [jax.ShapeDtypeStruct((N_CHIPS,) + b.shape, b.dtype) for b in blocks]
    return _split_start(blocks, shapes, _gather_copies(halve), name=name)


def gather_wait(started, halve, after, *, name):
    send_sems, recv_sems, srcs, lands, _ = started
    return _split_wait(send_sems, recv_sems, srcs, lands, after, _gather_copies(halve), name=name)


def gather_finish(blocks, lands, halve, token, *, name):
    n = len(blocks)

    def body(*refs):
        ins, outs = refs[:n], refs[2 * n + 1:3 * n + 1]
        fwd_send, fwd_recv, stage_in, stage_out = refs[3 * n + 1:3 * n + 5]
        bufs = refs[3 * n + 5:]
        x, y, c = _place()
        me = 2 * x + y
        chips = _other_chips(x, y)
        sibling = (x, y, 1 - c)
        fwds = []
        for t in range(n):
            if not halve[t]:
                continue
            h = ins[t].shape[0] // 2
            for j, (px, py) in enumerate(chips):
                landed = outs[t].at[2 * px + py, pl.ds(c * h, h)]
                cp = pltpu.make_async_remote_copy(src_ref=landed, dst_ref=landed, send_sem=fwd_send.at[3 * t + j],
                                                  recv_sem=fwd_recv.at[3 * t + j], device_id=sibling, device_id_type=MESH)
                cp.start()
                fwds.append(cp)
        for t in range(n):
            _staged_copy(ins[t], outs[t].at[me], bufs[t], stage_in, stage_out, t)
        for t in range(n):
            if not halve[t]:
                continue
            h = ins[t].shape[0] // 2
            for j, (px, py) in enumerate(chips):
                theirs = outs[t].at[2 * px + py, pl.ds((1 - c) * h, h)]
                pltpu.make_async_remote_copy(src_ref=theirs, dst_ref=theirs, send_sem=fwd_send.at[3 * t + j],
                                             recv_sem=fwd_recv.at[3 * t + j], device_id=sibling,
                                             device_id_type=MESH).wait_recv()
        for cp in fwds:
            cp.wait_send()

    return pl.pallas_call(
        body, name=name, out_shape=[jax.ShapeDtypeStruct(v.shape, v.dtype) for v in lands],
        in_specs=[ANY] * (2 * n) + [pl.BlockSpec(memory_space=pltpu.VMEM)], out_specs=[ANY] * n,
        input_output_aliases={n + i: i for i in range(n)},
        scratch_shapes=[pltpu.SemaphoreType.DMA((3 * n,))] * 2 + [pltpu.SemaphoreType.DMA((2 * n,))] * 2
        + [_stage_shape(b.shape[0], b.shape[1], b.dtype) for b in blocks],
        compiler_params=pltpu.CompilerParams(has_side_effects=True, vmem_limit_bytes=VMEM_LIMIT))(*blocks, *lands, token)


def _xchg_copies(ins, lands, send_sems, recv_sems, arrivals):
    x, y, c = _place()
    me = 2 * x + y
    cps = []
    for t in range(len(ins)):
        for j, (px, py) in enumerate(_other_chips(x, y)):
            landed = lands[t].at[2 * px + py]
            sem = dict(send_sem=send_sems.at[3 * t + j], recv_sem=recv_sems.at[3 * t + j], device_id=(px, py, c),
                       device_id_type=MESH)
            if arrivals:
                cps.append(pltpu.make_async_remote_copy(src_ref=landed, dst_ref=landed, **sem))
            else:
                cps.append(pltpu.make_async_remote_copy(src_ref=ins[t].at[2 * px + py], dst_ref=lands[t].at[me], **sem))
    return cps


def xchg_start(parts, *, name):
    return _split_start(parts, [jax.ShapeDtypeStruct(p.shape, p.dtype) for p in parts], _xchg_copies, name=name)


def xchg_wait(started, after, *, name):
    send_sems, recv_sems, srcs, lands, _ = started
    return _split_wait(send_sems, recv_sems, srcs, lands, after, _xchg_copies, name=name)


def xchg_finish(parts, lands, *, name):
    n = len(parts)

    def body(*refs):
        ins, outs = refs[:n], refs[2 * n:3 * n]
        stage_in, stage_out = refs[3 * n:3 * n + 2]
        bufs = refs[3 * n + 2:]
        x, y, _ = _place()
        me = 2 * x + y
        for t in range(n):
            _staged_copy(ins[t].at[me], outs[t].at[me], bufs[t], stage_in, stage_out, t)

    return pl.pallas_call(
        body, name=name, out_shape=[jax.ShapeDtypeStruct(v.shape, v.dtype) for v in lands],
        in_specs=[ANY] * (2 * n), out_specs=[ANY] * n, input_output_aliases={n + i: i for i in range(n)},
        scratch_shapes=[pltpu.SemaphoreType.DMA((2 * n,))] * 2
        + [_stage_shape(p.shape[1], p.shape[2], p.dtype) for p in parts],
        compiler_params=pltpu.CompilerParams(has_side_effects=True, vmem_limit_bytes=VMEM_LIMIT))(*parts, *lands)


def pair_split(grads, *, name):
    n = len(grads)

    def body(*refs):
        ins, recv = refs[:n], refs[n:2 * n]
        send_sems, recv_sems = refs[2 * n:]
        x, y, c = _place()
        sibling = (x, y, 1 - c)
        cps = []
        for t in range(n):
            h = ins[t].shape[1] // 2
            cp = pltpu.make_async_remote_copy(src_ref=ins[t].at[:, pl.ds((1 - c) * h, h)], dst_ref=recv[t],
                                              send_sem=send_sems.at[t], recv_sem=recv_sems.at[t], device_id=sibling,
                                              device_id_type=MESH)
            cp.start()
            cps.append(cp)
        for cp in cps:
            cp.wait()

    half = [jax.ShapeDtypeStruct((g.shape[0], g.shape[1] // 2, g.shape[2]), g.dtype) for g in grads]
    return pl.pallas_call(
        body, name=name, out_shape=half, in_specs=[ANY] * n, out_specs=[ANY] * n,
        scratch_shapes=[pltpu.SemaphoreType.DMA((n,))] * 2,
        compiler_params=pltpu.CompilerParams(has_side_effects=True))(*grads)


def chip_exchange(parts, *, name):
    n = len(parts)

    def body(*refs):
        ins, outs = refs[:n], refs[n:2 * n]
        send_sems, recv_sems, stage_in, stage_out = refs[2 * n:2 * n + 4]
        bufs = refs[2 * n + 4:]
        x, y, c = _place()
        me = 2 * x + y
        chips = _other_chips(x, y)
        cps = []
        for t in range(n):
            for j, (px, py) in enumerate(chips):
                cp = pltpu.make_async_remote_copy(src_ref=ins[t].at[2 * px + py], dst_ref=outs[t].at[me],
                                                  send_sem=send_sems.at[3 * t + j], recv_sem=recv_sems.at[3 * t + j],
                                                  device_id=(px, py, c), device_id_type=MESH)
                cp.start()
                cps.append(cp)
        for t in range(n):
            _staged_copy(ins[t].at[me], outs[t].at[me], bufs[t], stage_in, stage_out, t)
        for t in range(n):
            for j, (px, py) in enumerate(chips):
                landed = outs[t].at[2 * px + py]
                pltpu.make_async_remote_copy(src_ref=landed, dst_ref=landed, send_sem=send_sems.at[3 * t + j],
                                             recv_sem=recv_sems.at[3 * t + j], device_id=(px, py, c),
                                             device_id_type=MESH).wait_recv()
        for cp in cps:
            cp.wait_send()

    return pl.pallas_call(
        body, name=name, out_shape=[jax.ShapeDtypeStruct(p.shape, p.dtype) for p in parts],
        in_specs=[ANY] * n, out_specs=[ANY] * n,
        scratch_shapes=[pltpu.SemaphoreType.DMA((3 * n,))] * 2 + [pltpu.SemaphoreType.DMA((2 * n,))] * 2
        + [_stage_shape(p.shape[1], p.shape[2], p.dtype) for p in parts],
        compiler_params=pltpu.CompilerParams(has_side_effects=True, vmem_limit_bytes=VMEM_LIMIT))(*parts)


def pair_join(halves, *, name):
    n = len(halves)

    def body(*refs):
        ins, outs = refs[:n], refs[n:2 * n]
        send_sems, recv_sems, stage_in, stage_out = refs[2 * n:2 * n + 4]
        bufs = refs[2 * n + 4:]
        x, y, c = _place()
        sibling = (x, y, 1 - c)
        cps = []
        for t in range(n):
            h = ins[t].shape[0]
            cp = pltpu.make_async_remote_copy(src_ref=ins[t], dst_ref=outs[t].at[pl.ds(c * h, h)],
                                              send_sem=send_sems.at[t], recv_sem=recv_sems.at[t], device_id=sibling,
                                              device_id_type=MESH)
            cp.start()
            cps.append(cp)
        for t in range(n):
            h = ins[t].shape[0]
            _staged_copy(ins[t], outs[t].at[pl.ds(c * h, h)], bufs[t], stage_in, stage_out, t)
        for t in range(n):
            h = ins[t].shape[0]
            theirs = outs[t].at[pl.ds((1 - c) * h, h)]
            pltpu.make_async_remote_copy(src_ref=theirs, dst_ref=theirs, send_sem=send_sems.at[t],
                                         recv_sem=recv_sems.at[t], device_id=sibling, device_id_type=MESH).wait_recv()
        for cp in cps:
            cp.wait_send()

    return pl.pallas_call(
        body, name=name, out_shape=[jax.ShapeDtypeStruct((2 * p.shape[0], p.shape[1]), p.dtype) for p in halves],
        in_specs=[ANY] * n, out_specs=[ANY] * n,
        scratch_shapes=[pltpu.SemaphoreType.DMA((n,))] * 2 + [pltpu.SemaphoreType.DMA((2 * n,))] * 2
        + [_stage_shape(p.shape[0], p.shape[1], p.dtype) for p in halves],
        compiler_params=pltpu.CompilerParams(has_side_effects=True, vmem_limit_bytes=VMEM_LIMIT))(*halves)


def reduce_scatter(grads, pay_dtype, *, name):
    recv = pair_split(grads, name=name + "_split")
    part = add_pairs(grads, recv, pay_dtype, name=name + "_add2")
    got = chip_exchange(part, name=name + "_xchg")
    half = add_chips(got, name=name + "_add4")
    return pair_join(half, name=name + "_join")


def _in_sizes(d, w, n_heads):
    return (w, w, w, w, w, n_heads, n_heads, d, d)


def _wcat_from_gathered(wg, d, w, n_heads):
    full = jnp.concatenate([wg[s] for s in range(N_CHIPS)], axis=1)
    sizes = _in_sizes(d, w, n_heads)
    offs = [0]
    for s in sizes:
        offs.append(offs[-1] + s)
    pieces = [full[:, offs[i]:offs[i + 1]] for i in range(len(sizes))]
    u, q, k, v, z, beta, a, gs, gd = pieces
    pad = jnp.zeros((full.shape[0], SMALL_W - 2 * n_heads), full.dtype)
    return jnp.concatenate([gs, gd, u, q, k, v, z, beta, a, pad], axis=1)


def _wcat_grad_to_shards(dwcat, d, w, n_heads):
    gs, gd = dwcat[:, :d], dwcat[:, d:2 * d]
    o = 2 * d
    u, q, k, v, z = [dwcat[:, o + i * w:o + (i + 1) * w] for i in range(5)]
    o += 5 * w
    beta, a = dwcat[:, o:o + n_heads], dwcat[:, o + n_heads:o + 2 * n_heads]
    full = jnp.concatenate([u, q, k, v, z, beta, a, gs, gd], axis=1)
    return jnp.stack(jnp.split(full, N_CHIPS, axis=1))


def _s5_discretize(a_re, a_im, log_dt, b_re, b_im):
    dt = jnp.exp(log_dt)[:, None]
    mag = jnp.exp(a_re * dt)
    lbar_r, lbar_i = mag * jnp.cos(a_im * dt), mag * jnp.sin(a_im * dt)
    den = a_re * a_re + a_im * a_im
    zr, zi = _cmul(lbar_r - 1.0, lbar_i, a_re / den, -a_im / den)
    bbar_r, bbar_i = _cmul(zr[:, :, None], zi[:, :, None], b_re, b_im)
    return lbar_r, lbar_i, bbar_r, bbar_i


def _blockdiag_in(bbar):
    g, p, h = bbar.shape
    t = g // GROUPS_PER_TILE
    bb = bbar.reshape(t, GROUPS_PER_TILE, p, h).transpose(0, 1, 3, 2)
    eye = jnp.eye(GROUPS_PER_TILE, dtype=bbar.dtype)
    return jnp.einsum('tjhp,jk->tjhkp', bb, eye).reshape(t, GROUPS_PER_TILE * h, GROUPS_PER_TILE * p)


def _blockdiag_in_grad(dblk, g, p, h):
    t = g // GROUPS_PER_TILE
    d5 = dblk.reshape(t, GROUPS_PER_TILE, h, GROUPS_PER_TILE, p)
    eye = jnp.eye(GROUPS_PER_TILE, dtype=dblk.dtype)
    diag = jnp.einsum('tjhkp,jk->tjhp', d5, eye)
    return diag.transpose(0, 1, 3, 2).reshape(g, p, h)


def _blockdiag_out(cmat):
    g, h, p = cmat.shape
    t = g // GROUPS_PER_TILE
    cc = cmat.reshape(t, GROUPS_PER_TILE, h, p).transpose(0, 1, 3, 2)
    eye = jnp.eye(GROUPS_PER_TILE, dtype=cmat.dtype)
    return jnp.einsum('tjph,jk->tjpkh', cc, eye).reshape(t, GROUPS_PER_TILE * p, GROUPS_PER_TILE * h)


def _blockdiag_out_grad(dblk, g, h, p):
    t = g // GROUPS_PER_TILE
    d5 = dblk.reshape(t, GROUPS_PER_TILE, p, GROUPS_PER_TILE, h)
    eye = jnp.eye(GROUPS_PER_TILE, dtype=dblk.dtype)
    diag = jnp.einsum('tjpkh,jk->tjph', d5, eye)
    return diag.transpose(0, 1, 3, 2).reshape(g, h, p)


def _pad_row(v, width):
    return jnp.pad(v.reshape(1, -1), ((0, 0), (0, width - v.size)))


def _pack(arrs, rows_mult):
    flat = jnp.concatenate([a.reshape(-1) for a in arrs])
    per = 128 * rows_mult
    total = -(-flat.size // per) * per
    return jnp.pad(flat, (0, total - flat.size))


def _unpack(flat, like):
    out, o = [], 0
    for a in like:
        out.append(flat[o:o + a.size].reshape(a.shape))
        o += a.size
    return out


def kernel(x, ffn1_w_gu, ffn1_w_down, ln1_g, ln1_b, w_in, conv_w, ssm_a_re, ssm_a_im, ssm_log_dt, ssm_b_re, ssm_b_im, ssm_c_re, ssm_c_im, ssm_d, glu_w, glu_b, gdn_a_log, gdn_dt_bias, gdn_norm_w, w_br_ssm, w_br_gdn, w_out, ln2_g, ln2_b, ffn2_w_gu, ffn2_w_down, ln3_g, ln3_b, loss_target, m_ffn1_w_gu, m_ffn1_w_down, m_ln1_g, m_ln1_b, m_w_in, m_conv_w, m_ssm_a_re, m_ssm_a_im, m_ssm_log_dt, m_ssm_b_re, m_ssm_b_im, m_ssm_c_re, m_ssm_c_im, m_ssm_d, m_glu_w, m_glu_b, m_gdn_a_log, m_gdn_dt_bias, m_gdn_norm_w, m_w_br_ssm, m_w_br_gdn, m_w_out, m_ln2_g, m_ln2_b, m_ffn2_w_gu, m_ffn2_w_down, m_ln3_g, m_ln3_b, v_ffn1_w_gu, v_ffn1_w_down, v_ln1_g, v_ln1_b, v_w_in, v_conv_w, v_ssm_a_re, v_ssm_a_im, v_ssm_log_dt, v_ssm_b_re, v_ssm_b_im, v_ssm_c_re, v_ssm_c_im, v_ssm_d, v_glu_w, v_glu_b, v_gdn_a_log, v_gdn_dt_bias, v_gdn_norm_w, v_w_br_ssm, v_w_br_gdn, v_w_out, v_ln2_g, v_ln2_b, v_ffn2_w_gu, v_ffn2_w_down, v_ln3_g, v_ln3_b):
    args = locals()
    wts = {n: args[n] for n in WEIGHT_NAMES}
    mom = {n: args["m_" + n] for n in WEIGHT_NAMES}
    var = {n: args["v_" + n] for n in WEIGHT_NAMES}

    depth = ln1_g.shape[0]
    n_tok, d = x.shape[1], x.shape[2]
    w = glu_w.shape[-1]
    n_heads = gdn_a_log.shape[-1]
    n_groups, n_state, grp = ssm_b_re.shape[1], ssm_b_re.shape[2], ssm_b_re.shape[3]
    alpha = (2.0 * depth) ** 0.25
    u_col0 = 2 * d
    qkv_col0 = u_col0 + w
    z_col0 = u_col0 + 4 * w
    small_col0 = u_col0 + 5 * w
    x_idx, y_idx, _ = _place()
    chip = 2 * x_idx + y_idx

    xcur = x[0]
    xcur_b = xcur.astype(BF16)
    saved = []
    halve = [True] * len(BIG) + [False]

    def start_gather(layer, order_token):
        shards = [wts[n][layer].astype(BF16) for n in BIG] + [conv_w[layer] + order_token[0, 0]]
        return gather_start(shards, halve, name=f"gather_start_l{layer}")

    started = start_gather(0, jnp.zeros((1, 1), F32))
    after = xcur
    for l in range(depth):
        shards, lands, waited = gather_wait(started, halve, after, name=f"gather_wait_l{l}")
        started = start_gather(l + 1, waited) if l + 1 < depth else started
        gathered = gather_finish(shards, lands, halve, started[4], name="gather_finish")
        gw = dict(zip(BIG, gathered[:-1]))
        conv_full = jnp.concatenate([gathered[-1][s] for s in range(N_CHIPS)], axis=1)
        wgu1, wgu2 = gw['ffn1_w_gu'], gw['ffn2_w_gu']
        wd1 = gw['ffn1_w_down'].reshape(-1, d)
        wd2 = gw['ffn2_w_down'].reshape(-1, d)
        wcat = _wcat_from_gathered(gw['w_in'], d, w, n_heads)
        wglu = gw['glu_w'].reshape(w, w)
        wbs, wbd = gw['w_br_ssm'], gw['w_br_gdn']
        wout = gw['w_out'].reshape(d, d)
        f = wd1.shape[0]

        vec = lambda v: v[l].reshape(1, -1)
        x0, x0b = xcur, xcur_b
        h1 = mm(x0b, wgu1, b_nb=N_CHIPS, out_nb=2, name=f"ffn_up")
        r1, x1, x1b = down_res_ln(h1, wd1, x0, vec(ln1_g), vec(ln1_b), swiglu=True, alpha=alpha, scale=0.5,
                                  name="ffn_down")
        proj = mm(x1b, wcat, name="in_proj")
        (lbar_r, lbar_i, bbar_r, bbar_i), disc_vjp = jax.vjp(
            _s5_discretize, ssm_a_re[l], ssm_a_im[l], ssm_log_dt[l], ssm_b_re[l], ssm_b_im[l])
        s5w = (_blockdiag_in(bbar_r), _blockdiag_in(bbar_i), _blockdiag_out(ssm_c_re[l]), _blockdiag_out(ssm_c_im[l]),
               lbar_r.reshape(1, -1), lbar_i.reshape(1, -1), ssm_d[l].reshape(1, -1))
        ypre, y2, y2b = s5_fwd(proj, u_col0, *s5w, name="s5_fwd")
        gl = mm(y2b, wglu, name="glu_proj")
        ys = glu_gate_fwd(y2, gl, vec(glu_b), name="glu_gate")
        qkv = conv_fwd(proj, qkv_col0, conv_full, name="conv_fwd")
        gdn_rows = (_pad_row(gdn_a_log[l], HEAD_DIM), _pad_row(gdn_dt_bias[l], HEAD_DIM), gdn_norm_w[l].reshape(1, -1))
        yg, states = gdn_fwd(qkv, proj, z_col0, small_col0, *gdn_rows, n_heads, name="gdn_fwd")
        bs = mm(ys, wbs, b_nb=N_CHIPS, name="br_ssm")
        bd = mm(yg, wbd, b_nb=N_CHIPS, name="br_gdn")
        mrg = merge_fwd(proj, bs, bd, name="merge")
        r2, x2, x2b = down_res_ln(mrg, wout, x1, vec(ln2_g), vec(ln2_b), swiglu=False, alpha=alpha, scale=1.0,
                                  name="mix_out")
        h3 = mm(x2b, wgu2, b_nb=N_CHIPS, out_nb=2, name="ffn_up")
        r3, x3, x3b = down_res_ln(h3, wd2, x2, vec(ln3_g), vec(ln3_b), swiglu=True, alpha=alpha, scale=0.5,
                                  name="ffn_down")
        saved.append(dict(x0b=x0b, h1=h1, r1=r1, x1b=x1b, proj=proj, s5w=s5w, disc_vjp=disc_vjp, ypre=ypre, y2=y2,
                          y2b=y2b, gl=gl, ys=ys, qkv=qkv, gdn_rows=gdn_rows, yg=yg, states=states, bs=bs, bd=bd, mrg=mrg,
                          r2=r2, x2b=x2b, h3=h3, r3=r3, wgu1=wgu1, wgu2=wgu2, wd1=wd1, wd2=wd2, wcat=wcat, wglu=wglu,
                          wbs=wbs, wbd=wbd, wout=wout, conv_full=conv_full))
        xcur, xcur_b = x3, x3b
        after = x3

    dy, loss_blk = loss_head(xcur, loss_target[0], name="loss_head")
    loss = lax.psum(loss_blk[0, 0], ("x", "y", "c"))

    big_grads = {n: [None] * depth for n in BIG}
    small_grads = {n: [None] * depth for n in SMALL}
    def finish_reduction(layer, exchange, after_arr):
        parts, lands, _ = xchg_wait(exchange, after_arr, name=f"xchg_wait_l{layer}")
        half = add_chips(xchg_finish(parts, lands, name="xchg_finish"), name="rs_big_add4")
        for n, g in zip(BIG, pair_join(half, name="rs_big_join")):
            big_grads[n][layer] = g

    pending = None
    for l in reversed(range(depth)):
        s = saved[l]
        order = pending[1][4][0, 0] if pending is not None else 0.0
        vec = lambda v: v[l].reshape(1, -1)

        def ffn_back(dx_out, r, g_ln, xin, hh, wgu, wd):
            dr, drb, dg, db = ln_bwd(r, g_ln, dx_out, name="ln_bwd")
            dh, act = ffn_bwd_mid(dr, wd, hh, scale=0.5, name="ffn_bwd_mid")
            dwd = mm(act, drb, ta=True, out_dtype=BF16, out_scale=0.5, name="ffn_dwd")
            dwgu = mm(xin, dh, ta=True, b_nb=2, out_nb=N_CHIPS, out_dtype=BF16, name="ffn_dwgu")
            dxin = mm(dh, wgu, tb=True, a_nb=2, b_nb=N_CHIPS, add=dr, add_scale=alpha, name="ffn_dx")
            return dxin, dg, db, dwgu, dwd.reshape(N_CHIPS, -1, d)

        dx2, dg3, db3, dwgu2, dwd2 = ffn_back(dy, s['r3'], vec(ln3_g) + order, s['x2b'], s['h3'], s['wgu2'], s['wd2'])
        dr2, dr2b, dg2, db2 = ln_bwd(s['r2'], vec(ln2_g), dx2, name="ln_bwd")
        dmrg = mm(dr2b, s['wout'], tb=True, name="mix_dm")
        dwout = mm(s['mrg'], dr2b, ta=True, out_dtype=BF16, name="mix_dwout").reshape(N_CHIPS, -1, d)
        dbs, dbd, dgs, dgd = merge_bwd(s['proj'], s['bs'], s['bd'], dmrg, name="merge_bwd")
        dwbs = mm(s['ys'], dbs, ta=True, out_nb=N_CHIPS, out_dtype=BF16, name="br_dw")
        dwbd = mm(s['yg'], dbd, ta=True, out_nb=N_CHIPS, out_dtype=BF16, name="br_dw")
        dys = mm(dbs, s['wbs'], tb=True, b_nb=N_CHIPS, name="br_dx")
        dyg = mm(dbd, s['wbd'], tb=True, b_nb=N_CHIPS, name="br_dx")
        dy2a, dgl, dglu_b = glu_gate_bwd(s['y2'], s['gl'], vec(glu_b), dys, name="glu_gate_bwd")
        dwglu = mm(s['y2b'], dgl, ta=True, out_dtype=BF16, name="glu_dw").reshape(N_CHIPS, -1, w)
        dy2 = mm(dgl, s['wglu'], tb=True, add=dy2a, name="glu_dx")
        du, dbr, dbi, dcr, dci, dlr, dli, dd = s5_bwd(s['proj'], u_col0, s['ypre'], dy2, *s['s5w'], name="s5_bwd")
        dqkv_act, dz, dsmall, dalog, ddtb, dnw = gdn_bwd(s['qkv'], s['proj'], z_col0, small_col0, *s['gdn_rows'],
                                                           s['states'], dyg, n_heads, name="gdn_bwd")
        dqkv, dconv = conv_bwd(s['proj'], qkv_col0, s['conv_full'], dqkv_act, name="conv_bwd")
        dsmall_w = jnp.pad(dsmall, ((0, 0), (0, SMALL_W - HEAD_DIM)))
        dproj = jnp.concatenate([dgs, dgd, du, dqkv, dz, dsmall_w], axis=1)
        dwcat = mm(s['x1b'], dproj, ta=True, out_dtype=BF16, name="in_dw")
        dx1 = mm(dproj, s['wcat'], tb=True, add=dr2, add_scale=alpha, name="in_dx")
        dx0, dg1, db1, dwgu1, dwd1 = ffn_back(dx1, s['r1'], vec(ln1_g), s['x0b'], s['h1'], s['wgu1'], s['wd1'])
        dy = dx0

        da_re, da_im, dlog_dt, db_re, db_im = s['disc_vjp'](
            (dlr.reshape(n_groups, n_state), dli.reshape(n_groups, n_state),
             _blockdiag_in_grad(dbr, n_groups, n_state, grp), _blockdiag_in_grad(dbi, n_groups, n_state, grp)))
        sg = dict(ln1_g=dg1, ln1_b=db1, ln2_g=dg2, ln2_b=db2, ln3_g=dg3, ln3_b=db3, conv_w=dconv,
                  ssm_a_re=da_re, ssm_a_im=da_im, ssm_log_dt=dlog_dt, ssm_b_re=db_re, ssm_b_im=db_im,
                  ssm_c_re=_blockdiag_out_grad(dcr, n_groups, grp, n_state),
                  ssm_c_im=_blockdiag_out_grad(dci, n_groups, grp, n_state), ssm_d=dd, glu_b=dglu_b,
                  gdn_a_log=dalog[0, :n_heads], gdn_dt_bias=ddtb[0, :n_heads], gdn_norm_w=dnw)
        for n in SMALL:
            small_grads[n][l] = sg[n].reshape(-1)
        layer_grads = dict(ffn1_w_gu=dwgu1, ffn1_w_down=dwd1, w_in=_wcat_grad_to_shards(dwcat, d, w, n_heads),
                           glu_w=dwglu, w_br_ssm=dwbs, w_br_gdn=dwbd, w_out=dwout, ffn2_w_gu=dwgu2, ffn2_w_down=dwd2)
        if pending is not None:
            finish_reduction(pending[0], pending[1], dy)
        grads_l = [layer_grads[n] for n in BIG]
        part = add_pairs(grads_l, pair_split(grads_l, name="rs_big_split"), BF16, name="rs_big_add2")
        pending = (l, xchg_start(part, name=f"xchg_start_l{l}"))
    grad_x = dy[None]

    small_list = [jnp.stack(small_grads[n]) for n in SMALL]
    packed = _pack(small_list, 16 * N_CHIPS).reshape(N_CHIPS, -1, 128)
    red = reduce_scatter([packed], F32, name="rs_small")
    full = gather_chips(red, [True], name="gather_small")[0].reshape(-1)
    finish_reduction(pending[0], pending[1], full)
    small_red = dict(zip(SMALL, _unpack(full, small_list)))
    cw_cols = conv_w.shape[-1]
    dconv_full = small_red['conv_w'].reshape(depth, CONV_K, N_CHIPS, cw_cols)
    small_red['conv_w'] = lax.dynamic_index_in_dim(dconv_full, chip, axis=2, keepdims=False)

    grads = {}
    for n in BIG:
        grads[n] = jnp.stack(big_grads[n]).reshape(wts[n].shape)
    for n in SMALL:
        grads[n] = small_red[n].reshape(wts[n].shape)

    delta, new_m, new_v = {}, {}, {}
    for n in BIG:
        delta[n], new_m[n], new_v[n] = adamw(wts[n], grads[n], mom[n], var[n], name="adamw_big")
    pk = lambda dct: _pack([dct[n] for n in SMALL], 1024).reshape(-1, 128)
    sd, sm, sv = adamw(pk(wts), pk(grads), pk(mom), pk(var), name="adamw_small")
    like = [wts[n] for n in SMALL]
    for n, a, b, c in zip(SMALL, _unpack(sd.reshape(-1), like), _unpack(sm.reshape(-1), like),
                          _unpack(sv.reshape(-1), like)):
        delta[n], new_m[n], new_v[n] = a, b, c

    return (loss, grad_x, *[grads[n] for n in WEIGHT_NAMES], *[delta[n] for n in WEIGHT_NAMES],
            *[new_m[n] for n in WEIGHT_NAMES], *[new_v[n] for n in WEIGHT_NAMES])
```

```python
import math

import jax
import jax.numpy as jnp
from jax import lax
from jax.experimental import pallas as pl
from jax.experimental.pallas import tpu as pltpu

F32 = jnp.float32
BF16 = jnp.bfloat16
HI = lax.Precision.HIGHEST
MESH = pl.DeviceIdType.MESH

N_CHIPS = 4
SSM_GROUP = 16
SSM_STATE = 64
GROUPS_PER_TILE = 8
HEAD_DIM = 128
CHUNK = 64
CONV_K = 4
LN_EPS = 1e-5
RMS_EPS = 1e-6
L2_EPS = 1e-6
SMALL_W = 512
ADAM_LR, ADAM_B1, ADAM_B2, ADAM_EPS, ADAM_WD, ADAM_STEP = 0.001, 0.9, 0.999, 1e-08, 0.01, 10
VMEM_LIMIT = 56 * 1024 * 1024

WEIGHT_NAMES = ['ffn1_w_gu', 'ffn1_w_down', 'ln1_g', 'ln1_b', 'w_in', 'conv_w', 'ssm_a_re', 'ssm_a_im', 'ssm_log_dt',
                'ssm_b_re', 'ssm_b_im', 'ssm_c_re', 'ssm_c_im', 'ssm_d', 'glu_w', 'glu_b', 'gdn_a_log', 'gdn_dt_bias',
                'gdn_norm_w', 'w_br_ssm', 'w_br_gdn', 'w_out', 'ln2_g', 'ln2_b', 'ffn2_w_gu', 'ffn2_w_down', 'ln3_g',
                'ln3_b']
BIG = ['ffn1_w_gu', 'ffn1_w_down', 'w_in', 'glu_w', 'w_br_ssm', 'w_br_gdn', 'w_out', 'ffn2_w_gu', 'ffn2_w_down']
SMALL = [n for n in WEIGHT_NAMES if n not in BIG]


def _tile(dim, prefs):
    for p in prefs:
        if dim % p == 0:
            return p
    return dim


def _cparams(sem):
    return pltpu.CompilerParams(dimension_semantics=sem, vmem_limit_bytes=VMEM_LIMIT)


def _ln(r, g, b):
    mu = jnp.mean(r, axis=-1, keepdims=True)
    xc = r - mu
    var = jnp.mean(xc * xc, axis=-1, keepdims=True)
    return xc * lax.rsqrt(var + LN_EPS) * g + b


def _lshape(x, nb):
    return (x.shape[0], x.shape[1]) if nb == 1 else (x.shape[1], x.shape[2] * nb)


def _cb_spec(x, nb, tr, tc, rc):
    if nb == 1:
        return pl.BlockSpec((tr, tc), lambda *g: rc(*g))
    cps = x.shape[2] // tc

    def imap(*g):
        r, c = rc(*g)
        return (c // cps, r, c % cps)
    return pl.BlockSpec((None, tr, tc), imap)


MM_VMEM_BUDGET = 40 * 1024 * 1024
MM_TILES = (2048, 1024, 512, 256, 128)
MM_FULL_K = 2048


def mm(a, b, *, name, ta=False, tb=False, a_nb=1, b_nb=1, out_nb=1, out_dtype=F32, add=None, add_scale=1.0,
       out_scale=1.0):
    ar, ac = _lshape(a, a_nb)
    br, bc = _lshape(b, b_nb)
    m, k = (ac, ar) if ta else (ar, ac)
    k2, n = (bc, br) if tb else (br, bc)
    assert k == k2, (name, a.shape, b.shape)
    assert a.dtype == BF16 and b.dtype == BF16, name

    def lim(dim, *nbs):
        q = dim
        for nb in nbs:
            q = math.gcd(q, dim // nb)
        return q
    lm = lim(m, a_nb if ta else 1)
    ln = lim(n, out_nb, 1 if tb else b_nb)
    lk = lim(k, 1 if ta else a_nb, b_nb if tb else 1)
    so = jnp.dtype(out_dtype).itemsize
    if k <= MM_FULL_K and lk == k:
        tks = [k]
    else:
        tks = [t for t in range(MM_FULL_K, 127, -128) if lk % t == 0]
    best = None
    for ck in tks:
        for cm in MM_TILES:
            for cn in MM_TILES:
                if lm % cm or ln % cn:
                    continue
                est = 2 * (cm * ck * 2 + ck * cn * 2 + cm * cn * so + (cm * cn * 4 if add is not None else 0))
                est += cm * cn * 4 * (2 if k > ck else 1) + (cm * ck * 2 + 512 * cm * 4 if ta else 0)
                score = min(cm, 512) * cn * ck
                if est <= MM_VMEM_BUDGET and (best is None or score > best[0]):
                    best = (score, cm, cn, ck)
    _, tm, tn, tk = best
    nk = k // tk
    dn = (((1,), (1 if tb else 0,)), ((), ()))

    def body(*refs):
        a_ref, b_ref = refs[:2]
        add_ref = refs[2] if add is not None else None
        o_ref = refs[3 if add is not None else 2]
        scratch = refs[(4 if add is not None else 3):]
        acc = scratch[0] if nk > 1 else None
        kk = pl.program_id(2)

        if ta:
            at_ref = scratch[-1]

            def transpose_block():
                for r0 in range(0, tk, 512):
                    r1 = min(tk, r0 + 512)
                    at_ref[:, r0:r1] = a_ref[r0:r1, :].astype(F32).T.astype(BF16)
            if nk == 1:
                pl.when(pl.program_id(1) == 0)(transpose_block)
            else:
                transpose_block()
            av = at_ref[...]
        else:
            av = a_ref[...]
        part = lax.dot_general(av, b_ref[...], dn, preferred_element_type=F32)

        def finish(r):
            if out_scale != 1.0:
                r = r * out_scale
            if add is not None:
                r = r + add_scale * add_ref[...]
            o_ref[...] = r.astype(out_dtype)

        if nk == 1:
            finish(part)
        else:
            @pl.when(kk == 0)
            def _():
                acc[...] = part

            @pl.when(kk > 0)
            def _():
                acc[...] += part

            @pl.when(kk == nk - 1)
            def _():
                finish(acc[...])

    if ta:
        a_spec = _cb_spec(a, a_nb, tk, tm, lambda i, j, kk: (kk, i))
    else:
        a_spec = _cb_spec(a, a_nb, tm, tk, lambda i, j, kk: (i, kk))
    if tb:
        b_spec = _cb_spec(b, b_nb, tn, tk, lambda i, j, kk: (j, kk))
    else:
        b_spec = _cb_spec(b, b_nb, tk, tn, lambda i, j, kk: (kk, j))
    if out_nb == 1:
        out_shape = jax.ShapeDtypeStruct((m, n), out_dtype)
        out_spec = pl.BlockSpec((tm, tn), lambda i, j, kk: (i, j))
    else:
        out_shape = jax.ShapeDtypeStruct((out_nb, m, n // out_nb), out_dtype)
        out_spec = _cb_spec(out_shape, out_nb, tm, tn, lambda i, j, kk: (i, j))
    in_specs = [a_spec, b_spec]
    args = [a, b]
    if add is not None:
        in_specs.append(pl.BlockSpec((tm, tn), lambda i, j, kk: (i, j)))
        args.append(add)
    scratch = ([pltpu.VMEM((tm, tn), F32)] if nk > 1 else []) + ([pltpu.VMEM((tm, tk), BF16)] if ta else [])
    return pl.pallas_call(
        body, name=name, out_shape=out_shape, grid=(m // tm, n // tn, nk), in_specs=in_specs, out_specs=out_spec,
        scratch_shapes=scratch, compiler_params=_cparams(("parallel", "arbitrary", "arbitrary")))(*args)


def down_res_ln(src, w, x, g, b, *, swiglu, alpha, scale, name):
    n_tok, d = x.shape
    kdim = w.shape[0]
    tm = _tile(n_tok, (512, 256, 128))
    tk = _tile(kdim, (512, 256, 128))
    nk = kdim // tk

    def body(s_ref, w_ref, x_ref, g_ref, b_ref, r_ref, y_ref, yb_ref, acc):
        kk = pl.program_id(1)

        @pl.when(kk == 0)
        def _():
            acc[...] = jnp.zeros_like(acc)

        if swiglu:
            gate = s_ref[0]
            a = gate * jax.nn.sigmoid(gate) * s_ref[1]
        else:
            a = s_ref[...]
        acc[...] += jnp.dot(a.astype(BF16), w_ref[...], preferred_element_type=F32)

        @pl.when(kk == nk - 1)
        def _():
            r = alpha * x_ref[...] + scale * acc[...]
            r_ref[...] = r
            y = _ln(r, g_ref[...], b_ref[...])
            y_ref[...] = y
            yb_ref[...] = y.astype(BF16)

    if swiglu:
        s_spec = pl.BlockSpec((2, tm, tk), lambda i, kk: (0, i, kk))
    else:
        s_spec = pl.BlockSpec((tm, tk), lambda i, kk: (i, kk))
    row = pl.BlockSpec((tm, d), lambda i, kk: (i, 0))
    vec = pl.BlockSpec((1, d), lambda i, kk: (0, 0))
    return pl.pallas_call(
        body, name=name, out_shape=[jax.ShapeDtypeStruct((n_tok, d), F32)] * 2 + [jax.ShapeDtypeStruct((n_tok, d), BF16)],
        grid=(n_tok // tm, nk),
        in_specs=[s_spec, pl.BlockSpec((tk, d), lambda i, kk: (kk, 0)), row, vec, vec], out_specs=[row, row, row],
        scratch_shapes=[pltpu.VMEM((tm, d), F32)], compiler_params=_cparams(("parallel", "arbitrary")))(src, w, x, g, b)


def ln_bwd(r, g, dy, *, name):
    n_tok, d = r.shape
    tm = _tile(n_tok, (256, 128))

    def body(r_ref, g_ref, dy_ref, dr_ref, drb_ref, dg_ref, db_ref):
        i = pl.program_id(0)

        @pl.when(i == 0)
        def _():
            dg_ref[...] = jnp.zeros_like(dg_ref)
            db_ref[...] = jnp.zeros_like(db_ref)

        rv = r_ref[...]
        dyv = dy_ref[...]
        mu = jnp.mean(rv, axis=-1, keepdims=True)
        xc = rv - mu
        rstd = lax.rsqrt(jnp.mean(xc * xc, axis=-1, keepdims=True) + LN_EPS)
        xh = xc * rstd
        dxh = dyv * g_ref[...]
        dr = rstd * (dxh - jnp.mean(dxh, axis=-1, keepdims=True) - xh * jnp.mean(dxh * xh, axis=-1, keepdims=True))
        dr_ref[...] = dr
        drb_ref[...] = dr.astype(BF16)
        dg_ref[...] += jnp.sum(dyv * xh, axis=0, keepdims=True)
        db_ref[...] += jnp.sum(dyv, axis=0, keepdims=True)

    row = pl.BlockSpec((tm, d), lambda i: (i, 0))
    vec = pl.BlockSpec((1, d), lambda i: (0, 0))
    return pl.pallas_call(
        body, name=name, out_shape=[jax.ShapeDtypeStruct((n_tok, d), F32), jax.ShapeDtypeStruct((n_tok, d), BF16),
                                    jax.ShapeDtypeStruct((1, d), F32), jax.ShapeDtypeStruct((1, d), F32)],
        grid=(n_tok // tm,), in_specs=[row, vec, row], out_specs=[row, row, vec, vec],
        compiler_params=_cparams(("arbitrary",)))(r, g, dy)


def loss_head(y, target, *, name):
    n_tok, d = y.shape
    tm = _tile(n_tok, (256, 128))

    def body(y_ref, t_ref, dy_ref, l_ref):
        i = pl.program_id(0)

        @pl.when(i == 0)
        def _():
            l_ref[...] = jnp.zeros_like(l_ref)

        e = y_ref[...] - t_ref[...]
        dy_ref[...] = e * (1.0 / d)
        s = jnp.sum(jnp.mean(e * e, axis=-1, keepdims=True), axis=0, keepdims=True)
        l_ref[...] += 0.5 * s

    row = pl.BlockSpec((tm, d), lambda i: (i, 0))
    return pl.pallas_call(
        body, name=name, out_shape=[jax.ShapeDtypeStruct((n_tok, d), F32), jax.ShapeDtypeStruct((8, 128), F32)],
        grid=(n_tok // tm,), in_specs=[row, row], out_specs=[row, pl.BlockSpec((8, 128), lambda i: (0, 0))],
        compiler_params=_cparams(("arbitrary",)))(y, target)


def ffn_bwd_mid(dr, wd, h, *, scale, name):
    n_tok, d = dr.shape
    f = wd.shape[0]
    tm = _tile(n_tok, (512, 256, 128))
    tf = _tile(f, (512, 256, 128))

    def body(dr_ref, w_ref, h_ref, dh_ref, a_ref):
        dy = (scale * dr_ref[...]).astype(BF16)
        da = lax.dot_general(dy, w_ref[...], (((1,), (1,)), ((), ())), preferred_element_type=F32)
        gate = h_ref[0]
        up = h_ref[1]
        sg = jax.nn.sigmoid(gate)
        s = gate * sg
        a_ref[...] = (s * up).astype(BF16)
        dh_ref[0] = (da * up * (sg * (1.0 + gate * (1.0 - sg)))).astype(BF16)
        dh_ref[1] = (da * s).astype(BF16)

    return pl.pallas_call(
        body, name=name, out_shape=[jax.ShapeDtypeStruct((2, n_tok, f), BF16), jax.ShapeDtypeStruct((n_tok, f), BF16)],
        grid=(n_tok // tm, f // tf),
        in_specs=[pl.BlockSpec((tm, d), lambda i, j: (i, 0)), pl.BlockSpec((tf, d), lambda i, j: (j, 0)),
                  pl.BlockSpec((2, tm, tf), lambda i, j: (0, i, j))],
        out_specs=[pl.BlockSpec((2, tm, tf), lambda i, j: (0, i, j)), pl.BlockSpec((tm, tf), lambda i, j: (i, j))],
        compiler_params=_cparams(("parallel", "parallel")))(dr, wd, h)


def _cmul(ar, ai, br, bi):
    return ar * br - ai * bi, ar * bi + ai * br


def _scan_blocks(sr_ref, si_ref, lr, li, *, reverse):
    n_rows, width = sr_ref.shape
    n_blk = n_rows // 8
    row = lax.broadcasted_iota(jnp.int32, (8, width), 0)
    pr = jnp.broadcast_to(lr, (8, width))
    pi = jnp.broadcast_to(-li if reverse else li, (8, width))

    def shifted(v, dist, fill=0.0):
        if reverse:
            return jnp.where(row < 8 - dist, pltpu.roll(v, 8 - dist, 0), fill)
        return jnp.where(row >= dist, pltpu.roll(v, dist, 0), fill)

    p1 = (pr, pi)
    p2 = _cmul(*p1, *p1)
    p4 = _cmul(*p2, *p2)
    wr, wi = pr, pi
    for dist in (1, 2, 4):
        wr, wi = _cmul(wr, wi, shifted(wr, dist, 1.0), shifted(wi, dist, 0.0))
    edge = 0 if reverse else 7

    def step(i, carry):
        cr, ci = carry
        blk = (n_blk - 1 - i) if reverse else i
        r0 = pl.multiple_of(blk * 8, 8)
        xr = sr_ref[pl.ds(r0, 8), :]
        xi = si_ref[pl.ds(r0, 8), :]
        for dist, (qr, qi) in ((1, p1), (2, p2), (4, p4)):
            tr, ti = _cmul(qr, qi, shifted(xr, dist), shifted(xi, dist))
            xr, xi = xr + tr, xi + ti
        tr, ti = _cmul(wr, wi, cr, ci)
        xr, xi = xr + tr, xi + ti
        sr_ref[pl.ds(r0, 8), :] = xr
        si_ref[pl.ds(r0, 8), :] = xi
        br = jnp.where(row == edge, xr, 0.0)
        bi = jnp.where(row == edge, xi, 0.0)
        for dist in (1, 2, 4):
            br = br + pltpu.roll(br, dist, 0)
            bi = bi + pltpu.roll(bi, dist, 0)
        return br, bi

    zero = jnp.zeros((8, width), F32)
    lax.fori_loop(0, n_blk, step, (zero, zero), unroll=2)


def _s5_specs(n_tok, u_blk0):
    gw = GROUPS_PER_TILE * SSM_GROUP
    sw = GROUPS_PER_TILE * SSM_STATE
    u_spec = pl.BlockSpec((n_tok, gw), lambda t: (0, u_blk0 + t))
    col = pl.BlockSpec((n_tok, gw), lambda t: (0, t))
    bmat = pl.BlockSpec((None, gw, sw), lambda t: (t, 0, 0))
    cmat = pl.BlockSpec((None, sw, gw), lambda t: (t, 0, 0))
    lvec = pl.BlockSpec((1, sw), lambda t: (0, t))
    dvec = pl.BlockSpec((1, gw), lambda t: (0, t))
    return gw, sw, u_spec, col, bmat, cmat, lvec, dvec


def s5_fwd(proj, u_col0, bblk_r, bblk_i, cblk_r, cblk_i, lbar_r, lbar_i, dskip, *, name):
    n_tok = proj.shape[0]
    n_tiles = bblk_r.shape[0]
    gw, sw, u_spec, col, bmat, cmat, lvec, dvec = _s5_specs(n_tok, u_col0 // (GROUPS_PER_TILE * SSM_GROUP))

    def body(u_ref, br_ref, bi_ref, cr_ref, ci_ref, lr_ref, li_ref, d_ref, ypre_ref, y2_ref, y2b_ref, sr, si):
        u = u_ref[...]
        ub = u.astype(BF16)
        sr[...] = jnp.dot(ub, br_ref[...].astype(BF16), preferred_element_type=F32)
        si[...] = jnp.dot(ub, bi_ref[...].astype(BF16), preferred_element_type=F32)
        _scan_blocks(sr, si, lr_ref[...], li_ref[...], reverse=False)
        y = (jnp.dot(sr[...].astype(BF16), cr_ref[...].astype(BF16), preferred_element_type=F32)
             - jnp.dot(si[...].astype(BF16), ci_ref[...].astype(BF16), preferred_element_type=F32)
             + d_ref[...] * u)
        ypre_ref[...] = y
        y2 = jax.nn.gelu(y)
        y2_ref[...] = y2
        y2b_ref[...] = y2.astype(BF16)

    width = n_tiles * gw
    return pl.pallas_call(
        body, name=name,
        out_shape=[jax.ShapeDtypeStruct((n_tok, width), F32)] * 2 + [jax.ShapeDtypeStruct((n_tok, width), BF16)],
        grid=(n_tiles,), in_specs=[u_spec, bmat, bmat, cmat, cmat, lvec, lvec, dvec], out_specs=[col, col, col],
        scratch_shapes=[pltpu.VMEM((n_tok, sw), F32)] * 2,
        compiler_params=_cparams(("parallel",)))(proj, bblk_r, bblk_i, cblk_r, cblk_i, lbar_r, lbar_i, dskip)


def s5_bwd(proj, u_col0, ypre, dy2, bblk_r, bblk_i, cblk_r, cblk_i, lbar_r, lbar_i, dskip, *, name):
    n_tok = proj.shape[0]
    n_tiles = bblk_r.shape[0]
    gw, sw, u_spec, col, bmat, cmat, lvec, dvec = _s5_specs(n_tok, u_col0 // (GROUPS_PER_TILE * SSM_GROUP))
    rb = _tile(n_tok, (512, 256, 128))
    tn_dims = (((0,), (0,)), ((), ()))
    nt_dims = (((1,), (1,)), ((), ()))

    def body(u_ref, ypre_ref, dy2_ref, br_ref, bi_ref, cr_ref, ci_ref, lr_ref, li_ref, d_ref,
             du_ref, dbr_ref, dbi_ref, dcr_ref, dci_ref, dlr_ref, dli_ref, dd_ref, sr, si, gr, gi):
        u = u_ref[...]
        ub = u.astype(BF16)
        bmr = br_ref[...].astype(BF16)
        bmi = bi_ref[...].astype(BF16)
        cmr = cr_ref[...].astype(BF16)
        cmi = ci_ref[...].astype(BF16)
        lr = lr_ref[...]
        li = li_ref[...]
        _, gelu_vjp = jax.vjp(jax.nn.gelu, ypre_ref[...])
        dyp = gelu_vjp(dy2_ref[...])[0]
        dyb = dyp.astype(BF16)
        sr[...] = jnp.dot(ub, bmr, preferred_element_type=F32)
        si[...] = jnp.dot(ub, bmi, preferred_element_type=F32)
        _scan_blocks(sr, si, lr, li, reverse=False)
        gr[...] = lax.dot_general(dyb, cmr, nt_dims, preferred_element_type=F32)
        gi[...] = -lax.dot_general(dyb, cmi, nt_dims, preferred_element_type=F32)
        _scan_blocks(gr, gi, lr, li, reverse=True)
        srb = sr[...].astype(BF16)
        sib = si[...].astype(BF16)
        dcr_ref[...] = lax.dot_general(srb, dyb, tn_dims, preferred_element_type=F32)
        dci_ref[...] = -lax.dot_general(sib, dyb, tn_dims, preferred_element_type=F32)
        grb = gr[...].astype(BF16)
        gib = gi[...].astype(BF16)
        dbr_ref[...] = lax.dot_general(ub, grb, tn_dims, preferred_element_type=F32)
        dbi_ref[...] = lax.dot_general(ub, gib, tn_dims, preferred_element_type=F32)
        du_ref[...] = (lax.dot_general(grb, bmr, nt_dims, preferred_element_type=F32)
                       + lax.dot_general(gib, bmi, nt_dims, preferred_element_type=F32)
                       + d_ref[...] * dyp).astype(du_ref.dtype)
        dd_ref[...] = jnp.sum(dyp * u, axis=0, keepdims=True)
        inv = 1.0 / (lr * lr + li * li)
        qr = lr * inv
        qi = -li * inv
        acc_r = jnp.zeros((1, sw), F32)
        acc_i = jnp.zeros((1, sw), F32)
        for blk in range(n_tok // rb):
            rows = pl.ds(blk * rb, rb)
            ubb = u_ref[rows, :].astype(BF16)
            er = sr[rows, :] - jnp.dot(ubb, bmr, preferred_element_type=F32)
            ei = si[rows, :] - jnp.dot(ubb, bmi, preferred_element_type=F32)
            pr, pi = _cmul(er, ei, qr, qi)
            ar = gr[rows, :]
            ai = gi[rows, :]
            acc_r = acc_r + jnp.sum(ar * pr + ai * pi, axis=0, keepdims=True)
            acc_i = acc_i + jnp.sum(ai * pr - ar * pi, axis=0, keepdims=True)
        dlr_ref[...] = acc_r
        dli_ref[...] = acc_i

    width = n_tiles * gw
    out_shape = [jax.ShapeDtypeStruct((n_tok, width), BF16),
                 jax.ShapeDtypeStruct(bblk_r.shape, F32), jax.ShapeDtypeStruct(bblk_r.shape, F32),
                 jax.ShapeDtypeStruct(cblk_r.shape, F32), jax.ShapeDtypeStruct(cblk_r.shape, F32),
                 jax.ShapeDtypeStruct(lbar_r.shape, F32), jax.ShapeDtypeStruct(lbar_r.shape, F32),
                 jax.ShapeDtypeStruct(dskip.shape, F32)]
    return pl.pallas_call(
        body, name=name, out_shape=out_shape, grid=(n_tiles,),
        in_specs=[u_spec, col, col, bmat, bmat, cmat, cmat, lvec, lvec, dvec],
        out_specs=[col, bmat, bmat, cmat, cmat, lvec, lvec, dvec],
        scratch_shapes=[pltpu.VMEM((n_tok, sw), F32)] * 4,
        compiler_params=_cparams(("parallel",)))(proj, ypre, dy2, bblk_r, bblk_i, cblk_r, cblk_i, lbar_r, lbar_i, dskip)


CONV_ROWS = 256
CONV_COLS = 512


def _conv_pre(x_ref, w_ref, blk, n_blk):
    r0 = blk * CONV_ROWS
    if blk == 0:
        ext = jnp.concatenate([jnp.zeros((8, CONV_COLS), F32), x_ref[0:CONV_ROWS, :]], axis=0)
    else:
        ext = x_ref[r0 - 8:r0 + CONV_ROWS, :]
    taps = []
    c = None
    for j in range(CONV_K):
        s = CONV_K - 1 - j
        xs = ext[8:] if s == 0 else pltpu.roll(ext, s, 0)[8:]
        taps.append(xs)
        term = w_ref[j:j + 1, :] * xs
        c = term if c is None else c + term
    return c, taps


def conv_fwd(proj, col0, conv_w, *, name):
    n_tok = proj.shape[0]
    width = conv_w.shape[1]
    n_blk = n_tok // CONV_ROWS
    cb0 = col0 // CONV_COLS

    def body(x_ref, w_ref, o_ref):
        for blk in range(n_blk):
            c, _ = _conv_pre(x_ref, w_ref, blk, n_blk)
            o_ref[blk * CONV_ROWS:(blk + 1) * CONV_ROWS, :] = c * jax.nn.sigmoid(c)

    return pl.pallas_call(
        body, name=name, out_shape=jax.ShapeDtypeStruct((n_tok, width), F32), grid=(width // CONV_COLS,),
        in_specs=[pl.BlockSpec((n_tok, CONV_COLS), lambda j: (0, cb0 + j)),
                  pl.BlockSpec((CONV_K, CONV_COLS), lambda j: (0, j))],
        out_specs=pl.BlockSpec((n_tok, CONV_COLS), lambda j: (0, j)),
        compiler_params=_cparams(("parallel",)))(proj, conv_w)


def conv_bwd(proj, col0, conv_w, dout, *, name):
    n_tok = proj.shape[0]
    width = conv_w.shape[1]
    n_blk = n_tok // CONV_ROWS
    cb0 = col0 // CONV_COLS

    def body(x_ref, w_ref, do_ref, dx_ref, dw_ref, dc):
        dws = [jnp.zeros((1, CONV_COLS), F32) for _ in range(CONV_K)]
        for blk in range(n_blk):
            rows = slice(blk * CONV_ROWS, (blk + 1) * CONV_ROWS)
            c, taps = _conv_pre(x_ref, w_ref, blk, n_blk)
            sg = jax.nn.sigmoid(c)
            dcv = do_ref[rows, :] * (sg * (1.0 + c * (1.0 - sg)))
            dc[rows, :] = dcv
            for j in range(CONV_K):
                dws[j] = dws[j] + jnp.sum(dcv * taps[j], axis=0, keepdims=True)
        dc[n_tok:n_tok + 8, :] = jnp.zeros((8, CONV_COLS), F32)
        for j in range(CONV_K):
            dw_ref[j:j + 1, :] = dws[j]
        for blk in range(n_blk):
            r0 = blk * CONV_ROWS
            ext = dc[r0:r0 + CONV_ROWS + 8, :]
            dx = None
            for j in range(CONV_K):
                s = CONV_K - 1 - j
                sh = ext[:CONV_ROWS] if s == 0 else pltpu.roll(ext, CONV_ROWS + 8 - s, 0)[:CONV_ROWS]
                term = w_ref[j:j + 1, :] * sh
                dx = term if dx is None else dx + term
            dx_ref[r0:r0 + CONV_ROWS, :] = dx.astype(dx_ref.dtype)

    return pl.pallas_call(
        body, name=name, out_shape=[jax.ShapeDtypeStruct((n_tok, width), BF16), jax.ShapeDtypeStruct(conv_w.shape, F32)],
        grid=(width // CONV_COLS,),
        in_specs=[pl.BlockSpec((n_tok, CONV_COLS), lambda j: (0, cb0 + j)),
                  pl.BlockSpec((CONV_K, CONV_COLS), lambda j: (0, j)),
                  pl.BlockSpec((n_tok, CONV_COLS), lambda j: (0, j))],
        out_specs=[pl.BlockSpec((n_tok, CONV_COLS), lambda j: (0, j)), pl.BlockSpec((CONV_K, CONV_COLS), lambda j: (0, j))],
        scratch_shapes=[pltpu.VMEM((n_tok + 8, CONV_COLS), F32)],
        compiler_params=_cparams(("parallel",)))(proj, conv_w, dout)


GDN_PREC = lax.Precision.HIGH


def _neumann_inverse(lowers):
    n = lowers[0].shape[0]
    eye = (lax.broadcasted_iota(jnp.int32, (n, n), 0) == lax.broadcasted_iota(jnp.int32, (n, n), 1)).astype(F32)
    xs = [-l for l in lowers]
    ts = [eye + x for x in xs]
    power = 2
    while power < n:
        xs = [jnp.dot(x, x, precision=GDN_PREC, preferred_element_type=F32) for x in xs]
        ts = [t + jnp.dot(t, x, precision=GDN_PREC, preferred_element_type=F32) for t, x in zip(ts, xs)]
        power *= 2
    return tuple(ts)


@jax.custom_vjp
def _unit_lower_inverse(lowers):
    return _neumann_inverse(lowers)


def _unit_lower_inverse_fwd(lowers):
    ts = _neumann_inverse(lowers)
    return ts, ts


def _unit_lower_inverse_bwd(ts, cts):
    tn = (((0,), (0,)), ((), ()))
    nt = (((1,), (1,)), ((), ()))
    lefts = [lax.dot_general(t, ct, tn, precision=GDN_PREC, preferred_element_type=F32) for t, ct in zip(ts, cts)]
    return (tuple(-lax.dot_general(l, t, nt, precision=GDN_PREC, preferred_element_type=F32)
                  for l, t in zip(lefts, ts)),)


_unit_lower_inverse.defvjp(_unit_lower_inverse_fwd, _unit_lower_inverse_bwd)


def _gdn_head(head, n_heads, state, q, k, v, z, bsmall, alog_row, dtb_row, nw):
    c = CHUNK
    lane = lax.broadcasted_iota(jnp.int32, (c, HEAD_DIM), 1)
    lane1 = lax.broadcasted_iota(jnp.int32, (1, HEAD_DIM), 1)
    ri = lax.broadcasted_iota(jnp.int32, (c, c), 0)
    ci = lax.broadcasted_iota(jnp.int32, (c, c), 1)
    causal = ri >= ci
    strict = ri > ci
    tril = causal.astype(F32)
    bl = jnp.sum(jnp.where(lane == head, bsmall, 0.0), axis=-1, keepdims=True)
    al = jnp.sum(jnp.where(lane == n_heads + head, bsmall, 0.0), axis=-1, keepdims=True)
    alog = jnp.sum(jnp.where(lane1 == head, alog_row, 0.0), axis=-1, keepdims=True)
    dtb = jnp.sum(jnp.where(lane1 == head, dtb_row, 0.0), axis=-1, keepdims=True)

    qn = q * lax.rsqrt(jnp.sum(q * q, axis=-1, keepdims=True) + L2_EPS) * (HEAD_DIM ** -0.5)
    kn = k * lax.rsqrt(jnp.sum(k * k, axis=-1, keepdims=True) + L2_EPS)
    beta = jax.nn.sigmoid(bl)
    xg = al + dtb
    g = -jnp.exp(alog) * (jnp.maximum(xg, 0.0) + jnp.log(1.0 + jnp.exp(-jnp.abs(xg))))
    g_wide = jnp.broadcast_to(g, (c, HEAD_DIM))
    yield None
    gc = jnp.dot(tril, g_wide, precision=HI, preferred_element_type=F32)
    yield None
    gc_rows = jnp.broadcast_to(jnp.mean(gc, axis=-1, keepdims=True), (c, c))
    gc_cols = gc.T[:c, :]
    g_tot = jnp.sum(g, axis=0, keepdims=True)
    decay = jnp.exp(jnp.where(causal, gc_rows - gc_cols, -1e30))
    egc = jnp.exp(gc)
    kb = kn * beta
    knb = kn.astype(BF16)
    nt = (((1,), (1,)), ((), ()))
    yield None
    lower = jnp.where(strict, lax.dot_general(kb.astype(BF16), knb, nt, preferred_element_type=F32) * decay, 0.0)
    tinv = yield lower
    u_val = jnp.dot(tinv, v * beta, precision=GDN_PREC, preferred_element_type=F32)
    yield None
    w_key = jnp.dot(tinv, kb * egc, precision=GDN_PREC, preferred_element_type=F32)
    yield None
    attn = lax.dot_general(qn.astype(BF16), knb, nt, preferred_element_type=F32) * decay
    q_dec = qn * egc
    k_dec = kn * jnp.exp(g_tot - gc)
    sb = state.astype(BF16)
    yield None
    v_new = u_val - jnp.dot(w_key.astype(BF16), sb, preferred_element_type=F32)
    vnb = v_new.astype(BF16)
    yield None
    o = (jnp.dot(q_dec.astype(BF16), sb, preferred_element_type=F32)
         + jnp.dot(attn.astype(BF16), vnb, preferred_element_type=F32))
    yield None
    new_state = state * jnp.exp(g_tot) + lax.dot_general(k_dec.astype(BF16), vnb, (((0,), (0,)), ((), ())),
                                                         preferred_element_type=F32)
    yield None
    o = o * lax.rsqrt(jnp.mean(o * o, axis=-1, keepdims=True) + RMS_EPS) * nw
    o = o * (z * jax.nn.sigmoid(z))
    return o, new_state


def _gdn_step(n_heads, states, qs, ks, vs, zs, bsmall, alog_row, dtb_row, nw):
    gens = [_gdn_head(h, n_heads, states[h], qs[h], ks[h], vs[h], zs[h], bsmall, alog_row, dtb_row, nw)
            for h in range(n_heads)]
    lowers = [None] * n_heads
    while any(m is None for m in lowers):
        for h in range(n_heads):
            lowers[h] = next(gens[h])
    tinvs = _unit_lower_inverse(tuple(lowers))
    results = [None] * n_heads
    first = True
    while any(r is None for r in results):
        for h in range(n_heads):
            try:
                if first:
                    gens[h].send(tinvs[h])
                else:
                    next(gens[h])
            except StopIteration as stop:
                results[h] = stop.value
        first = False
    return tuple(r[0] for r in results), tuple(r[1] for r in results)


def _gdn_in_specs(n_heads, qkv_width_blocks, z_blk, small_blk, rev, n_chunks):
    w = n_heads * HEAD_DIM

    def cidx(i):
        return (n_chunks - 1 - i) if rev else i
    qs = pl.BlockSpec((CHUNK, w), lambda i: (cidx(i), 0))
    ks = pl.BlockSpec((CHUNK, w), lambda i: (cidx(i), 1))
    vs = pl.BlockSpec((CHUNK, w), lambda i: (cidx(i), 2))
    zs = pl.BlockSpec((CHUNK, w), lambda i: (cidx(i), z_blk))
    bs = pl.BlockSpec((CHUNK, HEAD_DIM), lambda i: (cidx(i), small_blk))
    pv = pl.BlockSpec((1, HEAD_DIM), lambda i: (0, 0))
    return cidx, qs, ks, vs, zs, bs, pv


def gdn_fwd(qkv, proj, z_col0, small_col0, alog_row, dtb_row, nw_row, n_heads, *, name):
    n_tok = qkv.shape[0]
    w = n_heads * HEAD_DIM
    n_chunks = n_tok // CHUNK
    cidx, qs, ks, vs, zs, bs, pv = _gdn_in_specs(n_heads, 3, z_col0 // w, small_col0 // HEAD_DIM, False, n_chunks)

    def body(q_ref, k_ref, v_ref, z_ref, b_ref, al_ref, dt_ref, nw_ref, o_ref, s_ref, state):
        @pl.when(pl.program_id(0) == 0)
        def _():
            state[...] = jnp.zeros_like(state)

        heads = range(n_heads)
        cols = [slice(h * HEAD_DIM, (h + 1) * HEAD_DIM) for h in heads]
        states = [state[h] for h in heads]
        for h in heads:
            s_ref[h] = states[h]
        outs, new_states = _gdn_step(n_heads, states, [q_ref[:, c] for c in cols], [k_ref[:, c] for c in cols],
                                     [v_ref[:, c] for c in cols], [z_ref[:, c] for c in cols], b_ref[...],
                                     al_ref[...], dt_ref[...], nw_ref[...])
        for h in heads:
            o_ref[:, cols[h]] = outs[h].astype(BF16)
            state[h] = new_states[h]

    return pl.pallas_call(
        body, name=name,
        out_shape=[jax.ShapeDtypeStruct((n_tok, w), BF16),
                   jax.ShapeDtypeStruct((n_chunks, n_heads, HEAD_DIM, HEAD_DIM), F32)],
        grid=(n_chunks,), in_specs=[qs, ks, vs, zs, bs, pv, pv, pv],
        out_specs=[pl.BlockSpec((CHUNK, w), lambda i: (i, 0)),
                   pl.BlockSpec((None, n_heads, HEAD_DIM, HEAD_DIM), lambda i: (i, 0, 0, 0))],
        scratch_shapes=[pltpu.VMEM((n_heads, HEAD_DIM, HEAD_DIM), F32)],
        compiler_params=_cparams(("arbitrary",)))(qkv, qkv, qkv, proj, proj, alog_row, dtb_row, nw_row)


def gdn_bwd(qkv, proj, z_col0, small_col0, alog_row, dtb_row, nw_row, states, dout, n_heads, *, name):
    n_tok = qkv.shape[0]
    w = n_heads * HEAD_DIM
    n_chunks = n_tok // CHUNK
    cidx, qs, ks, vs, zs, bs, pv = _gdn_in_specs(n_heads, 3, z_col0 // w, small_col0 // HEAD_DIM, True, n_chunks)

    def body(q_ref, k_ref, v_ref, z_ref, b_ref, al_ref, dt_ref, nw_ref, s_ref, do_ref,
             dqkv_ref, dz_ref, db_ref, dal_ref, ddt_ref, dnw_ref, dstate):
        @pl.when(pl.program_id(0) == 0)
        def _():
            dstate[...] = jnp.zeros_like(dstate)
            dal_ref[...] = jnp.zeros_like(dal_ref)
            ddt_ref[...] = jnp.zeros_like(ddt_ref)
            dnw_ref[...] = jnp.zeros_like(dnw_ref)

        heads = range(n_heads)
        cols = [slice(h * HEAD_DIM, (h + 1) * HEAD_DIM) for h in heads]

        def f(sts, q, k, v, z, bb, al, dt, nw):
            return _gdn_step(n_heads, sts, q, k, v, z, bb, al, dt, nw)
        _, vjp = jax.vjp(f, tuple(s_ref[h] for h in heads), tuple(q_ref[:, c] for c in cols),
                         tuple(k_ref[:, c] for c in cols), tuple(v_ref[:, c] for c in cols),
                         tuple(z_ref[:, c] for c in cols), b_ref[...], al_ref[...], dt_ref[...], nw_ref[...])
        dsts, dqs, dks, dvs, dzs, dbb, da, dd, dn = vjp((tuple(do_ref[:, c] for c in cols),
                                                         tuple(dstate[h] for h in heads)))
        for h in heads:
            dstate[h] = dsts[h]
            dqkv_ref[:, h * HEAD_DIM:(h + 1) * HEAD_DIM] = dqs[h]
            dqkv_ref[:, w + h * HEAD_DIM:w + (h + 1) * HEAD_DIM] = dks[h]
            dqkv_ref[:, 2 * w + h * HEAD_DIM:2 * w + (h + 1) * HEAD_DIM] = dvs[h]
            dz_ref[:, cols[h]] = dzs[h].astype(dz_ref.dtype)
        db_ref[...] = dbb.astype(db_ref.dtype)
        dal_ref[...] += da
        ddt_ref[...] += dd
        dnw_ref[...] += dn

    rowblk = pl.BlockSpec((CHUNK, w), lambda i: (cidx(i), 0))
    return pl.pallas_call(
        body, name=name,
        out_shape=[
            jax.ShapeDtypeStruct((n_tok, 3 * w), F32),
            jax.ShapeDtypeStruct((n_tok, w), BF16), jax.ShapeDtypeStruct((n_tok, HEAD_DIM), BF16),
            jax.ShapeDtypeStruct((1, HEAD_DIM), F32), jax.ShapeDtypeStruct((1, HEAD_DIM), F32),
            jax.ShapeDtypeStruct((1, HEAD_DIM), F32)],
        grid=(n_chunks,),
        in_specs=[qs, ks, vs, zs, bs, pv, pv, pv,
                  pl.BlockSpec((None, n_heads, HEAD_DIM, HEAD_DIM), lambda i: (cidx(i), 0, 0, 0)), rowblk],
        out_specs=[pl.BlockSpec((CHUNK, 3 * w), lambda i: (cidx(i), 0)), rowblk,
                   pl.BlockSpec((CHUNK, HEAD_DIM), lambda i: (cidx(i), 0)), pv, pv, pv],
        scratch_shapes=[pltpu.VMEM((n_heads, HEAD_DIM, HEAD_DIM), F32)],
        compiler_params=_cparams(("arbitrary",)))(qkv, qkv, qkv, proj, proj, alog_row, dtb_row, nw_row, states, dout)


def glu_gate_fwd(y2, gl, bias, *, name):
    n_tok, w = y2.shape
    tm = _tile(n_tok, (256, 128))

    def body(y_ref, g_ref, b_ref, o_ref):
        o_ref[...] = (y_ref[...] * jax.nn.sigmoid(g_ref[...] + b_ref[...])).astype(BF16)

    row = pl.BlockSpec((tm, w), lambda i: (i, 0))
    vec = pl.BlockSpec((1, w), lambda i: (0, 0))
    return pl.pallas_call(body, name=name, out_shape=jax.ShapeDtypeStruct((n_tok, w), BF16), grid=(n_tok // tm,),
                          in_specs=[row, row, vec], out_specs=row, compiler_params=_cparams(("parallel",)))(y2, gl, bias)


def glu_gate_bwd(y2, gl, bias, dys, *, name):
    n_tok, w = y2.shape
    tm = _tile(n_tok, (256, 128))

    def body(y_ref, g_ref, b_ref, d_ref, dy_ref, dg_ref, db_ref):
        @pl.when(pl.program_id(0) == 0)
        def _():
            db_ref[...] = jnp.zeros_like(db_ref)

        sg = jax.nn.sigmoid(g_ref[...] + b_ref[...])
        d = d_ref[...]
        dy_ref[...] = d * sg
        dg = d * y_ref[...] * sg * (1.0 - sg)
        dg_ref[...] = dg.astype(dg_ref.dtype)
        db_ref[...] += jnp.sum(dg, axis=0, keepdims=True)

    row = pl.BlockSpec((tm, w), lambda i: (i, 0))
    vec = pl.BlockSpec((1, w), lambda i: (0, 0))
    return pl.pallas_call(
        body, name=name, out_shape=[jax.ShapeDtypeStruct((n_tok, w), F32), jax.ShapeDtypeStruct((n_tok, w), BF16),
                                    jax.ShapeDtypeStruct((1, w), F32)],
        grid=(n_tok // tm,), in_specs=[row, row, vec, row], out_specs=[row, row, vec],
        compiler_params=_cparams(("arbitrary",)))(y2, gl, bias, dys)


def merge_fwd(proj, bs, bd, *, name):
    n_tok, d = bs.shape
    tm = _tile(n_tok, (256, 128))

    def body(gs_ref, gd_ref, bs_ref, bd_ref, o_ref):
        o_ref[...] = (jax.nn.sigmoid(gs_ref[...]) * bs_ref[...]
                      + jax.nn.sigmoid(gd_ref[...]) * bd_ref[...]).astype(BF16)

    row = pl.BlockSpec((tm, d), lambda i: (i, 0))
    return pl.pallas_call(
        body, name=name, out_shape=jax.ShapeDtypeStruct((n_tok, d), BF16), grid=(n_tok // tm,),
        in_specs=[row, pl.BlockSpec((tm, d), lambda i: (i, 1)), row, row], out_specs=row,
        compiler_params=_cparams(("parallel",)))(proj, proj, bs, bd)


def merge_bwd(proj, bs, bd, dm, *, name):
    n_tok, d = bs.shape
    tm = _tile(n_tok, (256, 128))

    def body(gs_ref, gd_ref, bs_ref, bd_ref, dm_ref, dbs_ref, dbd_ref, dgs_ref, dgd_ref):
        dmv = dm_ref[...]
        ss = jax.nn.sigmoid(gs_ref[...])
        sd = jax.nn.sigmoid(gd_ref[...])
        dbs_ref[...] = (ss * dmv).astype(BF16)
        dbd_ref[...] = (sd * dmv).astype(BF16)
        dgs_ref[...] = (dmv * bs_ref[...] * ss * (1.0 - ss)).astype(BF16)
        dgd_ref[...] = (dmv * bd_ref[...] * sd * (1.0 - sd)).astype(BF16)

    row = pl.BlockSpec((tm, d), lambda i: (i, 0))
    return pl.pallas_call(
        body, name=name, out_shape=[jax.ShapeDtypeStruct((n_tok, d), BF16)] * 4, grid=(n_tok // tm,),
        in_specs=[row, pl.BlockSpec((tm, d), lambda i: (i, 1)), row, row, row], out_specs=[row] * 4,
        compiler_params=_cparams(("parallel",)))(proj, proj, bs, bd, dm)


def add_pairs(grads, recv, out_dtype, *, name):
    core = jnp.reshape(lax.axis_index("c"), (1,)).astype(jnp.int32)
    outs = []
    for t, (a, b) in enumerate(zip(grads, recv)):
        n_sh, h, cols = b.shape
        tr = _tile(h, (256, 128, 64, 32, 16))
        nh = h // tr

        def body(c_ref, a_ref, b_ref, o_ref):
            o_ref[...] = (a_ref[...].astype(F32) + b_ref[...].astype(F32)).astype(out_dtype)

        grid_spec = pltpu.PrefetchScalarGridSpec(
            num_scalar_prefetch=1, grid=(n_sh, nh),
            in_specs=[pl.BlockSpec((None, tr, cols), lambda s, i, c_ref, nh=nh: (s, c_ref[0] * nh + i, 0)),
                      pl.BlockSpec((None, tr, cols), lambda s, i, c_ref: (s, i, 0))],
            out_specs=pl.BlockSpec((None, tr, cols), lambda s, i, c_ref: (s, i, 0)))
        outs.append(pl.pallas_call(body, name=f"{name}_{t}", out_shape=jax.ShapeDtypeStruct(b.shape, out_dtype),
                                   grid_spec=grid_spec, compiler_params=_cparams(("parallel", "parallel")))(core, a, b))
    return outs


def add_chips(parts, *, name):
    outs = []
    for t, p in enumerate(parts):
        _, h, cols = p.shape
        tr = _tile(h, (256, 128, 64, 32, 16))

        def body(p0, p1, p2, p3, o_ref):
            o_ref[...] = ((p0[...].astype(F32) + p1[...].astype(F32)) + p2[...].astype(F32)) + p3[...].astype(F32)

        specs = [pl.BlockSpec((None, tr, cols), lambda i, s=s: (s, i, 0)) for s in range(N_CHIPS)]
        outs.append(pl.pallas_call(body, name=f"{name}_{t}", out_shape=jax.ShapeDtypeStruct((h, cols), F32),
                                   grid=(h // tr,), in_specs=specs, out_specs=pl.BlockSpec((tr, cols), lambda i: (i, 0)),
                                   compiler_params=_cparams(("parallel",)))(p, p, p, p))
    return outs


ADAMW_BLOCK_BYTES = 3 * 512 * 1024


def adamw(w, g, m, v, *, name, pass_grad=False):
    shape = w.shape
    cols = shape[-1]
    rows = w.size // cols
    tr = _tile(rows, tuple(t for t in (1024, 512, 256, 128, 64, 32, 16, 8) if t * cols * 4 <= ADAMW_BLOCK_BYTES))
    c1 = 1.0 / (1.0 - ADAM_B1 ** ADAM_STEP)
    c2 = 1.0 / (1.0 - ADAM_B2 ** ADAM_STEP)

    def body(w_ref, g_ref, m_ref, v_ref, *out_refs):
        d_ref, nm_ref, nv_ref = out_refs[-3:]
        gv = g_ref[...]
        if pass_grad:
            out_refs[0][...] = gv
        nm = ADAM_B1 * m_ref[...] + (1.0 - ADAM_B1) * gv
        nv = ADAM_B2 * v_ref[...] + (1.0 - ADAM_B2) * (gv * gv)
        d_ref[...] = -ADAM_LR * ((nm * c1) / (jnp.sqrt(nv * c2) + ADAM_EPS) + ADAM_WD * w_ref[...])
        nm_ref[...] = nm
        nv_ref[...] = nv

    blk = pl.BlockSpec((tr, cols), lambda i: (i, 0))
    n_out = 4 if pass_grad else 3
    outs = pl.pallas_call(body, name=name, out_shape=[jax.ShapeDtypeStruct((rows, cols), F32)] * n_out,
                          grid=(rows // tr,), in_specs=[blk] * 4, out_specs=[blk] * n_out,
                          compiler_params=_cparams(("parallel",)))(
        w.reshape(rows, cols), g.reshape(rows, cols), m.reshape(rows, cols), v.reshape(rows, cols))
    return [o.reshape(shape) for o in outs]


def _place():
    return lax.axis_index("x"), lax.axis_index("y"), lax.axis_index("c")


def _other_chips(x, y):
    return [(1 - x, y), (x, 1 - y), (1 - x, 1 - y)]


ANY = pl.BlockSpec(memory_space=pl.ANY)
STAGE_BYTES = 1 << 20


def _stage_shape(rows, cols, dtype):
    mult = 32 // jnp.dtype(dtype).itemsize
    per_row = (-(-cols // 128) * 128) * jnp.dtype(dtype).itemsize
    chunk = max(mult, STAGE_BYTES // per_row // mult * mult)
    return pltpu.VMEM((2, min(chunk, rows), cols), dtype)


def _staged_copy(src, dst, buf, sem_in, sem_out, k):
    rows, chunk = src.shape[0], buf.shape[1]
    pending = []
    for i, r0 in enumerate(range(0, rows, chunk)):
        sz = min(chunk, rows - r0)
        slot = i % 2
        if i >= 2:
            pending[i - 2].wait()
        stage = buf.at[slot, pl.ds(0, sz)]
        cin = pltpu.make_async_copy(src.at[pl.ds(r0, sz)], stage, sem_in.at[2 * k + slot])
        cin.start()
        cin.wait()
        cout = pltpu.make_async_copy(stage, dst.at[pl.ds(r0, sz)], sem_out.at[2 * k + slot])
        cout.start()
        pending.append(cout)
    for cp in pending[max(0, len(pending) - 2):]:
        cp.wait()


def gather_chips(blocks, halve, *, name):
    n = len(blocks)

    def body(*refs):
        ins, outs = refs[:n], refs[n:2 * n]
        send_sems, recv_sems, fwd_send, fwd_recv, stage_in, stage_out = refs[2 * n:2 * n + 6]
        bufs = refs[2 * n + 6:]
        x, y, c = _place()
        me = 2 * x + y
        chips = _other_chips(x, y)
        sibling = (x, y, 1 - c)
        sends, fwds = [], []
        for t in range(n):
            for j, (px, py) in enumerate(chips):
                if halve[t]:
                    h = ins[t].shape[0] // 2
                    rows = pl.ds(c * h, h)
                    src, dst = ins[t].at[rows], outs[t].at[me, rows]
                else:
                    src, dst = ins[t], outs[t].at[me]
                cp = pltpu.make_async_remote_copy(src_ref=src, dst_ref=dst, send_sem=send_sems.at[3 * t + j],
                                                  recv_sem=recv_sems.at[3 * t + j], device_id=(px, py, c),
                                                  device_id_type=MESH)
                cp.start()
                sends.append(cp)
        for t in range(n):
            _staged_copy(ins[t], outs[t].at[me], bufs[t], stage_in, stage_out, t)
        for t in range(n):
            for j, (px, py) in enumerate(chips):
                src_chip = 2 * px + py
                if halve[t]:
                    h = ins[t].shape[0] // 2
                    rows = pl.ds(c * h, h)
                    landed = outs[t].at[src_chip, rows]
                    pltpu.make_async_remote_copy(src_ref=landed, dst_ref=landed, send_sem=send_sems.at[3 * t + j],
                                                 recv_sem=recv_sems.at[3 * t + j], device_id=(px, py, c),
                                                 device_id_type=MESH).wait_recv()
                    cp = pltpu.make_async_remote_copy(src_ref=landed, dst_ref=landed, send_sem=fwd_send.at[3 * t + j],
                                                      recv_sem=fwd_recv.at[3 * t + j], device_id=sibling,
                                                      device_id_type=MESH)
                    cp.start()
                    fwds.append(cp)
                else:
                    landed = outs[t].at[src_chip]
                    pltpu.make_async_remote_copy(src_ref=landed, dst_ref=landed, send_sem=send_sems.at[3 * t + j],
                                                 recv_sem=recv_sems.at[3 * t + j], device_id=(px, py, c),
                                                 device_id_type=MESH).wait_recv()
        for t in range(n):
            if not halve[t]:
                continue
            h = ins[t].shape[0] // 2
            for j, (px, py) in enumerate(chips):
                theirs = outs[t].at[2 * px + py, pl.ds((1 - c) * h, h)]
                pltpu.make_async_remote_copy(src_ref=theirs, dst_ref=theirs, send_sem=fwd_send.at[3 * t + j],
                                             recv_sem=fwd_recv.at[3 * t + j], device_id=sibling,
                                             device_id_type=MESH).wait_recv()
        for cp in sends + fwds:
            cp.wait_send()

    return pl.pallas_call(
        body, name=name, out_shape=[jax.ShapeDtypeStruct((N_CHIPS,) + b.shape, b.dtype) for b in blocks],
        in_specs=[ANY] * n, out_specs=[ANY] * n,
        scratch_shapes=[pltpu.SemaphoreType.DMA((3 * n,))] * 4 + [pltpu.SemaphoreType.DMA((2 * n,))] * 2
        + [_stage_shape(b.shape[0], b.shape[1], b.dtype) for b in blocks],
        compiler_params=pltpu.CompilerParams(has_side_effects=True, vmem_limit_bytes=VMEM_LIMIT))(*blocks)


HBM_SPEC = pl.BlockSpec(memory_space=pltpu.HBM)
SEM_SPEC = pl.BlockSpec(memory_space=pltpu.SEMAPHORE)
DATAFLOW = pltpu.SideEffectType.DATAFLOW_SIDE_EFFECTING


def _in_hbm(v):
    return pltpu.with_memory_space_constraint(v, pltpu.HBM)


def _split_start(srcs, land_shapes, make_copies, *, name):
    n = len(srcs)
    lands = [_in_hbm(lax.empty(s.shape, s.dtype)) for s in land_shapes]

    def body(*refs):
        ins, lands_in = refs[:n], refs[n:2 * n]
        send_sems, recv_sems = refs[2 * n], refs[2 * n + 1]
        token = refs[-1]
        for cp in make_copies(ins, lands_in, send_sems, recv_sems, False):
            cp.start()
        token[...] = jnp.zeros_like(token)

    outs = pl.pallas_call(
        body, name=name,
        out_shape=(pltpu.SemaphoreType.DMA((3 * n,)), pltpu.SemaphoreType.DMA((3 * n,)),
                   *[pltpu.HBM(s.shape, s.dtype) for s in srcs], *[pltpu.HBM(s.shape, s.dtype) for s in land_shapes],
                   jax.ShapeDtypeStruct((8, 128), F32)),
        in_specs=[HBM_SPEC] * (2 * n),
        out_specs=(SEM_SPEC, SEM_SPEC, *([HBM_SPEC] * (2 * n)), pl.BlockSpec(memory_space=pltpu.VMEM)),
        input_output_aliases={i: 2 + i for i in range(2 * n)},
        compiler_params=pltpu.CompilerParams(has_side_effects=DATAFLOW))(*[_in_hbm(s) for s in srcs], *lands)
    return outs[0], outs[1], list(outs[2:2 + n]), list(outs[2 + n:2 + 2 * n]), outs[-1]


def _split_wait(send_sems, recv_sems, srcs, lands, after, make_copies, *, name):
    n = len(srcs)

    def body(*refs):
        ins, lands_in = refs[:n], refs[n:2 * n]
        s_sems, r_sems = refs[2 * n], refs[2 * n + 1]
        token = refs[-1]
        for cp in make_copies(ins, lands_in, s_sems, r_sems, False):
            cp.wait_send()
        for cp in make_copies(ins, lands_in, s_sems, r_sems, True):
            cp.wait_recv()
        token[...] = jnp.zeros_like(token)

    outs = pl.pallas_call(
        body, name=name,
        out_shape=(*[pltpu.HBM(s.shape, s.dtype) for s in srcs], *[pltpu.HBM(s.shape, s.dtype) for s in lands],
                   jax.ShapeDtypeStruct((8, 128), F32)),
        in_specs=[HBM_SPEC] * (2 * n) + [SEM_SPEC, SEM_SPEC, ANY],
        out_specs=(*([HBM_SPEC] * (2 * n)), pl.BlockSpec(memory_space=pltpu.VMEM)),
        input_output_aliases={i: i for i in range(2 * n)},
        compiler_params=pltpu.CompilerParams(has_side_effects=DATAFLOW))(*srcs, *lands, send_sems, recv_sems, after)
    return list(outs[:n]), list(outs[n:2 * n]), outs[-1]


def _gather_copies(halve):
    def make(ins, lands, send_sems, recv_sems, arrivals):
        x, y, c = _place()
        me = 2 * x + y
        cps = []
        for t in range(len(ins)):
            for j, (px, py) in enumerate(_other_chips(x, y)):
                if halve[t]:
                    h = ins[t].shape[0] // 2
                    rows = pl.ds(c * h, h)
                    src, dst, landed = ins[t].at[rows], lands[t].at[me, rows], lands[t].at[2 * px + py, rows]
                else:
                    src, dst, landed = ins[t], lands[t].at[me], lands[t].at[2 * px + py]
                sem = dict(send_sem=send_sems.at[3 * t + j], recv_sem=recv_sems.at[3 * t + j], device_id=(px, py, c),
                           device_id_type=MESH)
                if arrivals:
                    cps.append(pltpu.make_async_remote_copy(src_ref=landed, dst_ref=landed, **sem))
                else:
                    cps.append(pltpu.make_async_remote_copy(src_ref=src, dst_ref=dst, **sem))
        return cps
    return make


def gather_start(blocks, halve, *, name):
    shapes = [jax.ShapeDtypeStruct((N_CHIPS,) + b.shape, b.dtype) for b in blocks]
    return _split_start(blocks, shapes, _gather_copies(halve), name=name)


def gather_wait(started, halve, after, *, name):
    send_sems, recv_sems, srcs, lands, _ = started
    return _split_wait(send_sems, recv_sems, srcs, lands, after, _gather_copies(halve), name=name)


def gather_finish(blocks, lands, halve, token, *, name):
    n = len(blocks)

    def body(*refs):
        ins, outs = refs[:n], refs[2 * n + 1:3 * n + 1]
        fwd_send, fwd_recv, stage_in, stage_out = refs[3 * n + 1:3 * n + 5]
        bufs = refs[3 * n + 5:]
        x, y, c = _place()
        me = 2 * x + y
        chips = _other_chips(x, y)
        sibling = (x, y, 1 - c)
        fwds = []
        for t in range(n):
            if not halve[t]:
                continue
            h = ins[t].shape[0] // 2
            for j, (px, py) in enumerate(chips):
                landed = outs[t].at[2 * px + py, pl.ds(c * h, h)]
                cp = pltpu.make_async_remote_copy(src_ref=landed, dst_ref=landed, send_sem=fwd_send.at[3 * t + j],
                                                  recv_sem=fwd_recv.at[3 * t + j], device_id=sibling, device_id_type=MESH)
                cp.start()
                fwds.append(cp)
        for t in range(n):
            _staged_copy(ins[t], outs[t].at[me], bufs[t], stage_in, stage_out, t)
        for t in range(n):
            if not halve[t]:
                continue
            h = ins[t].shape[0] // 2
            for j, (px, py) in enumerate(chips):
                theirs = outs[t].at[2 * px + py, pl.ds((1 - c) * h, h)]
                pltpu.make_async_remote_copy(src_ref=theirs, dst_ref=theirs, send_sem=fwd_send.at[3 * t + j],
                                             recv_sem=fwd_recv.at[3 * t + j], device_id=sibling,
                                             device_id_type=MESH).wait_recv()
        for cp in fwds:
            cp.wait_send()

    return pl.pallas_call(
        body, name=name, out_shape=[jax.ShapeDtypeStruct(v.shape, v.dtype) for v in lands],
        in_specs=[ANY] * (2 * n) + [pl.BlockSpec(memory_space=pltpu.VMEM)], out_specs=[ANY] * n,
        input_output_aliases={n + i: i for i in range(n)},
        scratch_shapes=[pltpu.SemaphoreType.DMA((3 * n,))] * 2 + [pltpu.SemaphoreType.DMA((2 * n,))] * 2
        + [_stage_shape(b.shape[0], b.shape[1], b.dtype) for b in blocks],
        compiler_params=pltpu.CompilerParams(has_side_effects=True, vmem_limit_bytes=VMEM_LIMIT))(*blocks, *lands, token)


def _xchg_copies(ins, lands, send_sems, recv_sems, arrivals):
    x, y, c = _place()
    me = 2 * x + y
    cps = []
    for t in range(len(ins)):
        for j, (px, py) in enumerate(_other_chips(x, y)):
            landed = lands[t].at[2 * px + py]
            sem = dict(send_sem=send_sems.at[3 * t + j], recv_sem=recv_sems.at[3 * t + j], device_id=(px, py, c),
                       device_id_type=MESH)
            if arrivals:
                cps.append(pltpu.make_async_remote_copy(src_ref=landed, dst_ref=landed, **sem))
            else:
                cps.append(pltpu.make_async_remote_copy(src_ref=ins[t].at[2 * px + py], dst_ref=lands[t].at[me], **sem))
    return cps


def xchg_start(parts, *, name):
    return _split_start(parts, [jax.ShapeDtypeStruct(p.shape, p.dtype) for p in parts], _xchg_copies, name=name)


def xchg_wait(started, after, *, name):
    send_sems, recv_sems, srcs, lands, _ = started
    return _split_wait(send_sems, recv_sems, srcs, lands, after, _xchg_copies, name=name)


def xchg_finish(parts, lands, *, name):
    n = len(parts)

    def body(*refs):
        ins, outs = refs[:n], refs[2 * n:3 * n]
        stage_in, stage_out = refs[3 * n:3 * n + 2]
        bufs = refs[3 * n + 2:]
        x, y, _ = _place()
        me = 2 * x + y
        for t in range(n):
            _staged_copy(ins[t].at[me], outs[t].at[me], bufs[t], stage_in, stage_out, t)

    return pl.pallas_call(
        body, name=name, out_shape=[jax.ShapeDtypeStruct(v.shape, v.dtype) for v in lands],
        in_specs=[ANY] * (2 * n), out_specs=[ANY] * n, input_output_aliases={n + i: i for i in range(n)},
        scratch_shapes=[pltpu.SemaphoreType.DMA((2 * n,))] * 2
        + [_stage_shape(p.shape[1], p.shape[2], p.dtype) for p in parts],
        compiler_params=pltpu.CompilerParams(has_side_effects=True, vmem_limit_bytes=VMEM_LIMIT))(*parts, *lands)


def pair_split(grads, *, name):
    n = len(grads)

    def body(*refs):
        ins, recv = refs[:n], refs[n:2 * n]
        send_sems, recv_sems = refs[2 * n:]
        x, y, c = _place()
        sibling = (x, y, 1 - c)
        cps = []
        for t in range(n):
            h = ins[t].shape[1] // 2
            cp = pltpu.make_async_remote_copy(src_ref=ins[t].at[:, pl.ds((1 - c) * h, h)], dst_ref=recv[t],
                                              send_sem=send_sems.at[t], recv_sem=recv_sems.at[t], device_id=sibling,
                                              device_id_type=MESH)
            cp.start()
            cps.append(cp)
        for cp in cps:
            cp.wait()

    half = [jax.ShapeDtypeStruct((g.shape[0], g.shape[1] // 2, g.shape[2]), g.dtype) for g in grads]
    return pl.pallas_call(
        body, name=name, out_shape=half, in_specs=[ANY] * n, out_specs=[ANY] * n,
        scratch_shapes=[pltpu.SemaphoreType.DMA((n,))] * 2,
        compiler_params=pltpu.CompilerParams(has_side_effects=True))(*grads)


def chip_exchange(parts, *, name):
    n = len(parts)

    def body(*refs):
        ins, outs = refs[:n], refs[n:2 * n]
        send_sems, recv_sems, stage_in, stage_out = refs[2 * n:2 * n + 4]
        bufs = refs[2 * n + 4:]
        x, y, c = _place()
        me = 2 * x + y
        chips = _other_chips(x, y)
        cps = []
        for t in range(n):
            for j, (px, py) in enumerate(chips):
                cp = pltpu.make_async_remote_copy(src_ref=ins[t].at[2 * px + py], dst_ref=outs[t].at[me],
                                                  send_sem=send_sems.at[3 * t + j], recv_sem=recv_sems.at[3 * t + j],
                                                  device_id=(px, py, c), device_id_type=MESH)
                cp.start()
                cps.append(cp)
        for t in range(n):
            _staged_copy(ins[t].at[me], outs[t].at[me], bufs[t], stage_in, stage_out, t)
        for t in range(n):
            for j, (px, py) in enumerate(chips):
                landed = outs[t].at[2 * px + py]
                pltpu.make_async_remote_copy(src_ref=landed, dst_ref=landed, send_sem=send_sems.at[3 * t + j],
                                             recv_sem=recv_sems.at[3 * t + j], device_id=(px, py, c),
                                             device_id_type=MESH).wait_recv()
        for cp in cps:
            cp.wait_send()

    return pl.pallas_call(
        body, name=name, out_shape=[jax.ShapeDtypeStruct(p.shape, p.dtype) for p in parts],
        in_specs=[ANY] * n, out_specs=[ANY] * n,
        scratch_shapes=[pltpu.SemaphoreType.DMA((3 * n,))] * 2 + [pltpu.SemaphoreType.DMA((2 * n,))] * 2
        + [_stage_shape(p.shape[1], p.shape[2], p.dtype) for p in parts],
        compiler_params=pltpu.CompilerParams(has_side_effects=True, vmem_limit_bytes=VMEM_LIMIT))(*parts)


def pair_join(halves, *, name):
    n = len(halves)

    def body(*refs):
        ins, outs = refs[:n], refs[n:2 * n]
        send_sems, recv_sems, stage_in, stage_out = refs[2 * n:2 * n + 4]
        bufs = refs[2 * n + 4:]
        x, y, c = _place()
        sibling = (x, y, 1 - c)
        cps = []
        for t in range(n):
            h = ins[t].shape[0]
            cp = pltpu.make_async_remote_copy(src_ref=ins[t], dst_ref=outs[t].at[pl.ds(c * h, h)],
                                              send_sem=send_sems.at[t], recv_sem=recv_sems.at[t], device_id=sibling,
                                              device_id_type=MESH)
            cp.start()
            cps.append(cp)
        for t in range(n):
            h = ins[t].shape[0]
            _staged_copy(ins[t], outs[t].at[pl.ds(c * h, h)], bufs[t], stage_in, stage_out, t)
        for t in range(n):
            h = ins[t].shape[0]
            theirs = outs[t].at[pl.ds((1 - c) * h, h)]
            pltpu.make_async_remote_copy(src_ref=theirs, dst_ref=theirs, send_sem=send_sems.at[t],
                                         recv_sem=recv_sems.at[t], device_id=sibling, device_id_type=MESH).wait_recv()
        for cp in cps:
            cp.wait_send()

    return pl.pallas_call(
        body, name=name, out_shape=[jax.ShapeDtypeStruct((2 * p.shape[0], p.shape[1]), p.dtype) for p in halves],
        in_specs=[ANY] * n, out_specs=[ANY] * n,
        scratch_shapes=[pltpu.SemaphoreType.DMA((n,))] * 2 + [pltpu.SemaphoreType.DMA((2 * n,))] * 2
        + [_stage_shape(p.shape[0], p.shape[1], p.dtype) for p in halves],
        compiler_params=pltpu.CompilerParams(has_side_effects=True, vmem_limit_bytes=VMEM_LIMIT))(*halves)


def pair_join_stacked(halves, stacks, layer, *, name):
    n = len(halves)

    def body(*refs):
        ins, outs = refs[:n], refs[2 * n:3 * n]
        send_sems, recv_sems, stage_in, stage_out = refs[3 * n:3 * n + 4]
        bufs = refs[3 * n + 4:]
        x, y, c = _place()
        sibling = (x, y, 1 - c)
        cps = []
        for t in range(n):
            h = ins[t].shape[0]
            cp = pltpu.make_async_remote_copy(src_ref=ins[t], dst_ref=outs[t].at[layer, pl.ds(c * h, h)],
                                              send_sem=send_sems.at[t], recv_sem=recv_sems.at[t], device_id=sibling,
                                              device_id_type=MESH)
            cp.start()
            cps.append(cp)
        for t in range(n):
            h = ins[t].shape[0]
            _staged_copy(ins[t], outs[t].at[layer, pl.ds(c * h, h)], bufs[t], stage_in, stage_out, t)
        for t in range(n):
            h = ins[t].shape[0]
            theirs = outs[t].at[layer, pl.ds((1 - c) * h, h)]
            pltpu.make_async_remote_copy(src_ref=theirs, dst_ref=theirs, send_sem=send_sems.at[t],
                                         recv_sem=recv_sems.at[t], device_id=sibling, device_id_type=MESH).wait_recv()
        for cp in cps:
            cp.wait_send()

    return pl.pallas_call(
        body, name=name, out_shape=[jax.ShapeDtypeStruct(s.shape, s.dtype) for s in stacks],
        in_specs=[ANY] * (2 * n), out_specs=[ANY] * n, input_output_aliases={n + i: i for i in range(n)},
        scratch_shapes=[pltpu.SemaphoreType.DMA((n,))] * 2 + [pltpu.SemaphoreType.DMA((2 * n,))] * 2
        + [_stage_shape(p.shape[0], p.shape[1], p.dtype) for p in halves],
        compiler_params=pltpu.CompilerParams(has_side_effects=True, vmem_limit_bytes=VMEM_LIMIT))(*halves, *stacks)


def reduce_scatter(grads, pay_dtype, *, name):
    recv = pair_split(grads, name=name + "_split")
    part = add_pairs(grads, recv, pay_dtype, name=name + "_add2")
    got = chip_exchange(part, name=name + "_xchg")
    half = add_chips(got, name=name + "_add4")
    return pair_join(half, name=name + "_join")


def _in_sizes(d, w, n_heads):
    return (w, w, w, w, w, n_heads, n_heads, d, d)


def _wcat_from_gathered(wg, d, w, n_heads):
    full = jnp.concatenate([wg[s] for s in range(N_CHIPS)], axis=1)
    sizes = _in_sizes(d, w, n_heads)
    offs = [0]
    for s in sizes:
        offs.append(offs[-1] + s)
    pieces = [full[:, offs[i]:offs[i + 1]] for i in range(len(sizes))]
    u, q, k, v, z, beta, a, gs, gd = pieces
    pad = jnp.zeros((full.shape[0], SMALL_W - 2 * n_heads), full.dtype)
    return jnp.concatenate([gs, gd, u, q, k, v, z, beta, a, pad], axis=1)


def _wcat_grad_to_shards(dwcat, d, w, n_heads):
    gs, gd = dwcat[:, :d], dwcat[:, d:2 * d]
    o = 2 * d
    u, q, k, v, z = [dwcat[:, o + i * w:o + (i + 1) * w] for i in range(5)]
    o += 5 * w
    beta, a = dwcat[:, o:o + n_heads], dwcat[:, o + n_heads:o + 2 * n_heads]
    full = jnp.concatenate([u, q, k, v, z, beta, a, gs, gd], axis=1)
    return jnp.stack(jnp.split(full, N_CHIPS, axis=1))


def _s5_discretize(a_re, a_im, log_dt, b_re, b_im):
    dt = jnp.exp(log_dt)[:, None]
    mag = jnp.exp(a_re * dt)
    lbar_r, lbar_i = mag * jnp.cos(a_im * dt), mag * jnp.sin(a_im * dt)
    den = a_re * a_re + a_im * a_im
    zr, zi = _cmul(lbar_r - 1.0, lbar_i, a_re / den, -a_im / den)
    bbar_r, bbar_i = _cmul(zr[:, :, None], zi[:, :, None], b_re, b_im)
    return lbar_r, lbar_i, bbar_r, bbar_i


def _blockdiag_in(bbar):
    g, p, h = bbar.shape
    t = g // GROUPS_PER_TILE
    bb = bbar.reshape(t, GROUPS_PER_TILE, p, h).transpose(0, 1, 3, 2)
    eye = jnp.eye(GROUPS_PER_TILE, dtype=bbar.dtype)
    return jnp.einsum('tjhp,jk->tjhkp', bb, eye).reshape(t, GROUPS_PER_TILE * h, GROUPS_PER_TILE * p)


def _blockdiag_in_grad(dblk, g, p, h):
    t = g // GROUPS_PER_TILE
    d5 = dblk.reshape(t, GROUPS_PER_TILE, h, GROUPS_PER_TILE, p)
    eye = jnp.eye(GROUPS_PER_TILE, dtype=dblk.dtype)
    diag = jnp.einsum('tjhkp,jk->tjhp', d5, eye)
    return diag.transpose(0, 1, 3, 2).reshape(g, p, h)


def _blockdiag_out(cmat):
    g, h, p = cmat.shape
    t = g // GROUPS_PER_TILE
    cc = cmat.reshape(t, GROUPS_PER_TILE, h, p).transpose(0, 1, 3, 2)
    eye = jnp.eye(GROUPS_PER_TILE, dtype=cmat.dtype)
    return jnp.einsum('tjph,jk->tjpkh', cc, eye).reshape(t, GROUPS_PER_TILE * p, GROUPS_PER_TILE * h)


def _blockdiag_out_grad(dblk, g, h, p):
    t = g // GROUPS_PER_TILE
    d5 = dblk.reshape(t, GROUPS_PER_TILE, p, GROUPS_PER_TILE, h)
    eye = jnp.eye(GROUPS_PER_TILE, dtype=dblk.dtype)
    diag = jnp.einsum('tjpkh,jk->tjph', d5, eye)
    return diag.transpose(0, 1, 3, 2).reshape(g, h, p)


def _pad_row(v, width):
    return jnp.pad(v.reshape(1, -1), ((0, 0), (0, width - v.size)))


def _pack(arrs, rows_mult):
    flat = jnp.concatenate([a.reshape(-1) for a in arrs])
    per = 128 * rows_mult
    total = -(-flat.size // per) * per
    return jnp.pad(flat, (0, total - flat.size))


def _unpack(flat, like):
    out, o = [], 0
    for a in like:
        out.append(flat[o:o + a.size].reshape(a.shape))
        o += a.size
    return out


def kernel(x, ffn1_w_gu, ffn1_w_down, ln1_g, ln1_b, w_in, conv_w, ssm_a_re, ssm_a_im, ssm_log_dt, ssm_b_re, ssm_b_im, ssm_c_re, ssm_c_im, ssm_d, glu_w, glu_b, gdn_a_log, gdn_dt_bias, gdn_norm_w, w_br_ssm, w_br_gdn, w_out, ln2_g, ln2_b, ffn2_w_gu, ffn2_w_down, ln3_g, ln3_b, loss_target, m_ffn1_w_gu, m_ffn1_w_down, m_ln1_g, m_ln1_b, m_w_in, m_conv_w, m_ssm_a_re, m_ssm_a_im, m_ssm_log_dt, m_ssm_b_re, m_ssm_b_im, m_ssm_c_re, m_ssm_c_im, m_ssm_d, m_glu_w, m_glu_b, m_gdn_a_log, m_gdn_dt_bias, m_gdn_norm_w, m_w_br_ssm, m_w_br_gdn, m_w_out, m_ln2_g, m_ln2_b, m_ffn2_w_gu, m_ffn2_w_down, m_ln3_g, m_ln3_b, v_ffn1_w_gu, v_ffn1_w_down, v_ln1_g, v_ln1_b, v_w_in, v_conv_w, v_ssm_a_re, v_ssm_a_im, v_ssm_log_dt, v_ssm_b_re, v_ssm_b_im, v_ssm_c_re, v_ssm_c_im, v_ssm_d, v_glu_w, v_glu_b, v_gdn_a_log, v_gdn_dt_bias, v_gdn_norm_w, v_w_br_ssm, v_w_br_gdn, v_w_out, v_ln2_g, v_ln2_b, v_ffn2_w_gu, v_ffn2_w_down, v_ln3_g, v_ln3_b):
    args = locals()
    wts = {n: args[n] for n in WEIGHT_NAMES}
    mom = {n: args["m_" + n] for n in WEIGHT_NAMES}
    var = {n: args["v_" + n] for n in WEIGHT_NAMES}

    depth = ln1_g.shape[0]
    n_tok, d = x.shape[1], x.shape[2]
    w = glu_w.shape[-1]
    n_heads = gdn_a_log.shape[-1]
    n_groups, n_state, grp = ssm_b_re.shape[1], ssm_b_re.shape[2], ssm_b_re.shape[3]
    alpha = (2.0 * depth) ** 0.25
    u_col0 = 2 * d
    qkv_col0 = u_col0 + w
    z_col0 = u_col0 + 4 * w
    small_col0 = u_col0 + 5 * w
    x_idx, y_idx, _ = _place()
    chip = 2 * x_idx + y_idx

    xcur = x[0]
    xcur_b = xcur.astype(BF16)
    saved = []
    halve = [True] * len(BIG) + [False]

    def start_gather(layer, order_token):
        shards = [wts[n][layer].astype(BF16) for n in BIG] + [conv_w[layer] + order_token[0, 0]]
        return gather_start(shards, halve, name=f"gather_start_l{layer}")

    started = start_gather(0, jnp.zeros((1, 1), F32))
    after = xcur
    for l in range(depth):
        shards, lands, waited = gather_wait(started, halve, after, name=f"gather_wait_l{l}")
        started = start_gather(l + 1, waited) if l + 1 < depth else started
        gathered = gather_finish(shards, lands, halve, started[4], name="gather_finish")
        gw = dict(zip(BIG, gathered[:-1]))
        conv_full = jnp.concatenate([gathered[-1][s] for s in range(N_CHIPS)], axis=1)
        wgu1, wgu2 = gw['ffn1_w_gu'], gw['ffn2_w_gu']
        wd1 = gw['ffn1_w_down'].reshape(-1, d)
        wd2 = gw['ffn2_w_down'].reshape(-1, d)
        wcat = _wcat_from_gathered(gw['w_in'], d, w, n_heads)
        wglu = gw['glu_w'].reshape(w, w)
        wbs, wbd = gw['w_br_ssm'], gw['w_br_gdn']
        wout = gw['w_out'].reshape(d, d)
        f = wd1.shape[0]

        vec = lambda v: v[l].reshape(1, -1)
        x0, x0b = xcur, xcur_b
        h1 = mm(x0b, wgu1, b_nb=N_CHIPS, out_nb=2, name=f"ffn_up")
        r1, x1, x1b = down_res_ln(h1, wd1, x0, vec(ln1_g), vec(ln1_b), swiglu=True, alpha=alpha, scale=0.5,
                                  name="ffn_down")
        proj = mm(x1b, wcat, name="in_proj")
        (lbar_r, lbar_i, bbar_r, bbar_i), disc_vjp = jax.vjp(
            _s5_discretize, ssm_a_re[l], ssm_a_im[l], ssm_log_dt[l], ssm_b_re[l], ssm_b_im[l])
        s5w = (_blockdiag_in(bbar_r), _blockdiag_in(bbar_i), _blockdiag_out(ssm_c_re[l]), _blockdiag_out(ssm_c_im[l]),
               lbar_r.reshape(1, -1), lbar_i.reshape(1, -1), ssm_d[l].reshape(1, -1))
        ypre, y2, y2b = s5_fwd(proj, u_col0, *s5w, name="s5_fwd")
        gl = mm(y2b, wglu, name="glu_proj")
        ys = glu_gate_fwd(y2, gl, vec(glu_b), name="glu_gate")
        qkv = conv_fwd(proj, qkv_col0, conv_full, name="conv_fwd")
        gdn_rows = (_pad_row(gdn_a_log[l], HEAD_DIM), _pad_row(gdn_dt_bias[l], HEAD_DIM), gdn_norm_w[l].reshape(1, -1))
        yg, states = gdn_fwd(qkv, proj, z_col0, small_col0, *gdn_rows, n_heads, name="gdn_fwd")
        bs = mm(ys, wbs, b_nb=N_CHIPS, name="br_ssm")
        bd = mm(yg, wbd, b_nb=N_CHIPS, name="br_gdn")
        mrg = merge_fwd(proj, bs, bd, name="merge")
        r2, x2, x2b = down_res_ln(mrg, wout, x1, vec(ln2_g), vec(ln2_b), swiglu=False, alpha=alpha, scale=1.0,
                                  name="mix_out")
        h3 = mm(x2b, wgu2, b_nb=N_CHIPS, out_nb=2, name="ffn_up")
        r3, x3, x3b = down_res_ln(h3, wd2, x2, vec(ln3_g), vec(ln3_b), swiglu=True, alpha=alpha, scale=0.5,
                                  name="ffn_down")
        saved.append(dict(x0b=x0b, h1=h1, r1=r1, x1b=x1b, proj=proj, s5w=s5w, disc_vjp=disc_vjp, ypre=ypre, y2=y2,
                          y2b=y2b, gl=gl, ys=ys, qkv=qkv, gdn_rows=gdn_rows, yg=yg, states=states, bs=bs, bd=bd, mrg=mrg,
                          r2=r2, x2b=x2b, h3=h3, r3=r3, wgu1=wgu1, wgu2=wgu2, wd1=wd1, wd2=wd2, wcat=wcat, wglu=wglu,
                          wbs=wbs, wbd=wbd, wout=wout, conv_full=conv_full))
        xcur, xcur_b = x3, x3b
        after = x3

    dy, loss_blk = loss_head(xcur, loss_target[0], name="loss_head")
    loss = lax.psum(loss_blk[0, 0], ("x", "y", "c"))

    small_grads = {n: [None] * depth for n in SMALL}
    group_b = ['ffn1_w_gu', 'ffn1_w_down']
    group_a = [n for n in BIG if n not in group_b]
    grad_bufs = {n: lax.empty(wts[n].shape, F32) for n in BIG}

    def start_reduction(names, grads_by_name, tag, layer):
        gl = [grads_by_name[n] for n in names]
        part = add_pairs(gl, pair_split(gl, name="rs_split_" + tag), BF16, name="rs_add2_" + tag)
        return names, layer, tag, xchg_start(part, name=f"xchg_start_{tag}_l{layer}")

    def finish_reduction(item, after_arr):
        names, layer, tag, exchange = item
        parts, lands, _ = xchg_wait(exchange, after_arr, name=f"xchg_wait_{tag}_l{layer}")
        half = add_chips(xchg_finish(parts, lands, name="xchg_finish_" + tag), name="rs_add4_" + tag)
        joined = pair_join_stacked(half, [grad_bufs[n] for n in names], layer, name="rs_join_" + tag)
        for n, b in zip(names, joined):
            grad_bufs[n] = b

    pending = []
    for l in reversed(range(depth)):
        s = saved[l]
        order = pending[-1][3][4][0, 0] if pending else 0.0
        vec = lambda v: v[l].reshape(1, -1)

        def ffn_back(dx_out, r, g_ln, xin, hh, wgu, wd):
            dr, drb, dg, db = ln_bwd(r, g_ln, dx_out, name="ln_bwd")
            dh, act = ffn_bwd_mid(dr, wd, hh, scale=0.5, name="ffn_bwd_mid")
            dwd = mm(act, drb, ta=True, out_dtype=BF16, out_scale=0.5, name="ffn_dwd")
            dwgu = mm(xin, dh, ta=True, b_nb=2, out_nb=N_CHIPS, out_dtype=BF16, name="ffn_dwgu")
            dxin = mm(dh, wgu, tb=True, a_nb=2, b_nb=N_CHIPS, add=dr, add_scale=alpha, name="ffn_dx")
            return dxin, dg, db, dwgu, dwd.reshape(N_CHIPS, -1, d)

        dx2, dg3, db3, dwgu2, dwd2 = ffn_back(dy, s['r3'], vec(ln3_g) + order, s['x2b'], s['h3'], s['wgu2'], s['wd2'])
        dr2, dr2b, dg2, db2 = ln_bwd(s['r2'], vec(ln2_g), dx2, name="ln_bwd")
        dmrg = mm(dr2b, s['wout'], tb=True, name="mix_dm")
        dwout = mm(s['mrg'], dr2b, ta=True, out_dtype=BF16, name="mix_dwout").reshape(N_CHIPS, -1, d)
        dbs, dbd, dgs, dgd = merge_bwd(s['proj'], s['bs'], s['bd'], dmrg, name="merge_bwd")
        dwbs = mm(s['ys'], dbs, ta=True, out_nb=N_CHIPS, out_dtype=BF16, name="br_dw")
        dwbd = mm(s['yg'], dbd, ta=True, out_nb=N_CHIPS, out_dtype=BF16, name="br_dw")
        dys = mm(dbs, s['wbs'], tb=True, b_nb=N_CHIPS, name="br_dx")
        dyg = mm(dbd, s['wbd'], tb=True, b_nb=N_CHIPS, name="br_dx")
        dy2a, dgl, dglu_b = glu_gate_bwd(s['y2'], s['gl'], vec(glu_b), dys, name="glu_gate_bwd")
        dwglu = mm(s['y2b'], dgl, ta=True, out_dtype=BF16, name="glu_dw").reshape(N_CHIPS, -1, w)
        dy2 = mm(dgl, s['wglu'], tb=True, add=dy2a, name="glu_dx")
        du, dbr, dbi, dcr, dci, dlr, dli, dd = s5_bwd(s['proj'], u_col0, s['ypre'], dy2, *s['s5w'], name="s5_bwd")
        dqkv_act, dz, dsmall, dalog, ddtb, dnw = gdn_bwd(s['qkv'], s['proj'], z_col0, small_col0, *s['gdn_rows'],
                                                           s['states'], dyg, n_heads, name="gdn_bwd")
        dqkv, dconv = conv_bwd(s['proj'], qkv_col0, s['conv_full'], dqkv_act, name="conv_bwd")
        dsmall_w = jnp.pad(dsmall, ((0, 0), (0, SMALL_W - HEAD_DIM)))
        dproj = jnp.concatenate([dgs, dgd, du, dqkv, dz, dsmall_w], axis=1)
        dwcat = mm(s['x1b'], dproj, ta=True, out_dtype=BF16, name="in_dw")
        dx1 = mm(dproj, s['wcat'], tb=True, add=dr2, add_scale=alpha, name="in_dx")
        item_a = start_reduction(group_a, dict(w_in=_wcat_grad_to_shards(dwcat, d, w, n_heads), glu_w=dwglu,
                                               w_br_ssm=dwbs, w_br_gdn=dwbd, w_out=dwout, ffn2_w_gu=dwgu2,
                                               ffn2_w_down=dwd2), "a", l)
        dx0, dg1, db1, dwgu1, dwd1 = ffn_back(dx1, s['r1'], vec(ln1_g) + item_a[3][4][0, 0], s['x0b'], s['h1'],
                                              s['wgu1'], s['wd1'])
        dy = dx0

        da_re, da_im, dlog_dt, db_re, db_im = s['disc_vjp'](
            (dlr.reshape(n_groups, n_state), dli.reshape(n_groups, n_state),
             _blockdiag_in_grad(dbr, n_groups, n_state, grp), _blockdiag_in_grad(dbi, n_groups, n_state, grp)))
        sg = dict(ln1_g=dg1, ln1_b=db1, ln2_g=dg2, ln2_b=db2, ln3_g=dg3, ln3_b=db3, conv_w=dconv,
                  ssm_a_re=da_re, ssm_a_im=da_im, ssm_log_dt=dlog_dt, ssm_b_re=db_re, ssm_b_im=db_im,
                  ssm_c_re=_blockdiag_out_grad(dcr, n_groups, grp, n_state),
                  ssm_c_im=_blockdiag_out_grad(dci, n_groups, grp, n_state), ssm_d=dd, glu_b=dglu_b,
                  gdn_a_log=dalog[0, :n_heads], gdn_dt_bias=ddtb[0, :n_heads], gdn_norm_w=dnw)
        for n in SMALL:
            small_grads[n][l] = sg[n].reshape(-1)
        for item in pending:
            finish_reduction(item, dy)
        pending = [item_a, start_reduction(group_b, dict(ffn1_w_gu=dwgu1, ffn1_w_down=dwd1), "b", l)]
    grad_x = dy[None]

    small_list = [jnp.stack(small_grads[n]) for n in SMALL]
    packed = _pack(small_list, 512 * N_CHIPS).reshape(N_CHIPS, -1, 128)
    red = reduce_scatter([packed], F32, name="rs_small")
    full = gather_chips(red, [True], name="gather_small")[0].reshape(-1)
    for item in pending:
        finish_reduction(item, full)
    small_red = dict(zip(SMALL, _unpack(full, small_list)))
    cw_cols = conv_w.shape[-1]
    dconv_full = small_red['conv_w'].reshape(depth, CONV_K, N_CHIPS, cw_cols)
    small_red['conv_w'] = lax.dynamic_index_in_dim(dconv_full, chip, axis=2, keepdims=False)

    grads = {}
    for n in BIG:
        grads[n] = grad_bufs[n]
    for n in SMALL:
        grads[n] = small_red[n].reshape(wts[n].shape)

    delta, new_m, new_v = {}, {}, {}
    for n in BIG:
        grads[n], delta[n], new_m[n], new_v[n] = adamw(wts[n], grads[n], mom[n], var[n], name="adamw_big",
                                                       pass_grad=True)
    pk = lambda dct: _pack([dct[n] for n in SMALL], 1024).reshape(-1, 128)
    sd, sm, sv = adamw(pk(wts), pk(grads), pk(mom), pk(var), name="adamw_small")
    like = [wts[n] for n in SMALL]
    for n, a, b, c in zip(SMALL, _unpack(sd.reshape(-1), like), _unpack(sm.reshape(-1), like),
                          _unpack(sv.reshape(-1), like)):
        delta[n], new_m[n], new_v[n] = a, b, c

    return (loss, grad_x, *[grads[n] for n in WEIGHT_NAMES], *[delta[n] for n in WEIGHT_NAMES],
            *[new_m[n] for n in WEIGHT_NAMES], *[new_v[n] for n in WEIGHT_NAMES])
```

```python
import math

import jax
import jax.numpy as jnp
from jax import lax
from jax.experimental import pallas as pl
from jax.experimental.pallas import tpu as pltpu

F32 = jnp.float32
BF16 = jnp.bfloat16
HI = lax.Precision.HIGHEST
MESH = pl.DeviceIdType.MESH

N_CHIPS = 4
SSM_GROUP = 16
SSM_STATE = 64
GROUPS_PER_TILE = 8
HEAD_DIM = 128
CHUNK = 64
CONV_K = 4
LN_EPS = 1e-5
RMS_EPS = 1e-6
L2_EPS = 1e-6
SMALL_W = 512
ADAM_LR, ADAM_B1, ADAM_B2, ADAM_EPS, ADAM_WD, ADAM_STEP = 0.001, 0.9, 0.999, 1e-08, 0.01, 10
VMEM_LIMIT = 56 * 1024 * 1024

WEIGHT_NAMES = ['ffn1_w_gu', 'ffn1_w_down', 'ln1_g', 'ln1_b', 'w_in', 'conv_w', 'ssm_a_re', 'ssm_a_im', 'ssm_log_dt',
                'ssm_b_re', 'ssm_b_im', 'ssm_c_re', 'ssm_c_im', 'ssm_d', 'glu_w', 'glu_b', 'gdn_a_log', 'gdn_dt_bias',
                'gdn_norm_w', 'w_br_ssm', 'w_br_gdn', 'w_out', 'ln2_g', 'ln2_b', 'ffn2_w_gu', 'ffn2_w_down', 'ln3_g',
                'ln3_b']
BIG = ['ffn1_w_gu', 'ffn1_w_down', 'w_in', 'glu_w', 'w_br_ssm', 'w_br_gdn', 'w_out', 'ffn2_w_gu', 'ffn2_w_down']
SMALL = [n for n in WEIGHT_NAMES if n not in BIG]


def _tile(dim, prefs):
    for p in prefs:
        if dim % p == 0:
            return p
    return dim


def _cparams(sem):
    return pltpu.CompilerParams(dimension_semantics=sem, vmem_limit_bytes=VMEM_LIMIT)


def _ln(r, g, b):
    mu = jnp.mean(r, axis=-1, keepdims=True)
    xc = r - mu
    var = jnp.mean(xc * xc, axis=-1, keepdims=True)
    return xc * lax.rsqrt(var + LN_EPS) * g + b


def _lshape(x, nb):
    return (x.shape[0], x.shape[1]) if nb == 1 else (x.shape[1], x.shape[2] * nb)


def _cb_spec(x, nb, tr, tc, rc):
    if nb == 1:
        return pl.BlockSpec((tr, tc), lambda *g: rc(*g))
    cps = x.shape[2] // tc

    def imap(*g):
        r, c = rc(*g)
        return (c // cps, r, c % cps)
    return pl.BlockSpec((None, tr, tc), imap)


MM_VMEM_BUDGET = 40 * 1024 * 1024
MM_TILES = (2048, 1024, 512, 256, 128)
MM_FULL_K = 2048


def mm(a, b, *, name, ta=False, tb=False, a_nb=1, b_nb=1, out_nb=1, out_dtype=F32, add=None, add_scale=1.0,
       out_scale=1.0):
    ar, ac = _lshape(a, a_nb)
    br, bc = _lshape(b, b_nb)
    m, k = (ac, ar) if ta else (ar, ac)
    k2, n = (bc, br) if tb else (br, bc)
    assert k == k2, (name, a.shape, b.shape)
    assert a.dtype == BF16 and b.dtype == BF16, name

    def lim(dim, *nbs):
        q = dim
        for nb in nbs:
            q = math.gcd(q, dim // nb)
        return q
    lm = lim(m, a_nb if ta else 1)
    ln = lim(n, out_nb, 1 if tb else b_nb)
    lk = lim(k, 1 if ta else a_nb, b_nb if tb else 1)
    so = jnp.dtype(out_dtype).itemsize
    if k <= MM_FULL_K and lk == k:
        tks = [k]
    else:
        tks = [t for t in range(MM_FULL_K, 127, -128) if lk % t == 0]
    best = None
    for ck in tks:
        for cm in MM_TILES:
            for cn in MM_TILES:
                if lm % cm or ln % cn:
                    continue
                est = 2 * (cm * ck * 2 + ck * cn * 2 + cm * cn * so + (cm * cn * 4 if add is not None else 0))
                est += cm * cn * 4 * (2 if k > ck else 1) + (cm * ck * 2 + 512 * cm * 4 if ta else 0)
                score = min(cm, 512) * cn * ck
                if est <= MM_VMEM_BUDGET and (best is None or score > best[0]):
                    best = (score, cm, cn, ck)
    _, tm, tn, tk = best
    nk = k // tk
    dn = (((1,), (1 if tb else 0,)), ((), ()))

    def body(*refs):
        a_ref, b_ref = refs[:2]
        add_ref = refs[2] if add is not None else None
        o_ref = refs[3 if add is not None else 2]
        scratch = refs[(4 if add is not None else 3):]
        acc = scratch[0] if nk > 1 else None
        kk = pl.program_id(2)

        if ta:
            at_ref = scratch[-1]

            def transpose_block():
                for r0 in range(0, tk, 512):
                    r1 = min(tk, r0 + 512)
                    at_ref[:, r0:r1] = a_ref[r0:r1, :].astype(F32).T.astype(BF16)
            if nk == 1:
                pl.when(pl.program_id(1) == 0)(transpose_block)
            else:
                transpose_block()
            av = at_ref[...]
        else:
            av = a_ref[...]
        part = lax.dot_general(av, b_ref[...], dn, preferred_element_type=F32)

        def finish(r):
            if out_scale != 1.0:
                r = r * out_scale
            if add is not None:
                r = r + add_scale * add_ref[...]
            o_ref[...] = r.astype(out_dtype)

        if nk == 1:
            finish(part)
        else:
            @pl.when(kk == 0)
            def _():
                acc[...] = part

            @pl.when(kk > 0)
            def _():
                acc[...] += part

            @pl.when(kk == nk - 1)
            def _():
                finish(acc[...])

    if ta:
        a_spec = _cb_spec(a, a_nb, tk, tm, lambda i, j, kk: (kk, i))
    else:
        a_spec = _cb_spec(a, a_nb, tm, tk, lambda i, j, kk: (i, kk))
    if tb:
        b_spec = _cb_spec(b, b_nb, tn, tk, lambda i, j, kk: (j, kk))
    else:
        b_spec = _cb_spec(b, b_nb, tk, tn, lambda i, j, kk: (kk, j))
    if out_nb == 1:
        out_shape = jax.ShapeDtypeStruct((m, n), out_dtype)
        out_spec = pl.BlockSpec((tm, tn), lambda i, j, kk: (i, j))
    else:
        out_shape = jax.ShapeDtypeStruct((out_nb, m, n // out_nb), out_dtype)
        out_spec = _cb_spec(out_shape, out_nb, tm, tn, lambda i, j, kk: (i, j))
    in_specs = [a_spec, b_spec]
    args = [a, b]
    if add is not None:
        in_specs.append(pl.BlockSpec((tm, tn), lambda i, j, kk: (i, j)))
        args.append(add)
    scratch = ([pltpu.VMEM((tm, tn), F32)] if nk > 1 else []) + ([pltpu.VMEM((tm, tk), BF16)] if ta else [])
    return pl.pallas_call(
        body, name=name, out_shape=out_shape, grid=(m // tm, n // tn, nk), in_specs=in_specs, out_specs=out_spec,
        scratch_shapes=scratch, compiler_params=_cparams(("parallel", "arbitrary", "arbitrary")))(*args)


def down_res_ln(src, w, x, g, b, *, swiglu, alpha, scale, name):
    n_tok, d = x.shape
    kdim = w.shape[0]
    tm = _tile(n_tok, (512, 256, 128))
    tk = _tile(kdim, (512, 256, 128))
    nk = kdim // tk

    def body(s_ref, w_ref, x_ref, g_ref, b_ref, r_ref, y_ref, yb_ref, acc):
        kk = pl.program_id(1)

        @pl.when(kk == 0)
        def _():
            acc[...] = jnp.zeros_like(acc)

        if swiglu:
            gate = s_ref[0]
            a = gate * jax.nn.sigmoid(gate) * s_ref[1]
        else:
            a = s_ref[...]
        acc[...] += jnp.dot(a.astype(BF16), w_ref[...], preferred_element_type=F32)

        @pl.when(kk == nk - 1)
        def _():
            r = alpha * x_ref[...] + scale * acc[...]
            r_ref[...] = r
            y = _ln(r, g_ref[...], b_ref[...])
            y_ref[...] = y
            yb_ref[...] = y.astype(BF16)

    if swiglu:
        s_spec = pl.BlockSpec((2, tm, tk), lambda i, kk: (0, i, kk))
    else:
        s_spec = pl.BlockSpec((tm, tk), lambda i, kk: (i, kk))
    row = pl.BlockSpec((tm, d), lambda i, kk: (i, 0))
    vec = pl.BlockSpec((1, d), lambda i, kk: (0, 0))
    return pl.pallas_call(
        body, name=name, out_shape=[jax.ShapeDtypeStruct((n_tok, d), F32)] * 2 + [jax.ShapeDtypeStruct((n_tok, d), BF16)],
        grid=(n_tok // tm, nk),
        in_specs=[s_spec, pl.BlockSpec((tk, d), lambda i, kk: (kk, 0)), row, vec, vec], out_specs=[row, row, row],
        scratch_shapes=[pltpu.VMEM((tm, d), F32)], compiler_params=_cparams(("parallel", "arbitrary")))(src, w, x, g, b)


def ln_bwd(r, g, dy, *, name):
    n_tok, d = r.shape
    tm = _tile(n_tok, (256, 128))

    def body(r_ref, g_ref, dy_ref, dr_ref, drb_ref, dg_ref, db_ref):
        i = pl.program_id(0)

        @pl.when(i == 0)
        def _():
            dg_ref[...] = jnp.zeros_like(dg_ref)
            db_ref[...] = jnp.zeros_like(db_ref)

        rv = r_ref[...]
        dyv = dy_ref[...]
        mu = jnp.mean(rv, axis=-1, keepdims=True)
        xc = rv - mu
        rstd = lax.rsqrt(jnp.mean(xc * xc, axis=-1, keepdims=True) + LN_EPS)
        xh = xc * rstd
        dxh = dyv * g_ref[...]
        dr = rstd * (dxh - jnp.mean(dxh, axis=-1, keepdims=True) - xh * jnp.mean(dxh * xh, axis=-1, keepdims=True))
        dr_ref[...] = dr
        drb_ref[...] = dr.astype(BF16)
        dg_ref[...] += jnp.sum(dyv * xh, axis=0, keepdims=True)
        db_ref[...] += jnp.sum(dyv, axis=0, keepdims=True)

    row = pl.BlockSpec((tm, d), lambda i: (i, 0))
    vec = pl.BlockSpec((1, d), lambda i: (0, 0))
    return pl.pallas_call(
        body, name=name, out_shape=[jax.ShapeDtypeStruct((n_tok, d), F32), jax.ShapeDtypeStruct((n_tok, d), BF16),
                                    jax.ShapeDtypeStruct((1, d), F32), jax.ShapeDtypeStruct((1, d), F32)],
        grid=(n_tok // tm,), in_specs=[row, vec, row], out_specs=[row, row, vec, vec],
        compiler_params=_cparams(("arbitrary",)))(r, g, dy)


def loss_head(y, target, *, name):
    n_tok, d = y.shape
    tm = _tile(n_tok, (256, 128))

    def body(y_ref, t_ref, dy_ref, l_ref):
        i = pl.program_id(0)

        @pl.when(i == 0)
        def _():
            l_ref[...] = jnp.zeros_like(l_ref)

        e = y_ref[...] - t_ref[...]
        dy_ref[...] = e * (1.0 / d)
        s = jnp.sum(jnp.mean(e * e, axis=-1, keepdims=True), axis=0, keepdims=True)
        l_ref[...] += 0.5 * s

    row = pl.BlockSpec((tm, d), lambda i: (i, 0))
    return pl.pallas_call(
        body, name=name, out_shape=[jax.ShapeDtypeStruct((n_tok, d), F32), jax.ShapeDtypeStruct((8, 128), F32)],
        grid=(n_tok // tm,), in_specs=[row, row], out_specs=[row, pl.BlockSpec((8, 128), lambda i: (0, 0))],
        compiler_params=_cparams(("arbitrary",)))(y, target)


def ffn_bwd_mid(dr, wd, h, *, scale, name):
    n_tok, d = dr.shape
    f = wd.shape[0]
    tm = _tile(n_tok, (512, 256, 128))
    tf = _tile(f, (512, 256, 128))

    def body(dr_ref, w_ref, h_ref, dh_ref, a_ref):
        dy = (scale * dr_ref[...]).astype(BF16)
        da = lax.dot_general(dy, w_ref[...], (((1,), (1,)), ((), ())), preferred_element_type=F32)
        gate = h_ref[0]
        up = h_ref[1]
        sg = jax.nn.sigmoid(gate)
        s = gate * sg
        a_ref[...] = (s * up).astype(BF16)
        dh_ref[0] = (da * up * (sg * (1.0 + gate * (1.0 - sg)))).astype(BF16)
        dh_ref[1] = (da * s).astype(BF16)

    return pl.pallas_call(
        body, name=name, out_shape=[jax.ShapeDtypeStruct((2, n_tok, f), BF16), jax.ShapeDtypeStruct((n_tok, f), BF16)],
        grid=(n_tok // tm, f // tf),
        in_specs=[pl.BlockSpec((tm, d), lambda i, j: (i, 0)), pl.BlockSpec((tf, d), lambda i, j: (j, 0)),
                  pl.BlockSpec((2, tm, tf), lambda i, j: (0, i, j))],
        out_specs=[pl.BlockSpec((2, tm, tf), lambda i, j: (0, i, j)), pl.BlockSpec((tm, tf), lambda i, j: (i, j))],
        compiler_params=_cparams(("parallel", "parallel")))(dr, wd, h)


def _cmul(ar, ai, br, bi):
    return ar * br - ai * bi, ar * bi + ai * br


def _scan_blocks(sr_ref, si_ref, lr, li, *, reverse):
    n_rows, width = sr_ref.shape
    n_blk = n_rows // 8
    row = lax.broadcasted_iota(jnp.int32, (8, width), 0)
    pr = jnp.broadcast_to(lr, (8, width))
    pi = jnp.broadcast_to(-li if reverse else li, (8, width))

    def shifted(v, dist, fill=0.0):
        if reverse:
            return jnp.where(row < 8 - dist, pltpu.roll(v, 8 - dist, 0), fill)
        return jnp.where(row >= dist, pltpu.roll(v, dist, 0), fill)

    p1 = (pr, pi)
    p2 = _cmul(*p1, *p1)
    p4 = _cmul(*p2, *p2)
    wr, wi = pr, pi
    for dist in (1, 2, 4):
        wr, wi = _cmul(wr, wi, shifted(wr, dist, 1.0), shifted(wi, dist, 0.0))
    edge = 0 if reverse else 7

    def step(i, carry):
        cr, ci = carry
        blk = (n_blk - 1 - i) if reverse else i
        r0 = pl.multiple_of(blk * 8, 8)
        xr = sr_ref[pl.ds(r0, 8), :]
        xi = si_ref[pl.ds(r0, 8), :]
        for dist, (qr, qi) in ((1, p1), (2, p2), (4, p4)):
            tr, ti = _cmul(qr, qi, shifted(xr, dist), shifted(xi, dist))
            xr, xi = xr + tr, xi + ti
        tr, ti = _cmul(wr, wi, cr, ci)
        xr, xi = xr + tr, xi + ti
        sr_ref[pl.ds(r0, 8), :] = xr
        si_ref[pl.ds(r0, 8), :] = xi
        br = jnp.where(row == edge, xr, 0.0)
        bi = jnp.where(row == edge, xi, 0.0)
        for dist in (1, 2, 4):
            br = br + pltpu.roll(br, dist, 0)
            bi = bi + pltpu.roll(bi, dist, 0)
        return br, bi

    zero = jnp.zeros((8, width), F32)
    lax.fori_loop(0, n_blk, step, (zero, zero), unroll=2)


def _s5_specs(n_tok, u_blk0):
    gw = GROUPS_PER_TILE * SSM_GROUP
    sw = GROUPS_PER_TILE * SSM_STATE
    u_spec = pl.BlockSpec((n_tok, gw), lambda t: (0, u_blk0 + t))
    col = pl.BlockSpec((n_tok, gw), lambda t: (0, t))
    bmat = pl.BlockSpec((None, gw, sw), lambda t: (t, 0, 0))
    cmat = pl.BlockSpec((None, sw, gw), lambda t: (t, 0, 0))
    lvec = pl.BlockSpec((1, sw), lambda t: (0, t))
    dvec = pl.BlockSpec((1, gw), lambda t: (0, t))
    return gw, sw, u_spec, col, bmat, cmat, lvec, dvec


def s5_fwd(proj, u_col0, bblk_r, bblk_i, cblk_r, cblk_i, lbar_r, lbar_i, dskip, *, name):
    n_tok = proj.shape[0]
    n_tiles = bblk_r.shape[0]
    gw, sw, u_spec, col, bmat, cmat, lvec, dvec = _s5_specs(n_tok, u_col0 // (GROUPS_PER_TILE * SSM_GROUP))

    def body(u_ref, br_ref, bi_ref, cr_ref, ci_ref, lr_ref, li_ref, d_ref, ypre_ref, y2_ref, y2b_ref, sr, si):
        u = u_ref[...]
        ub = u.astype(BF16)
        sr[...] = jnp.dot(ub, br_ref[...].astype(BF16), preferred_element_type=F32)
        si[...] = jnp.dot(ub, bi_ref[...].astype(BF16), preferred_element_type=F32)
        _scan_blocks(sr, si, lr_ref[...], li_ref[...], reverse=False)
        y = (jnp.dot(sr[...].astype(BF16), cr_ref[...].astype(BF16), preferred_element_type=F32)
             - jnp.dot(si[...].astype(BF16), ci_ref[...].astype(BF16), preferred_element_type=F32)
             + d_ref[...] * u)
        ypre_ref[...] = y
        y2 = jax.nn.gelu(y)
        y2_ref[...] = y2
        y2b_ref[...] = y2.astype(BF16)

    width = n_tiles * gw
    return pl.pallas_call(
        body, name=name,
        out_shape=[jax.ShapeDtypeStruct((n_tok, width), F32)] * 2 + [jax.ShapeDtypeStruct((n_tok, width), BF16)],
        grid=(n_tiles,), in_specs=[u_spec, bmat, bmat, cmat, cmat, lvec, lvec, dvec], out_specs=[col, col, col],
        scratch_shapes=[pltpu.VMEM((n_tok, sw), F32)] * 2,
        compiler_params=_cparams(("parallel",)))(proj, bblk_r, bblk_i, cblk_r, cblk_i, lbar_r, lbar_i, dskip)


def s5_bwd(proj, u_col0, ypre, dy2, bblk_r, bblk_i, cblk_r, cblk_i, lbar_r, lbar_i, dskip, *, name):
    n_tok = proj.shape[0]
    n_tiles = bblk_r.shape[0]
    gw, sw, u_spec, col, bmat, cmat, lvec, dvec = _s5_specs(n_tok, u_col0 // (GROUPS_PER_TILE * SSM_GROUP))
    rb = _tile(n_tok, (512, 256, 128))
    tn_dims = (((0,), (0,)), ((), ()))
    nt_dims = (((1,), (1,)), ((), ()))

    def body(u_ref, ypre_ref, dy2_ref, br_ref, bi_ref, cr_ref, ci_ref, lr_ref, li_ref, d_ref,
             du_ref, dbr_ref, dbi_ref, dcr_ref, dci_ref, dlr_ref, dli_ref, dd_ref, sr, si, gr, gi):
        u = u_ref[...]
        ub = u.astype(BF16)
        bmr = br_ref[...].astype(BF16)
        bmi = bi_ref[...].astype(BF16)
        cmr = cr_ref[...].astype(BF16)
        cmi = ci_ref[...].astype(BF16)
        lr = lr_ref[...]
        li = li_ref[...]
        _, gelu_vjp = jax.vjp(jax.nn.gelu, ypre_ref[...])
        dyp = gelu_vjp(dy2_ref[...])[0]
        dyb = dyp.astype(BF16)
        sr[...] = jnp.dot(ub, bmr, preferred_element_type=F32)
        si[...] = jnp.dot(ub, bmi, preferred_element_type=F32)
        _scan_blocks(sr, si, lr, li, reverse=False)
        gr[...] = lax.dot_general(dyb, cmr, nt_dims, preferred_element_type=F32)
        gi[...] = -lax.dot_general(dyb, cmi, nt_dims, preferred_element_type=F32)
        _scan_blocks(gr, gi, lr, li, reverse=True)
        srb = sr[...].astype(BF16)
        sib = si[...].astype(BF16)
        dcr_ref[...] = lax.dot_general(srb, dyb, tn_dims, preferred_element_type=F32)
        dci_ref[...] = -lax.dot_general(sib, dyb, tn_dims, preferred_element_type=F32)
        grb = gr[...].astype(BF16)
        gib = gi[...].astype(BF16)
        dbr_ref[...] = lax.dot_general(ub, grb, tn_dims, preferred_element_type=F32)
        dbi_ref[...] = lax.dot_general(ub, gib, tn_dims, preferred_element_type=F32)
        du_ref[...] = (lax.dot_general(grb, bmr, nt_dims, preferred_element_type=F32)
                       + lax.dot_general(gib, bmi, nt_dims, preferred_element_type=F32)
                       + d_ref[...] * dyp).astype(du_ref.dtype)
        dd_ref[...] = jnp.sum(dyp * u, axis=0, keepdims=True)
        inv = 1.0 / (lr * lr + li * li)
        qr = lr * inv
        qi = -li * inv
        acc_r = jnp.zeros((1, sw), F32)
        acc_i = jnp.zeros((1, sw), F32)
        for blk in range(n_tok // rb):
            rows = pl.ds(blk * rb, rb)
            ubb = u_ref[rows, :].astype(BF16)
            er = sr[rows, :] - jnp.dot(ubb, bmr, preferred_element_type=F32)
            ei = si[rows, :] - jnp.dot(ubb, bmi, preferred_element_type=F32)
            pr, pi = _cmul(er, ei, qr, qi)
            ar = gr[rows, :]
            ai = gi[rows, :]
            acc_r = acc_r + jnp.sum(ar * pr + ai * pi, axis=0, keepdims=True)
            acc_i = acc_i + jnp.sum(ai * pr - ar * pi, axis=0, keepdims=True)
        dlr_ref[...] = acc_r
        dli_ref[...] = acc_i

    width = n_tiles * gw
    out_shape = [jax.ShapeDtypeStruct((n_tok, width), BF16),
                 jax.ShapeDtypeStruct(bblk_r.shape, F32), jax.ShapeDtypeStruct(bblk_r.shape, F32),
                 jax.ShapeDtypeStruct(cblk_r.shape, F32), jax.ShapeDtypeStruct(cblk_r.shape, F32),
                 jax.ShapeDtypeStruct(lbar_r.shape, F32), jax.ShapeDtypeStruct(lbar_r.shape, F32),
                 jax.ShapeDtypeStruct(dskip.shape, F32)]
    return pl.pallas_call(
        body, name=name, out_shape=out_shape, grid=(n_tiles,),
        in_specs=[u_spec, col, col, bmat, bmat, cmat, cmat, lvec, lvec, dvec],
        out_specs=[col, bmat, bmat, cmat, cmat, lvec, lvec, dvec],
        scratch_shapes=[pltpu.VMEM((n_tok, sw), F32)] * 4,
        compiler_params=_cparams(("parallel",)))(proj, ypre, dy2, bblk_r, bblk_i, cblk_r, cblk_i, lbar_r, lbar_i, dskip)


CONV_ROWS = 256
CONV_COLS = 512


def _conv_pre(x_ref, w_ref, blk, n_blk):
    r0 = blk * CONV_ROWS
    if blk == 0:
        ext = jnp.concatenate([jnp.zeros((8, CONV_COLS), F32), x_ref[0:CONV_ROWS, :]], axis=0)
    else:
        ext = x_ref[r0 - 8:r0 + CONV_ROWS, :]
    taps = []
    c = None
    for j in range(CONV_K):
        s = CONV_K - 1 - j
        xs = ext[8:] if s == 0 else pltpu.roll(ext, s, 0)[8:]
        taps.append(xs)
        term = w_ref[j:j + 1, :] * xs
        c = term if c is None else c + term
    return c, taps


def conv_fwd(proj, col0, conv_w, *, name):
    n_tok = proj.shape[0]
    width = conv_w.shape[1]
    n_blk = n_tok // CONV_ROWS
    cb0 = col0 // CONV_COLS

    def body(x_ref, w_ref, o_ref):
        for blk in range(n_blk):
            c, _ = _conv_pre(x_ref, w_ref, blk, n_blk)
            o_ref[blk * CONV_ROWS:(blk + 1) * CONV_ROWS, :] = c * jax.nn.sigmoid(c)

    return pl.pallas_call(
        body, name=name, out_shape=jax.ShapeDtypeStruct((n_tok, width), F32), grid=(width // CONV_COLS,),
        in_specs=[pl.BlockSpec((n_tok, CONV_COLS), lambda j: (0, cb0 + j)),
                  pl.BlockSpec((CONV_K, CONV_COLS), lambda j: (0, j))],
        out_specs=pl.BlockSpec((n_tok, CONV_COLS), lambda j: (0, j)),
        compiler_params=_cparams(("parallel",)))(proj, conv_w)


def conv_bwd(proj, col0, conv_w, dout, *, name):
    n_tok = proj.shape[0]
    width = conv_w.shape[1]
    n_blk = n_tok // CONV_ROWS
    cb0 = col0 // CONV_COLS

    def body(x_ref, w_ref, do_ref, dx_ref, dw_ref, dc):
        dws = [jnp.zeros((1, CONV_COLS), F32) for _ in range(CONV_K)]
        for blk in range(n_blk):
            rows = slice(blk * CONV_ROWS, (blk + 1) * CONV_ROWS)
            c, taps = _conv_pre(x_ref, w_ref, blk, n_blk)
            sg = jax.nn.sigmoid(c)
            dcv = do_ref[rows, :] * (sg * (1.0 + c * (1.0 - sg)))
            dc[rows, :] = dcv
            for j in range(CONV_K):
                dws[j] = dws[j] + jnp.sum(dcv * taps[j], axis=0, keepdims=True)
        dc[n_tok:n_tok + 8, :] = jnp.zeros((8, CONV_COLS), F32)
        for j in range(CONV_K):
            dw_ref[j:j + 1, :] = dws[j]
        for blk in range(n_blk):
            r0 = blk * CONV_ROWS
            ext = dc[r0:r0 + CONV_ROWS + 8, :]
            dx = None
            for j in range(CONV_K):
                s = CONV_K - 1 - j
                sh = ext[:CONV_ROWS] if s == 0 else pltpu.roll(ext, CONV_ROWS + 8 - s, 0)[:CONV_ROWS]
                term = w_ref[j:j + 1, :] * sh
                dx = term if dx is None else dx + term
            dx_ref[r0:r0 + CONV_ROWS, :] = dx.astype(dx_ref.dtype)

    return pl.pallas_call(
        body, name=name, out_shape=[jax.ShapeDtypeStruct((n_tok, width), BF16), jax.ShapeDtypeStruct(conv_w.shape, F32)],
        grid=(width // CONV_COLS,),
        in_specs=[pl.BlockSpec((n_tok, CONV_COLS), lambda j: (0, cb0 + j)),
                  pl.BlockSpec((CONV_K, CONV_COLS), lambda j: (0, j)),
                  pl.BlockSpec((n_tok, CONV_COLS), lambda j: (0, j))],
        out_specs=[pl.BlockSpec((n_tok, CONV_COLS), lambda j: (0, j)), pl.BlockSpec((CONV_K, CONV_COLS), lambda j: (0, j))],
        scratch_shapes=[pltpu.VMEM((n_tok + 8, CONV_COLS), F32)],
        compiler_params=_cparams(("parallel",)))(proj, conv_w, dout)


GDN_PREC = lax.Precision.HIGH


def _neumann_inverse(lowers):
    n = lowers[0].shape[0]
    eye = (lax.broadcasted_iota(jnp.int32, (n, n), 0) == lax.broadcasted_iota(jnp.int32, (n, n), 1)).astype(F32)
    xs = [-l for l in lowers]
    ts = [eye + x for x in xs]
    power = 2
    while power < n:
        xs = [jnp.dot(x, x, precision=GDN_PREC, preferred_element_type=F32) for x in xs]
        ts = [t + jnp.dot(t, x, precision=GDN_PREC, preferred_element_type=F32) for t, x in zip(ts, xs)]
        power *= 2
    return tuple(ts)


@jax.custom_vjp
def _unit_lower_inverse(lowers):
    return _neumann_inverse(lowers)


def _unit_lower_inverse_fwd(lowers):
    ts = _neumann_inverse(lowers)
    return ts, ts


def _unit_lower_inverse_bwd(ts, cts):
    tn = (((0,), (0,)), ((), ()))
    nt = (((1,), (1,)), ((), ()))
    lefts = [lax.dot_general(t, ct, tn, precision=GDN_PREC, preferred_element_type=F32) for t, ct in zip(ts, cts)]
    return (tuple(-lax.dot_general(l, t, nt, precision=GDN_PREC, preferred_element_type=F32)
                  for l, t in zip(lefts, ts)),)


_unit_lower_inverse.defvjp(_unit_lower_inverse_fwd, _unit_lower_inverse_bwd)


def _gdn_head(head, n_heads, state, q, k, v, z, bsmall, alog_row, dtb_row, nw):
    c = CHUNK
    lane = lax.broadcasted_iota(jnp.int32, (c, HEAD_DIM), 1)
    lane1 = lax.broadcasted_iota(jnp.int32, (1, HEAD_DIM), 1)
    ri = lax.broadcasted_iota(jnp.int32, (c, c), 0)
    ci = lax.broadcasted_iota(jnp.int32, (c, c), 1)
    causal = ri >= ci
    strict = ri > ci
    tril = causal.astype(F32)
    bl = jnp.sum(jnp.where(lane == head, bsmall, 0.0), axis=-1, keepdims=True)
    al = jnp.sum(jnp.where(lane == n_heads + head, bsmall, 0.0), axis=-1, keepdims=True)
    alog = jnp.sum(jnp.where(lane1 == head, alog_row, 0.0), axis=-1, keepdims=True)
    dtb = jnp.sum(jnp.where(lane1 == head, dtb_row, 0.0), axis=-1, keepdims=True)

    qn = q * lax.rsqrt(jnp.sum(q * q, axis=-1, keepdims=True) + L2_EPS) * (HEAD_DIM ** -0.5)
    kn = k * lax.rsqrt(jnp.sum(k * k, axis=-1, keepdims=True) + L2_EPS)
    beta = jax.nn.sigmoid(bl)
    xg = al + dtb
    g = -jnp.exp(alog) * (jnp.maximum(xg, 0.0) + jnp.log(1.0 + jnp.exp(-jnp.abs(xg))))
    g_wide = jnp.broadcast_to(g, (c, HEAD_DIM))
    yield None
    gc = jnp.dot(tril, g_wide, precision=HI, preferred_element_type=F32)
    yield None
    gc_rows = jnp.broadcast_to(jnp.mean(gc, axis=-1, keepdims=True), (c, c))
    gc_cols = gc.T[:c, :]
    g_tot = jnp.sum(g, axis=0, keepdims=True)
    decay = jnp.exp(jnp.where(causal, gc_rows - gc_cols, -1e30))
    egc = jnp.exp(gc)
    kb = kn * beta
    knb = kn.astype(BF16)
    nt = (((1,), (1,)), ((), ()))
    yield None
    lower = jnp.where(strict, lax.dot_general(kb.astype(BF16), knb, nt, preferred_element_type=F32) * decay, 0.0)
    tinv = yield lower
    u_val = jnp.dot(tinv, v * beta, precision=GDN_PREC, preferred_element_type=F32)
    yield None
    w_key = jnp.dot(tinv, kb * egc, precision=GDN_PREC, preferred_element_type=F32)
    yield None
    attn = lax.dot_general(qn.astype(BF16), knb, nt, preferred_element_type=F32) * decay
    q_dec = qn * egc
    k_dec = kn * jnp.exp(g_tot - gc)
    sb = state.astype(BF16)
    yield None
    v_new = u_val - jnp.dot(w_key.astype(BF16), sb, preferred_element_type=F32)
    vnb = v_new.astype(BF16)
    yield None
    o = (jnp.dot(q_dec.astype(BF16), sb, preferred_element_type=F32)
         + jnp.dot(attn.astype(BF16), vnb, preferred_element_type=F32))
    yield None
    new_state = state * jnp.exp(g_tot) + lax.dot_general(k_dec.astype(BF16), vnb, (((0,), (0,)), ((), ())),
                                                         preferred_element_type=F32)
    yield None
    o = o * lax.rsqrt(jnp.mean(o * o, axis=-1, keepdims=True) + RMS_EPS) * nw
    o = o * (z * jax.nn.sigmoid(z))
    return o, new_state


def _gdn_step(n_heads, states, qs, ks, vs, zs, bsmall, alog_row, dtb_row, nw):
    gens = [_gdn_head(h, n_heads, states[h], qs[h], ks[h], vs[h], zs[h], bsmall, alog_row, dtb_row, nw)
            for h in range(n_heads)]
    lowers = [None] * n_heads
    while any(m is None for m in lowers):
        for h in range(n_heads):
            lowers[h] = next(gens[h])
    tinvs = _unit_lower_inverse(tuple(lowers))
    results = [None] * n_heads
    first = True
    while any(r is None for r in results):
        for h in range(n_heads):
            try:
                if first:
                    gens[h].send(tinvs[h])
                else:
                    next(gens[h])
            except StopIteration as stop:
                results[h] = stop.value
        first = False
    return tuple(r[0] for r in results), tuple(r[1] for r in results)


def _gdn_in_specs(n_heads, qkv_width_blocks, z_blk, small_blk, rev, n_chunks):
    w = n_heads * HEAD_DIM

    def cidx(i):
        return (n_chunks - 1 - i) if rev else i
    qs = pl.BlockSpec((CHUNK, w), lambda i: (cidx(i), 0))
    ks = pl.BlockSpec((CHUNK, w), lambda i: (cidx(i), 1))
    vs = pl.BlockSpec((CHUNK, w), lambda i: (cidx(i), 2))
    zs = pl.BlockSpec((CHUNK, w), lambda i: (cidx(i), z_blk))
    bs = pl.BlockSpec((CHUNK, HEAD_DIM), lambda i: (cidx(i), small_blk))
    pv = pl.BlockSpec((1, HEAD_DIM), lambda i: (0, 0))
    return cidx, qs, ks, vs, zs, bs, pv


def gdn_fwd(qkv, proj, z_col0, small_col0, alog_row, dtb_row, nw_row, n_heads, *, name):
    n_tok = qkv.shape[0]
    w = n_heads * HEAD_DIM
    n_chunks = n_tok // CHUNK
    cidx, qs, ks, vs, zs, bs, pv = _gdn_in_specs(n_heads, 3, z_col0 // w, small_col0 // HEAD_DIM, False, n_chunks)

    def body(q_ref, k_ref, v_ref, z_ref, b_ref, al_ref, dt_ref, nw_ref, o_ref, s_ref, state):
        @pl.when(pl.program_id(0) == 0)
        def _():
            state[...] = jnp.zeros_like(state)

        heads = range(n_heads)
        cols = [slice(h * HEAD_DIM, (h + 1) * HEAD_DIM) for h in heads]
        states = [state[h] for h in heads]
        for h in heads:
            s_ref[h] = states[h]
        outs, new_states = _gdn_step(n_heads, states, [q_ref[:, c] for c in cols], [k_ref[:, c] for c in cols],
                                     [v_ref[:, c] for c in cols], [z_ref[:, c] for c in cols], b_ref[...],
                                     al_ref[...], dt_ref[...], nw_ref[...])
        for h in heads:
            o_ref[:, cols[h]] = outs[h].astype(BF16)
            state[h] = new_states[h]

    return pl.pallas_call(
        body, name=name,
        out_shape=[jax.ShapeDtypeStruct((n_tok, w), BF16),
                   jax.ShapeDtypeStruct((n_chunks, n_heads, HEAD_DIM, HEAD_DIM), F32)],
        grid=(n_chunks,), in_specs=[qs, ks, vs, zs, bs, pv, pv, pv],
        out_specs=[pl.BlockSpec((CHUNK, w), lambda i: (i, 0)),
                   pl.BlockSpec((None, n_heads, HEAD_DIM, HEAD_DIM), lambda i: (i, 0, 0, 0))],
        scratch_shapes=[pltpu.VMEM((n_heads, HEAD_DIM, HEAD_DIM), F32)],
        compiler_params=_cparams(("arbitrary",)))(qkv, qkv, qkv, proj, proj, alog_row, dtb_row, nw_row)


def gdn_bwd(qkv, proj, z_col0, small_col0, alog_row, dtb_row, nw_row, states, dout, n_heads, *, name):
    n_tok = qkv.shape[0]
    w = n_heads * HEAD_DIM
    n_chunks = n_tok // CHUNK
    cidx, qs, ks, vs, zs, bs, pv = _gdn_in_specs(n_heads, 3, z_col0 // w, small_col0 // HEAD_DIM, True, n_chunks)

    def body(q_ref, k_ref, v_ref, z_ref, b_ref, al_ref, dt_ref, nw_ref, s_ref, do_ref,
             dqkv_ref, dz_ref, db_ref, dal_ref, ddt_ref, dnw_ref, dstate):
        @pl.when(pl.program_id(0) == 0)
        def _():
            dstate[...] = jnp.zeros_like(dstate)
            dal_ref[...] = jnp.zeros_like(dal_ref)
            ddt_ref[...] = jnp.zeros_like(ddt_ref)
            dnw_ref[...] = jnp.zeros_like(dnw_ref)

        heads = range(n_heads)
        cols = [slice(h * HEAD_DIM, (h + 1) * HEAD_DIM) for h in heads]

        def f(sts, q, k, v, z, bb, al, dt, nw):
            return _gdn_step(n_heads, sts, q, k, v, z, bb, al, dt, nw)
        _, vjp = jax.vjp(f, tuple(s_ref[h] for h in heads), tuple(q_ref[:, c] for c in cols),
                         tuple(k_ref[:, c] for c in cols), tuple(v_ref[:, c] for c in cols),
                         tuple(z_ref[:, c] for c in cols), b_ref[...], al_ref[...], dt_ref[...], nw_ref[...])
        dsts, dqs, dks, dvs, dzs, dbb, da, dd, dn = vjp((tuple(do_ref[:, c] for c in cols),
                                                         tuple(dstate[h] for h in heads)))
        for h in heads:
            dstate[h] = dsts[h]
            dqkv_ref[:, h * HEAD_DIM:(h + 1) * HEAD_DIM] = dqs[h]
            dqkv_ref[:, w + h * HEAD_DIM:w + (h + 1) * HEAD_DIM] = dks[h]
            dqkv_ref[:, 2 * w + h * HEAD_DIM:2 * w + (h + 1) * HEAD_DIM] = dvs[h]
            dz_ref[:, cols[h]] = dzs[h].astype(dz_ref.dtype)
        db_ref[...] = dbb.astype(db_ref.dtype)
        dal_ref[...] += da
        ddt_ref[...] += dd
        dnw_ref[...] += dn

    rowblk = pl.BlockSpec((CHUNK, w), lambda i: (cidx(i), 0))
    return pl.pallas_call(
        body, name=name,
        out_shape=[
            jax.ShapeDtypeStruct((n_tok, 3 * w), F32),
            jax.ShapeDtypeStruct((n_tok, w), BF16), jax.ShapeDtypeStruct((n_tok, HEAD_DIM), BF16),
            jax.ShapeDtypeStruct((1, HEAD_DIM), F32), jax.ShapeDtypeStruct((1, HEAD_DIM), F32),
            jax.ShapeDtypeStruct((1, HEAD_DIM), F32)],
        grid=(n_chunks,),
        in_specs=[qs, ks, vs, zs, bs, pv, pv, pv,
                  pl.BlockSpec((None, n_heads, HEAD_DIM, HEAD_DIM), lambda i: (cidx(i), 0, 0, 0)), rowblk],
        out_specs=[pl.BlockSpec((CHUNK, 3 * w), lambda i: (cidx(i), 0)), rowblk,
                   pl.BlockSpec((CHUNK, HEAD_DIM), lambda i: (cidx(i), 0)), pv, pv, pv],
        scratch_shapes=[pltpu.VMEM((n_heads, HEAD_DIM, HEAD_DIM), F32)],
        compiler_params=_cparams(("arbitrary",)))(qkv, qkv, qkv, proj, proj, alog_row, dtb_row, nw_row, states, dout)


def glu_gate_fwd(y2, gl, bias, *, name):
    n_tok, w = y2.shape
    tm = _tile(n_tok, (256, 128))

    def body(y_ref, g_ref, b_ref, o_ref):
        o_ref[...] = (y_ref[...] * jax.nn.sigmoid(g_ref[...] + b_ref[...])).astype(BF16)

    row = pl.BlockSpec((tm, w), lambda i: (i, 0))
    vec = pl.BlockSpec((1, w), lambda i: (0, 0))
    return pl.pallas_call(body, name=name, out_shape=jax.ShapeDtypeStruct((n_tok, w), BF16), grid=(n_tok // tm,),
                          in_specs=[row, row, vec], out_specs=row, compiler_params=_cparams(("parallel",)))(y2, gl, bias)


def glu_gate_bwd(y2, gl, bias, dys, *, name):
    n_tok, w = y2.shape
    tm = _tile(n_tok, (256, 128))

    def body(y_ref, g_ref, b_ref, d_ref, dy_ref, dg_ref, db_ref):
        @pl.when(pl.program_id(0) == 0)
        def _():
            db_ref[...] = jnp.zeros_like(db_ref)

        sg = jax.nn.sigmoid(g_ref[...] + b_ref[...])
        d = d_ref[...]
        dy_ref[...] = d * sg
        dg = d * y_ref[...] * sg * (1.0 - sg)
        dg_ref[...] = dg.astype(dg_ref.dtype)
        db_ref[...] += jnp.sum(dg, axis=0, keepdims=True)

    row = pl.BlockSpec((tm, w), lambda i: (i, 0))
    vec = pl.BlockSpec((1, w), lambda i: (0, 0))
    return pl.pallas_call(
        body, name=name, out_shape=[jax.ShapeDtypeStruct((n_tok, w), F32), jax.ShapeDtypeStruct((n_tok, w), BF16),
                                    jax.ShapeDtypeStruct((1, w), F32)],
        grid=(n_tok // tm,), in_specs=[row, row, vec, row], out_specs=[row, row, vec],
        compiler_params=_cparams(("arbitrary",)))(y2, gl, bias, dys)


def merge_fwd(proj, bs, bd, *, name):
    n_tok, d = bs.shape
    tm = _tile(n_tok, (256, 128))

    def body(gs_ref, gd_ref, bs_ref, bd_ref, o_ref):
        o_ref[...] = (jax.nn.sigmoid(gs_ref[...]) * bs_ref[...]
                      + jax.nn.sigmoid(gd_ref[...]) * bd_ref[...]).astype(BF16)

    row = pl.BlockSpec((tm, d), lambda i: (i, 0))
    return pl.pallas_call(
        body, name=name, out_shape=jax.ShapeDtypeStruct((n_tok, d), BF16), grid=(n_tok // tm,),
        in_specs=[row, pl.BlockSpec((tm, d), lambda i: (i, 1)), row, row], out_specs=row,
        compiler_params=_cparams(("parallel",)))(proj, proj, bs, bd)


def merge_bwd(proj, bs, bd, dm, *, name):
    n_tok, d = bs.shape
    tm = _tile(n_tok, (256, 128))

    def body(gs_ref, gd_ref, bs_ref, bd_ref, dm_ref, dbs_ref, dbd_ref, dgs_ref, dgd_ref):
        dmv = dm_ref[...]
        ss = jax.nn.sigmoid(gs_ref[...])
        sd = jax.nn.sigmoid(gd_ref[...])
        dbs_ref[...] = (ss * dmv).astype(BF16)
        dbd_ref[...] = (sd * dmv).astype(BF16)
        dgs_ref[...] = (dmv * bs_ref[...] * ss * (1.0 - ss)).astype(BF16)
        dgd_ref[...] = (dmv * bd_ref[...] * sd * (1.0 - sd)).astype(BF16)

    row = pl.BlockSpec((tm, d), lambda i: (i, 0))
    return pl.pallas_call(
        body, name=name, out_shape=[jax.ShapeDtypeStruct((n_tok, d), BF16)] * 4, grid=(n_tok // tm,),
        in_specs=[row, pl.BlockSpec((tm, d), lambda i: (i, 1)), row, row, row], out_specs=[row] * 4,
        compiler_params=_cparams(("parallel",)))(proj, proj, bs, bd, dm)


def add_pairs(grads, recv, out_dtype, *, name):
    core = jnp.reshape(lax.axis_index("c"), (1,)).astype(jnp.int32)
    outs = []
    for t, (a, b) in enumerate(zip(grads, recv)):
        n_sh, h, cols = b.shape
        tr = _tile(h, (256, 128, 64, 32, 16))
        nh = h // tr

        def body(c_ref, a_ref, b_ref, o_ref):
            o_ref[...] = (a_ref[...].astype(F32) + b_ref[...].astype(F32)).astype(out_dtype)

        grid_spec = pltpu.PrefetchScalarGridSpec(
            num_scalar_prefetch=1, grid=(n_sh, nh),
            in_specs=[pl.BlockSpec((None, tr, cols), lambda s, i, c_ref, nh=nh: (s, c_ref[0] * nh + i, 0)),
                      pl.BlockSpec((None, tr, cols), lambda s, i, c_ref: (s, i, 0))],
            out_specs=pl.BlockSpec((None, tr, cols), lambda s, i, c_ref: (s, i, 0)))
        outs.append(pl.pallas_call(body, name=f"{name}_{t}", out_shape=jax.ShapeDtypeStruct(b.shape, out_dtype),
                                   grid_spec=grid_spec, compiler_params=_cparams(("parallel", "parallel")))(core, a, b))
    return outs


def add_chips(parts, *, name):
    outs = []
    for t, p in enumerate(parts):
        _, h, cols = p.shape
        tr = _tile(h, (256, 128, 64, 32, 16))

        def body(p0, p1, p2, p3, o_ref):
            o_ref[...] = ((p0[...].astype(F32) + p1[...].astype(F32)) + p2[...].astype(F32)) + p3[...].astype(F32)

        specs = [pl.BlockSpec((None, tr, cols), lambda i, s=s: (s, i, 0)) for s in range(N_CHIPS)]
        outs.append(pl.pallas_call(body, name=f"{name}_{t}", out_shape=jax.ShapeDtypeStruct((h, cols), F32),
                                   grid=(h // tr,), in_specs=specs, out_specs=pl.BlockSpec((tr, cols), lambda i: (i, 0)),
                                   compiler_params=_cparams(("parallel",)))(p, p, p, p))
    return outs


ADAMW_BLOCK_BYTES = 3 * 512 * 1024


def adamw(w, g, m, v, *, name, pass_grad=False):
    shape = w.shape
    cols = shape[-1]
    lead = shape[0] if w.ndim > 2 else 1
    rows = w.size // (cols * lead)
    lanes = -(-cols // 128) * 128
    tr = _tile(rows, tuple(t for t in (1024, 512, 256, 128, 64, 32, 16, 8) if t * lanes * 4 <= ADAMW_BLOCK_BYTES))
    c1 = 1.0 / (1.0 - ADAM_B1 ** ADAM_STEP)
    c2 = 1.0 / (1.0 - ADAM_B2 ** ADAM_STEP)

    def body(w_ref, g_ref, m_ref, v_ref, *out_refs):
        d_ref, nm_ref, nv_ref = out_refs[-3:]
        gv = g_ref[...]
        if pass_grad:
            out_refs[0][...] = gv
        nm = ADAM_B1 * m_ref[...] + (1.0 - ADAM_B1) * gv
        nv = ADAM_B2 * v_ref[...] + (1.0 - ADAM_B2) * (gv * gv)
        d_ref[...] = -ADAM_LR * ((nm * c1) / (jnp.sqrt(nv * c2) + ADAM_EPS) + ADAM_WD * w_ref[...])
        nm_ref[...] = nm
        nv_ref[...] = nv

    blk = pl.BlockSpec((None, tr, cols), lambda a, i: (a, i, 0))
    n_out = 4 if pass_grad else 3
    view = (lead, rows, cols)
    outs = pl.pallas_call(body, name=name, out_shape=[jax.ShapeDtypeStruct(view, F32)] * n_out,
                          grid=(lead, rows // tr), in_specs=[blk] * 4, out_specs=[blk] * n_out,
                          compiler_params=_cparams(("parallel", "parallel")))(
        w.reshape(view), g.reshape(view), m.reshape(view), v.reshape(view))
    return [o.reshape(shape) for o in outs]


def _place():
    return lax.axis_index("x"), lax.axis_index("y"), lax.axis_index("c")


def _other_chips(x, y):
    return [(1 - x, y), (x, 1 - y), (1 - x, 1 - y)]


ANY = pl.BlockSpec(memory_space=pl.ANY)
STAGE_BYTES = 1 << 20


def _stage_shape(rows, cols, dtype):
    mult = 32 // jnp.dtype(dtype).itemsize
    per_row = (-(-cols // 128) * 128) * jnp.dtype(dtype).itemsize
    chunk = max(mult, STAGE_BYTES // per_row // mult * mult)
    return pltpu.VMEM((2, min(chunk, rows), cols), dtype)


def _staged_copy(src, dst, buf, sem_in, sem_out, k):
    rows, chunk = src.shape[0], buf.shape[1]
    pending = []
    for i, r0 in enumerate(range(0, rows, chunk)):
        sz = min(chunk, rows - r0)
        slot = i % 2
        if i >= 2:
            pending[i - 2].wait()
        stage = buf.at[slot, pl.ds(0, sz)]
        cin = pltpu.make_async_copy(src.at[pl.ds(r0, sz)], stage, sem_in.at[2 * k + slot])
        cin.start()
        cin.wait()
        cout = pltpu.make_async_copy(stage, dst.at[pl.ds(r0, sz)], sem_out.at[2 * k + slot])
        cout.start()
        pending.append(cout)
    for cp in pending[max(0, len(pending) - 2):]:
        cp.wait()


def gather_chips(blocks, halve, *, name):
    n = len(blocks)

    def body(*refs):
        ins, outs = refs[:n], refs[n:2 * n]
        send_sems, recv_sems, fwd_send, fwd_recv, stage_in, stage_out = refs[2 * n:2 * n + 6]
        bufs = refs[2 * n + 6:]
        x, y, c = _place()
        me = 2 * x + y
        chips = _other_chips(x, y)
        sibling = (x, y, 1 - c)
        sends, fwds = [], []
        for t in range(n):
            for j, (px, py) in enumerate(chips):
                if halve[t]:
                    h = ins[t].shape[0] // 2
                    rows = pl.ds(c * h, h)
                    src, dst = ins[t].at[rows], outs[t].at[me, rows]
                else:
                    src, dst = ins[t], outs[t].at[me]
                cp = pltpu.make_async_remote_copy(src_ref=src, dst_ref=dst, send_sem=send_sems.at[3 * t + j],
                                                  recv_sem=recv_sems.at[3 * t + j], device_id=(px, py, c),
                                                  device_id_type=MESH)
                cp.start()
                sends.append(cp)
        for t in range(n):
            _staged_copy(ins[t], outs[t].at[me], bufs[t], stage_in, stage_out, t)
        for t in range(n):
            for j, (px, py) in enumerate(chips):
                src_chip = 2 * px + py
                if halve[t]:
                    h = ins[t].shape[0] // 2
                    rows = pl.ds(c * h, h)
                    landed = outs[t].at[src_chip, rows]
                    pltpu.make_async_remote_copy(src_ref=landed, dst_ref=landed, send_sem=send_sems.at[3 * t + j],
                                                 recv_sem=recv_sems.at[3 * t + j], device_id=(px, py, c),
                                                 device_id_type=MESH).wait_recv()
                    cp = pltpu.make_async_remote_copy(src_ref=landed, dst_ref=landed, send_sem=fwd_send.at[3 * t + j],
                                                      recv_sem=fwd_recv.at[3 * t + j], device_id=sibling,
                                                      device_id_type=MESH)
                    cp.start()
                    fwds.append(cp)
                else:
                    landed = outs[t].at[src_chip]
                    pltpu.make_async_remote_copy(src_ref=landed, dst_ref=landed, send_sem=send_sems.at[3 * t + j],
                                                 recv_sem=recv_sems.at[3 * t + j], device_id=(px, py, c),
                                                 device_id_type=MESH).wait_recv()
        for t in range(n):
            if not halve[t]:
                continue
            h = ins[t].shape[0] // 2
            for j, (px, py) in enumerate(chips):
                theirs = outs[t].at[2 * px + py, pl.ds((1 - c) * h, h)]
                pltpu.make_async_remote_copy(src_ref=theirs, dst_ref=theirs, send_sem=fwd_send.at[3 * t + j],
                                             recv_sem=fwd_recv.at[3 * t + j], device_id=sibling,
                                             device_id_type=MESH).wait_recv()
        for cp in sends + fwds:
            cp.wait_send()

    return pl.pallas_call(
        body, name=name, out_shape=[jax.ShapeDtypeStruct((N_CHIPS,) + b.shape, b.dtype) for b in blocks],
        in_specs=[ANY] * n, out_specs=[ANY] * n,
        scratch_shapes=[pltpu.SemaphoreType.DMA((3 * n,))] * 4 + [pltpu.SemaphoreType.DMA((2 * n,))] * 2
        + [_stage_shape(b.shape[0], b.shape[1], b.dtype) for b in blocks],
        compiler_params=pltpu.CompilerParams(has_side_effects=True, vmem_limit_bytes=VMEM_LIMIT))(*blocks)


HBM_SPEC = pl.BlockSpec(memory_space=pltpu.HBM)
SEM_SPEC = pl.BlockSpec(memory_space=pltpu.SEMAPHORE)
DATAFLOW = pltpu.SideEffectType.DATAFLOW_SIDE_EFFECTING


def _in_hbm(v):
    return pltpu.with_memory_space_constraint(v, pltpu.HBM)


def _split_start(srcs, land_shapes, make_copies, *, name):
    n = len(srcs)
    lands = [_in_hbm(lax.empty(s.shape, s.dtype)) for s in land_shapes]

    def body(*refs):
        ins, lands_in = refs[:n], refs[n:2 * n]
        send_sems, recv_sems = refs[2 * n], refs[2 * n + 1]
        token = refs[-1]
        for cp in make_copies(ins, lands_in, send_sems, recv_sems, False):
            cp.start()
        token[...] = jnp.zeros_like(token)

    outs = pl.pallas_call(
        body, name=name,
        out_shape=(pltpu.SemaphoreType.DMA((3 * n,)), pltpu.SemaphoreType.DMA((3 * n,)),
                   *[pltpu.HBM(s.shape, s.dtype) for s in srcs], *[pltpu.HBM(s.shape, s.dtype) for s in land_shapes],
                   jax.ShapeDtypeStruct((8, 128), F32)),
        in_specs=[HBM_SPEC] * (2 * n),
        out_specs=(SEM_SPEC, SEM_SPEC, *([HBM_SPEC] * (2 * n)), pl.BlockSpec(memory_space=pltpu.VMEM)),
        input_output_aliases={i: 2 + i for i in range(2 * n)},
        compiler_params=pltpu.CompilerParams(has_side_effects=DATAFLOW))(*[_in_hbm(s) for s in srcs], *lands)
    return outs[0], outs[1], list(outs[2:2 + n]), list(outs[2 + n:2 + 2 * n]), outs[-1]


def _split_wait(send_sems, recv_sems, srcs, lands, after, make_copies, *, name):
    n = len(srcs)

    def body(*refs):
        ins, lands_in = refs[:n], refs[n:2 * n]
        s_sems, r_sems = refs[2 * n], refs[2 * n + 1]
        token = refs[-1]
        for cp in make_copies(ins, lands_in, s_sems, r_sems, False):
            cp.wait_send()
        for cp in make_copies(ins, lands_in, s_sems, r_sems, True):
            cp.wait_recv()
        token[...] = jnp.zeros_like(token)

    outs = pl.pallas_call(
        body, name=name,
        out_shape=(*[pltpu.HBM(s.shape, s.dtype) for s in srcs], *[pltpu.HBM(s.shape, s.dtype) for s in lands],
                   jax.ShapeDtypeStruct((8, 128), F32)),
        in_specs=[HBM_SPEC] * (2 * n) + [SEM_SPEC, SEM_SPEC, ANY],
        out_specs=(*([HBM_SPEC] * (2 * n)), pl.BlockSpec(memory_space=pltpu.VMEM)),
        input_output_aliases={i: i for i in range(2 * n)},
        compiler_params=pltpu.CompilerParams(has_side_effects=DATAFLOW))(*srcs, *lands, send_sems, recv_sems, after)
    return list(outs[:n]), list(outs[n:2 * n]), outs[-1]


def _gather_copies(halve):
    def make(ins, lands, send_sems, recv_sems, arrivals):
        x, y, c = _place()
        me = 2 * x + y
        cps = []
        for t in range(len(ins)):
            for j, (px, py) in enumerate(_other_chips(x, y)):
                if halve[t]:
                    h = ins[t].shape[0] // 2
                    rows = pl.ds(c * h, h)
                    src, dst, landed = ins[t].at[rows], lands[t].at[me, rows], lands[t].at[2 * px + py, rows]
                else:
                    src, dst, landed = ins[t], lands[t].at[me], lands[t].at[2 * px + py]
                sem = dict(send_sem=send_sems.at[3 * t + j], recv_sem=recv_sems.at[3 * t + j], device_id=(px, py, c),
                           device_id_type=MESH)
                if arrivals:
                    cps.append(pltpu.make_async_remote_copy(src_ref=landed, dst_ref=landed, **sem))
                else:
                    cps.append(pltpu.make_async_remote_copy(src_ref=src, dst_ref=dst, **sem))
        return cps
    return make


def gather_start(blocks, halve, *, name):
    shapes = [jax.ShapeDtypeStruct((N_CHIPS,) + b.shape, b.dtype) for b in blocks]
    return _split_start(blocks, shapes, _gather_copies(halve), name=name)


def gather_wait(started, halve, after, *, name):
    send_sems, recv_sems, srcs, lands, _ = started
    return _split_wait(send_sems, recv_sems, srcs, lands, after, _gather_copies(halve), name=name)


def gather_finish(blocks, lands, halve, token, *, name):
    n = len(blocks)

    def body(*refs):
        ins, outs = refs[:n], refs[2 * n + 1:3 * n + 1]
        fwd_send, fwd_recv, stage_in, stage_out = refs[3 * n + 1:3 * n + 5]
        bufs = refs[3 * n + 5:]
        x, y, c = _place()
        me = 2 * x + y
        chips = _other_chips(x, y)
        sibling = (x, y, 1 - c)
        fwds = []
        for t in range(n):
            if not halve[t]:
                continue
            h = ins[t].shape[0] // 2
            for j, (px, py) in enumerate(chips):
                landed = outs[t].at[2 * px + py, pl.ds(c * h, h)]
                cp = pltpu.make_async_remote_copy(src_ref=landed, dst_ref=landed, send_sem=fwd_send.at[3 * t + j],
                                                  recv_sem=fwd_recv.at[3 * t + j], device_id=sibling, device_id_type=MESH)
                cp.start()
                fwds.append(cp)
        for t in range(n):
            _staged_copy(ins[t], outs[t].at[me], bufs[t], stage_in, stage_out, t)
        for t in range(n):
            if not halve[t]:
                continue
            h = ins[t].shape[0] // 2
            for j, (px, py) in enumerate(chips):
                theirs = outs[t].at[2 * px + py, pl.ds((1 - c) * h, h)]
                pltpu.make_async_remote_copy(src_ref=theirs, dst_ref=theirs, send_sem=fwd_send.at[3 * t + j],
                                             recv_sem=fwd_recv.at[3 * t + j], device_id=sibling,
                                             device_id_type=MESH).wait_recv()
        for cp in fwds:
            cp.wait_send()

    return pl.pallas_call(
        body, name=name, out_shape=[jax.ShapeDtypeStruct(v.shape, v.dtype) for v in lands],
        in_specs=[ANY] * (2 * n) + [pl.BlockSpec(memory_space=pltpu.VMEM)], out_specs=[ANY] * n,
        input_output_aliases={n + i: i for i in range(n)},
        scratch_shapes=[pltpu.SemaphoreType.DMA((3 * n,))] * 2 + [pltpu.SemaphoreType.DMA((2 * n,))] * 2
        + [_stage_shape(b.shape[0], b.shape[1], b.dtype) for b in blocks],
        compiler_params=pltpu.CompilerParams(has_side_effects=True, vmem_limit_bytes=VMEM_LIMIT))(*blocks, *lands, token)


def _xchg_copies(ins, lands, send_sems, recv_sems, arrivals):
    x, y, c = _place()
    me = 2 * x + y
    cps = []
    for t in range(len(ins)):
        for j, (px, py) in enumerate(_other_chips(x, y)):
            landed = lands[t].at[2 * px + py]
            sem = dict(send_sem=send_sems.at[3 * t + j], recv_sem=recv_sems.at[3 * t + j], device_id=(px, py, c),
                       device_id_type=MESH)
            if arrivals:
                cps.append(pltpu.make_async_remote_copy(src_ref=landed, dst_ref=landed, **sem))
            else:
                cps.append(pltpu.make_async_remote_copy(src_ref=ins[t].at[2 * px + py], dst_ref=lands[t].at[me], **sem))
    return cps


def xchg_start(parts, *, name):
    return _split_start(parts, [jax.ShapeDtypeStruct(p.shape, p.dtype) for p in parts], _xchg_copies, name=name)


def xchg_wait(started, after, *, name):
    send_sems, recv_sems, srcs, lands, _ = started
    return _split_wait(send_sems, recv_sems, srcs, lands, after, _xchg_copies, name=name)


def xchg_finish(parts, lands, *, name):
    n = len(parts)

    def body(*refs):
        ins, outs = refs[:n], refs[2 * n:3 * n]
        stage_in, stage_out = refs[3 * n:3 * n + 2]
        bufs = refs[3 * n + 2:]
        x, y, _ = _place()
        me = 2 * x + y
        for t in range(n):
            _staged_copy(ins[t].at[me], outs[t].at[me], bufs[t], stage_in, stage_out, t)

    return pl.pallas_call(
        body, name=name, out_shape=[jax.ShapeDtypeStruct(v.shape, v.dtype) for v in lands],
        in_specs=[ANY] * (2 * n), out_specs=[ANY] * n, input_output_aliases={n + i: i for i in range(n)},
        scratch_shapes=[pltpu.SemaphoreType.DMA((2 * n,))] * 2
        + [_stage_shape(p.shape[1], p.shape[2], p.dtype) for p in parts],
        compiler_params=pltpu.CompilerParams(has_side_effects=True, vmem_limit_bytes=VMEM_LIMIT))(*parts, *lands)


def pair_split(grads, *, name):
    n = len(grads)

    def body(*refs):
        ins, recv = refs[:n], refs[n:2 * n]
        send_sems, recv_sems = refs[2 * n:]
        x, y, c = _place()
        sibling = (x, y, 1 - c)
        cps = []
        for t in range(n):
            h = ins[t].shape[1] // 2
            cp = pltpu.make_async_remote_copy(src_ref=ins[t].at[:, pl.ds((1 - c) * h, h)], dst_ref=recv[t],
                                              send_sem=send_sems.at[t], recv_sem=recv_sems.at[t], device_id=sibling,
                                              device_id_type=MESH)
            cp.start()
            cps.append(cp)
        for cp in cps:
            cp.wait()

    half = [jax.ShapeDtypeStruct((g.shape[0], g.shape[1] // 2, g.shape[2]), g.dtype) for g in grads]
    return pl.pallas_call(
        body, name=name, out_shape=half, in_specs=[ANY] * n, out_specs=[ANY] * n,
        scratch_shapes=[pltpu.SemaphoreType.DMA((n,))] * 2,
        compiler_params=pltpu.CompilerParams(has_side_effects=True))(*grads)


def chip_exchange(parts, *, name):
    n = len(parts)

    def body(*refs):
        ins, outs = refs[:n], refs[n:2 * n]
        send_sems, recv_sems, stage_in, stage_out = refs[2 * n:2 * n + 4]
        bufs = refs[2 * n + 4:]
        x, y, c = _place()
        me = 2 * x + y
        chips = _other_chips(x, y)
        cps = []
        for t in range(n):
            for j, (px, py) in enumerate(chips):
                cp = pltpu.make_async_remote_copy(src_ref=ins[t].at[2 * px + py], dst_ref=outs[t].at[me],
                                                  send_sem=send_sems.at[3 * t + j], recv_sem=recv_sems.at[3 * t + j],
                                                  device_id=(px, py, c), device_id_type=MESH)
                cp.start()
                cps.append(cp)
        for t in range(n):
            _staged_copy(ins[t].at[me], outs[t].at[me], bufs[t], stage_in, stage_out, t)
        for t in range(n):
            for j, (px, py) in enumerate(chips):
                landed = outs[t].at[2 * px + py]
                pltpu.make_async_remote_copy(src_ref=landed, dst_ref=landed, send_sem=send_sems.at[3 * t + j],
                                             recv_sem=recv_sems.at[3 * t + j], device_id=(px, py, c),
                                             device_id_type=MESH).wait_recv()
        for cp in cps:
            cp.wait_send()

    return pl.pallas_call(
        body, name=name, out_shape=[jax.ShapeDtypeStruct(p.shape, p.dtype) for p in parts],
        in_specs=[ANY] * n, out_specs=[ANY] * n,
        scratch_shapes=[pltpu.SemaphoreType.DMA((3 * n,))] * 2 + [pltpu.SemaphoreType.DMA((2 * n,))] * 2
        + [_stage_shape(p.shape[1], p.shape[2], p.dtype) for p in parts],
        compiler_params=pltpu.CompilerParams(has_side_effects=True, vmem_limit_bytes=VMEM_LIMIT))(*parts)


def pair_join(halves, *, name):
    n = len(halves)

    def body(*refs):
        ins, outs = refs[:n], refs[n:2 * n]
        send_sems, recv_sems, stage_in, stage_out = refs[2 * n:2 * n + 4]
        bufs = refs[2 * n + 4:]
        x, y, c = _place()
        sibling = (x, y, 1 - c)
        cps = []
        for t in range(n):
            h = ins[t].shape[0]
            cp = pltpu.make_async_remote_copy(src_ref=ins[t], dst_ref=outs[t].at[pl.ds(c * h, h)],
                                              send_sem=send_sems.at[t], recv_sem=recv_sems.at[t], device_id=sibling,
                                              device_id_type=MESH)
            cp.start()
            cps.append(cp)
        for t in range(n):
            h = ins[t].shape[0]
            _staged_copy(ins[t], outs[t].at[pl.ds(c * h, h)], bufs[t], stage_in, stage_out, t)
        for t in range(n):
            h = ins[t].shape[0]
            theirs = outs[t].at[pl.ds((1 - c) * h, h)]
            pltpu.make_async_remote_copy(src_ref=theirs, dst_ref=theirs, send_sem=send_sems.at[t],
                                         recv_sem=recv_sems.at[t], device_id=sibling, device_id_type=MESH).wait_recv()
        for cp in cps:
            cp.wait_send()

    return pl.pallas_call(
        body, name=name, out_shape=[jax.ShapeDtypeStruct((2 * p.shape[0], p.shape[1]), p.dtype) for p in halves],
        in_specs=[ANY] * n, out_specs=[ANY] * n,
        scratch_shapes=[pltpu.SemaphoreType.DMA((n,))] * 2 + [pltpu.SemaphoreType.DMA((2 * n,))] * 2
        + [_stage_shape(p.shape[0], p.shape[1], p.dtype) for p in halves],
        compiler_params=pltpu.CompilerParams(has_side_effects=True, vmem_limit_bytes=VMEM_LIMIT))(*halves)


def pair_join_stacked(halves, stacks, layer, *, name):
    n = len(halves)

    def body(*refs):
        ins, outs = refs[:n], refs[2 * n:3 * n]
        send_sems, recv_sems, stage_in, stage_out = refs[3 * n:3 * n + 4]
        bufs = refs[3 * n + 4:]
        x, y, c = _place()
        sibling = (x, y, 1 - c)
        cps = []
        for t in range(n):
            h = ins[t].shape[0]
            cp = pltpu.make_async_remote_copy(src_ref=ins[t], dst_ref=outs[t].at[layer, pl.ds(c * h, h)],
                                              send_sem=send_sems.at[t], recv_sem=recv_sems.at[t], device_id=sibling,
                                              device_id_type=MESH)
            cp.start()
            cps.append(cp)
        for t in range(n):
            h = ins[t].shape[0]
            _staged_copy(ins[t], outs[t].at[layer, pl.ds(c * h, h)], bufs[t], stage_in, stage_out, t)
        for t in range(n):
            h = ins[t].shape[0]
            theirs = outs[t].at[layer, pl.ds((1 - c) * h, h)]
            pltpu.make_async_remote_copy(src_ref=theirs, dst_ref=theirs, send_sem=send_sems.at[t],
                                         recv_sem=recv_sems.at[t], device_id=sibling, device_id_type=MESH).wait_recv()
        for cp in cps:
            cp.wait_send()

    return pl.pallas_call(
        body, name=name, out_shape=[jax.ShapeDtypeStruct(s.shape, s.dtype) for s in stacks],
        in_specs=[ANY] * (2 * n), out_specs=[ANY] * n, input_output_aliases={n + i: i for i in range(n)},
        scratch_shapes=[pltpu.SemaphoreType.DMA((n,))] * 2 + [pltpu.SemaphoreType.DMA((2 * n,))] * 2
        + [_stage_shape(p.shape[0], p.shape[1], p.dtype) for p in halves],
        compiler_params=pltpu.CompilerParams(has_side_effects=True, vmem_limit_bytes=VMEM_LIMIT))(*halves, *stacks)


def reduce_scatter(grads, pay_dtype, *, name):
    recv = pair_split(grads, name=name + "_split")
    part = add_pairs(grads, recv, pay_dtype, name=name + "_add2")
    got = chip_exchange(part, name=name + "_xchg")
    half = add_chips(got, name=name + "_add4")
    return pair_join(half, name=name + "_join")


def _in_sizes(d, w, n_heads):
    return (w, w, w, w, w, n_heads, n_heads, d, d)


def _wcat_from_gathered(wg, d, w, n_heads):
    full = jnp.concatenate([wg[s] for s in range(N_CHIPS)], axis=1)
    sizes = _in_sizes(d, w, n_heads)
    offs = [0]
    for s in sizes:
        offs.append(offs[-1] + s)
    pieces = [full[:, offs[i]:offs[i + 1]] for i in range(len(sizes))]
    u, q, k, v, z, beta, a, gs, gd = pieces
    pad = jnp.zeros((full.shape[0], SMALL_W - 2 * n_heads), full.dtype)
    return jnp.concatenate([gs, gd, u, q, k, v, z, beta, a, pad], axis=1)


def _wcat_grad_to_shards(dwcat, d, w, n_heads):
    gs, gd = dwcat[:, :d], dwcat[:, d:2 * d]
    o = 2 * d
    u, q, k, v, z = [dwcat[:, o + i * w:o + (i + 1) * w] for i in range(5)]
    o += 5 * w
    beta, a = dwcat[:, o:o + n_heads], dwcat[:, o + n_heads:o + 2 * n_heads]
    full = jnp.concatenate([u, q, k, v, z, beta, a, gs, gd], axis=1)
    return jnp.stack(jnp.split(full, N_CHIPS, axis=1))


def _s5_discretize(a_re, a_im, log_dt, b_re, b_im):
    dt = jnp.exp(log_dt)[:, None]
    mag = jnp.exp(a_re * dt)
    lbar_r, lbar_i = mag * jnp.cos(a_im * dt), mag * jnp.sin(a_im * dt)
    den = a_re * a_re + a_im * a_im
    zr, zi = _cmul(lbar_r - 1.0, lbar_i, a_re / den, -a_im / den)
    bbar_r, bbar_i = _cmul(zr[:, :, None], zi[:, :, None], b_re, b_im)
    return lbar_r, lbar_i, bbar_r, bbar_i


def _blockdiag_in(bbar):
    g, p, h = bbar.shape
    t = g // GROUPS_PER_TILE
    bb = bbar.reshape(t, GROUPS_PER_TILE, p, h).transpose(0, 1, 3, 2)
    eye = jnp.eye(GROUPS_PER_TILE, dtype=bbar.dtype)
    return jnp.einsum('tjhp,jk->tjhkp', bb, eye).reshape(t, GROUPS_PER_TILE * h, GROUPS_PER_TILE * p)


def _blockdiag_in_grad(dblk, g, p, h):
    t = g // GROUPS_PER_TILE
    d5 = dblk.reshape(t, GROUPS_PER_TILE, h, GROUPS_PER_TILE, p)
    eye = jnp.eye(GROUPS_PER_TILE, dtype=dblk.dtype)
    diag = jnp.einsum('tjhkp,jk->tjhp', d5, eye)
    return diag.transpose(0, 1, 3, 2).reshape(g, p, h)


def _blockdiag_out(cmat):
    g, h, p = cmat.shape
    t = g // GROUPS_PER_TILE
    cc = cmat.reshape(t, GROUPS_PER_TILE, h, p).transpose(0, 1, 3, 2)
    eye = jnp.eye(GROUPS_PER_TILE, dtype=cmat.dtype)
    return jnp.einsum('tjph,jk->tjpkh', cc, eye).reshape(t, GROUPS_PER_TILE * p, GROUPS_PER_TILE * h)


def _blockdiag_out_grad(dblk, g, h, p):
    t = g // GROUPS_PER_TILE
    d5 = dblk.reshape(t, GROUPS_PER_TILE, p, GROUPS_PER_TILE, h)
    eye = jnp.eye(GROUPS_PER_TILE, dtype=dblk.dtype)
    diag = jnp.einsum('tjpkh,jk->tjph', d5, eye)
    return diag.transpose(0, 1, 3, 2).reshape(g, h, p)


def _pad_row(v, width):
    return jnp.pad(v.reshape(1, -1), ((0, 0), (0, width - v.size)))


def _pack(arrs, rows_mult):
    flat = jnp.concatenate([a.reshape(-1) for a in arrs])
    per = 128 * rows_mult
    total = -(-flat.size // per) * per
    return jnp.pad(flat, (0, total - flat.size))


def _unpack(flat, like):
    out, o = [], 0
    for a in like:
        out.append(flat[o:o + a.size].reshape(a.shape))
        o += a.size
    return out


def kernel(x, ffn1_w_gu, ffn1_w_down, ln1_g, ln1_b, w_in, conv_w, ssm_a_re, ssm_a_im, ssm_log_dt, ssm_b_re, ssm_b_im, ssm_c_re, ssm_c_im, ssm_d, glu_w, glu_b, gdn_a_log, gdn_dt_bias, gdn_norm_w, w_br_ssm, w_br_gdn, w_out, ln2_g, ln2_b, ffn2_w_gu, ffn2_w_down, ln3_g, ln3_b, loss_target, m_ffn1_w_gu, m_ffn1_w_down, m_ln1_g, m_ln1_b, m_w_in, m_conv_w, m_ssm_a_re, m_ssm_a_im, m_ssm_log_dt, m_ssm_b_re, m_ssm_b_im, m_ssm_c_re, m_ssm_c_im, m_ssm_d, m_glu_w, m_glu_b, m_gdn_a_log, m_gdn_dt_bias, m_gdn_norm_w, m_w_br_ssm, m_w_br_gdn, m_w_out, m_ln2_g, m_ln2_b, m_ffn2_w_gu, m_ffn2_w_down, m_ln3_g, m_ln3_b, v_ffn1_w_gu, v_ffn1_w_down, v_ln1_g, v_ln1_b, v_w_in, v_conv_w, v_ssm_a_re, v_ssm_a_im, v_ssm_log_dt, v_ssm_b_re, v_ssm_b_im, v_ssm_c_re, v_ssm_c_im, v_ssm_d, v_glu_w, v_glu_b, v_gdn_a_log, v_gdn_dt_bias, v_gdn_norm_w, v_w_br_ssm, v_w_br_gdn, v_w_out, v_ln2_g, v_ln2_b, v_ffn2_w_gu, v_ffn2_w_down, v_ln3_g, v_ln3_b):
    args = locals()
    wts = {n: args[n] for n in WEIGHT_NAMES}
    mom = {n: args["m_" + n] for n in WEIGHT_NAMES}
    var = {n: args["v_" + n] for n in WEIGHT_NAMES}

    depth = ln1_g.shape[0]
    n_tok, d = x.shape[1], x.shape[2]
    w = glu_w.shape[-1]
    n_heads = gdn_a_log.shape[-1]
    n_groups, n_state, grp = ssm_b_re.shape[1], ssm_b_re.shape[2], ssm_b_re.shape[3]
    alpha = (2.0 * depth) ** 0.25
    u_col0 = 2 * d
    qkv_col0 = u_col0 + w
    z_col0 = u_col0 + 4 * w
    small_col0 = u_col0 + 5 * w
    x_idx, y_idx, _ = _place()
    chip = 2 * x_idx + y_idx

    xcur = x[0]
    xcur_b = xcur.astype(BF16)
    saved = []
    halve = [True] * len(BIG) + [False]

    def start_gather(layer, order_token):
        shards = [wts[n][layer].astype(BF16) for n in BIG] + [conv_w[layer] + order_token[0, 0]]
        return gather_start(shards, halve, name=f"gather_start_l{layer}")

    started = start_gather(0, jnp.zeros((1, 1), F32))
    after = xcur
    for l in range(depth):
        shards, lands, waited = gather_wait(started, halve, after, name=f"gather_wait_l{l}")
        started = start_gather(l + 1, waited) if l + 1 < depth else started
        gathered = gather_finish(shards, lands, halve, started[4], name="gather_finish")
        gw = dict(zip(BIG, gathered[:-1]))
        conv_full = jnp.concatenate([gathered[-1][s] for s in range(N_CHIPS)], axis=1)
        wgu1, wgu2 = gw['ffn1_w_gu'], gw['ffn2_w_gu']
        wd1 = gw['ffn1_w_down'].reshape(-1, d)
        wd2 = gw['ffn2_w_down'].reshape(-1, d)
        wcat = _wcat_from_gathered(gw['w_in'], d, w, n_heads)
        wglu = gw['glu_w'].reshape(w, w)
        wbs, wbd = gw['w_br_ssm'], gw['w_br_gdn']
        wout = gw['w_out'].reshape(d, d)
        f = wd1.shape[0]

        vec = lambda v: v[l].reshape(1, -1)
        x0, x0b = xcur, xcur_b
        h1 = mm(x0b, wgu1, b_nb=N_CHIPS, out_nb=2, name=f"ffn_up")
        r1, x1, x1b = down_res_ln(h1, wd1, x0, vec(ln1_g), vec(ln1_b), swiglu=True, alpha=alpha, scale=0.5,
                                  name="ffn_down")
        proj = mm(x1b, wcat, name="in_proj")
        (lbar_r, lbar_i, bbar_r, bbar_i), disc_vjp = jax.vjp(
            _s5_discretize, ssm_a_re[l], ssm_a_im[l], ssm_log_dt[l], ssm_b_re[l], ssm_b_im[l])
        s5w = (_blockdiag_in(bbar_r), _blockdiag_in(bbar_i), _blockdiag_out(ssm_c_re[l]), _blockdiag_out(ssm_c_im[l]),
               lbar_r.reshape(1, -1), lbar_i.reshape(1, -1), ssm_d[l].reshape(1, -1))
        ypre, y2, y2b = s5_fwd(proj, u_col0, *s5w, name="s5_fwd")
        gl = mm(y2b, wglu, name="glu_proj")
        ys = glu_gate_fwd(y2, gl, vec(glu_b), name="glu_gate")
        qkv = conv_fwd(proj, qkv_col0, conv_full, name="conv_fwd")
        gdn_rows = (_pad_row(gdn_a_log[l], HEAD_DIM), _pad_row(gdn_dt_bias[l], HEAD_DIM), gdn_norm_w[l].reshape(1, -1))
        yg, states = gdn_fwd(qkv, proj, z_col0, small_col0, *gdn_rows, n_heads, name="gdn_fwd")
        bs = mm(ys, wbs, b_nb=N_CHIPS, name="br_ssm")
        bd = mm(yg, wbd, b_nb=N_CHIPS, name="br_gdn")
        mrg = merge_fwd(proj, bs, bd, name="merge")
        r2, x2, x2b = down_res_ln(mrg, wout, x1, vec(ln2_g), vec(ln2_b), swiglu=False, alpha=alpha, scale=1.0,
                                  name="mix_out")
        h3 = mm(x2b, wgu2, b_nb=N_CHIPS, out_nb=2, name="ffn_up")
        r3, x3, x3b = down_res_ln(h3, wd2, x2, vec(ln3_g), vec(ln3_b), swiglu=True, alpha=alpha, scale=0.5,
                                  name="ffn_down")
        saved.append(dict(x0b=x0b, h1=h1, r1=r1, x1b=x1b, proj=proj, s5w=s5w, disc_vjp=disc_vjp, ypre=ypre, y2=y2,
                          y2b=y2b, gl=gl, ys=ys, qkv=qkv, gdn_rows=gdn_rows, yg=yg, states=states, bs=bs, bd=bd, mrg=mrg,
                          r2=r2, x2b=x2b, h3=h3, r3=r3, wgu1=wgu1, wgu2=wgu2, wd1=wd1, wd2=wd2, wcat=wcat, wglu=wglu,
                          wbs=wbs, wbd=wbd, wout=wout, conv_full=conv_full))
        xcur, xcur_b = x3, x3b
        after = x3

    dy, loss_blk = loss_head(xcur, loss_target[0], name="loss_head")
    loss = lax.psum(loss_blk[0, 0], ("x", "y", "c"))

    small_grads = {n: [None] * depth for n in SMALL}
    group_b = ['ffn1_w_gu', 'ffn1_w_down']
    group_a = [n for n in BIG if n not in group_b]
    grad_bufs = {n: lax.empty(wts[n].shape, F32) for n in BIG}

    def start_reduction(names, grads_by_name, tag, layer):
        gl = [grads_by_name[n] for n in names]
        part = add_pairs(gl, pair_split(gl, name="rs_split_" + tag), BF16, name="rs_add2_" + tag)
        return names, layer, tag, xchg_start(part, name=f"xchg_start_{tag}_l{layer}")

    def finish_reduction(item, after_arr):
        names, layer, tag, exchange = item
        parts, lands, _ = xchg_wait(exchange, after_arr, name=f"xchg_wait_{tag}_l{layer}")
        half = add_chips(xchg_finish(parts, lands, name="xchg_finish_" + tag), name="rs_add4_" + tag)
        joined = pair_join_stacked(half, [grad_bufs[n] for n in names], layer, name="rs_join_" + tag)
        for n, b in zip(names, joined):
            grad_bufs[n] = b

    pending = []
    for l in reversed(range(depth)):
        s = saved[l]
        order = pending[-1][3][4][0, 0] if pending else 0.0
        vec = lambda v: v[l].reshape(1, -1)

        def ffn_back(dx_out, r, g_ln, xin, hh, wgu, wd):
            dr, drb, dg, db = ln_bwd(r, g_ln, dx_out, name="ln_bwd")
            dh, act = ffn_bwd_mid(dr, wd, hh, scale=0.5, name="ffn_bwd_mid")
            dwd = mm(act, drb, ta=True, out_dtype=BF16, out_scale=0.5, name="ffn_dwd")
            dwgu = mm(xin, dh, ta=True, b_nb=2, out_nb=N_CHIPS, out_dtype=BF16, name="ffn_dwgu")
            dxin = mm(dh, wgu, tb=True, a_nb=2, b_nb=N_CHIPS, add=dr, add_scale=alpha, name="ffn_dx")
            return dxin, dg, db, dwgu, dwd.reshape(N_CHIPS, -1, d)

        dx2, dg3, db3, dwgu2, dwd2 = ffn_back(dy, s['r3'], vec(ln3_g) + order, s['x2b'], s['h3'], s['wgu2'], s['wd2'])
        dr2, dr2b, dg2, db2 = ln_bwd(s['r2'], vec(ln2_g), dx2, name="ln_bwd")
        dmrg = mm(dr2b, s['wout'], tb=True, name="mix_dm")
        dwout = mm(s['mrg'], dr2b, ta=True, out_dtype=BF16, name="mix_dwout").reshape(N_CHIPS, -1, d)
        dbs, dbd, dgs, dgd = merge_bwd(s['proj'], s['bs'], s['bd'], dmrg, name="merge_bwd")
        dwbs = mm(s['ys'], dbs, ta=True, out_nb=N_CHIPS, out_dtype=BF16, name="br_dw")
        dwbd = mm(s['yg'], dbd, ta=True, out_nb=N_CHIPS, out_dtype=BF16, name="br_dw")
        dys = mm(dbs, s['wbs'], tb=True, b_nb=N_CHIPS, name="br_dx")
        dyg = mm(dbd, s['wbd'], tb=True, b_nb=N_CHIPS, name="br_dx")
        dy2a, dgl, dglu_b = glu_gate_bwd(s['y2'], s['gl'], vec(glu_b), dys, name="glu_gate_bwd")
        dwglu = mm(s['y2b'], dgl, ta=True, out_dtype=BF16, name="glu_dw").reshape(N_CHIPS, -1, w)
        dy2 = mm(dgl, s['wglu'], tb=True, add=dy2a, name="glu_dx")
        du, dbr, dbi, dcr, dci, dlr, dli, dd = s5_bwd(s['proj'], u_col0, s['ypre'], dy2, *s['s5w'], name="s5_bwd")
        dqkv_act, dz, dsmall, dalog, ddtb, dnw = gdn_bwd(s['qkv'], s['proj'], z_col0, small_col0, *s['gdn_rows'],
                                                           s['states'], dyg, n_heads, name="gdn_bwd")
        dqkv, dconv = conv_bwd(s['proj'], qkv_col0, s['conv_full'], dqkv_act, name="conv_bwd")
        dsmall_w = jnp.pad(dsmall, ((0, 0), (0, SMALL_W - HEAD_DIM)))
        dproj = jnp.concatenate([dgs, dgd, du, dqkv, dz, dsmall_w], axis=1)
        dwcat = mm(s['x1b'], dproj, ta=True, out_dtype=BF16, name="in_dw")
        dx1 = mm(dproj, s['wcat'], tb=True, add=dr2, add_scale=alpha, name="in_dx")
        item_a = start_reduction(group_a, dict(w_in=_wcat_grad_to_shards(dwcat, d, w, n_heads), glu_w=dwglu,
                                               w_br_ssm=dwbs, w_br_gdn=dwbd, w_out=dwout, ffn2_w_gu=dwgu2,
                                               ffn2_w_down=dwd2), "a", l)
        dx0, dg1, db1, dwgu1, dwd1 = ffn_back(dx1, s['r1'], vec(ln1_g) + item_a[3][4][0, 0], s['x0b'], s['h1'],
                                              s['wgu1'], s['wd1'])
        dy = dx0

        da_re, da_im, dlog_dt, db_re, db_im = s['disc_vjp'](
            (dlr.reshape(n_groups, n_state), dli.reshape(n_groups, n_state),
             _blockdiag_in_grad(dbr, n_groups, n_state, grp), _blockdiag_in_grad(dbi, n_groups, n_state, grp)))
        sg = dict(ln1_g=dg1, ln1_b=db1, ln2_g=dg2, ln2_b=db2, ln3_g=dg3, ln3_b=db3, conv_w=dconv,
                  ssm_a_re=da_re, ssm_a_im=da_im, ssm_log_dt=dlog_dt, ssm_b_re=db_re, ssm_b_im=db_im,
                  ssm_c_re=_blockdiag_out_grad(dcr, n_groups, grp, n_state),
                  ssm_c_im=_blockdiag_out_grad(dci, n_groups, grp, n_state), ssm_d=dd, glu_b=dglu_b,
                  gdn_a_log=dalog[0, :n_heads], gdn_dt_bias=ddtb[0, :n_heads], gdn_norm_w=dnw)
        for n in SMALL:
            small_grads[n][l] = sg[n].reshape(-1)
        for item in pending:
            finish_reduction(item, dy)
        pending = [item_a, start_reduction(group_b, dict(ffn1_w_gu=dwgu1, ffn1_w_down=dwd1), "b", l)]
    grad_x = dy[None]

    small_list = [jnp.stack(small_grads[n]) for n in SMALL]
    packed = _pack(small_list, 512 * N_CHIPS).reshape(N_CHIPS, -1, 128)
    red = reduce_scatter([packed], F32, name="rs_small")
    full = gather_chips(red, [True], name="gather_small")[0].reshape(-1)
    for item in pending:
        finish_reduction(item, full)
    small_red = dict(zip(SMALL, _unpack(full, small_list)))
    cw_cols = conv_w.shape[-1]
    dconv_full = small_red['conv_w'].reshape(depth, CONV_K, N_CHIPS, cw_cols)
    small_red['conv_w'] = lax.dynamic_index_in_dim(dconv_full, chip, axis=2, keepdims=False)

    grads = {}
    for n in BIG:
        grads[n] = grad_bufs[n]
    for n in SMALL:
        grads[n] = small_red[n].reshape(wts[n].shape)

    delta, new_m, new_v = {}, {}, {}
    for n in BIG:
        grads[n], delta[n], new_m[n], new_v[n] = adamw(wts[n], grads[n], mom[n], var[n], name="adamw_big",
                                                       pass_grad=True)
    for n in SMALL:
        delta[n], new_m[n], new_v[n] = adamw(wts[n], grads[n], mom[n], var[n], name="adamw_small")

    return (loss, grad_x, *[grads[n] for n in WEIGHT_NAMES], *[delta[n] for n in WEIGHT_NAMES],
            *[new_m[n] for n in WEIGHT_NAMES], *[new_v[n] for n in WEIGHT_NAMES])
```

```python
import math

import jax
import jax.numpy as jnp
from jax import lax
from jax.experimental import pallas as pl
from jax.experimental.pallas import tpu as pltpu

F32 = jnp.float32
BF16 = jnp.bfloat16
HI = lax.Precision.HIGHEST
MESH = pl.DeviceIdType.MESH

N_CHIPS = 4
SSM_GROUP = 16
SSM_STATE = 64
GROUPS_PER_TILE = 8
HEAD_DIM = 128
CHUNK = 64
CONV_K = 4
LN_EPS = 1e-5
RMS_EPS = 1e-6
L2_EPS = 1e-6
SMALL_W = 512
ADAM_LR, ADAM_B1, ADAM_B2, ADAM_EPS, ADAM_WD, ADAM_STEP = 0.001, 0.9, 0.999, 1e-08, 0.01, 10
VMEM_LIMIT = 56 * 1024 * 1024

WEIGHT_NAMES = ['ffn1_w_gu', 'ffn1_w_down', 'ln1_g', 'ln1_b', 'w_in', 'conv_w', 'ssm_a_re', 'ssm_a_im', 'ssm_log_dt',
                'ssm_b_re', 'ssm_b_im', 'ssm_c_re', 'ssm_c_im', 'ssm_d', 'glu_w', 'glu_b', 'gdn_a_log', 'gdn_dt_bias',
                'gdn_norm_w', 'w_br_ssm', 'w_br_gdn', 'w_out', 'ln2_g', 'ln2_b', 'ffn2_w_gu', 'ffn2_w_down', 'ln3_g',
                'ln3_b']
BIG = ['ffn1_w_gu', 'ffn1_w_down', 'w_in', 'glu_w', 'w_br_ssm', 'w_br_gdn', 'w_out', 'ffn2_w_gu', 'ffn2_w_down']
SMALL = [n for n in WEIGHT_NAMES if n not in BIG]


def _tile(dim, prefs):
    for p in prefs:
        if dim % p == 0:
            return p
    return dim


def _cparams(sem):
    return pltpu.CompilerParams(dimension_semantics=sem, vmem_limit_bytes=VMEM_LIMIT)


def _ln(r, g, b):
    mu = jnp.mean(r, axis=-1, keepdims=True)
    xc = r - mu
    var = jnp.mean(xc * xc, axis=-1, keepdims=True)
    return xc * lax.rsqrt(var + LN_EPS) * g + b


def _lshape(x, nb):
    return (x.shape[0], x.shape[1]) if nb == 1 else (x.shape[1], x.shape[2] * nb)


def _cb_spec(x, nb, tr, tc, rc):
    if nb == 1:
        return pl.BlockSpec((tr, tc), lambda *g: rc(*g))
    cps = x.shape[2] // tc

    def imap(*g):
        r, c = rc(*g)
        return (c // cps, r, c % cps)
    return pl.BlockSpec((None, tr, tc), imap)


MM_VMEM_BUDGET = 40 * 1024 * 1024
MM_TILES = (2048, 1024, 512, 256, 128)
MM_FULL_K = 2048


def mm(a, b, *, name, ta=False, tb=False, a_nb=1, b_nb=1, out_nb=1, out_dtype=F32, add=None, add_scale=1.0,
       out_scale=1.0):
    ar, ac = _lshape(a, a_nb)
    br, bc = _lshape(b, b_nb)
    m, k = (ac, ar) if ta else (ar, ac)
    k2, n = (bc, br) if tb else (br, bc)
    assert k == k2, (name, a.shape, b.shape)
    assert a.dtype == BF16 and b.dtype == BF16, name

    def lim(dim, *nbs):
        q = dim
        for nb in nbs:
            q = math.gcd(q, dim // nb)
        return q
    lm = lim(m, a_nb if ta else 1)
    ln = lim(n, out_nb, 1 if tb else b_nb)
    lk = lim(k, 1 if ta else a_nb, b_nb if tb else 1)
    so = jnp.dtype(out_dtype).itemsize
    if k <= MM_FULL_K and lk == k:
        tks = [k]
    else:
        tks = [t for t in range(MM_FULL_K, 127, -128) if lk % t == 0]
    best = None
    for ck in tks:
        for cm in MM_TILES:
            for cn in MM_TILES:
                if lm % cm or ln % cn:
                    continue
                est = 2 * (cm * ck * 2 + ck * cn * 2 + cm * cn * so + (cm * cn * 4 if add is not None else 0))
                est += cm * cn * 4 * (2 if k > ck else 1) + (cm * ck * 2 + 512 * cm * 4 if ta else 0)
                score = min(cm, 512) * cn * ck
                if est <= MM_VMEM_BUDGET and (best is None or score > best[0]):
                    best = (score, cm, cn, ck)
    _, tm, tn, tk = best
    nk = k // tk
    dn = (((1,), (1 if tb else 0,)), ((), ()))

    def body(*refs):
        a_ref, b_ref = refs[:2]
        add_ref = refs[2] if add is not None else None
        o_ref = refs[3 if add is not None else 2]
        scratch = refs[(4 if add is not None else 3):]
        acc = scratch[0] if nk > 1 else None
        kk = pl.program_id(2)

        if ta:
            at_ref = scratch[-1]

            def transpose_block():
                for r0 in range(0, tk, 512):
                    r1 = min(tk, r0 + 512)
                    at_ref[:, r0:r1] = a_ref[r0:r1, :].astype(F32).T.astype(BF16)
            if nk == 1:
                pl.when(pl.program_id(1) == 0)(transpose_block)
            else:
                transpose_block()
            av = at_ref[...]
        else:
            av = a_ref[...]
        part = lax.dot_general(av, b_ref[...], dn, preferred_element_type=F32)

        def finish(r):
            if out_scale != 1.0:
                r = r * out_scale
            if add is not None:
                r = r + add_scale * add_ref[...]
            o_ref[...] = r.astype(out_dtype)

        if nk == 1:
            finish(part)
        else:
            @pl.when(kk == 0)
            def _():
                acc[...] = part

            @pl.when(kk > 0)
            def _():
                acc[...] += part

            @pl.when(kk == nk - 1)
            def _():
                finish(acc[...])

    if ta:
        a_spec = _cb_spec(a, a_nb, tk, tm, lambda i, j, kk: (kk, i))
    else:
        a_spec = _cb_spec(a, a_nb, tm, tk, lambda i, j, kk: (i, kk))
    if tb:
        b_spec = _cb_spec(b, b_nb, tn, tk, lambda i, j, kk: (j, kk))
    else:
        b_spec = _cb_spec(b, b_nb, tk, tn, lambda i, j, kk: (kk, j))
    if out_nb == 1:
        out_shape = jax.ShapeDtypeStruct((m, n), out_dtype)
        out_spec = pl.BlockSpec((tm, tn), lambda i, j, kk: (i, j))
    else:
        out_shape = jax.ShapeDtypeStruct((out_nb, m, n // out_nb), out_dtype)
        out_spec = _cb_spec(out_shape, out_nb, tm, tn, lambda i, j, kk: (i, j))
    in_specs = [a_spec, b_spec]
    args = [a, b]
    if add is not None:
        in_specs.append(pl.BlockSpec((tm, tn), lambda i, j, kk: (i, j)))
        args.append(add)
    scratch = ([pltpu.VMEM((tm, tn), F32)] if nk > 1 else []) + ([pltpu.VMEM((tm, tk), BF16)] if ta else [])
    return pl.pallas_call(
        body, name=name, out_shape=out_shape, grid=(m // tm, n // tn, nk), in_specs=in_specs, out_specs=out_spec,
        scratch_shapes=scratch, compiler_params=_cparams(("parallel", "arbitrary", "arbitrary")))(*args)


def down_res_ln(src, w, x, g, b, *, swiglu, alpha, scale, name):
    n_tok, d = x.shape
    kdim = w.shape[0]
    tm = _tile(n_tok, (512, 256, 128))
    tk = _tile(kdim, (512, 256, 128))
    nk = kdim // tk

    def body(s_ref, w_ref, x_ref, g_ref, b_ref, r_ref, y_ref, yb_ref, acc):
        kk = pl.program_id(1)

        @pl.when(kk == 0)
        def _():
            acc[...] = jnp.zeros_like(acc)

        if swiglu:
            gate = s_ref[0]
            a = gate * jax.nn.sigmoid(gate) * s_ref[1]
        else:
            a = s_ref[...]
        acc[...] += jnp.dot(a.astype(BF16), w_ref[...], preferred_element_type=F32)

        @pl.when(kk == nk - 1)
        def _():
            r = alpha * x_ref[...] + scale * acc[...]
            r_ref[...] = r
            y = _ln(r, g_ref[...], b_ref[...])
            y_ref[...] = y
            yb_ref[...] = y.astype(BF16)

    if swiglu:
        s_spec = pl.BlockSpec((2, tm, tk), lambda i, kk: (0, i, kk))
    else:
        s_spec = pl.BlockSpec((tm, tk), lambda i, kk: (i, kk))
    row = pl.BlockSpec((tm, d), lambda i, kk: (i, 0))
    vec = pl.BlockSpec((1, d), lambda i, kk: (0, 0))
    return pl.pallas_call(
        body, name=name, out_shape=[jax.ShapeDtypeStruct((n_tok, d), F32)] * 2 + [jax.ShapeDtypeStruct((n_tok, d), BF16)],
        grid=(n_tok // tm, nk),
        in_specs=[s_spec, pl.BlockSpec((tk, d), lambda i, kk: (kk, 0)), row, vec, vec], out_specs=[row, row, row],
        scratch_shapes=[pltpu.VMEM((tm, d), F32)], compiler_params=_cparams(("parallel", "arbitrary")))(src, w, x, g, b)


def ln_bwd(r, g, dy, *, name):
    n_tok, d = r.shape
    tm = _tile(n_tok, (256, 128))

    def body(r_ref, g_ref, dy_ref, dr_ref, drb_ref, dg_ref, db_ref):
        i = pl.program_id(0)

        @pl.when(i == 0)
        def _():
            dg_ref[...] = jnp.zeros_like(dg_ref)
            db_ref[...] = jnp.zeros_like(db_ref)

        rv = r_ref[...]
        dyv = dy_ref[...]
        mu = jnp.mean(rv, axis=-1, keepdims=True)
        xc = rv - mu
        rstd = lax.rsqrt(jnp.mean(xc * xc, axis=-1, keepdims=True) + LN_EPS)
        xh = xc * rstd
        dxh = dyv * g_ref[...]
        dr = rstd * (dxh - jnp.mean(dxh, axis=-1, keepdims=True) - xh * jnp.mean(dxh * xh, axis=-1, keepdims=True))
        dr_ref[...] = dr
        drb_ref[...] = dr.astype(BF16)
        dg_ref[...] += jnp.sum(dyv * xh, axis=0, keepdims=True)
        db_ref[...] += jnp.sum(dyv, axis=0, keepdims=True)

    row = pl.BlockSpec((tm, d), lambda i: (i, 0))
    vec = pl.BlockSpec((1, d), lambda i: (0, 0))
    return pl.pallas_call(
        body, name=name, out_shape=[jax.ShapeDtypeStruct((n_tok, d), F32), jax.ShapeDtypeStruct((n_tok, d), BF16),
                                    jax.ShapeDtypeStruct((1, d), F32), jax.ShapeDtypeStruct((1, d), F32)],
        grid=(n_tok // tm,), in_specs=[row, vec, row], out_specs=[row, row, vec, vec],
        compiler_params=_cparams(("arbitrary",)))(r, g, dy)


def loss_head(y, target, *, name):
    n_tok, d = y.shape
    tm = _tile(n_tok, (256, 128))

    def body(y_ref, t_ref, dy_ref, l_ref):
        i = pl.program_id(0)

        @pl.when(i == 0)
        def _():
            l_ref[...] = jnp.zeros_like(l_ref)

        e = y_ref[...] - t_ref[...]
        dy_ref[...] = e * (1.0 / d)
        s = jnp.sum(jnp.mean(e * e, axis=-1, keepdims=True), axis=0, keepdims=True)
        l_ref[...] += 0.5 * s

    row = pl.BlockSpec((tm, d), lambda i: (i, 0))
    return pl.pallas_call(
        body, name=name, out_shape=[jax.ShapeDtypeStruct((n_tok, d), F32), jax.ShapeDtypeStruct((8, 128), F32)],
        grid=(n_tok // tm,), in_specs=[row, row], out_specs=[row, pl.BlockSpec((8, 128), lambda i: (0, 0))],
        compiler_params=_cparams(("arbitrary",)))(y, target)


def ffn_bwd_mid(dr, wd, h, *, scale, name):
    n_tok, d = dr.shape
    f = wd.shape[0]
    tm = _tile(n_tok, (512, 256, 128))
    tf = _tile(f, (512, 256, 128))

    def body(dr_ref, w_ref, h_ref, dh_ref, a_ref):
        dy = (scale * dr_ref[...]).astype(BF16)
        da = lax.dot_general(dy, w_ref[...], (((1,), (1,)), ((), ())), preferred_element_type=F32)
        gate = h_ref[0]
        up = h_ref[1]
        sg = jax.nn.sigmoid(gate)
        s = gate * sg
        a_ref[...] = (s * up).astype(BF16)
        dh_ref[0] = (da * up * (sg * (1.0 + gate * (1.0 - sg)))).astype(BF16)
        dh_ref[1] = (da * s).astype(BF16)

    return pl.pallas_call(
        body, name=name, out_shape=[jax.ShapeDtypeStruct((2, n_tok, f), BF16), jax.ShapeDtypeStruct((n_tok, f), BF16)],
        grid=(n_tok // tm, f // tf),
        in_specs=[pl.BlockSpec((tm, d), lambda i, j: (i, 0)), pl.BlockSpec((tf, d), lambda i, j: (j, 0)),
                  pl.BlockSpec((2, tm, tf), lambda i, j: (0, i, j))],
        out_specs=[pl.BlockSpec((2, tm, tf), lambda i, j: (0, i, j)), pl.BlockSpec((tm, tf), lambda i, j: (i, j))],
        compiler_params=_cparams(("parallel", "parallel")))(dr, wd, h)


def _cmul(ar, ai, br, bi):
    return ar * br - ai * bi, ar * bi + ai * br


def _scan_blocks(sr_ref, si_ref, lr, li, *, reverse):
    n_rows, width = sr_ref.shape
    n_blk = n_rows // 8
    row = lax.broadcasted_iota(jnp.int32, (8, width), 0)
    pr = jnp.broadcast_to(lr, (8, width))
    pi = jnp.broadcast_to(-li if reverse else li, (8, width))

    def shifted(v, dist, fill=0.0):
        if reverse:
            return jnp.where(row < 8 - dist, pltpu.roll(v, 8 - dist, 0), fill)
        return jnp.where(row >= dist, pltpu.roll(v, dist, 0), fill)

    p1 = (pr, pi)
    p2 = _cmul(*p1, *p1)
    p4 = _cmul(*p2, *p2)
    wr, wi = pr, pi
    for dist in (1, 2, 4):
        wr, wi = _cmul(wr, wi, shifted(wr, dist, 1.0), shifted(wi, dist, 0.0))
    edge = 0 if reverse else 7

    def step(i, carry):
        cr, ci = carry
        blk = (n_blk - 1 - i) if reverse else i
        r0 = pl.multiple_of(blk * 8, 8)
        xr = sr_ref[pl.ds(r0, 8), :]
        xi = si_ref[pl.ds(r0, 8), :]
        for dist, (qr, qi) in ((1, p1), (2, p2), (4, p4)):
            tr, ti = _cmul(qr, qi, shifted(xr, dist), shifted(xi, dist))
            xr, xi = xr + tr, xi + ti
        tr, ti = _cmul(wr, wi, cr, ci)
        xr, xi = xr + tr, xi + ti
        sr_ref[pl.ds(r0, 8), :] = xr
        si_ref[pl.ds(r0, 8), :] = xi
        br = jnp.where(row == edge, xr, 0.0)
        bi = jnp.where(row == edge, xi, 0.0)
        for dist in (1, 2, 4):
            br = br + pltpu.roll(br, dist, 0)
            bi = bi + pltpu.roll(bi, dist, 0)
        return br, bi

    zero = jnp.zeros((8, width), F32)
    lax.fori_loop(0, n_blk, step, (zero, zero), unroll=2)


def _s5_specs(n_tok, u_blk0):
    gw = GROUPS_PER_TILE * SSM_GROUP
    sw = GROUPS_PER_TILE * SSM_STATE
    u_spec = pl.BlockSpec((n_tok, gw), lambda t: (0, u_blk0 + t))
    col = pl.BlockSpec((n_tok, gw), lambda t: (0, t))
    bmat = pl.BlockSpec((None, gw, sw), lambda t: (t, 0, 0))
    cmat = pl.BlockSpec((None, sw, gw), lambda t: (t, 0, 0))
    lvec = pl.BlockSpec((1, sw), lambda t: (0, t))
    dvec = pl.BlockSpec((1, gw), lambda t: (0, t))
    return gw, sw, u_spec, col, bmat, cmat, lvec, dvec


def s5_fwd(proj, u_col0, bblk_r, bblk_i, cblk_r, cblk_i, lbar_r, lbar_i, dskip, *, name):
    n_tok = proj.shape[0]
    n_tiles = bblk_r.shape[0]
    gw, sw, u_spec, col, bmat, cmat, lvec, dvec = _s5_specs(n_tok, u_col0 // (GROUPS_PER_TILE * SSM_GROUP))

    def body(u_ref, br_ref, bi_ref, cr_ref, ci_ref, lr_ref, li_ref, d_ref, ypre_ref, y2_ref, y2b_ref, sr, si):
        u = u_ref[...]
        ub = u.astype(BF16)
        sr[...] = jnp.dot(ub, br_ref[...].astype(BF16), preferred_element_type=F32)
        si[...] = jnp.dot(ub, bi_ref[...].astype(BF16), preferred_element_type=F32)
        _scan_blocks(sr, si, lr_ref[...], li_ref[...], reverse=False)
        y = (jnp.dot(sr[...].astype(BF16), cr_ref[...].astype(BF16), preferred_element_type=F32)
             - jnp.dot(si[...].astype(BF16), ci_ref[...].astype(BF16), preferred_element_type=F32)
             + d_ref[...] * u)
        ypre_ref[...] = y
        y2 = jax.nn.gelu(y)
        y2_ref[...] = y2
        y2b_ref[...] = y2.astype(BF16)

    width = n_tiles * gw
    return pl.pallas_call(
        body, name=name,
        out_shape=[jax.ShapeDtypeStruct((n_tok, width), F32)] * 2 + [jax.ShapeDtypeStruct((n_tok, width), BF16)],
        grid=(n_tiles,), in_specs=[u_spec, bmat, bmat, cmat, cmat, lvec, lvec, dvec], out_specs=[col, col, col],
        scratch_shapes=[pltpu.VMEM((n_tok, sw), F32)] * 2,
        compiler_params=_cparams(("parallel",)))(proj, bblk_r, bblk_i, cblk_r, cblk_i, lbar_r, lbar_i, dskip)


def s5_bwd(proj, u_col0, ypre, dy2, bblk_r, bblk_i, cblk_r, cblk_i, lbar_r, lbar_i, dskip, *, name):
    n_tok = proj.shape[0]
    n_tiles = bblk_r.shape[0]
    gw, sw, u_spec, col, bmat, cmat, lvec, dvec = _s5_specs(n_tok, u_col0 // (GROUPS_PER_TILE * SSM_GROUP))
    rb = _tile(n_tok, (512, 256, 128))
    tn_dims = (((0,), (0,)), ((), ()))
    nt_dims = (((1,), (1,)), ((), ()))

    def body(u_ref, ypre_ref, dy2_ref, br_ref, bi_ref, cr_ref, ci_ref, lr_ref, li_ref, d_ref,
             du_ref, dbr_ref, dbi_ref, dcr_ref, dci_ref, dlr_ref, dli_ref, dd_ref, sr, si, gr, gi):
        u = u_ref[...]
        ub = u.astype(BF16)
        bmr = br_ref[...].astype(BF16)
        bmi = bi_ref[...].astype(BF16)
        cmr = cr_ref[...].astype(BF16)
        cmi = ci_ref[...].astype(BF16)
        lr = lr_ref[...]
        li = li_ref[...]
        _, gelu_vjp = jax.vjp(jax.nn.gelu, ypre_ref[...])
        dyp = gelu_vjp(dy2_ref[...])[0]
        dyb = dyp.astype(BF16)
        sr[...] = jnp.dot(ub, bmr, preferred_element_type=F32)
        si[...] = jnp.dot(ub, bmi, preferred_element_type=F32)
        _scan_blocks(sr, si, lr, li, reverse=False)
        gr[...] = lax.dot_general(dyb, cmr, nt_dims, preferred_element_type=F32)
        gi[...] = -lax.dot_general(dyb, cmi, nt_dims, preferred_element_type=F32)
        _scan_blocks(gr, gi, lr, li, reverse=True)
        srb = sr[...].astype(BF16)
        sib = si[...].astype(BF16)
        dcr_ref[...] = lax.dot_general(srb, dyb, tn_dims, preferred_element_type=F32)
        dci_ref[...] = -lax.dot_general(sib, dyb, tn_dims, preferred_element_type=F32)
        grb = gr[...].astype(BF16)
        gib = gi[...].astype(BF16)
        dbr_ref[...] = lax.dot_general(ub, grb, tn_dims, preferred_element_type=F32)
        dbi_ref[...] = lax.dot_general(ub, gib, tn_dims, preferred_element_type=F32)
        du_ref[...] = (lax.dot_general(grb, bmr, nt_dims, preferred_element_type=F32)
                       + lax.dot_general(gib, bmi, nt_dims, preferred_element_type=F32)
                       + d_ref[...] * dyp).astype(du_ref.dtype)
        dd_ref[...] = jnp.sum(dyp * u, axis=0, keepdims=True)
        inv = 1.0 / (lr * lr + li * li)
        qr = lr * inv
        qi = -li * inv
        acc_r = jnp.zeros((1, sw), F32)
        acc_i = jnp.zeros((1, sw), F32)
        for blk in range(n_tok // rb):
            rows = pl.ds(blk * rb, rb)
            ubb = u_ref[rows, :].astype(BF16)
            er = sr[rows, :] - jnp.dot(ubb, bmr, preferred_element_type=F32)
            ei = si[rows, :] - jnp.dot(ubb, bmi, preferred_element_type=F32)
            pr, pi = _cmul(er, ei, qr, qi)
            ar = gr[rows, :]
            ai = gi[rows, :]
            acc_r = acc_r + jnp.sum(ar * pr + ai * pi, axis=0, keepdims=True)
            acc_i = acc_i + jnp.sum(ai * pr - ar * pi, axis=0, keepdims=True)
        dlr_ref[...] = acc_r
        dli_ref[...] = acc_i

    width = n_tiles * gw
    out_shape = [jax.ShapeDtypeStruct((n_tok, width), BF16),
                 jax.ShapeDtypeStruct(bblk_r.shape, F32), jax.ShapeDtypeStruct(bblk_r.shape, F32),
                 jax.ShapeDtypeStruct(cblk_r.shape, F32), jax.ShapeDtypeStruct(cblk_r.shape, F32),
                 jax.ShapeDtypeStruct(lbar_r.shape, F32), jax.ShapeDtypeStruct(lbar_r.shape, F32),
                 jax.ShapeDtypeStruct(dskip.shape, F32)]
    return pl.pallas_call(
        body, name=name, out_shape=out_shape, grid=(n_tiles,),
        in_specs=[u_spec, col, col, bmat, bmat, cmat, cmat, lvec, lvec, dvec],
        out_specs=[col, bmat, bmat, cmat, cmat, lvec, lvec, dvec],
        scratch_shapes=[pltpu.VMEM((n_tok, sw), F32)] * 4,
        compiler_params=_cparams(("parallel",)))(proj, ypre, dy2, bblk_r, bblk_i, cblk_r, cblk_i, lbar_r, lbar_i, dskip)


CONV_ROWS = 256
CONV_COLS = 512


def _conv_pre(x_ref, w_ref, blk, n_blk):
    r0 = blk * CONV_ROWS
    if blk == 0:
        ext = jnp.concatenate([jnp.zeros((8, CONV_COLS), F32), x_ref[0:CONV_ROWS, :]], axis=0)
    else:
        ext = x_ref[r0 - 8:r0 + CONV_ROWS, :]
    taps = []
    c = None
    for j in range(CONV_K):
        s = CONV_K - 1 - j
        xs = ext[8:] if s == 0 else pltpu.roll(ext, s, 0)[8:]
        taps.append(xs)
        term = w_ref[j:j + 1, :] * xs
        c = term if c is None else c + term
    return c, taps


def conv_fwd(proj, col0, conv_w, *, name):
    n_tok = proj.shape[0]
    width = conv_w.shape[1]
    n_blk = n_tok // CONV_ROWS
    cb0 = col0 // CONV_COLS

    def body(x_ref, w_ref, o_ref):
        for blk in range(n_blk):
            c, _ = _conv_pre(x_ref, w_ref, blk, n_blk)
            o_ref[blk * CONV_ROWS:(blk + 1) * CONV_ROWS, :] = c * jax.nn.sigmoid(c)

    return pl.pallas_call(
        body, name=name, out_shape=jax.ShapeDtypeStruct((n_tok, width), F32), grid=(width // CONV_COLS,),
        in_specs=[pl.BlockSpec((n_tok, CONV_COLS), lambda j: (0, cb0 + j)),
                  pl.BlockSpec((CONV_K, CONV_COLS), lambda j: (0, j))],
        out_specs=pl.BlockSpec((n_tok, CONV_COLS), lambda j: (0, j)),
        compiler_params=_cparams(("parallel",)))(proj, conv_w)


def conv_bwd(proj, col0, conv_w, dout, *, name):
    n_tok = proj.shape[0]
    width = conv_w.shape[1]
    n_blk = n_tok // CONV_ROWS
    cb0 = col0 // CONV_COLS

    def body(x_ref, w_ref, do_ref, dx_ref, dw_ref, dc):
        dws = [jnp.zeros((1, CONV_COLS), F32) for _ in range(CONV_K)]
        for blk in range(n_blk):
            rows = slice(blk * CONV_ROWS, (blk + 1) * CONV_ROWS)
            c, taps = _conv_pre(x_ref, w_ref, blk, n_blk)
            sg = jax.nn.sigmoid(c)
            dcv = do_ref[rows, :] * (sg * (1.0 + c * (1.0 - sg)))
            dc[rows, :] = dcv
            for j in range(CONV_K):
                dws[j] = dws[j] + jnp.sum(dcv * taps[j], axis=0, keepdims=True)
        dc[n_tok:n_tok + 8, :] = jnp.zeros((8, CONV_COLS), F32)
        for j in range(CONV_K):
            dw_ref[j:j + 1, :] = dws[j]
        for blk in range(n_blk):
            r0 = blk * CONV_ROWS
            ext = dc[r0:r0 + CONV_ROWS + 8, :]
            dx = None
            for j in range(CONV_K):
                s = CONV_K - 1 - j
                sh = ext[:CONV_ROWS] if s == 0 else pltpu.roll(ext, CONV_ROWS + 8 - s, 0)[:CONV_ROWS]
                term = w_ref[j:j + 1, :] * sh
                dx = term if dx is None else dx + term
            dx_ref[r0:r0 + CONV_ROWS, :] = dx.astype(dx_ref.dtype)

    return pl.pallas_call(
        body, name=name, out_shape=[jax.ShapeDtypeStruct((n_tok, width), BF16), jax.ShapeDtypeStruct(conv_w.shape, F32)],
        grid=(width // CONV_COLS,),
        in_specs=[pl.BlockSpec((n_tok, CONV_COLS), lambda j: (0, cb0 + j)),
                  pl.BlockSpec((CONV_K, CONV_COLS), lambda j: (0, j)),
                  pl.BlockSpec((n_tok, CONV_COLS), lambda j: (0, j))],
        out_specs=[pl.BlockSpec((n_tok, CONV_COLS), lambda j: (0, j)), pl.BlockSpec((CONV_K, CONV_COLS), lambda j: (0, j))],
        scratch_shapes=[pltpu.VMEM((n_tok + 8, CONV_COLS), F32)],
        compiler_params=_cparams(("parallel",)))(proj, conv_w, dout)


GDN_PREC = lax.Precision.HIGH


def _neumann_inverse(lowers):
    n = lowers[0].shape[0]
    eye = (lax.broadcasted_iota(jnp.int32, (n, n), 0) == lax.broadcasted_iota(jnp.int32, (n, n), 1)).astype(F32)
    xs = [-l for l in lowers]
    ts = [eye + x for x in xs]
    power = 2
    while power < n:
        xs = [jnp.dot(x, x, precision=GDN_PREC, preferred_element_type=F32) for x in xs]
        ts = [t + jnp.dot(t, x, precision=GDN_PREC, preferred_element_type=F32) for t, x in zip(ts, xs)]
        power *= 2
    return tuple(ts)


@jax.custom_vjp
def _unit_lower_inverse(lowers):
    return _neumann_inverse(lowers)


def _unit_lower_inverse_fwd(lowers):
    ts = _neumann_inverse(lowers)
    return ts, ts


def _unit_lower_inverse_bwd(ts, cts):
    tn = (((0,), (0,)), ((), ()))
    nt = (((1,), (1,)), ((), ()))
    lefts = [lax.dot_general(t, ct, tn, precision=GDN_PREC, preferred_element_type=F32) for t, ct in zip(ts, cts)]
    return (tuple(-lax.dot_general(l, t, nt, precision=GDN_PREC, preferred_element_type=F32)
                  for l, t in zip(lefts, ts)),)


_unit_lower_inverse.defvjp(_unit_lower_inverse_fwd, _unit_lower_inverse_bwd)


def _gdn_head(head, n_heads, state, q, k, v, z, bsmall, alog_row, dtb_row, nw):
    c = CHUNK
    lane = lax.broadcasted_iota(jnp.int32, (c, HEAD_DIM), 1)
    lane1 = lax.broadcasted_iota(jnp.int32, (1, HEAD_DIM), 1)
    ri = lax.broadcasted_iota(jnp.int32, (c, c), 0)
    ci = lax.broadcasted_iota(jnp.int32, (c, c), 1)
    causal = ri >= ci
    strict = ri > ci
    tril = causal.astype(F32)
    bl = jnp.sum(jnp.where(lane == head, bsmall, 0.0), axis=-1, keepdims=True)
    al = jnp.sum(jnp.where(lane == n_heads + head, bsmall, 0.0), axis=-1, keepdims=True)
    alog = jnp.sum(jnp.where(lane1 == head, alog_row, 0.0), axis=-1, keepdims=True)
    dtb = jnp.sum(jnp.where(lane1 == head, dtb_row, 0.0), axis=-1, keepdims=True)

    qn = q * lax.rsqrt(jnp.sum(q * q, axis=-1, keepdims=True) + L2_EPS) * (HEAD_DIM ** -0.5)
    kn = k * lax.rsqrt(jnp.sum(k * k, axis=-1, keepdims=True) + L2_EPS)
    beta = jax.nn.sigmoid(bl)
    xg = al + dtb
    g = -jnp.exp(alog) * (jnp.maximum(xg, 0.0) + jnp.log(1.0 + jnp.exp(-jnp.abs(xg))))
    g_wide = jnp.broadcast_to(g, (c, HEAD_DIM))
    yield None
    gc = jnp.dot(tril, g_wide, precision=HI, preferred_element_type=F32)
    yield None
    gc_rows = jnp.broadcast_to(jnp.mean(gc, axis=-1, keepdims=True), (c, c))
    gc_cols = gc.T[:c, :]
    g_tot = jnp.sum(g, axis=0, keepdims=True)
    decay = jnp.exp(jnp.where(causal, gc_rows - gc_cols, -1e30))
    egc = jnp.exp(gc)
    kb = kn * beta
    knb = kn.astype(BF16)
    nt = (((1,), (1,)), ((), ()))
    yield None
    lower = jnp.where(strict, lax.dot_general(kb.astype(BF16), knb, nt, preferred_element_type=F32) * decay, 0.0)
    tinv = yield lower
    u_val = jnp.dot(tinv, v * beta, precision=GDN_PREC, preferred_element_type=F32)
    yield None
    w_key = jnp.dot(tinv, kb * egc, precision=GDN_PREC, preferred_element_type=F32)
    yield None
    attn = lax.dot_general(qn.astype(BF16), knb, nt, preferred_element_type=F32) * decay
    q_dec = qn * egc
    k_dec = kn * jnp.exp(g_tot - gc)
    sb = state.astype(BF16)
    yield None
    v_new = u_val - jnp.dot(w_key.astype(BF16), sb, preferred_element_type=F32)
    vnb = v_new.astype(BF16)
    yield None
    o = (jnp.dot(q_dec.astype(BF16), sb, preferred_element_type=F32)
         + jnp.dot(attn.astype(BF16), vnb, preferred_element_type=F32))
    yield None
    new_state = state * jnp.exp(g_tot) + lax.dot_general(k_dec.astype(BF16), vnb, (((0,), (0,)), ((), ())),
                                                         preferred_element_type=F32)
    yield None
    o = o * lax.rsqrt(jnp.mean(o * o, axis=-1, keepdims=True) + RMS_EPS) * nw
    o = o * (z * jax.nn.sigmoid(z))
    return o, new_state


def _gdn_step(n_heads, states, qs, ks, vs, zs, bsmall, alog_row, dtb_row, nw):
    gens = [_gdn_head(h, n_heads, states[h], qs[h], ks[h], vs[h], zs[h], bsmall, alog_row, dtb_row, nw)
            for h in range(n_heads)]
    lowers = [None] * n_heads
    while any(m is None for m in lowers):
        for h in range(n_heads):
            lowers[h] = next(gens[h])
    tinvs = _unit_lower_inverse(tuple(lowers))
    results = [None] * n_heads
    first = True
    while any(r is None for r in results):
        for h in range(n_heads):
            try:
                if first:
                    gens[h].send(tinvs[h])
                else:
                    next(gens[h])
            except StopIteration as stop:
                results[h] = stop.value
        first = False
    return tuple(r[0] for r in results), tuple(r[1] for r in results)


def _gdn_in_specs(n_heads, qkv_width_blocks, z_blk, small_blk, rev, n_chunks):
    w = n_heads * HEAD_DIM

    def cidx(i):
        return (n_chunks - 1 - i) if rev else i
    qs = pl.BlockSpec((CHUNK, w), lambda i: (cidx(i), 0))
    ks = pl.BlockSpec((CHUNK, w), lambda i: (cidx(i), 1))
    vs = pl.BlockSpec((CHUNK, w), lambda i: (cidx(i), 2))
    zs = pl.BlockSpec((CHUNK, w), lambda i: (cidx(i), z_blk))
    bs = pl.BlockSpec((CHUNK, HEAD_DIM), lambda i: (cidx(i), small_blk))
    pv = pl.BlockSpec((1, HEAD_DIM), lambda i: (0, 0))
    return cidx, qs, ks, vs, zs, bs, pv


def gdn_fwd(qkv, proj, z_col0, small_col0, alog_row, dtb_row, nw_row, n_heads, *, name):
    n_tok = qkv.shape[0]
    w = n_heads * HEAD_DIM
    n_chunks = n_tok // CHUNK
    cidx, qs, ks, vs, zs, bs, pv = _gdn_in_specs(n_heads, 3, z_col0 // w, small_col0 // HEAD_DIM, False, n_chunks)

    def body(q_ref, k_ref, v_ref, z_ref, b_ref, al_ref, dt_ref, nw_ref, o_ref, s_ref, state):
        @pl.when(pl.program_id(0) == 0)
        def _():
            state[...] = jnp.zeros_like(state)

        heads = range(n_heads)
        cols = [slice(h * HEAD_DIM, (h + 1) * HEAD_DIM) for h in heads]
        states = [state[h] for h in heads]
        for h in heads:
            s_ref[h] = states[h]
        outs, new_states = _gdn_step(n_heads, states, [q_ref[:, c] for c in cols], [k_ref[:, c] for c in cols],
                                     [v_ref[:, c] for c in cols], [z_ref[:, c] for c in cols], b_ref[...],
                                     al_ref[...], dt_ref[...], nw_ref[...])
        for h in heads:
            o_ref[:, cols[h]] = outs[h].astype(BF16)
            state[h] = new_states[h]

    return pl.pallas_call(
        body, name=name,
        out_shape=[jax.ShapeDtypeStruct((n_tok, w), BF16),
                   jax.ShapeDtypeStruct((n_chunks, n_heads, HEAD_DIM, HEAD_DIM), F32)],
        grid=(n_chunks,), in_specs=[qs, ks, vs, zs, bs, pv, pv, pv],
        out_specs=[pl.BlockSpec((CHUNK, w), lambda i: (i, 0)),
                   pl.BlockSpec((None, n_heads, HEAD_DIM, HEAD_DIM), lambda i: (i, 0, 0, 0))],
        scratch_shapes=[pltpu.VMEM((n_heads, HEAD_DIM, HEAD_DIM), F32)],
        compiler_params=_cparams(("arbitrary",)))(qkv, qkv, qkv, proj, proj, alog_row, dtb_row, nw_row)


def gdn_bwd(qkv, proj, z_col0, small_col0, alog_row, dtb_row, nw_row, states, dout, n_heads, *, name):
    n_tok = qkv.shape[0]
    w = n_heads * HEAD_DIM
    n_chunks = n_tok // CHUNK
    cidx, qs, ks, vs, zs, bs, pv = _gdn_in_specs(n_heads, 3, z_col0 // w, small_col0 // HEAD_DIM, True, n_chunks)

    def body(q_ref, k_ref, v_ref, z_ref, b_ref, al_ref, dt_ref, nw_ref, s_ref, do_ref,
             dqkv_ref, dz_ref, db_ref, dal_ref, ddt_ref, dnw_ref, dstate):
        @pl.when(pl.program_id(0) == 0)
        def _():
            dstate[...] = jnp.zeros_like(dstate)
            dal_ref[...] = jnp.zeros_like(dal_ref)
            ddt_ref[...] = jnp.zeros_like(ddt_ref)
            dnw_ref[...] = jnp.zeros_like(dnw_ref)

        heads = range(n_heads)
        cols = [slice(h * HEAD_DIM, (h + 1) * HEAD_DIM) for h in heads]

        def f(sts, q, k, v, z, bb, al, dt, nw):
            return _gdn_step(n_heads, sts, q, k, v, z, bb, al, dt, nw)
        _, vjp = jax.vjp(f, tuple(s_ref[h] for h in heads), tuple(q_ref[:, c] for c in cols),
                         tuple(k_ref[:, c] for c in cols), tuple(v_ref[:, c] for c in cols),
                         tuple(z_ref[:, c] for c in cols), b_ref[...], al_ref[...], dt_ref[...], nw_ref[...])
        dsts, dqs, dks, dvs, dzs, dbb, da, dd, dn = vjp((tuple(do_ref[:, c] for c in cols),
                                                         tuple(dstate[h] for h in heads)))
        for h in heads:
            dstate[h] = dsts[h]
            dqkv_ref[:, h * HEAD_DIM:(h + 1) * HEAD_DIM] = dqs[h]
            dqkv_ref[:, w + h * HEAD_DIM:w + (h + 1) * HEAD_DIM] = dks[h]
            dqkv_ref[:, 2 * w + h * HEAD_DIM:2 * w + (h + 1) * HEAD_DIM] = dvs[h]
            dz_ref[:, cols[h]] = dzs[h].astype(dz_ref.dtype)
        db_ref[...] = dbb.astype(db_ref.dtype)
        dal_ref[...] += da
        ddt_ref[...] += dd
        dnw_ref[...] += dn

    rowblk = pl.BlockSpec((CHUNK, w), lambda i: (cidx(i), 0))
    return pl.pallas_call(
        body, name=name,
        out_shape=[
            jax.ShapeDtypeStruct((n_tok, 3 * w), F32),
            jax.ShapeDtypeStruct((n_tok, w), BF16), jax.ShapeDtypeStruct((n_tok, HEAD_DIM), BF16),
            jax.ShapeDtypeStruct((1, HEAD_DIM), F32), jax.ShapeDtypeStruct((1, HEAD_DIM), F32),
            jax.ShapeDtypeStruct((1, HEAD_DIM), F32)],
        grid=(n_chunks,),
        in_specs=[qs, ks, vs, zs, bs, pv, pv, pv,
                  pl.BlockSpec((None, n_heads, HEAD_DIM, HEAD_DIM), lambda i: (cidx(i), 0, 0, 0)), rowblk],
        out_specs=[pl.BlockSpec((CHUNK, 3 * w), lambda i: (cidx(i), 0)), rowblk,
                   pl.BlockSpec((CHUNK, HEAD_DIM), lambda i: (cidx(i), 0)), pv, pv, pv],
        scratch_shapes=[pltpu.VMEM((n_heads, HEAD_DIM, HEAD_DIM), F32)],
        compiler_params=_cparams(("arbitrary",)))(qkv, qkv, qkv, proj, proj, alog_row, dtb_row, nw_row, states, dout)


def glu_gate_fwd(y2, gl, bias, *, name):
    n_tok, w = y2.shape
    tm = _tile(n_tok, (256, 128))

    def body(y_ref, g_ref, b_ref, o_ref):
        o_ref[...] = (y_ref[...] * jax.nn.sigmoid(g_ref[...] + b_ref[...])).astype(BF16)

    row = pl.BlockSpec((tm, w), lambda i: (i, 0))
    vec = pl.BlockSpec((1, w), lambda i: (0, 0))
    return pl.pallas_call(body, name=name, out_shape=jax.ShapeDtypeStruct((n_tok, w), BF16), grid=(n_tok // tm,),
                          in_specs=[row, row, vec], out_specs=row, compiler_params=_cparams(("parallel",)))(y2, gl, bias)


def glu_gate_bwd(y2, gl, bias, dys, *, name):
    n_tok, w = y2.shape
    tm = _tile(n_tok, (256, 128))

    def body(y_ref, g_ref, b_ref, d_ref, dy_ref, dg_ref, db_ref):
        @pl.when(pl.program_id(0) == 0)
        def _():
            db_ref[...] = jnp.zeros_like(db_ref)

        sg = jax.nn.sigmoid(g_ref[...] + b_ref[...])
        d = d_ref[...]
        dy_ref[...] = d * sg
        dg = d * y_ref[...] * sg * (1.0 - sg)
        dg_ref[...] = dg.astype(dg_ref.dtype)
        db_ref[...] += jnp.sum(dg, axis=0, keepdims=True)

    row = pl.BlockSpec((tm, w), lambda i: (i, 0))
    vec = pl.BlockSpec((1, w), lambda i: (0, 0))
    return pl.pallas_call(
        body, name=name, out_shape=[jax.ShapeDtypeStruct((n_tok, w), F32), jax.ShapeDtypeStruct((n_tok, w), BF16),
                                    jax.ShapeDtypeStruct((1, w), F32)],
        grid=(n_tok // tm,), in_specs=[row, row, vec, row], out_specs=[row, row, vec],
        compiler_params=_cparams(("arbitrary",)))(y2, gl, bias, dys)


def merge_fwd(proj, bs, bd, *, name):
    n_tok, d = bs.shape
    tm = _tile(n_tok, (256, 128))

    def body(gs_ref, gd_ref, bs_ref, bd_ref, o_ref):
        o_ref[...] = (jax.nn.sigmoid(gs_ref[...]) * bs_ref[...]
                      + jax.nn.sigmoid(gd_ref[...]) * bd_ref[...]).astype(BF16)

    row = pl.BlockSpec((tm, d), lambda i: (i, 0))
    return pl.pallas_call(
        body, name=name, out_shape=jax.ShapeDtypeStruct((n_tok, d), BF16), grid=(n_tok // tm,),
        in_specs=[row, pl.BlockSpec((tm, d), lambda i: (i, 1)), row, row], out_specs=row,
        compiler_params=_cparams(("parallel",)))(proj, proj, bs, bd)


def merge_bwd(proj, bs, bd, dm, *, name):
    n_tok, d = bs.shape
    tm = _tile(n_tok, (256, 128))

    def body(gs_ref, gd_ref, bs_ref, bd_ref, dm_ref, dbs_ref, dbd_ref, dgs_ref, dgd_ref):
        dmv = dm_ref[...]
        ss = jax.nn.sigmoid(gs_ref[...])
        sd = jax.nn.sigmoid(gd_ref[...])
        dbs_ref[...] = (ss * dmv).astype(BF16)
        dbd_ref[...] = (sd * dmv).astype(BF16)
        dgs_ref[...] = (dmv * bs_ref[...] * ss * (1.0 - ss)).astype(BF16)
        dgd_ref[...] = (dmv * bd_ref[...] * sd * (1.0 - sd)).astype(BF16)

    row = pl.BlockSpec((tm, d), lambda i: (i, 0))
    return pl.pallas_call(
        body, name=name, out_shape=[jax.ShapeDtypeStruct((n_tok, d), BF16)] * 4, grid=(n_tok // tm,),
        in_specs=[row, pl.BlockSpec((tm, d), lambda i: (i, 1)), row, row, row], out_specs=[row] * 4,
        compiler_params=_cparams(("parallel",)))(proj, proj, bs, bd, dm)


def add_pairs(grads, recv, out_dtype, *, name):
    core = jnp.reshape(lax.axis_index("c"), (1,)).astype(jnp.int32)
    outs = []
    for t, (a, b) in enumerate(zip(grads, recv)):
        n_sh, h, cols = b.shape
        tr = _tile(h, (256, 128, 64, 32, 16))
        nh = h // tr

        def body(c_ref, a_ref, b_ref, o_ref):
            o_ref[...] = (a_ref[...].astype(F32) + b_ref[...].astype(F32)).astype(out_dtype)

        grid_spec = pltpu.PrefetchScalarGridSpec(
            num_scalar_prefetch=1, grid=(n_sh, nh),
            in_specs=[pl.BlockSpec((None, tr, cols), lambda s, i, c_ref, nh=nh: (s, c_ref[0] * nh + i, 0)),
                      pl.BlockSpec((None, tr, cols), lambda s, i, c_ref: (s, i, 0))],
            out_specs=pl.BlockSpec((None, tr, cols), lambda s, i, c_ref: (s, i, 0)))
        outs.append(pl.pallas_call(body, name=f"{name}_{t}", out_shape=jax.ShapeDtypeStruct(b.shape, out_dtype),
                                   grid_spec=grid_spec, compiler_params=_cparams(("parallel", "parallel")))(core, a, b))
    return outs


def add_chips(parts, *, name):
    outs = []
    for t, p in enumerate(parts):
        _, h, cols = p.shape
        tr = _tile(h, (256, 128, 64, 32, 16))

        def body(p0, p1, p2, p3, o_ref):
            o_ref[...] = ((p0[...].astype(F32) + p1[...].astype(F32)) + p2[...].astype(F32)) + p3[...].astype(F32)

        specs = [pl.BlockSpec((None, tr, cols), lambda i, s=s: (s, i, 0)) for s in range(N_CHIPS)]
        outs.append(pl.pallas_call(body, name=f"{name}_{t}", out_shape=jax.ShapeDtypeStruct((h, cols), F32),
                                   grid=(h // tr,), in_specs=specs, out_specs=pl.BlockSpec((tr, cols), lambda i: (i, 0)),
                                   compiler_params=_cparams(("parallel",)))(p, p, p, p))
    return outs


ADAMW_BLOCK_BYTES = 3 * 512 * 1024


def adamw(w, g, m, v, *, name, pass_grad=False):
    shape = w.shape
    cols = shape[-1]
    lead = shape[0] if (w.ndim > 2 and cols % 128 == 0) else 1
    rows = w.size // (cols * lead)
    lanes = -(-cols // 128) * 128
    tr = _tile(rows, tuple(t for t in (1024, 512, 256, 128, 64, 32, 16, 8) if t * lanes * 4 <= ADAMW_BLOCK_BYTES))
    c1 = 1.0 / (1.0 - ADAM_B1 ** ADAM_STEP)
    c2 = 1.0 / (1.0 - ADAM_B2 ** ADAM_STEP)

    def body(w_ref, g_ref, m_ref, v_ref, *out_refs):
        d_ref, nm_ref, nv_ref = out_refs[-3:]
        gv = g_ref[...]
        if pass_grad:
            out_refs[0][...] = gv
        nm = ADAM_B1 * m_ref[...] + (1.0 - ADAM_B1) * gv
        nv = ADAM_B2 * v_ref[...] + (1.0 - ADAM_B2) * (gv * gv)
        d_ref[...] = -ADAM_LR * ((nm * c1) / (jnp.sqrt(nv * c2) + ADAM_EPS) + ADAM_WD * w_ref[...])
        nm_ref[...] = nm
        nv_ref[...] = nv

    blk = pl.BlockSpec((None, tr, cols), lambda a, i: (a, i, 0))
    n_out = 4 if pass_grad else 3
    view = (lead, rows, cols)
    outs = pl.pallas_call(body, name=name, out_shape=[jax.ShapeDtypeStruct(view, F32)] * n_out,
                          grid=(lead, rows // tr), in_specs=[blk] * 4, out_specs=[blk] * n_out,
                          compiler_params=_cparams(("parallel", "parallel")))(
        w.reshape(view), g.reshape(view), m.reshape(view), v.reshape(view))
    return [o.reshape(shape) for o in outs]


def _place():
    return lax.axis_index("x"), lax.axis_index("y"), lax.axis_index("c")


def _other_chips(x, y):
    return [(1 - x, y), (x, 1 - y), (1 - x, 1 - y)]


ANY = pl.BlockSpec(memory_space=pl.ANY)
STAGE_BYTES = 1 << 20


def _stage_shape(rows, cols, dtype):
    mult = 32 // jnp.dtype(dtype).itemsize
    per_row = (-(-cols // 128) * 128) * jnp.dtype(dtype).itemsize
    chunk = max(mult, STAGE_BYTES // per_row // mult * mult)
    return pltpu.VMEM((2, min(chunk, rows), cols), dtype)


def _staged_copy(src, dst, buf, sem_in, sem_out, k):
    rows, chunk = src.shape[0], buf.shape[1]
    pending = []
    for i, r0 in enumerate(range(0, rows, chunk)):
        sz = min(chunk, rows - r0)
        slot = i % 2
        if i >= 2:
            pending[i - 2].wait()
        stage = buf.at[slot, pl.ds(0, sz)]
        cin = pltpu.make_async_copy(src.at[pl.ds(r0, sz)], stage, sem_in.at[2 * k + slot])
        cin.start()
        cin.wait()
        cout = pltpu.make_async_copy(stage, dst.at[pl.ds(r0, sz)], sem_out.at[2 * k + slot])
        cout.start()
        pending.append(cout)
    for cp in pending[max(0, len(pending) - 2):]:
        cp.wait()


def gather_chips(blocks, halve, *, name):
    n = len(blocks)

    def body(*refs):
        ins, outs = refs[:n], refs[n:2 * n]
        send_sems, recv_sems, fwd_send, fwd_recv, stage_in, stage_out = refs[2 * n:2 * n + 6]
        bufs = refs[2 * n + 6:]
        x, y, c = _place()
        me = 2 * x + y
        chips = _other_chips(x, y)
        sibling = (x, y, 1 - c)
        sends, fwds = [], []
        for t in range(n):
            for j, (px, py) in enumerate(chips):
                if halve[t]:
                    h = ins[t].shape[0] // 2
                    rows = pl.ds(c * h, h)
                    src, dst = ins[t].at[rows], outs[t].at[me, rows]
                else:
                    src, dst = ins[t], outs[t].at[me]
                cp = pltpu.make_async_remote_copy(src_ref=src, dst_ref=dst, send_sem=send_sems.at[3 * t + j],
                                                  recv_sem=recv_sems.at[3 * t + j], device_id=(px, py, c),
                                                  device_id_type=MESH)
                cp.start()
                sends.append(cp)
        for t in range(n):
            _staged_copy(ins[t], outs[t].at[me], bufs[t], stage_in, stage_out, t)
        for t in range(n):
            for j, (px, py) in enumerate(chips):
                src_chip = 2 * px + py
                if halve[t]:
                    h = ins[t].shape[0] // 2
                    rows = pl.ds(c * h, h)
                    landed = outs[t].at[src_chip, rows]
                    pltpu.make_async_remote_copy(src_ref=landed, dst_ref=landed, send_sem=send_sems.at[3 * t + j],
                                                 recv_sem=recv_sems.at[3 * t + j], device_id=(px, py, c),
                                                 device_id_type=MESH).wait_recv()
                    cp = pltpu.make_async_remote_copy(src_ref=landed, dst_ref=landed, send_sem=fwd_send.at[3 * t + j],
                                                      recv_sem=fwd_recv.at[3 * t + j], device_id=sibling,
                                                      device_id_type=MESH)
                    cp.start()
                    fwds.append(cp)
                else:
                    landed = outs[t].at[src_chip]
                    pltpu.make_async_remote_copy(src_ref=landed, dst_ref=landed, send_sem=send_sems.at[3 * t + j],
                                                 recv_sem=recv_sems.at[3 * t + j], device_id=(px, py, c),
                                                 device_id_type=MESH).wait_recv()
        for t in range(n):
            if not halve[t]:
                continue
            h = ins[t].shape[0] // 2
            for j, (px, py) in enumerate(chips):
                theirs = outs[t].at[2 * px + py, pl.ds((1 - c) * h, h)]
                pltpu.make_async_remote_copy(src_ref=theirs, dst_ref=theirs, send_sem=fwd_send.at[3 * t + j],
                                             recv_sem=fwd_recv.at[3 * t + j], device_id=sibling,
                                             device_id_type=MESH).wait_recv()
        for cp in sends + fwds:
            cp.wait_send()

    return pl.pallas_call(
        body, name=name, out_shape=[jax.ShapeDtypeStruct((N_CHIPS,) + b.shape, b.dtype) for b in blocks],
        in_specs=[ANY] * n, out_specs=[ANY] * n,
        scratch_shapes=[pltpu.SemaphoreType.DMA((3 * n,))] * 4 + [pltpu.SemaphoreType.DMA((2 * n,))] * 2
        + [_stage_shape(b.shape[0], b.shape[1], b.dtype) for b in blocks],
        compiler_params=pltpu.CompilerParams(has_side_effects=True, vmem_limit_bytes=VMEM_LIMIT))(*blocks)


HBM_SPEC = pl.BlockSpec(memory_space=pltpu.HBM)
SEM_SPEC = pl.BlockSpec(memory_space=pltpu.SEMAPHORE)
DATAFLOW = pltpu.SideEffectType.DATAFLOW_SIDE_EFFECTING


def _in_hbm(v):
    return pltpu.with_memory_space_constraint(v, pltpu.HBM)


def _split_start(srcs, land_shapes, make_copies, *, name):
    n = len(srcs)
    lands = [_in_hbm(lax.empty(s.shape, s.dtype)) for s in land_shapes]

    def body(*refs):
        ins, lands_in = refs[:n], refs[n:2 * n]
        send_sems, recv_sems = refs[2 * n], refs[2 * n + 1]
        token = refs[-1]
        for cp in make_copies(ins, lands_in, send_sems, recv_sems, False):
            cp.start()
        token[...] = jnp.zeros_like(token)

    outs = pl.pallas_call(
        body, name=name,
        out_shape=(pltpu.SemaphoreType.DMA((3 * n,)), pltpu.SemaphoreType.DMA((3 * n,)),
                   *[pltpu.HBM(s.shape, s.dtype) for s in srcs], *[pltpu.HBM(s.shape, s.dtype) for s in land_shapes],
                   jax.ShapeDtypeStruct((8, 128), F32)),
        in_specs=[HBM_SPEC] * (2 * n),
        out_specs=(SEM_SPEC, SEM_SPEC, *([HBM_SPEC] * (2 * n)), pl.BlockSpec(memory_space=pltpu.VMEM)),
        input_output_aliases={i: 2 + i for i in range(2 * n)},
        compiler_params=pltpu.CompilerParams(has_side_effects=DATAFLOW))(*[_in_hbm(s) for s in srcs], *lands)
    return outs[0], outs[1], list(outs[2:2 + n]), list(outs[2 + n:2 + 2 * n]), outs[-1]


def _split_wait(send_sems, recv_sems, srcs, lands, after, make_copies, *, name):
    n = len(srcs)

    def body(*refs):
        ins, lands_in = refs[:n], refs[n:2 * n]
        s_sems, r_sems = refs[2 * n], refs[2 * n + 1]
        token = refs[-1]
        for cp in make_copies(ins, lands_in, s_sems, r_sems, False):
            cp.wait_send()
        for cp in make_copies(ins, lands_in, s_sems, r_sems, True):
            cp.wait_recv()
        token[...] = jnp.zeros_like(token)

    outs = pl.pallas_call(
        body, name=name,
        out_shape=(*[pltpu.HBM(s.shape, s.dtype) for s in srcs], *[pltpu.HBM(s.shape, s.dtype) for s in lands],
                   jax.ShapeDtypeStruct((8, 128), F32)),
        in_specs=[HBM_SPEC] * (2 * n) + [SEM_SPEC, SEM_SPEC, ANY],
        out_specs=(*([HBM_SPEC] * (2 * n)), pl.BlockSpec(memory_space=pltpu.VMEM)),
        input_output_aliases={i: i for i in range(2 * n)},
        compiler_params=pltpu.CompilerParams(has_side_effects=DATAFLOW))(*srcs, *lands, send_sems, recv_sems, after)
    return list(outs[:n]), list(outs[n:2 * n]), outs[-1]


def _gather_copies(halve):
    def make(ins, lands, send_sems, recv_sems, arrivals):
        x, y, c = _place()
        me = 2 * x + y
        cps = []
        for t in range(len(ins)):
            for j, (px, py) in enumerate(_other_chips(x, y)):
                if halve[t]:
                    h = ins[t].shape[0] // 2
                    rows = pl.ds(c * h, h)
                    src, dst, landed = ins[t].at[rows], lands[t].at[me, rows], lands[t].at[2 * px + py, rows]
                else:
                    src, dst, landed = ins[t], lands[t].at[me], lands[t].at[2 * px + py]
                sem = dict(send_sem=send_sems.at[3 * t + j], recv_sem=recv_sems.at[3 * t + j], device_id=(px, py, c),
                           device_id_type=MESH)
                if arrivals:
                    cps.append(pltpu.make_async_remote_copy(src_ref=landed, dst_ref=landed, **sem))
                else:
                    cps.append(pltpu.make_async_remote_copy(src_ref=src, dst_ref=dst, **sem))
        return cps
    return make


def gather_start(blocks, halve, *, name):
    shapes = [jax.ShapeDtypeStruct((N_CHIPS,) + b.shape, b.dtype) for b in blocks]
    return _split_start(blocks, shapes, _gather_copies(halve), name=name)


def gather_wait(started, halve, after, *, name):
    send_sems, recv_sems, srcs, lands, _ = started
    return _split_wait(send_sems, recv_sems, srcs, lands, after, _gather_copies(halve), name=name)


def gather_finish(blocks, lands, halve, token, *, name):
    n = len(blocks)

    def body(*refs):
        ins, outs = refs[:n], refs[2 * n + 1:3 * n + 1]
        fwd_send, fwd_recv, stage_in, stage_out = refs[3 * n + 1:3 * n + 5]
        bufs = refs[3 * n + 5:]
        x, y, c = _place()
        me = 2 * x + y
        chips = _other_chips(x, y)
        sibling = (x, y, 1 - c)
        fwds = []
        for t in range(n):
            if not halve[t]:
                continue
            h = ins[t].shape[0] // 2
            for j, (px, py) in enumerate(chips):
                landed = outs[t].at[2 * px + py, pl.ds(c * h, h)]
                cp = pltpu.make_async_remote_copy(src_ref=landed, dst_ref=landed, send_sem=fwd_send.at[3 * t + j],
                                                  recv_sem=fwd_recv.at[3 * t + j], device_id=sibling, device_id_type=MESH)
                cp.start()
                fwds.append(cp)
        for t in range(n):
            _staged_copy(ins[t], outs[t].at[me], bufs[t], stage_in, stage_out, t)
        for t in range(n):
            if not halve[t]:
                continue
            h = ins[t].shape[0] // 2
            for j, (px, py) in enumerate(chips):
                theirs = outs[t].at[2 * px + py, pl.ds((1 - c) * h, h)]
                pltpu.make_async_remote_copy(src_ref=theirs, dst_ref=theirs, send_sem=fwd_send.at[3 * t + j],
                                             recv_sem=fwd_recv.at[3 * t + j], device_id=sibling,
                                             device_id_type=MESH).wait_recv()
        for cp in fwds:
            cp.wait_send()

    return pl.pallas_call(
        body, name=name, out_shape=[jax.ShapeDtypeStruct(v.shape, v.dtype) for v in lands],
        in_specs=[ANY] * (2 * n) + [pl.BlockSpec(memory_space=pltpu.VMEM)], out_specs=[ANY] * n,
        input_output_aliases={n + i: i for i in range(n)},
        scratch_shapes=[pltpu.SemaphoreType.DMA((3 * n,))] * 2 + [pltpu.SemaphoreType.DMA((2 * n,))] * 2
        + [_stage_shape(b.shape[0], b.shape[1], b.dtype) for b in blocks],
        compiler_params=pltpu.CompilerParams(has_side_effects=True, vmem_limit_bytes=VMEM_LIMIT))(*blocks, *lands, token)


def _xchg_copies(ins, lands, send_sems, recv_sems, arrivals):
    x, y, c = _place()
    me = 2 * x + y
    cps = []
    for t in range(len(ins)):
        for j, (px, py) in enumerate(_other_chips(x, y)):
            landed = lands[t].at[2 * px + py]
            sem = dict(send_sem=send_sems.at[3 * t + j], recv_sem=recv_sems.at[3 * t + j], device_id=(px, py, c),
                       device_id_type=MESH)
            if arrivals:
                cps.append(pltpu.make_async_remote_copy(src_ref=landed, dst_ref=landed, **sem))
            else:
                cps.append(pltpu.make_async_remote_copy(src_ref=ins[t].at[2 * px + py], dst_ref=lands[t].at[me], **sem))
    return cps


def xchg_start(parts, *, name):
    return _split_start(parts, [jax.ShapeDtypeStruct(p.shape, p.dtype) for p in parts], _xchg_copies, name=name)


def xchg_wait(started, after, *, name):
    send_sems, recv_sems, srcs, lands, _ = started
    return _split_wait(send_sems, recv_sems, srcs, lands, after, _xchg_copies, name=name)


def xchg_finish(parts, lands, *, name):
    n = len(parts)

    def body(*refs):
        ins, outs = refs[:n], refs[2 * n:3 * n]
        stage_in, stage_out = refs[3 * n:3 * n + 2]
        bufs = refs[3 * n + 2:]
        x, y, _ = _place()
        me = 2 * x + y
        for t in range(n):
            _staged_copy(ins[t].at[me], outs[t].at[me], bufs[t], stage_in, stage_out, t)

    return pl.pallas_call(
        body, name=name, out_shape=[jax.ShapeDtypeStruct(v.shape, v.dtype) for v in lands],
        in_specs=[ANY] * (2 * n), out_specs=[ANY] * n, input_output_aliases={n + i: i for i in range(n)},
        scratch_shapes=[pltpu.SemaphoreType.DMA((2 * n,))] * 2
        + [_stage_shape(p.shape[1], p.shape[2], p.dtype) for p in parts],
        compiler_params=pltpu.CompilerParams(has_side_effects=True, vmem_limit_bytes=VMEM_LIMIT))(*parts, *lands)


def pair_split(grads, *, name):
    n = len(grads)

    def body(*refs):
        ins, recv = refs[:n], refs[n:2 * n]
        send_sems, recv_sems = refs[2 * n:]
        x, y, c = _place()
        sibling = (x, y, 1 - c)
        cps = []
        for t in range(n):
            h = ins[t].shape[1] // 2
            cp = pltpu.make_async_remote_copy(src_ref=ins[t].at[:, pl.ds((1 - c) * h, h)], dst_ref=recv[t],
                                              send_sem=send_sems.at[t], recv_sem=recv_sems.at[t], device_id=sibling,
                                              device_id_type=MESH)
            cp.start()
            cps.append(cp)
        for cp in cps:
            cp.wait()

    half = [jax.ShapeDtypeStruct((g.shape[0], g.shape[1] // 2, g.shape[2]), g.dtype) for g in grads]
    return pl.pallas_call(
        body, name=name, out_shape=half, in_specs=[ANY] * n, out_specs=[ANY] * n,
        scratch_shapes=[pltpu.SemaphoreType.DMA((n,))] * 2,
        compiler_params=pltpu.CompilerParams(has_side_effects=True))(*grads)


def chip_exchange(parts, *, name):
    n = len(parts)

    def body(*refs):
        ins, outs = refs[:n], refs[n:2 * n]
        send_sems, recv_sems, stage_in, stage_out = refs[2 * n:2 * n + 4]
        bufs = refs[2 * n + 4:]
        x, y, c = _place()
        me = 2 * x + y
        chips = _other_chips(x, y)
        cps = []
        for t in range(n):
            for j, (px, py) in enumerate(chips):
                cp = pltpu.make_async_remote_copy(src_ref=ins[t].at[2 * px + py], dst_ref=outs[t].at[me],
                                                  send_sem=send_sems.at[3 * t + j], recv_sem=recv_sems.at[3 * t + j],
                                                  device_id=(px, py, c), device_id_type=MESH)
                cp.start()
                cps.append(cp)
        for t in range(n):
            _staged_copy(ins[t].at[me], outs[t].at[me], bufs[t], stage_in, stage_out, t)
        for t in range(n):
            for j, (px, py) in enumerate(chips):
                landed = outs[t].at[2 * px + py]
                pltpu.make_async_remote_copy(src_ref=landed, dst_ref=landed, send_sem=send_sems.at[3 * t + j],
                                             recv_sem=recv_sems.at[3 * t + j], device_id=(px, py, c),
                                             device_id_type=MESH).wait_recv()
        for cp in cps:
            cp.wait_send()

    return pl.pallas_call(
        body, name=name, out_shape=[jax.ShapeDtypeStruct(p.shape, p.dtype) for p in parts],
        in_specs=[ANY] * n, out_specs=[ANY] * n,
        scratch_shapes=[pltpu.SemaphoreType.DMA((3 * n,))] * 2 + [pltpu.SemaphoreType.DMA((2 * n,))] * 2
        + [_stage_shape(p.shape[1], p.shape[2], p.dtype) for p in parts],
        compiler_params=pltpu.CompilerParams(has_side_effects=True, vmem_limit_bytes=VMEM_LIMIT))(*parts)


def pair_join(halves, *, name):
    n = len(halves)

    def body(*refs):
        ins, outs = refs[:n], refs[n:2 * n]
        send_sems, recv_sems, stage_in, stage_out = refs[2 * n:2 * n + 4]
        bufs = refs[2 * n + 4:]
        x, y, c = _place()
        sibling = (x, y, 1 - c)
        cps = []
        for t in range(n):
            h = ins[t].shape[0]
            cp = pltpu.make_async_remote_copy(src_ref=ins[t], dst_ref=outs[t].at[pl.ds(c * h, h)],
                                              send_sem=send_sems.at[t], recv_sem=recv_sems.at[t], device_id=sibling,
                                              device_id_type=MESH)
            cp.start()
            cps.append(cp)
        for t in range(n):
            h = ins[t].shape[0]
            _staged_copy(ins[t], outs[t].at[pl.ds(c * h, h)], bufs[t], stage_in, stage_out, t)
        for t in range(n):
            h = ins[t].shape[0]
            theirs = outs[t].at[pl.ds((1 - c) * h, h)]
            pltpu.make_async_remote_copy(src_ref=theirs, dst_ref=theirs, send_sem=send_sems.at[t],
                                         recv_sem=recv_sems.at[t], device_id=sibling, device_id_type=MESH).wait_recv()
        for cp in cps:
            cp.wait_send()

    return pl.pallas_call(
        body, name=name, out_shape=[jax.ShapeDtypeStruct((2 * p.shape[0], p.shape[1]), p.dtype) for p in halves],
        in_specs=[ANY] * n, out_specs=[ANY] * n,
        scratch_shapes=[pltpu.SemaphoreType.DMA((n,))] * 2 + [pltpu.SemaphoreType.DMA((2 * n,))] * 2
        + [_stage_shape(p.shape[0], p.shape[1], p.dtype) for p in halves],
        compiler_params=pltpu.CompilerParams(has_side_effects=True, vmem_limit_bytes=VMEM_LIMIT))(*halves)


def pair_join_stacked(halves, stacks, layer, *, name):
    n = len(halves)

    def body(*refs):
        ins, outs = refs[:n], refs[2 * n:3 * n]
        send_sems, recv_sems, stage_in, stage_out = refs[3 * n:3 * n + 4]
        bufs = refs[3 * n + 4:]
        x, y, c = _place()
        sibling = (x, y, 1 - c)
        cps = []
        for t in range(n):
            h = ins[t].shape[0]
            cp = pltpu.make_async_remote_copy(src_ref=ins[t], dst_ref=outs[t].at[layer, pl.ds(c * h, h)],
                                              send_sem=send_sems.at[t], recv_sem=recv_sems.at[t], device_id=sibling,
                                              device_id_type=MESH)
            cp.start()
            cps.append(cp)
        for t in range(n):
            h = ins[t].shape[0]
            _staged_copy(ins[t], outs[t].at[layer, pl.ds(c * h, h)], bufs[t], stage_in, stage_out, t)
        for t in range(n):
            h = ins[t].shape[0]
            theirs = outs[t].at[layer, pl.ds((1 - c) * h, h)]
            pltpu.make_async_remote_copy(src_ref=theirs, dst_ref=theirs, send_sem=send_sems.at[t],
                                         recv_sem=recv_sems.at[t], device_id=sibling, device_id_type=MESH).wait_recv()
        for cp in cps:
            cp.wait_send()

    return pl.pallas_call(
        body, name=name, out_shape=[jax.ShapeDtypeStruct(s.shape, s.dtype) for s in stacks],
        in_specs=[ANY] * (2 * n), out_specs=[ANY] * n, input_output_aliases={n + i: i for i in range(n)},
        scratch_shapes=[pltpu.SemaphoreType.DMA((n,))] * 2 + [pltpu.SemaphoreType.DMA((2 * n,))] * 2
        + [_stage_shape(p.shape[0], p.shape[1], p.dtype) for p in halves],
        compiler_params=pltpu.CompilerParams(has_side_effects=True, vmem_limit_bytes=VMEM_LIMIT))(*halves, *stacks)


def reduce_scatter(grads, pay_dtype, *, name):
    recv = pair_split(grads, name=name + "_split")
    part = add_pairs(grads, recv, pay_dtype, name=name + "_add2")
    got = chip_exchange(part, name=name + "_xchg")
    half = add_chips(got, name=name + "_add4")
    return pair_join(half, name=name + "_join")


def _in_sizes(d, w, n_heads):
    return (w, w, w, w, w, n_heads, n_heads, d, d)


def _wcat_from_gathered(wg, d, w, n_heads):
    full = jnp.concatenate([wg[s] for s in range(N_CHIPS)], axis=1)
    sizes = _in_sizes(d, w, n_heads)
    offs = [0]
    for s in sizes:
        offs.append(offs[-1] + s)
    pieces = [full[:, offs[i]:offs[i + 1]] for i in range(len(sizes))]
    u, q, k, v, z, beta, a, gs, gd = pieces
    pad = jnp.zeros((full.shape[0], SMALL_W - 2 * n_heads), full.dtype)
    return jnp.concatenate([gs, gd, u, q, k, v, z, beta, a, pad], axis=1)


def _wcat_grad_to_shards(dwcat, d, w, n_heads):
    gs, gd = dwcat[:, :d], dwcat[:, d:2 * d]
    o = 2 * d
    u, q, k, v, z = [dwcat[:, o + i * w:o + (i + 1) * w] for i in range(5)]
    o += 5 * w
    beta, a = dwcat[:, o:o + n_heads], dwcat[:, o + n_heads:o + 2 * n_heads]
    full = jnp.concatenate([u, q, k, v, z, beta, a, gs, gd], axis=1)
    return jnp.stack(jnp.split(full, N_CHIPS, axis=1))


def _s5_discretize(a_re, a_im, log_dt, b_re, b_im):
    dt = jnp.exp(log_dt)[:, None]
    mag = jnp.exp(a_re * dt)
    lbar_r, lbar_i = mag * jnp.cos(a_im * dt), mag * jnp.sin(a_im * dt)
    den = a_re * a_re + a_im * a_im
    zr, zi = _cmul(lbar_r - 1.0, lbar_i, a_re / den, -a_im / den)
    bbar_r, bbar_i = _cmul(zr[:, :, None], zi[:, :, None], b_re, b_im)
    return lbar_r, lbar_i, bbar_r, bbar_i


def _blockdiag_in(bbar):
    g, p, h = bbar.shape
    t = g // GROUPS_PER_TILE
    bb = bbar.reshape(t, GROUPS_PER_TILE, p, h).transpose(0, 1, 3, 2)
    eye = jnp.eye(GROUPS_PER_TILE, dtype=bbar.dtype)
    return jnp.einsum('tjhp,jk->tjhkp', bb, eye).reshape(t, GROUPS_PER_TILE * h, GROUPS_PER_TILE * p)


def _blockdiag_in_grad(dblk, g, p, h):
    t = g // GROUPS_PER_TILE
    d5 = dblk.reshape(t, GROUPS_PER_TILE, h, GROUPS_PER_TILE, p)
    eye = jnp.eye(GROUPS_PER_TILE, dtype=dblk.dtype)
    diag = jnp.einsum('tjhkp,jk->tjhp', d5, eye)
    return diag.transpose(0, 1, 3, 2).reshape(g, p, h)


def _blockdiag_out(cmat):
    g, h, p = cmat.shape
    t = g // GROUPS_PER_TILE
    cc = cmat.reshape(t, GROUPS_PER_TILE, h, p).transpose(0, 1, 3, 2)
    eye = jnp.eye(GROUPS_PER_TILE, dtype=cmat.dtype)
    return jnp.einsum('tjph,jk->tjpkh', cc, eye).reshape(t, GROUPS_PER_TILE * p, GROUPS_PER_TILE * h)


def _blockdiag_out_grad(dblk, g, h, p):
    t = g // GROUPS_PER_TILE
    d5 = dblk.reshape(t, GROUPS_PER_TILE, p, GROUPS_PER_TILE, h)
    eye = jnp.eye(GROUPS_PER_TILE, dtype=dblk.dtype)
    diag = jnp.einsum('tjpkh,jk->tjph', d5, eye)
    return diag.transpose(0, 1, 3, 2).reshape(g, h, p)


def _pad_row(v, width):
    return jnp.pad(v.reshape(1, -1), ((0, 0), (0, width - v.size)))


def _pack(arrs, rows_mult):
    flat = jnp.concatenate([a.reshape(-1) for a in arrs])
    per = 128 * rows_mult
    total = -(-flat.size // per) * per
    return jnp.pad(flat, (0, total - flat.size))


def _unpack(flat, like):
    out, o = [], 0
    for a in like:
        out.append(flat[o:o + a.size].reshape(a.shape))
        o += a.size
    return out


def kernel(x, ffn1_w_gu, ffn1_w_down, ln1_g, ln1_b, w_in, conv_w, ssm_a_re, ssm_a_im, ssm_log_dt, ssm_b_re, ssm_b_im, ssm_c_re, ssm_c_im, ssm_d, glu_w, glu_b, gdn_a_log, gdn_dt_bias, gdn_norm_w, w_br_ssm, w_br_gdn, w_out, ln2_g, ln2_b, ffn2_w_gu, ffn2_w_down, ln3_g, ln3_b, loss_target, m_ffn1_w_gu, m_ffn1_w_down, m_ln1_g, m_ln1_b, m_w_in, m_conv_w, m_ssm_a_re, m_ssm_a_im, m_ssm_log_dt, m_ssm_b_re, m_ssm_b_im, m_ssm_c_re, m_ssm_c_im, m_ssm_d, m_glu_w, m_glu_b, m_gdn_a_log, m_gdn_dt_bias, m_gdn_norm_w, m_w_br_ssm, m_w_br_gdn, m_w_out, m_ln2_g, m_ln2_b, m_ffn2_w_gu, m_ffn2_w_down, m_ln3_g, m_ln3_b, v_ffn1_w_gu, v_ffn1_w_down, v_ln1_g, v_ln1_b, v_w_in, v_conv_w, v_ssm_a_re, v_ssm_a_im, v_ssm_log_dt, v_ssm_b_re, v_ssm_b_im, v_ssm_c_re, v_ssm_c_im, v_ssm_d, v_glu_w, v_glu_b, v_gdn_a_log, v_gdn_dt_bias, v_gdn_norm_w, v_w_br_ssm, v_w_br_gdn, v_w_out, v_ln2_g, v_ln2_b, v_ffn2_w_gu, v_ffn2_w_down, v_ln3_g, v_ln3_b):
    args = locals()
    wts = {n: args[n] for n in WEIGHT_NAMES}
    mom = {n: args["m_" + n] for n in WEIGHT_NAMES}
    var = {n: args["v_" + n] for n in WEIGHT_NAMES}

    depth = ln1_g.shape[0]
    n_tok, d = x.shape[1], x.shape[2]
    w = glu_w.shape[-1]
    n_heads = gdn_a_log.shape[-1]
    n_groups, n_state, grp = ssm_b_re.shape[1], ssm_b_re.shape[2], ssm_b_re.shape[3]
    alpha = (2.0 * depth) ** 0.25
    u_col0 = 2 * d
    qkv_col0 = u_col0 + w
    z_col0 = u_col0 + 4 * w
    small_col0 = u_col0 + 5 * w
    x_idx, y_idx, _ = _place()
    chip = 2 * x_idx + y_idx

    xcur = x[0]
    xcur_b = xcur.astype(BF16)
    saved = []
    halve = [True] * len(BIG) + [False]

    def start_gather(layer, order_token):
        shards = [wts[n][layer].astype(BF16) for n in BIG] + [conv_w[layer] + order_token[0, 0]]
        return gather_start(shards, halve, name=f"gather_start_l{layer}")

    started = start_gather(0, jnp.zeros((1, 1), F32))
    after = xcur
    for l in range(depth):
        shards, lands, waited = gather_wait(started, halve, after, name=f"gather_wait_l{l}")
        started = start_gather(l + 1, waited) if l + 1 < depth else started
        gathered = gather_finish(shards, lands, halve, started[4], name="gather_finish")
        gw = dict(zip(BIG, gathered[:-1]))
        conv_full = jnp.concatenate([gathered[-1][s] for s in range(N_CHIPS)], axis=1)
        wgu1, wgu2 = gw['ffn1_w_gu'], gw['ffn2_w_gu']
        wd1 = gw['ffn1_w_down'].reshape(-1, d)
        wd2 = gw['ffn2_w_down'].reshape(-1, d)
        wcat = _wcat_from_gathered(gw['w_in'], d, w, n_heads)
        wglu = gw['glu_w'].reshape(w, w)
        wbs, wbd = gw['w_br_ssm'], gw['w_br_gdn']
        wout = gw['w_out'].reshape(d, d)
        f = wd1.shape[0]

        vec = lambda v: v[l].reshape(1, -1)
        x0, x0b = xcur, xcur_b
        h1 = mm(x0b, wgu1, b_nb=N_CHIPS, out_nb=2, name=f"ffn_up")
        r1, x1, x1b = down_res_ln(h1, wd1, x0, vec(ln1_g), vec(ln1_b), swiglu=True, alpha=alpha, scale=0.5,
                                  name="ffn_down")
        proj = mm(x1b, wcat, name="in_proj")
        (lbar_r, lbar_i, bbar_r, bbar_i), disc_vjp = jax.vjp(
            _s5_discretize, ssm_a_re[l], ssm_a_im[l], ssm_log_dt[l], ssm_b_re[l], ssm_b_im[l])
        s5w = (_blockdiag_in(bbar_r), _blockdiag_in(bbar_i), _blockdiag_out(ssm_c_re[l]), _blockdiag_out(ssm_c_im[l]),
               lbar_r.reshape(1, -1), lbar_i.reshape(1, -1), ssm_d[l].reshape(1, -1))
        ypre, y2, y2b = s5_fwd(proj, u_col0, *s5w, name="s5_fwd")
        gl = mm(y2b, wglu, name="glu_proj")
        ys = glu_gate_fwd(y2, gl, vec(glu_b), name="glu_gate")
        qkv = conv_fwd(proj, qkv_col0, conv_full, name="conv_fwd")
        gdn_rows = (_pad_row(gdn_a_log[l], HEAD_DIM), _pad_row(gdn_dt_bias[l], HEAD_DIM), gdn_norm_w[l].reshape(1, -1))
        yg, states = gdn_fwd(qkv, proj, z_col0, small_col0, *gdn_rows, n_heads, name="gdn_fwd")
        bs = mm(ys, wbs, b_nb=N_CHIPS, name="br_ssm")
        bd = mm(yg, wbd, b_nb=N_CHIPS, name="br_gdn")
        mrg = merge_fwd(proj, bs, bd, name="merge")
        r2, x2, x2b = down_res_ln(mrg, wout, x1, vec(ln2_g), vec(ln2_b), swiglu=False, alpha=alpha, scale=1.0,
                                  name="mix_out")
        h3 = mm(x2b, wgu2, b_nb=N_CHIPS, out_nb=2, name="ffn_up")
        r3, x3, x3b = down_res_ln(h3, wd2, x2, vec(ln3_g), vec(ln3_b), swiglu=True, alpha=alpha, scale=0.5,
                                  name="ffn_down")
        saved.append(dict(x0b=x0b, h1=h1, r1=r1, x1b=x1b, proj=proj, s5w=s5w, disc_vjp=disc_vjp, ypre=ypre, y2=y2,
                          y2b=y2b, gl=gl, ys=ys, qkv=qkv, gdn_rows=gdn_rows, yg=yg, states=states, bs=bs, bd=bd, mrg=mrg,
                          r2=r2, x2b=x2b, h3=h3, r3=r3, wgu1=wgu1, wgu2=wgu2, wd1=wd1, wd2=wd2, wcat=wcat, wglu=wglu,
                          wbs=wbs, wbd=wbd, wout=wout, conv_full=conv_full))
        xcur, xcur_b = x3, x3b
        after = x3

    dy, loss_blk = loss_head(xcur, loss_target[0], name="loss_head")
    loss = lax.psum(loss_blk[0, 0], ("x", "y", "c"))

    small_grads = {n: [None] * depth for n in SMALL}
    group_b = ['ffn1_w_gu', 'ffn1_w_down']
    group_a = [n for n in BIG if n not in group_b]
    grad_bufs = {n: lax.empty(wts[n].shape, F32) for n in BIG}

    def start_reduction(names, grads_by_name, tag, layer):
        gl = [grads_by_name[n] for n in names]
        part = add_pairs(gl, pair_split(gl, name="rs_split_" + tag), BF16, name="rs_add2_" + tag)
        return names, layer, tag, xchg_start(part, name=f"xchg_start_{tag}_l{layer}")

    def finish_reduction(item, after_arr):
        names, layer, tag, exchange = item
        parts, lands, _ = xchg_wait(exchange, after_arr, name=f"xchg_wait_{tag}_l{layer}")
        half = add_chips(xchg_finish(parts, lands, name="xchg_finish_" + tag), name="rs_add4_" + tag)
        joined = pair_join_stacked(half, [grad_bufs[n] for n in names], layer, name="rs_join_" + tag)
        for n, b in zip(names, joined):
            grad_bufs[n] = b

    pending = []
    for l in reversed(range(depth)):
        s = saved[l]
        order = pending[-1][3][4][0, 0] if pending else 0.0
        vec = lambda v: v[l].reshape(1, -1)

        def ffn_back(dx_out, r, g_ln, xin, hh, wgu, wd):
            dr, drb, dg, db = ln_bwd(r, g_ln, dx_out, name="ln_bwd")
            dh, act = ffn_bwd_mid(dr, wd, hh, scale=0.5, name="ffn_bwd_mid")
            dwd = mm(act, drb, ta=True, out_dtype=BF16, out_scale=0.5, name="ffn_dwd")
            dwgu = mm(xin, dh, ta=True, b_nb=2, out_nb=N_CHIPS, out_dtype=BF16, name="ffn_dwgu")
            dxin = mm(dh, wgu, tb=True, a_nb=2, b_nb=N_CHIPS, add=dr, add_scale=alpha, name="ffn_dx")
            return dxin, dg, db, dwgu, dwd.reshape(N_CHIPS, -1, d)

        dx2, dg3, db3, dwgu2, dwd2 = ffn_back(dy, s['r3'], vec(ln3_g) + order, s['x2b'], s['h3'], s['wgu2'], s['wd2'])
        dr2, dr2b, dg2, db2 = ln_bwd(s['r2'], vec(ln2_g), dx2, name="ln_bwd")
        dmrg = mm(dr2b, s['wout'], tb=True, name="mix_dm")
        dwout = mm(s['mrg'], dr2b, ta=True, out_dtype=BF16, name="mix_dwout").reshape(N_CHIPS, -1, d)
        dbs, dbd, dgs, dgd = merge_bwd(s['proj'], s['bs'], s['bd'], dmrg, name="merge_bwd")
        dwbs = mm(s['ys'], dbs, ta=True, out_nb=N_CHIPS, out_dtype=BF16, name="br_dw")
        dwbd = mm(s['yg'], dbd, ta=True, out_nb=N_CHIPS, out_dtype=BF16, name="br_dw")
        dys = mm(dbs, s['wbs'], tb=True, b_nb=N_CHIPS, name="br_dx")
        dyg = mm(dbd, s['wbd'], tb=True, b_nb=N_CHIPS, name="br_dx")
        dy2a, dgl, dglu_b = glu_gate_bwd(s['y2'], s['gl'], vec(glu_b), dys, name="glu_gate_bwd")
        dwglu = mm(s['y2b'], dgl, ta=True, out_dtype=BF16, name="glu_dw").reshape(N_CHIPS, -1, w)
        dy2 = mm(dgl, s['wglu'], tb=True, add=dy2a, name="glu_dx")
        du, dbr, dbi, dcr, dci, dlr, dli, dd = s5_bwd(s['proj'], u_col0, s['ypre'], dy2, *s['s5w'], name="s5_bwd")
        dqkv_act, dz, dsmall, dalog, ddtb, dnw = gdn_bwd(s['qkv'], s['proj'], z_col0, small_col0, *s['gdn_rows'],
                                                           s['states'], dyg, n_heads, name="gdn_bwd")
        dqkv, dconv = conv_bwd(s['proj'], qkv_col0, s['conv_full'], dqkv_act, name="conv_bwd")
        dsmall_w = jnp.pad(dsmall, ((0, 0), (0, SMALL_W - HEAD_DIM)))
        dproj = jnp.concatenate([dgs, dgd, du, dqkv, dz, dsmall_w], axis=1)
        dwcat = mm(s['x1b'], dproj, ta=True, out_dtype=BF16, name="in_dw")
        dx1 = mm(dproj, s['wcat'], tb=True, add=dr2, add_scale=alpha, name="in_dx")
        item_a = start_reduction(group_a, dict(w_in=_wcat_grad_to_shards(dwcat, d, w, n_heads), glu_w=dwglu,
                                               w_br_ssm=dwbs, w_br_gdn=dwbd, w_out=dwout, ffn2_w_gu=dwgu2,
                                               ffn2_w_down=dwd2), "a", l)
        dx0, dg1, db1, dwgu1, dwd1 = ffn_back(dx1, s['r1'], vec(ln1_g) + item_a[3][4][0, 0], s['x0b'], s['h1'],
                                              s['wgu1'], s['wd1'])
        dy = dx0

        da_re, da_im, dlog_dt, db_re, db_im = s['disc_vjp'](
            (dlr.reshape(n_groups, n_state), dli.reshape(n_groups, n_state),
             _blockdiag_in_grad(dbr, n_groups, n_state, grp), _blockdiag_in_grad(dbi, n_groups, n_state, grp)))
        sg = dict(ln1_g=dg1, ln1_b=db1, ln2_g=dg2, ln2_b=db2, ln3_g=dg3, ln3_b=db3, conv_w=dconv,
                  ssm_a_re=da_re, ssm_a_im=da_im, ssm_log_dt=dlog_dt, ssm_b_re=db_re, ssm_b_im=db_im,
                  ssm_c_re=_blockdiag_out_grad(dcr, n_groups, grp, n_state),
                  ssm_c_im=_blockdiag_out_grad(dci, n_groups, grp, n_state), ssm_d=dd, glu_b=dglu_b,
                  gdn_a_log=dalog[0, :n_heads], gdn_dt_bias=ddtb[0, :n_heads], gdn_norm_w=dnw)
        for n in SMALL:
            small_grads[n][l] = sg[n].reshape(-1)
        for item in pending:
            finish_reduction(item, dy)
        pending = [item_a, start_reduction(group_b, dict(ffn1_w_gu=dwgu1, ffn1_w_down=dwd1), "b", l)]
    grad_x = dy[None]

    small_list = [jnp.stack(small_grads[n]) for n in SMALL]
    packed = _pack(small_list, 512 * N_CHIPS).reshape(N_CHIPS, -1, 128)
    red = reduce_scatter([packed], F32, name="rs_small")
    full = gather_chips(red, [True], name="gather_small")[0].reshape(-1)
    for item in pending:
        finish_reduction(item, full)
    small_red = dict(zip(SMALL, _unpack(full, small_list)))
    cw_cols = conv_w.shape[-1]
    dconv_full = small_red['conv_w'].reshape(depth, CONV_K, N_CHIPS, cw_cols)
    small_red['conv_w'] = lax.dynamic_index_in_dim(dconv_full, chip, axis=2, keepdims=False)

    grads = {}
    for n in BIG:
        grads[n] = grad_bufs[n]
    for n in SMALL:
        grads[n] = small_red[n].reshape(wts[n].shape)

    delta, new_m, new_v = {}, {}, {}
    for n in BIG:
        grads[n], delta[n], new_m[n], new_v[n] = adamw(wts[n], grads[n], mom[n], var[n], name="adamw_big",
                                                       pass_grad=True)
    for n in SMALL:
        delta[n], new_m[n], new_v[n] = adamw(wts[n], grads[n], mom[n], var[n], name="adamw_small")

    return (loss, grad_x, *[grads[n] for n in WEIGHT_NAMES], *[delta[n] for n in WEIGHT_NAMES],
            *[new_m[n] for n in WEIGHT_NAMES], *[new_v[n] for n in WEIGHT_NAMES])
```
